```python
import math
import jax
import jax.numpy as jnp
from jax import lax
import numpy as np

D_MODEL = 1024
BATCH = 32
SEQ = 256
DEPTH = 4
DEC_BATCH = 2
DEC_SEQ = 2048
PAST_LEN = 256

GRID_W = 64
BLOCK_Q = 128
ROPE_BASE = 10000.0
EPS = 1e-6
N_BRANCH = 4
BRANCH_WIDTH = D_MODEL // 2
A_HEAD_DIM = 64
A_HEADS = BRANCH_WIDTH // A_HEAD_DIM
A_KV_HEADS = 2
B_GROUPS = 4
B_GROUP_DIM = BRANCH_WIDTH // B_GROUPS
B_CHUNK = 128
C_HEADS = 4
C_HEAD_DIM = BRANCH_WIDTH // C_HEADS
C_CHUNK = 128
D_HEADS = 4
D_VDIM = BRANCH_WIDTH // D_HEADS
D_HALF_DIM = D_VDIM // 2
PEER_HEADS = 8
PEER_NKEYS = 128
PEER_EXPERTS = PEER_NKEYS * PEER_NKEYS
PEER_QDIM = 256
PEER_TOPK = 16
PEER_BLOCK = 128
ALPHA = (2 * DEPTH) ** 0.25
BETA = (8 * DEPTH) ** -0.25

IN_SPLITS = (A_HEADS * A_HEAD_DIM, A_KV_HEADS * A_HEAD_DIM, A_KV_HEADS * A_HEAD_DIM,
             BRANCH_WIDTH, BRANCH_WIDTH,
             BRANCH_WIDTH, BRANCH_WIDTH, BRANCH_WIDTH, BRANCH_WIDTH, 2 * 2 * C_HEADS,
             D_HEADS * 2 * D_HALF_DIM, D_HEADS * 2 * D_HALF_DIM, D_HEADS * D_VDIM,
             N_BRANCH * D_MODEL)
IN_WIDTH = sum(IN_SPLITS)
IN_SPLIT_POINTS = tuple(int(p) for p in np.cumsum(IN_SPLITS)[:-1])

kernel_name = 'hybrid_flow_backbone_step'

F32 = jnp.float32


def layer_norm(x, g, b):
    xf = x.astype(F32)
    mu = jnp.mean(xf, axis=-1, keepdims=True)
    var = jnp.mean(jnp.square(xf - mu), axis=-1, keepdims=True)
    return ((xf - mu) * lax.rsqrt(var + EPS) * g + b).astype(x.dtype)


def head_layer_norm(x, g):
    xf = x.astype(F32)
    mu = jnp.mean(xf, axis=-1, keepdims=True)
    var = jnp.mean(jnp.square(xf - mu), axis=-1, keepdims=True)
    return ((xf - mu) * lax.rsqrt(var + EPS) * g).astype(x.dtype)


def rms_norm(x, g):
    xf = x.astype(F32)
    return (xf * lax.rsqrt(jnp.mean(xf * xf, axis=-1, keepdims=True) + EPS) * g).astype(x.dtype)


def rope_2d(x):
    T, d = x.shape[1], x.shape[-1]
    rows = T // GRID_W
    row = jnp.repeat(jnp.arange(rows, dtype=F32), GRID_W)
    col = jnp.tile(jnp.arange(GRID_W, dtype=F32), rows)
    quarter = d // 4
    inv = ROPE_BASE ** (-jnp.arange(quarter, dtype=F32) / quarter)
    bshape = (T,) + (1,) * (x.ndim - 3) + (quarter,)
    xf = x.astype(F32)

    def rot(xa, pos):
        ang = (pos[:, None] * inv).reshape(bshape)
        cos, sin = jnp.cos(ang), jnp.sin(ang)
        x1, x2 = xa[..., :quarter], xa[..., quarter:]
        return jnp.concatenate([x1 * cos - x2 * sin, x2 * cos + x1 * sin], axis=-1)

    return jnp.concatenate([rot(xf[..., :2 * quarter], row), rot(xf[..., 2 * quarter:], col)], axis=-1).astype(x.dtype)


def gqa_attend(q, k, v):
    B, T, H, d = q.shape
    G = k.shape[2]
    R = H // G
    nb = T // BLOCK_Q
    qb = q.reshape(B, nb, BLOCK_Q, G, R, d).transpose(1, 0, 2, 3, 4, 5)
    scale = d ** -0.5

    def one(qblk):
        s = jnp.einsum('bqgrd,bkgd->bgrqk', qblk, k).astype(F32) * scale
        p = jax.nn.softmax(s, axis=-1).astype(v.dtype)
        return jnp.einsum('bgrqk,bkgd->bqgrd', p, v)

    o = lax.map(one, qb)
    return o.transpose(1, 0, 2, 3, 4, 5).reshape(B, T, H * d)


def diff_attend(q, k, v, lam):
    B, T, H, _, d = q.shape
    nb = T // BLOCK_Q
    qb = q.reshape(B, nb, BLOCK_Q, H, 2, d).transpose(1, 0, 2, 3, 4, 5)
    scale = d ** -0.5

    def one(qblk):
        s = jnp.einsum('bqhjd,bkhjd->bhjqk', qblk, k).astype(F32) * scale
        p = jax.nn.softmax(s, axis=-1)
        a = (p[:, :, 0] - lam * p[:, :, 1]).astype(v.dtype)
        return jnp.einsum('bhqk,bkhe->bqhe', a, v)

    o = lax.map(one, qb)
    return o.transpose(1, 0, 2, 3, 4).reshape(B, T, H, v.shape[-1])


def chunk_mlp(u, v, ws, bias):
    B, T, _ = v.shape
    nc = T // B_CHUNK
    vf = v.astype(F32)
    mu = jnp.mean(vf, axis=-1, keepdims=True)
    var = jnp.mean(jnp.square(vf - mu), axis=-1, keepdims=True)
    vn = ((vf - mu) * lax.rsqrt(var + EPS)).astype(v.dtype).reshape(B, nc, B_CHUNK, B_GROUPS, B_GROUP_DIM)
    s = jnp.einsum('gpq,bcqgd->bcpgd', ws, vn) + bias.T[None, None, :, :, None]
    return u * s.reshape(B, T, BRANCH_WIDTH)


def mlstm_chunkwise(q, k, v, ig, lf, C0, n0, m0):
    B, T, H, d = q.shape
    nc = T // C_CHUNK
    L = C_CHUNK

    def to_chunks(a):
        return a.reshape(B, nc, L, H, d).transpose(1, 0, 3, 2, 4)

    def gates_to_chunks(a):
        return a.reshape(B, nc, L, H).transpose(1, 0, 3, 2)

    causal = jnp.tril(jnp.ones((L, L), dtype=bool))

    def step(carry, xs):
        C, n, m = carry
        qc, kc, vc, ic, fc = xs
        qf, kf, vf = qc.astype(F32), kc.astype(F32), vc.astype(F32)
        b = jnp.cumsum(fc, axis=-1)
        dlog = jnp.where(causal, b[..., :, None] - b[..., None, :] + ic[..., None, :], -jnp.inf)
        m_t = jnp.maximum(jnp.max(dlog, axis=-1), b + m[..., None])
        w = jnp.einsum('bhtd,bhsd->bhts', qf, kf) * jnp.exp(dlog - m_t[..., None])
        inter = jnp.exp(b + m[..., None] - m_t)
        num = jnp.einsum('bhts,bhsd->bhtd', w, vf) + inter[..., None] * jnp.einsum('bhtd,bhde->bhte', qf, C)
        den = jnp.sum(w, axis=-1) + inter * jnp.einsum('bhtd,bhd->bht', qf, n)
        h = num / jnp.maximum(jnp.abs(den), jnp.exp(-m_t))[..., None]
        b_last = b[..., -1]
        g = b_last[..., None] - b + ic
        m_new = jnp.maximum(b_last + m, jnp.max(g, axis=-1))
        decay = jnp.exp(b_last + m - m_new)
        ws = jnp.exp(g - m_new[..., None])
        C_new = decay[..., None, None] * C + jnp.einsum('bhs,bhsd,bhse->bhde', ws, kf, vf)
        n_new = decay[..., None] * n + jnp.einsum('bhs,bhsd->bhd', ws, kf)
        return (C_new, n_new, m_new), h.astype(q.dtype)

    init = (C0.astype(F32), n0.astype(F32), m0.astype(F32))
    xs = (to_chunks(q), to_chunks(k), to_chunks(v), gates_to_chunks(ig), gates_to_chunks(lf))
    (C, n, m), hs = lax.scan(step, init, xs)
    h = hs.transpose(1, 0, 3, 2, 4).reshape(B, T, H, d)
    return h, C, n, m


def mlstm_branch(q, k, v, o, graw, gate_bias, gn, init):
    B, T = q.shape[:2]
    k = k * (C_HEAD_DIM ** -0.5)
    gts = graw.reshape(B, T, 2, 2, C_HEADS).astype(F32) + gate_bias.astype(F32)
    if init is None:
        C0 = jnp.zeros((B, 2, C_HEADS, C_HEAD_DIM, C_HEAD_DIM), F32)
        n0 = jnp.zeros((B, 2, C_HEADS, C_HEAD_DIM), F32)
        m0 = jnp.zeros((B, 2, C_HEADS), F32)
    else:
        C0, n0, m0 = init
    h_f, C_f, n_f, m_f = mlstm_chunkwise(q, k, v, gts[:, :, 0, 0], jax.nn.log_sigmoid(gts[:, :, 0, 1]),
                                         C0[:, 0], n0[:, 0], m0[:, 0])
    fl = lambda a: jnp.flip(a, axis=1)
    h_b, C_b, n_b, m_b = mlstm_chunkwise(fl(q), fl(k), fl(v), fl(gts[:, :, 1, 0]),
                                         fl(jax.nn.log_sigmoid(gts[:, :, 1, 1])),
                                         C0[:, 1], n0[:, 1], m0[:, 1])
    h = h_f + fl(h_b)
    y = head_layer_norm(h, gn).reshape(B, T, BRANCH_WIDTH) * jax.nn.sigmoid(o)
    state = (jnp.stack([C_f, C_b], axis=1), jnp.stack([n_f, n_b], axis=1), jnp.stack([m_f, m_b], axis=1))
    return y, state


def merge_branches(ys, gates, w_branch, w_out):
    B, T, _ = gates.shape
    g = jax.nn.sigmoid(gates.reshape(B, T, N_BRANCH, D_MODEL))
    proj = jnp.einsum('btne,ned->btnd', jnp.stack(ys, axis=2), w_branch)
    return jnp.sum(g * proj, axis=2) @ w_out


def peer_ffn(h, wq, keys, u, v):
    B, T, D = h.shape
    N = B * T
    x = h.reshape(N, D)
    q = (x @ wq).reshape(N, PEER_HEADS, 2, PEER_QDIM // 2)
    s = jnp.einsum('nhpd,hpkd->nhpk', q, keys).astype(F32)
    s1, i1 = lax.top_k(s[:, :, 0], PEER_TOPK)
    s2, i2 = lax.top_k(s[:, :, 1], PEER_TOPK)
    cand = (s1[..., :, None] + s2[..., None, :]).reshape(N, PEER_HEADS, PEER_TOPK * PEER_TOPK)
    best, pos = lax.top_k(cand, PEER_TOPK)
    e1 = jnp.take_along_axis(i1, pos // PEER_TOPK, axis=-1)
    e2 = jnp.take_along_axis(i2, pos % PEER_TOPK, axis=-1)
    experts = (e1 * PEER_NKEYS + e2).reshape(N, PEER_HEADS * PEER_TOPK)
    gate = jax.nn.softmax(best, axis=-1).reshape(N, PEER_HEADS * PEER_TOPK).astype(h.dtype)
    nb = N // PEER_BLOCK

    def one(args):
        xb, eb, gb = args
        a = jax.nn.gelu(jnp.einsum('nkd,nd->nk', u[eb], xb), approximate=False)
        return jnp.einsum('nk,nkd->nd', gb * a, v[eb])

    out = lax.map(one, (x.reshape(nb, PEER_BLOCK, D), experts.reshape(nb, PEER_BLOCK, -1),
                        gate.reshape(nb, PEER_BLOCK, -1)))
    return out.reshape(B, T, D)


def trunk_layer(x, mod, lp, layer_idx, ctx):
    B, T, _ = x.shape
    sh1, sc1, g1, sh2, sc2, g2 = jnp.split(mod, 6, axis=-1)
    h = x * (1.0 + sc1) + sh1
    (aq, ak, av, bu, bv, cq, ck, cv, co, cg, dq, dk, dv, gates) = jnp.split(h @ lp['w_in'], IN_SPLIT_POINTS, axis=-1)
    aq = rms_norm(aq.reshape(B, T, A_HEADS, A_HEAD_DIM), lp['attn_qk_gain'][0])
    ak = rms_norm(ak.reshape(B, T, A_KV_HEADS, A_HEAD_DIM), lp['attn_qk_gain'][1])
    av = av.reshape(B, T, A_KV_HEADS, A_HEAD_DIM)
    cq = cq.reshape(B, T, C_HEADS, C_HEAD_DIM)
    ck = ck.reshape(B, T, C_HEADS, C_HEAD_DIM)
    cv = cv.reshape(B, T, C_HEADS, C_HEAD_DIM)
    dq = dq.reshape(B, T, D_HEADS, 2, D_HALF_DIM)
    dk = dk.reshape(B, T, D_HEADS, 2, D_HALF_DIM)
    dv = dv.reshape(B, T, D_HEADS, D_VDIM)
    lam_init = 0.8 - 0.6 * math.exp(-0.3 * layer_idx)
    lv = lp['diff_lambda'].astype(F32)
    lam = jnp.exp(jnp.sum(lv[0] * lv[1])) - jnp.exp(jnp.sum(lv[2] * lv[3])) + lam_init
    if ctx is None:
        ya = gqa_attend(aq, ak, av)
        yd = diff_attend(dq, dk, dv, lam)
        yc, c_state = mlstm_branch(cq, ck, cv, co, cg, lp['mlstm_gate_bias'], lp['mlstm_gn'], None)
        new_state = (ak, av, dk, dv, c_state[0], c_state[1], c_state[2])
    else:
        cak, cav, cdk, cdv, cC, cn, cm = ctx
        ya = gqa_attend(rope_2d(aq),
                        jnp.concatenate([rope_2d(ak), cak.astype(ak.dtype)], axis=1),
                        jnp.concatenate([av, cav.astype(av.dtype)], axis=1))
        yd = diff_attend(rope_2d(dq),
                         jnp.concatenate([rope_2d(dk), cdk.astype(dk.dtype)], axis=1),
                         jnp.concatenate([dv, cdv.astype(dv.dtype)], axis=1), lam)
        yc, _ = mlstm_branch(cq, ck, cv, co, cg, lp['mlstm_gate_bias'], lp['mlstm_gn'], (cC, cn, cm))
        new_state = None
    yb = chunk_mlp(bu, bv, lp['gmlp_ws'], lp['gmlp_b'])
    yd = (rms_norm(yd, lp['diff_gn']) * (1.0 - lam_init)).reshape(B, T, BRANCH_WIDTH)
    y = merge_branches((ya, yb, yc, yd), gates, lp['w_branch'], lp['w_out'])
    x = layer_norm(ALPHA * x + g1 * y, lp['ln_g'][0], lp['ln_b'][0])
    h2 = x * (1.0 + sc2) + sh2
    f = peer_ffn(h2, lp['peer_wq'], lp['peer_keys'], lp['peer_u'], lp['peer_v'])
    x = layer_norm(ALPHA * x + g2 * f, lp['ln_g'][1], lp['ln_b'][1])
    return x, new_state


def setup_inputs(seed: int = 0) -> dict:
    key = jax.random.key(seed)
    ks = jax.random.split(key, 32)

    def nrm(k, shape, s):
        return jax.random.normal(k, shape, F32) * s

    D = D_MODEL
    return {
        'x_prompt': nrm(ks[0], (BATCH, SEQ, D), 1.0),
        'x_sample': nrm(ks[1], (DEC_BATCH, DEC_SEQ, D), 1.0),
        'cache_a_k': nrm(ks[2], (DEC_BATCH, DEPTH, PAST_LEN, A_KV_HEADS, A_HEAD_DIM), 1.0),
        'cache_a_v': nrm(ks[3], (DEC_BATCH, DEPTH, PAST_LEN, A_KV_HEADS, A_HEAD_DIM), 1.0),
        'cache_d_k': nrm(ks[4], (DEC_BATCH, DEPTH, PAST_LEN, D_HEADS, 2, D_HALF_DIM), 1.0),
        'cache_d_v': nrm(ks[5], (DEC_BATCH, DEPTH, PAST_LEN, D_HEADS, D_VDIM), 1.0),
        'state_c_C': nrm(ks[6], (DEC_BATCH, DEPTH, 2, C_HEADS, C_HEAD_DIM, C_HEAD_DIM), 0.05),
        'state_c_n': nrm(ks[7], (DEC_BATCH, DEPTH, 2, C_HEADS, C_HEAD_DIM), 0.5),
        'state_c_m': nrm(ks[8], (DEC_BATCH, DEPTH, 2, C_HEADS), 0.5),
        'c': nrm(ks[9], (DEC_BATCH, D), 1.0),
        'c_ctx': nrm(ks[10], (D,), 1.0),
        'w_mod': nrm(ks[11], (DEPTH, D, 6 * D), 0.5 * D ** -0.5),
        'b_mod': nrm(ks[12], (DEPTH, 6 * D), 0.02),
        'w_in': nrm(ks[13], (DEPTH, D, IN_WIDTH), D ** -0.5),
        'attn_qk_gain': 1.0 + nrm(ks[14], (DEPTH, 2, A_HEAD_DIM), 0.02),
        'gmlp_ws': nrm(ks[15], (DEPTH, B_GROUPS, B_CHUNK, B_CHUNK), B_CHUNK ** -0.5),
        'gmlp_b': 1.0 + nrm(ks[16], (DEPTH, B_GROUPS, B_CHUNK), 0.02),
        'mlstm_gate_bias': nrm(ks[17], (DEPTH, 2, 2, C_HEADS), 0.1) + jnp.array([0.0, 3.0], F32)[None, None, :, None],
        'mlstm_gn': 1.0 + nrm(ks[18], (DEPTH, C_HEADS, C_HEAD_DIM), 0.02),
        'diff_lambda': nrm(ks[19], (DEPTH, 4, D_HALF_DIM), 0.1),
        'diff_gn': 1.0 + nrm(ks[20], (DEPTH, D_HEADS, D_VDIM), 0.02),
        'w_branch': nrm(ks[21], (DEPTH, N_BRANCH, BRANCH_WIDTH, D), BRANCH_WIDTH ** -0.5),
        'w_out': nrm(ks[22], (DEPTH, D, D), BETA * D ** -0.5),
        'ln_g': 1.0 + nrm(ks[23], (DEPTH, 2, D), 0.02),
        'ln_b': nrm(ks[24], (DEPTH, 2, D), 0.02),
        'peer_wq': nrm(ks[25], (DEPTH, D, PEER_HEADS * PEER_QDIM), D ** -0.5),
        'peer_keys': nrm(ks[26], (DEPTH, PEER_HEADS, 2, PEER_NKEYS, PEER_QDIM // 2), (PEER_QDIM // 2) ** -0.5),
        'peer_u': nrm(ks[27], (DEPTH, PEER_EXPERTS, D), D ** -0.5),
        'peer_v': nrm(ks[28], (DEPTH, PEER_EXPERTS, D), BETA),
    }


def reference(x_prompt, x_sample, cache_a_k, cache_a_v, cache_d_k, cache_d_v, state_c_C, state_c_n, state_c_m,
              c, c_ctx, w_mod, b_mod, w_in, attn_qk_gain, gmlp_ws, gmlp_b, mlstm_gate_bias, mlstm_gn,
              diff_lambda, diff_gn, w_branch, w_out, ln_g, ln_b, peer_wq, peer_keys, peer_u, peer_v):
    y_p = x_prompt
    y_s = x_sample
    st_ak, st_av, st_dk, st_dv, st_C, st_n, st_m = [], [], [], [], [], [], []
    for l in range(DEPTH):
        lp = {'w_in': w_in[l], 'attn_qk_gain': attn_qk_gain[l], 'gmlp_ws': gmlp_ws[l], 'gmlp_b': gmlp_b[l],
              'mlstm_gate_bias': mlstm_gate_bias[l], 'mlstm_gn': mlstm_gn[l], 'diff_lambda': diff_lambda[l],
              'diff_gn': diff_gn[l], 'w_branch': w_branch[l], 'w_out': w_out[l], 'ln_g': ln_g[l], 'ln_b': ln_b[l],
              'peer_wq': peer_wq[l], 'peer_keys': peer_keys[l], 'peer_u': peer_u[l], 'peer_v': peer_v[l]}
        mod_ctx = (jax.nn.silu(c_ctx) @ w_mod[l] + b_mod[l])[None, None, :]
        y_p, st = trunk_layer(y_p, mod_ctx, lp, l, None)
        st_ak.append(st[0]); st_av.append(st[1]); st_dk.append(st[2]); st_dv.append(st[3])
        st_C.append(st[4]); st_n.append(st[5]); st_m.append(st[6])
        mod_lat = (jax.nn.silu(c) @ w_mod[l] + b_mod[l])[:, None, :]
        ctx = (cache_a_k[:, l], cache_a_v[:, l], cache_d_k[:, l], cache_d_v[:, l],
               state_c_C[:, l], state_c_n[:, l], state_c_m[:, l])
        y_s, _ = trunk_layer(y_s, mod_lat, lp, l, ctx)
    new_a_k = jnp.stack(st_ak, axis=1)
    new_a_v = jnp.stack(st_av, axis=1)
    new_d_k = jnp.stack(st_dk, axis=1)
    new_d_v = jnp.stack(st_dv, axis=1)
    new_c_C = jnp.stack(st_C, axis=1)
    new_c_n = jnp.stack(st_n, axis=1)
    new_c_m = jnp.stack(st_m, axis=1)
    return (y_p, y_s, new_a_k, new_a_v, new_d_k, new_d_v, new_c_C, new_c_n, new_c_m)
```

```python
import functools
import math

import numpy as np
import jax
import jax.numpy as jnp
from jax import lax
from jax.experimental import pallas as pl
from jax.experimental.pallas import tpu as pltpu

F32 = jnp.float32
BF16 = jnp.bfloat16
HIGHEST = lax.Precision.HIGHEST

D_MODEL = 1024
DEPTH = 4
GRID_W = 64
ROPE_BASE = 10000.0
EPS = 1e-6
BRANCH_WIDTH = D_MODEL // 2
A_HEAD_DIM = 64
A_HEADS = 8
A_KV_HEADS = 2
B_GROUPS = 4
CHUNK = 128
C_HEADS = 4
C_HEAD_DIM = 128
D_HEADS = 4
D_VDIM = 128
D_HALF_DIM = 64
PEER_HEADS = 8
PEER_NKEYS = 128
PEER_EXPERTS = PEER_NKEYS * PEER_NKEYS
PEER_QDIM = 256
PEER_TOPK = 16
ALPHA = (2 * DEPTH) ** 0.25

_OFF_A = 0
_OFF_B = 768
_OFF_C = 1792
_OFF_CG = 3840
_OFF_D = 3856
_OFF_G = 5392
_C_WIDTH = 4 * BRANCH_WIDTH + 128

VMEM_LIMIT_BYTES = 56 * 1024 * 1024
NEG_INF = float("-inf")


def _cparams(*sem):
    return pltpu.CompilerParams(dimension_semantics=sem, vmem_limit_bytes=VMEM_LIMIT_BYTES)


def _dot(a, b):
    return jnp.dot(a, b, preferred_element_type=F32)


def _dot_nt(a, b):
    return lax.dot_general(a, b, (((1,), (1,)), ((), ())), preferred_element_type=F32)


def _modulated(x, mod_ref, which):
    base = 3 * D_MODEL * which
    sh = mod_ref[:, base:base + D_MODEL]
    sc = mod_ref[:, base + D_MODEL:base + 2 * D_MODEL]
    return x * (1.0 + sc) + sh


def _layer_norm_rows(z, g, b):
    mu = jnp.mean(z, axis=-1, keepdims=True)
    zc = z - mu
    var = jnp.mean(zc * zc, axis=-1, keepdims=True)
    return zc * lax.rsqrt(var + EPS) * g + b


def _rope(x, cos, sin_signed, lane):
    w = x.shape[1]
    nxt = pltpu.roll(x, w - 16, 1)
    prv = pltpu.roll(x, 16, 1)
    partner = jnp.where((lane % 32) < 16, nxt, prv)
    return x * cos + partner * sin_signed


def _tile_lanes(t, n):
    return t if n == 1 else jnp.concatenate([t] * n, axis=1)


def _mod_kernel(c_ref, w_ref, b_ref, o_ref):
    c = c_ref[...]
    s = c * jax.nn.sigmoid(c)
    o_ref[...] = jnp.dot(s, w_ref[...], precision=HIGHEST, preferred_element_type=F32) + b_ref[...]


def _modulation(c_rows, w_mod, b_mod):
    n_col = 6 * D_MODEL // 1024
    return pl.pallas_call(
        _mod_kernel,
        out_shape=jax.ShapeDtypeStruct((DEPTH, 8, 6 * D_MODEL), F32),
        grid=(DEPTH, n_col),
        in_specs=[
            pl.BlockSpec((8, D_MODEL), lambda l, j: (0, 0)),
            pl.BlockSpec((None, D_MODEL, 1024), lambda l, j: (l, 0, j)),
            pl.BlockSpec((None, 1, 1024), lambda l, j: (l, 0, j)),
        ],
        out_specs=pl.BlockSpec((None, 8, 1024), lambda l, j: (l, 0, j)),
        compiler_params=_cparams("arbitrary", "arbitrary"),
        name="modulation",
    )(c_rows, w_mod, b_mod.reshape(DEPTH, 1, 6 * D_MODEL))


def _attn_a_kernel(*refs, seq, tq, n_cache, rope, proj_rows):
    if rope:
        (x_ref, mod_ref, w_ref, gain_ref, bd_ref, rep_ref, cos_ref, sin_ref, ck_ref, cv_ref,
         y_ref, q_s, k_s, v_s) = refs
    else:
        (x_ref, mod_ref, w_ref, gain_ref, bd_ref, rep_ref,
         y_ref, nk_ref, nv_ref, q_s, k_s, v_s) = refs
    qi = pl.program_id(1)

    @pl.when(qi == 0)
    def _project():
        lane = lax.broadcasted_iota(jnp.int32, (1, 640), 1)
        for r0 in range(0, seq, proj_rows):
            rows = pl.ds(r0, proj_rows)
            h = _modulated(x_ref[rows, :], mod_ref, 0).astype(BF16)
            p = _dot(h, w_ref[...])
            qk = p[:, :640]
            sq = qk * qk
            hi = sq.astype(BF16)
            lo = (sq - hi.astype(F32)).astype(BF16)
            ms = _dot(hi, bd_ref[...]) + _dot(lo, bd_ref[...])
            qk = qk * lax.rsqrt(ms + EPS) * gain_ref[...]
            v = p[:, 640:768]
            if not rope:
                nk_ref[rows, :] = qk[:, 512:640]
                nv_ref[rows, :] = v
            else:
                cos = _tile_lanes(cos_ref[rows, :], 5)
                sin = _tile_lanes(sin_ref[rows, :], 5)
                qk = _rope(qk, cos, sin, lane)
            q_s[rows, :] = (qk[:, :512] * (A_HEAD_DIM ** -0.5)).astype(BF16)
            k_s[rows, :] = _dot(qk[:, 512:640].astype(BF16), rep_ref[...]).astype(BF16)
            v_s[rows, :] = _dot(v.astype(BF16), rep_ref[...]).astype(BF16)
        if n_cache:
            crow = pl.ds(seq, n_cache)
            k_s[crow, :] = _dot(ck_ref[...].astype(BF16), rep_ref[...]).astype(BF16)
            v_s[crow, :] = _dot(cv_ref[...].astype(BF16), rep_ref[...]).astype(BF16)

    head_of_lane = lax.broadcasted_iota(jnp.int32, (1, 256), 1) // A_HEAD_DIM
    qb = q_s[pl.ds(pl.multiple_of(qi * tq, tq), tq), :]
    for g in range(A_KV_HEADS):
        cols = slice(g * 256, (g + 1) * 256)
        qg = qb[:, cols]
        kg = k_s[:, cols]
        vg = v_s[:, cols]
        acc = jnp.zeros((tq, 256), F32)
        for r in range(A_HEADS // A_KV_HEADS):
            sel = head_of_lane == r
            qm = jnp.where(sel, qg, jnp.zeros_like(qg))
            s = _dot_nt(qm, kg)
            m = jnp.max(s, axis=-1, keepdims=True)
            p = jnp.exp(s - m)
            l = jnp.sum(p, axis=-1, keepdims=True)
            o = _dot(p.astype(BF16), vg)
            acc = acc + jnp.where(sel, o * (1.0 / l), 0.0)
        y_ref[:, cols] = acc.astype(BF16)


def _attn_a(x, mod, w_a, gain_row, bd, rep, *, n_seq, seq, tq, rope=None, cache=None, layer=0):
    n_q = seq // tq
    n_cache = 0 if cache is None else cache[0].shape[2]
    proj_rows = min(seq, 512)
    kern = functools.partial(_attn_a_kernel, seq=seq, tq=tq, n_cache=n_cache, rope=rope is not None,
                             proj_rows=proj_rows)
    const = lambda s, q: (0, 0)
    in_specs = [
        pl.BlockSpec((seq, D_MODEL), lambda s, q: (s, 0)),
        pl.BlockSpec((None, 1, 6 * D_MODEL), lambda s, q: (s if mod.shape[0] > 1 else 0, 0, 0)),
        pl.BlockSpec(w_a.shape, const),
        pl.BlockSpec(gain_row.shape, const),
        pl.BlockSpec(bd.shape, const),
        pl.BlockSpec(rep.shape, const),
    ]
    args = [x, mod, w_a, gain_row, bd, rep]
    n_tok = n_seq * seq
    y_spec = pl.BlockSpec((tq, BRANCH_WIDTH), lambda s, q: (s * n_q + q, 0))
    y_shape = jax.ShapeDtypeStruct((n_tok, BRANCH_WIDTH), BF16)
    if rope is not None:
        cos, sin = rope
        in_specs += [pl.BlockSpec(cos.shape, const), pl.BlockSpec(sin.shape, const),
                     pl.BlockSpec((None, None, n_cache, 128), lambda s, q: (s, layer, 0, 0)),
                     pl.BlockSpec((None, None, n_cache, 128), lambda s, q: (s, layer, 0, 0))]
        args += [cos, sin, cache[0], cache[1]]
        out_shape, out_specs = y_shape, y_spec
    else:
        kv_shape = jax.ShapeDtypeStruct((n_tok, 128), F32)
        kv_spec = pl.BlockSpec((seq, 128), lambda s, q: (s, 0))
        out_shape, out_specs = (y_shape, kv_shape, kv_shape), (y_spec, kv_spec, kv_spec)
    return pl.pallas_call(
        kern, out_shape=out_shape, grid=(n_seq, n_q), in_specs=in_specs, out_specs=out_specs,
        scratch_shapes=[pltpu.VMEM((seq, 512), BF16), pltpu.VMEM((seq + n_cache, 512), BF16),
                        pltpu.VMEM((seq + n_cache, 512), BF16)],
        compiler_params=_cparams("arbitrary", "arbitrary"),
        name="branch_a_lat" if rope is not None else "branch_a_ctx",
    )(*args)


def _attn_d_kernel(*refs, seq, tq, n_cache, rope, proj_rows, lam_init):
    if rope:
        (x_ref, mod_ref, w_ref, lam_ref, gn_ref, cos_ref, sin_ref, ck_ref, cv_ref,
         y_ref, q_s, k_s, v_s) = refs
    else:
        (x_ref, mod_ref, w_ref, lam_ref, gn_ref,
         y_ref, nk_ref, nv_ref, q_s, k_s, v_s) = refs
    qi = pl.program_id(1)

    @pl.when(qi == 0)
    def _project():
        lane = lax.broadcasted_iota(jnp.int32, (1, 512), 1)
        for r0 in range(0, seq, proj_rows):
            rows = pl.ds(r0, proj_rows)
            h = _modulated(x_ref[rows, :], mod_ref, 0).astype(BF16)
            p = _dot(h, w_ref[...])
            dq, dk, dv = p[:, :512], p[:, 512:1024], p[:, 1024:1536]
            if not rope:
                nk_ref[rows, :] = dk
                nv_ref[rows, :] = dv
            else:
                cos = _tile_lanes(cos_ref[rows, :], 4)
                sin = _tile_lanes(sin_ref[rows, :], 4)
                dq = _rope(dq, cos, sin, lane)
                dk = _rope(dk, cos, sin, lane)
            q_s[rows, :] = (dq * (D_HALF_DIM ** -0.5)).astype(BF16)
            k_s[rows, :] = dk.astype(BF16)
            v_s[rows, :] = dv.astype(BF16)
        if n_cache:
            crow = pl.ds(seq, n_cache)
            k_s[crow, :] = ck_ref[...].astype(BF16)
            v_s[crow, :] = cv_ref[...].astype(BF16)

    lv = lam_ref[...]
    lam = (jnp.exp(jnp.sum(lv[0:1] * lv[1:2], axis=-1, keepdims=True))
           - jnp.exp(jnp.sum(lv[2:3] * lv[3:4], axis=-1, keepdims=True)) + lam_init)
    half_of_lane = lax.broadcasted_iota(jnp.int32, (1, 128), 1) // D_HALF_DIM
    qb = q_s[pl.ds(pl.multiple_of(qi * tq, tq), tq), :]
    for hd in range(D_HEADS):
        cols = slice(hd * 128, (hd + 1) * 128)
        qh = qb[:, cols]
        kh = k_s[:, cols]
        vh = v_s[:, cols]
        probs = []
        for j in range(2):
            qm = jnp.where(half_of_lane == j, qh, jnp.zeros_like(qh))
            s = _dot_nt(qm, kh)
            m = jnp.max(s, axis=-1, keepdims=True)
            p = jnp.exp(s - m)
            l = jnp.sum(p, axis=-1, keepdims=True)
            probs.append(p * (1.0 / l))
        a = (probs[0] - lam * probs[1]).astype(BF16)
        o = _dot(a, vh)
        ms = jnp.mean(o * o, axis=-1, keepdims=True)
        o = o * lax.rsqrt(ms + EPS) * gn_ref[:, cols] * (1.0 - lam_init)
        y_ref[:, cols] = o.astype(BF16)


def _attn_d(x, mod, w_d, lam_params, gn_row, *, n_seq, seq, tq, lam_init, rope=None, cache=None, layer=0):
    n_q = seq // tq
    n_cache = 0 if cache is None else cache[0].shape[2]
    proj_rows = min(seq, 512)
    kern = functools.partial(_attn_d_kernel, seq=seq, tq=tq, n_cache=n_cache, rope=rope is not None,
                             proj_rows=proj_rows, lam_init=lam_init)
    const = lambda s, q: (0, 0)
    in_specs = [
        pl.BlockSpec((seq, D_MODEL), lambda s, q: (s, 0)),
        pl.BlockSpec((None, 1, 6 * D_MODEL), lambda s, q: (s if mod.shape[0] > 1 else 0, 0, 0)),
        pl.BlockSpec(w_d.shape, const),
        pl.BlockSpec(lam_params.shape, const),
        pl.BlockSpec(gn_row.shape, const),
    ]
    args = [x, mod, w_d, lam_params, gn_row]
    n_tok = n_seq * seq
    y_spec = pl.BlockSpec((tq, BRANCH_WIDTH), lambda s, q: (s * n_q + q, 0))
    y_shape = jax.ShapeDtypeStruct((n_tok, BRANCH_WIDTH), BF16)
    if rope is not None:
        cos, sin = rope
        in_specs += [pl.BlockSpec(cos.shape, const), pl.BlockSpec(sin.shape, const),
                     pl.BlockSpec((None, None, n_cache, 512), lambda s, q: (s, layer, 0, 0)),
                     pl.BlockSpec((None, None, n_cache, 512), lambda s, q: (s, layer, 0, 0))]
        args += [cos, sin, cache[0], cache[1]]
        out_shape, out_specs = y_shape, y_spec
    else:
        kv_shape = jax.ShapeDtypeStruct((n_tok, 512), F32)
        kv_spec = pl.BlockSpec((seq, 512), lambda s, q: (s, 0))
        out_shape, out_specs = (y_shape, kv_shape, kv_shape), (y_spec, kv_spec, kv_spec)
    return pl.pallas_call(
        kern, out_shape=out_shape, grid=(n_seq, n_q), in_specs=in_specs, out_specs=out_specs,
        scratch_shapes=[pltpu.VMEM((seq, 512), BF16), pltpu.VMEM((seq + n_cache, 512), BF16),
                        pltpu.VMEM((seq + n_cache, 512), BF16)],
        compiler_params=_cparams("arbitrary", "arbitrary"),
        name="branch_d_lat" if rope is not None else "branch_d_ctx",
    )(*args)


def _gmlp_kernel(x_ref, mod_ref, w_ref, ws_ref, bias_ref, y_ref, *, rows):
    h = _modulated(x_ref[...], mod_ref, 0).astype(BF16)
    p = _dot(h, w_ref[...])
    u, v = p[:, :BRANCH_WIDTH], p[:, BRANCH_WIDTH:]
    mu = jnp.mean(v, axis=-1, keepdims=True)
    vc = v - mu
    var = jnp.mean(vc * vc, axis=-1, keepdims=True)
    vn = (vc * lax.rsqrt(var + EPS)).astype(BF16)
    for c in range(rows // CHUNK):
        rs = slice(c * CHUNK, (c + 1) * CHUNK)
        for g in range(B_GROUPS):
            cs = slice(g * 128, (g + 1) * 128)
            s = _dot(ws_ref[g].astype(BF16), vn[rs, cs]) + bias_ref[:, cs]
            y_ref[rs, cs] = (u[rs, cs] * s).astype(BF16)


def _gmlp(x, mod, w_b, ws, bias_full, *, seq, rows):
    n_tok = x.shape[0]
    per_seq = seq // rows
    kern = functools.partial(_gmlp_kernel, rows=rows)
    return pl.pallas_call(
        kern, out_shape=jax.ShapeDtypeStruct((n_tok, BRANCH_WIDTH), BF16),
        grid=(n_tok // rows,),
        in_specs=[
            pl.BlockSpec((rows, D_MODEL), lambda i: (i, 0)),
            pl.BlockSpec((None, 1, 6 * D_MODEL), lambda i: (i // per_seq if mod.shape[0] > 1 else 0, 0, 0)),
            pl.BlockSpec(w_b.shape, lambda i: (0, 0)),
            pl.BlockSpec(ws.shape, lambda i: (0, 0, 0)),
            pl.BlockSpec(bias_full.shape, lambda i: (0, 0)),
        ],
        out_specs=pl.BlockSpec((rows, BRANCH_WIDTH), lambda i: (i, 0)),
        compiler_params=_cparams("arbitrary"),
        name="branch_b",
    )(x, mod, w_b, ws, bias_full)


def _mlstm_kernel(*refs, seq, has_init, proj_rows):
    if has_init:
        (x_ref, mod_ref, w_ref, gb_ref, gn_ref, tril_ref, triu_ref, c0_ref, n0_ref, m0_ref,
         y_ref, q_s, k_s, v_s, o_s, g_s, hf_s, hb_s, c_s, n_s, m_s) = refs
    else:
        (x_ref, mod_ref, w_ref, gb_ref, gn_ref, tril_ref, triu_ref,
         y_ref, cout_ref, nout_ref, mout_ref, q_s, k_s, v_s, o_s, g_s, hf_s, hb_s, c_s, n_s, m_s) = refs
    n_chunk = seq // CHUNK
    for r0 in range(0, seq, proj_rows):
        rows = pl.ds(r0, proj_rows)
        h = _modulated(x_ref[rows, :], mod_ref, 0).astype(BF16)
        p = _dot(h, w_ref[...])
        q_s[rows, :] = p[:, 0:512].astype(BF16)
        k_s[rows, :] = (p[:, 512:1024] * (C_HEAD_DIM ** -0.5)).astype(BF16)
        v_s[rows, :] = p[:, 1024:1536].astype(BF16)
        o_s[rows, :] = p[:, 1536:2048]
        g_s[rows, :] = p[:, 2048:2176] + gb_ref[...]
    if has_init:
        c_s[...] = c0_ref[...]
        n_s[...] = n0_ref[...]
        m_s[...] = m0_ref[...]
    else:
        c_s[...] = jnp.zeros_like(c_s)
        n_s[...] = jnp.zeros_like(n_s)
        m_s[...] = jnp.zeros_like(m_s)

    tril = tril_ref[...]
    triu = triu_ref[...]
    row_i = lax.broadcasted_iota(jnp.int32, (CHUNK, CHUNK), 0)
    col_i = lax.broadcasted_iota(jnp.int32, (CHUNK, CHUNK), 1)
    masks = (col_i <= row_i, col_i >= row_i)

    def chunk_step(c, carry):
        for direction in range(2):
            cc = c if direction == 0 else n_chunk - 1 - c
            rows = pl.ds(pl.multiple_of(cc * CHUNK, CHUNK), CHUNK)
            gates = g_s[rows, :]
            logf = jnp.minimum(gates, 0.0) - jnp.log1p(jnp.exp(-jnp.abs(gates)))
            tri_col = tril if direction == 0 else triu
            tri_row = triu if direction == 0 else tril
            b_col_all = jnp.dot(tri_col, logf, precision=HIGHEST, preferred_element_type=F32)
            gates_t = gates.T
            b_row_all = jnp.dot(logf.T, tri_row, precision=HIGHEST, preferred_element_type=F32)
            last = CHUNK - 1 if direction == 0 else 0
            h_out = hf_s if direction == 0 else hb_s
            for hd in range(C_HEADS):
                ic = direction * 8 + hd
                fc = direction * 8 + 4 + hd
                sidx = direction * 4 + hd
                cols = slice(hd * 128, (hd + 1) * 128)
                b_col = b_col_all[:, fc:fc + 1]
                b_row = b_row_all[fc:fc + 1, :]
                i_row = gates_t[ic:ic + 1, :]
                i_col = gates[:, ic:ic + 1]
                m_prev = m_s[sidx:sidx + 1, 0:1]
                qh = q_s[rows, cols]
                kh = k_s[rows, cols]
                vh = v_s[rows, cols]
                dlog = jnp.where(masks[direction], b_col - b_row + i_row, NEG_INF)
                m_t = jnp.maximum(jnp.max(dlog, axis=-1, keepdims=True), b_col + m_prev)
                w = _dot_nt(qh, kh) * jnp.exp(dlog - m_t)
                inter = jnp.exp(b_col + m_prev - m_t)
                c_prev = c_s[sidx]
                n_prev = n_s[sidx:sidx + 1, :]
                num = _dot(w.astype(BF16), vh) + inter * _dot(qh, c_prev.astype(BF16))
                den = (jnp.sum(w, axis=-1, keepdims=True)
                       + inter * jnp.sum(qh.astype(F32) * n_prev, axis=-1, keepdims=True))
                h_out[rows, cols] = num / jnp.maximum(jnp.abs(den), jnp.exp(-m_t))
                b_last = b_col[last:last + 1, :]
                g_col = b_last - b_col + i_col
                m_new = jnp.maximum(b_last + m_prev, jnp.max(g_col, axis=0, keepdims=True))
                decay = jnp.exp(b_last + m_prev - m_new)
                ksc = kh.astype(F32) * jnp.exp(g_col - m_new)
                c_s[sidx] = decay * c_prev + _dot(ksc.T.astype(BF16), vh)
                n_s[sidx:sidx + 1, :] = decay * n_prev + jnp.sum(ksc, axis=0, keepdims=True)
                m_s[sidx:sidx + 1, :] = jnp.broadcast_to(m_new, (1, 128))
        return carry

    lax.fori_loop(0, n_chunk, chunk_step, 0)

    for r0 in range(0, seq, proj_rows):
        rows = pl.ds(r0, proj_rows)
        hsum = hf_s[rows, :] + hb_s[rows, :]
        gate = jax.nn.sigmoid(o_s[rows, :])
        for hd in range(C_HEADS):
            cols = slice(hd * 128, (hd + 1) * 128)
            hh = hsum[:, cols]
            mu = jnp.mean(hh, axis=-1, keepdims=True)
            hc = hh - mu
            var = jnp.mean(hc * hc, axis=-1, keepdims=True)
            y_ref[rows, cols] = (hc * lax.rsqrt(var + EPS) * gn_ref[:, cols] * gate[:, cols]).astype(BF16)
    if not has_init:
        cout_ref[...] = c_s[...]
        nout_ref[...] = n_s[...]
        mout_ref[...] = m_s[...]


def _mlstm(x, mod, w_c, gate_bias_row, gn_row, tril, triu, *, n_seq, seq, init=None, layer=0):
    proj_rows = min(seq, 512)
    kern = functools.partial(_mlstm_kernel, seq=seq, has_init=init is not None, proj_rows=proj_rows)
    const = lambda s: (0, 0)
    in_specs = [
        pl.BlockSpec((seq, D_MODEL), lambda s: (s, 0)),
        pl.BlockSpec((None, 1, 6 * D_MODEL), lambda s: (s if mod.shape[0] > 1 else 0, 0, 0)),
        pl.BlockSpec(w_c.shape, const),
        pl.BlockSpec(gate_bias_row.shape, const),
        pl.BlockSpec(gn_row.shape, const),
        pl.BlockSpec(tril.shape, const),
        pl.BlockSpec(triu.shape, const),
    ]
    args = [x, mod, w_c, gate_bias_row, gn_row, tril, triu]
    n_tok = n_seq * seq
    y_shape = jax.ShapeDtypeStruct((n_tok, BRANCH_WIDTH), BF16)
    y_spec = pl.BlockSpec((seq, BRANCH_WIDTH), lambda s: (s, 0))
    if init is not None:
        c0, n0, m0 = init
        in_specs += [pl.BlockSpec((None, None, 8, 128, 128), lambda s: (s, layer, 0, 0, 0)),
                     pl.BlockSpec((None, None, 8, 128), lambda s: (s, layer, 0, 0)),
                     pl.BlockSpec((None, None, 8, 128), lambda s: (s, layer, 0, 0))]
        args += [c0, n0, m0]
        out_shape, out_specs = y_shape, y_spec
    else:
        out_shape = (y_shape, jax.ShapeDtypeStruct((n_seq, 8, 128, 128), F32),
                     jax.ShapeDtypeStruct((n_seq, 8, 128), F32), jax.ShapeDtypeStruct((n_seq, 8, 128), F32))
        out_specs = (y_spec, pl.BlockSpec((None, 8, 128, 128), lambda s: (s, 0, 0, 0)),
                     pl.BlockSpec((None, 8, 128), lambda s: (s, 0, 0)),
                     pl.BlockSpec((None, 8, 128), lambda s: (s, 0, 0)))
    return pl.pallas_call(
        kern, out_shape=out_shape, grid=(n_seq,), in_specs=in_specs, out_specs=out_specs,
        scratch_shapes=[pltpu.VMEM((seq, 512), BF16), pltpu.VMEM((seq, 512), BF16), pltpu.VMEM((seq, 512), BF16),
                        pltpu.VMEM((seq, 512), F32), pltpu.VMEM((seq, 128), F32),
                        pltpu.VMEM((seq, 512), F32), pltpu.VMEM((seq, 512), F32),
                        pltpu.VMEM((8, 128, 128), F32), pltpu.VMEM((8, 128), F32), pltpu.VMEM((8, 128), F32)],
        compiler_params=_cparams("arbitrary"),
        name="branch_c_lat" if init is not None else "branch_c_ctx",
    )(*args)


def _merge_kernel(x_ref, mod_ref, ya_ref, yb_ref, yc_ref, yd_ref, wg_ref, wbr_ref, wout_ref, lng_ref, lnb_ref,
                  o_ref):
    x = x_ref[...]
    h = _modulated(x, mod_ref, 0).astype(BF16)
    mix = None
    for n, y_ref in enumerate((ya_ref, yb_ref, yc_ref, yd_ref)):
        gate = jax.nn.sigmoid(_dot(h, wg_ref[:, n * D_MODEL:(n + 1) * D_MODEL]))
        term = gate * _dot(y_ref[...], wbr_ref[n])
        mix = term if mix is None else mix + term
    out = _dot(mix.astype(BF16), wout_ref[...])
    g1 = mod_ref[:, 2 * D_MODEL:3 * D_MODEL]
    o_ref[...] = _layer_norm_rows(ALPHA * x + g1 * out, lng_ref[...], lnb_ref[...])


def _merge(x, mod, ys, w_g, w_br, w_out, ln_g, ln_b, *, seq, rows):
    n_tok = x.shape[0]
    per_seq = seq // rows
    tok = lambda i: (i, 0)
    c2 = lambda i: (0, 0)
    return pl.pallas_call(
        _merge_kernel, out_shape=jax.ShapeDtypeStruct((n_tok, D_MODEL), F32),
        grid=(n_tok // rows,),
        in_specs=[
            pl.BlockSpec((rows, D_MODEL), tok),
            pl.BlockSpec((None, 1, 6 * D_MODEL), lambda i: (i // per_seq if mod.shape[0] > 1 else 0, 0, 0)),
            pl.BlockSpec((rows, BRANCH_WIDTH), tok), pl.BlockSpec((rows, BRANCH_WIDTH), tok),
            pl.BlockSpec((rows, BRANCH_WIDTH), tok), pl.BlockSpec((rows, BRANCH_WIDTH), tok),
            pl.BlockSpec(w_g.shape, c2), pl.BlockSpec(w_br.shape, lambda i: (0, 0, 0)),
            pl.BlockSpec(w_out.shape, c2), pl.BlockSpec(ln_g.shape, c2), pl.BlockSpec(ln_b.shape, c2),
        ],
        out_specs=pl.BlockSpec((rows, D_MODEL), tok),
        compiler_params=_cparams("arbitrary"),
        name="merge",
    )(x, mod, *ys, w_g, w_br, w_out, ln_g, ln_b)


def _top16(s):
    rank = jnp.full(s.shape, float(PEER_TOPK + 1), F32)
    cur = s
    vals = []
    for r in range(PEER_TOPK):
        mx = jnp.max(cur, axis=0, keepdims=True)
        hit = cur == mx
        rank = jnp.where(hit, float(r + 1), rank)
        cur = jnp.where(hit, NEG_INF, cur)
        vals.append(mx)
    return jnp.concatenate(vals, axis=0), rank


_PAIR_LIMIT = tuple(PEER_TOPK // k for k in range(1, PEER_TOPK + 1))


def _route_kernel(x_ref, mod_ref, wq_ref, keys_ref, cnt_ref, e1_ref, r2_ref, e2_ref):
    h2 = _modulated(x_ref[...], mod_ref, 1).astype(BF16)
    q = _dot(h2, wq_ref[...])
    s1 = _dot_nt(keys_ref[0].astype(BF16), q[:, :128].astype(BF16))
    s2 = _dot_nt(keys_ref[1].astype(BF16), q[:, 128:].astype(BF16))
    a, rank1 = _top16(s1)
    b, rank2 = _top16(s2)
    ea = jnp.exp(a - a[0:1])
    eb = jnp.exp(b - b[0:1])
    sub = lax.broadcasted_iota(jnp.int32, (PEER_TOPK, 1), 0)
    cands, gates = [], []
    for k1 in range(8):
        keep = sub < _PAIR_LIMIT[k1]
        cands.append(jnp.where(keep, a[k1:k1 + 1] + b, NEG_INF))
        gates.append(ea[k1:k1 + 1] * eb)
    cands.append(a[8:16] + b[0:1])
    gates.append(ea[8:16] * eb[0:1])
    cand = jnp.concatenate(cands, axis=0)
    cur = cand
    thr = None
    for _ in range(PEER_TOPK):
        thr = jnp.max(cur, axis=0, keepdims=True)
        cur = jnp.where(cur == thr, NEG_INF, cur)
    counts, z = [], None
    for k1 in range(9):
        chosen = cands[k1] >= thr
        part = jnp.sum(jnp.where(chosen, gates[k1], 0.0), axis=0, keepdims=True)
        z = part if z is None else z + part
        ones = jnp.where(chosen, 1.0, 0.0)
        counts.append(jnp.sum(ones, axis=0, keepdims=True) if k1 < 8 else ones)
    cnt_sorted = jnp.concatenate(counts, axis=0)
    cnt = jnp.zeros_like(s1)
    for r in range(PEER_TOPK):
        cnt = jnp.where(rank1 == float(r + 1), cnt_sorted[r:r + 1], cnt)
    cnt_ref[...] = cnt
    e1_ref[...] = jnp.where(rank1 <= float(PEER_TOPK), jnp.exp(s1 - a[0:1]) * (1.0 / z), 0.0)
    r2_ref[...] = rank2
    e2_ref[...] = jnp.where(rank2 <= float(PEER_TOPK), jnp.exp(s2 - b[0:1]), 0.0)


def _route(x1, mod, wq, keys, *, seq, cols):
    n_tok = x1.shape[0]
    per_seq = seq // cols
    shp = jax.ShapeDtypeStruct((PEER_HEADS, PEER_NKEYS, n_tok), F32)
    spec = pl.BlockSpec((None, PEER_NKEYS, cols), lambda i, h: (h, 0, i))
    return pl.pallas_call(
        _route_kernel, out_shape=(shp, shp, shp, shp),
        grid=(n_tok // cols, PEER_HEADS),
        in_specs=[
            pl.BlockSpec((cols, D_MODEL), lambda i, h: (i, 0)),
            pl.BlockSpec((None, 1, 6 * D_MODEL), lambda i, h: (i // per_seq if mod.shape[0] > 1 else 0, 0, 0)),
            pl.BlockSpec((D_MODEL, PEER_QDIM), lambda i, h: (0, h)),
            pl.BlockSpec((None, 2, PEER_NKEYS, PEER_QDIM // 2), lambda i, h: (h, 0, 0, 0)),
        ],
        out_specs=(spec, spec, spec, spec),
        compiler_params=_cparams("arbitrary", "arbitrary"),
        name="peer_route",
    )(x1, mod, wq, keys)


def _peer_kernel(x_ref, mod_ref, u_ref, vt_ref, cnt_ref, e1_ref, r2_ref, e2_ref, lng_ref, lnb_ref,
                 o_ref, h2t_s, acc_s, *, key_rows):
    e = pl.program_id(1)

    @pl.when(e == 0)
    def _init():
        h2 = _modulated(x_ref[...], mod_ref, 1)
        h2t_s[...] = h2.T.astype(BF16)
        acc_s[...] = jnp.zeros_like(acc_s)

    act = _dot(u_ref[...], h2t_s[...])
    act = 0.5 * act * (1.0 + lax.erf(act * (2.0 ** -0.5)))
    pieces = []
    for r in range(key_rows):
        i = e * key_rows + r
        g = None
        for hd in range(PEER_HEADS):
            cnt_row = cnt_ref[hd, pl.ds(i, 1), :]
            e1_row = e1_ref[hd, pl.ds(i, 1), :]
            term = jnp.where(r2_ref[hd] <= cnt_row, e2_ref[hd], 0.0) * e1_row
            g = term if g is None else g + term
        pieces.append((g * act[r * PEER_NKEYS:(r + 1) * PEER_NKEYS, :]).astype(BF16))
    ga = jnp.concatenate(pieces, axis=0)
    acc_s[...] += _dot(vt_ref[...], ga)

    @pl.when(e == pl.num_programs(1) - 1)
    def _finish():
        x = x_ref[...]
        g2 = mod_ref[:, 5 * D_MODEL:6 * D_MODEL]
        o_ref[...] = _layer_norm_rows(ALPHA * x + g2 * acc_s[...].T, lng_ref[...], lnb_ref[...])


def _peer(x1, mod, u, vt, route, ln_g, ln_b, *, seq, cols, key_rows):
    n_tok = x1.shape[0]
    per_seq = seq // cols
    n_exp = key_rows * PEER_NKEYS
    kern = functools.partial(_peer_kernel, key_rows=key_rows)
    rspec = pl.BlockSpec((PEER_HEADS, PEER_NKEYS, cols), lambda i, e: (0, 0, i))
    return pl.pallas_call(
        kern, out_shape=jax.ShapeDtypeStruct((n_tok, D_MODEL), F32),
        grid=(n_tok // cols, PEER_EXPERTS // n_exp),
        in_specs=[
            pl.BlockSpec((cols, D_MODEL), lambda i, e: (i, 0)),
            pl.BlockSpec((None, 1, 6 * D_MODEL), lambda i, e: (i // per_seq if mod.shape[0] > 1 else 0, 0, 0)),
            pl.BlockSpec((n_exp, D_MODEL), lambda i, e: (e, 0)),
            pl.BlockSpec((D_MODEL, n_exp), lambda i, e: (0, e)),
            rspec, rspec, rspec, rspec,
            pl.BlockSpec(ln_g.shape, lambda i, e: (0, 0)), pl.BlockSpec(ln_b.shape, lambda i, e: (0, 0)),
        ],
        out_specs=pl.BlockSpec((cols, D_MODEL), lambda i, e: (i, 0)),
        scratch_shapes=[pltpu.VMEM((D_MODEL, cols), BF16), pltpu.VMEM((D_MODEL, cols), F32)],
        compiler_params=_cparams("arbitrary", "arbitrary"),
        name="peer_experts",
    )(x1, mod, u, vt, *route, ln_g, ln_b)


def _rope_tables(seq):
    t = np.arange(seq)
    pos = np.stack([t // GRID_W, t % GRID_W], axis=1).astype(np.float64)
    inv = ROPE_BASE ** (-np.arange(16, dtype=np.float64) / 16)
    lane = np.arange(64)
    ang = pos[:, lane // 32] * inv[lane % 16][None, :]
    sign = np.where((lane % 32) < 16, -1.0, 1.0)[None, :]
    cos = np.tile(np.cos(ang), (1, 2)).astype(np.float32)
    sin = np.tile(np.sin(ang) * sign, (1, 2)).astype(np.float32)
    return jnp.asarray(cos), jnp.asarray(sin)


def _static_tables():
    lane = np.arange(640)
    bd = (lane[:, None] // 64 == lane[None, :] // 64).astype(np.float32) / 64.0
    src = np.arange(128)
    dst = np.arange(512)
    rep = ((src[:, None] // 64 == dst[None, :] // 256) & (src[:, None] % 64 == dst[None, :] % 64))
    idx = np.arange(CHUNK)
    tril = (idx[None, :] <= idx[:, None]).astype(np.float32)
    triu = (idx[None, :] >= idx[:, None]).astype(np.float32)
    return (jnp.asarray(bd, BF16), jnp.asarray(rep.astype(np.float32), BF16), jnp.asarray(tril), jnp.asarray(triu))


def _layer_params(l, w_in, attn_qk_gain, gmlp_ws, gmlp_b, mlstm_gate_bias, mlstm_gn, diff_lambda, diff_gn,
                  w_branch, w_out, ln_g, ln_b, peer_wq, peer_keys, peer_u, peer_v):
    w = w_in[l]
    p = {}
    p["w_a"] = w[:, _OFF_A:_OFF_B].astype(BF16)
    p["w_b"] = w[:, _OFF_B:_OFF_C].astype(BF16)
    p["w_c"] = jnp.concatenate([w[:, _OFF_C:_OFF_D], jnp.zeros((D_MODEL, 112), F32)], axis=1).astype(BF16)
    p["w_d"] = w[:, _OFF_D:_OFF_G].astype(BF16)
    p["w_g"] = w[:, _OFF_G:].astype(BF16)
    gain = attn_qk_gain[l]
    p["gain_row"] = jnp.concatenate([jnp.tile(gain[0], A_HEADS), jnp.tile(gain[1], A_KV_HEADS)])[None, :]
    p["ws"] = gmlp_ws[l]
    p["bias_full"] = jnp.repeat(gmlp_b[l].T, 128, axis=1)
    p["gate_bias_row"] = jnp.concatenate([mlstm_gate_bias[l].reshape(16), jnp.zeros((112,), F32)])[None, :]
    p["mlstm_gn_row"] = mlstm_gn[l].reshape(1, BRANCH_WIDTH)
    p["lam"] = diff_lambda[l]
    p["diff_gn_row"] = diff_gn[l].reshape(1, BRANCH_WIDTH)
    p["w_br"] = w_branch[l].astype(BF16)
    p["w_out"] = w_out[l].astype(BF16)
    p["ln_g0"], p["ln_b0"] = ln_g[l, 0][None, :], ln_b[l, 0][None, :]
    p["ln_g1"], p["ln_b1"] = ln_g[l, 1][None, :], ln_b[l, 1][None, :]
    p["wq"] = peer_wq[l].astype(BF16)
    p["keys"] = peer_keys[l]
    p["u"] = peer_u[l].astype(BF16)
    p["vt"] = peer_v[l].T.astype(BF16)
    return p


def _trunk_layer(x, mod, p, tabs, *, l, n_seq, seq, cfg, ctx_cache=None):
    bd, rep, tril, triu = tabs
    lam_init = 0.8 - 0.6 * math.exp(-0.3 * l)
    state = None
    if ctx_cache is None:
        ya, nk, nv = _attn_a(x, mod, p["w_a"], p["gain_row"], bd, rep, n_seq=n_seq, seq=seq, tq=cfg["tq"])
        yd, ndk, ndv = _attn_d(x, mod, p["w_d"], p["lam"], p["diff_gn_row"], n_seq=n_seq, seq=seq, tq=cfg["tq"],
                               lam_init=lam_init)
        yc, c_new, n_new, m_new = _mlstm(x, mod, p["w_c"], p["gate_bias_row"], p["mlstm_gn_row"], tril, triu,
                                         n_seq=n_seq, seq=seq)
        state = (nk, nv, ndk, ndv, c_new, n_new, m_new)
    else:
        rope, cak, cav, cdk, cdv, c0, n0, m0 = ctx_cache
        ya = _attn_a(x, mod, p["w_a"], p["gain_row"], bd, rep, n_seq=n_seq, seq=seq, tq=cfg["tq"],
                     rope=rope, cache=(cak, cav), layer=l)
        yd = _attn_d(x, mod, p["w_d"], p["lam"], p["diff_gn_row"], n_seq=n_seq, seq=seq, tq=cfg["tq"],
                     lam_init=lam_init, rope=rope, cache=(cdk, cdv), layer=l)
        yc = _mlstm(x, mod, p["w_c"], p["gate_bias_row"], p["mlstm_gn_row"], tril, triu,
                    n_seq=n_seq, seq=seq, init=(c0, n0, m0), layer=l)
    yb = _gmlp(x, mod, p["w_b"], p["ws"], p["bias_full"], seq=seq, rows=cfg["rows"])
    x1 = _merge(x, mod, (ya, yb, yc, yd), p["w_g"], p["w_br"], p["w_out"], p["ln_g0"], p["ln_b0"],
                seq=seq, rows=cfg["rows"])
    route = _route(x1, mod, p["wq"], p["keys"], seq=seq, cols=cfg["cols"])
    x2 = _peer(x1, mod, p["u"], p["vt"], route, p["ln_g1"], p["ln_b1"], seq=seq, cols=cfg["cols"],
               key_rows=cfg["key_rows"])
    return x2, state


def kernel(x_prompt, x_sample, cache_a_k, cache_a_v, cache_d_k, cache_d_v, state_c_C, state_c_n, state_c_m,
           c, c_ctx, w_mod, b_mod, w_in, attn_qk_gain, gmlp_ws, gmlp_b, mlstm_gate_bias, mlstm_gn,
           diff_lambda, diff_gn, w_branch, w_out, ln_g, ln_b, peer_wq, peer_keys, peer_u, peer_v):
    batch, seq, _ = x_prompt.shape
    dec_batch, dec_seq, _ = x_sample.shape
    past = cache_a_k.shape[2]
    c_rows = jnp.concatenate([c_ctx[None, :], c, jnp.zeros((8 - 1 - dec_batch, D_MODEL), F32)], axis=0)
    mods = _modulation(c_rows, w_mod, b_mod)
    tabs = _static_tables()
    rope = _rope_tables(dec_seq)
    cak = cache_a_k.reshape(dec_batch, DEPTH, past, 128)
    cav = cache_a_v.reshape(dec_batch, DEPTH, past, 128)
    cdk = cache_d_k.reshape(dec_batch, DEPTH, past, 512)
    cdv = cache_d_v.reshape(dec_batch, DEPTH, past, 512)
    c0 = state_c_C.reshape(dec_batch, DEPTH, 8, 128, 128)
    n0 = state_c_n.reshape(dec_batch, DEPTH, 8, 128)
    m0 = jnp.broadcast_to(state_c_m.reshape(dec_batch, DEPTH, 8, 1), (dec_batch, DEPTH, 8, 128))
    cfg_ctx = dict(tq=seq, rows=min(512, seq), cols=512, key_rows=8)
    cfg_lat = dict(tq=min(256, dec_seq), rows=min(512, dec_seq), cols=512, key_rows=8)
    y_p = x_prompt.reshape(batch * seq, D_MODEL)
    y_s = x_sample.reshape(dec_batch * dec_seq, D_MODEL)
    states = []
    for l in range(DEPTH):
        p = _layer_params(l, w_in, attn_qk_gain, gmlp_ws, gmlp_b, mlstm_gate_bias, mlstm_gn, diff_lambda, diff_gn,
                          w_branch, w_out, ln_g, ln_b, peer_wq, peer_keys, peer_u, peer_v)
        mod_ctx = mods[l, 0:1].reshape(1, 1, 6 * D_MODEL)
        mod_lat = mods[l, 1:1 + dec_batch].reshape(dec_batch, 1, 6 * D_MODEL)
        y_p, st = _trunk_layer(y_p, mod_ctx, p, tabs, l=l, n_seq=batch, seq=seq, cfg=cfg_ctx)
        states.append(st)
        y_s, _ = _trunk_layer(y_s, mod_lat, p, tabs, l=l, n_seq=dec_batch, seq=dec_seq, cfg=cfg_lat,
                              ctx_cache=(rope, cak, cav, cdk, cdv, c0, n0, m0))
    nk = jnp.stack([s[0].reshape(batch, seq, A_KV_HEADS, A_HEAD_DIM) for s in states], axis=1)
    nv = jnp.stack([s[1].reshape(batch, seq, A_KV_HEADS, A_HEAD_DIM) for s in states], axis=1)
    ndk = jnp.stack([s[2].reshape(batch, seq, D_HEADS, 2, D_HALF_DIM) for s in states], axis=1)
    ndv = jnp.stack([s[3].reshape(batch, seq, D_HEADS, D_VDIM) for s in states], axis=1)
    nc = jnp.stack([s[4].reshape(batch, 2, C_HEADS, C_HEAD_DIM, C_HEAD_DIM) for s in states], axis=1)
    nn = jnp.stack([s[5].reshape(batch, 2, C_HEADS, C_HEAD_DIM) for s in states], axis=1)
    nm = jnp.stack([s[6][:, :, 0].reshape(batch, 2, C_HEADS) for s in states], axis=1)
    return (y_p.reshape(batch, seq, D_MODEL), y_s.reshape(dec_batch, dec_seq, D_MODEL), nk, nv, ndk, ndv, nc, nn, nm)
```

```python
import functools
import math

import numpy as np
import jax
import jax.numpy as jnp
from jax import lax
from jax.experimental import pallas as pl
from jax.experimental.pallas import tpu as pltpu

F32 = jnp.float32
BF16 = jnp.bfloat16
HIGHEST = lax.Precision.HIGHEST

D_MODEL = 1024
DEPTH = 4
GRID_W = 64
ROPE_BASE = 10000.0
EPS = 1e-6
BRANCH_WIDTH = D_MODEL // 2
A_HEAD_DIM = 64
A_HEADS = 8
A_KV_HEADS = 2
B_GROUPS = 4
CHUNK = 128
C_HEADS = 4
C_HEAD_DIM = 128
D_HEADS = 4
D_VDIM = 128
D_HALF_DIM = 64
PEER_HEADS = 8
PEER_NKEYS = 128
PEER_EXPERTS = PEER_NKEYS * PEER_NKEYS
PEER_QDIM = 256
PEER_TOPK = 16
ALPHA = (2 * DEPTH) ** 0.25

_OFF_A = 0
_OFF_B = 768
_OFF_C = 1792
_OFF_CG = 3840
_OFF_D = 3856
_OFF_G = 5392
_C_WIDTH = 4 * BRANCH_WIDTH + 128

VMEM_LIMIT_BYTES = 56 * 1024 * 1024
NEG_INF = float("-inf")


def _cparams(*sem, flags=None):
    return pltpu.CompilerParams(dimension_semantics=sem, vmem_limit_bytes=VMEM_LIMIT_BYTES, flags=flags)


def _dot(a, b):
    return jnp.dot(a, b, preferred_element_type=F32)


def _dot_nt(a, b):
    return lax.dot_general(a, b, (((1,), (1,)), ((), ())), preferred_element_type=F32)


def _modulated(x, mod_ref, which):
    base = 3 * D_MODEL * which
    sh = mod_ref[:, base:base + D_MODEL]
    sc = mod_ref[:, base + D_MODEL:base + 2 * D_MODEL]
    return x * (1.0 + sc) + sh


def _layer_norm_rows(z, g, b):
    mu = jnp.mean(z, axis=-1, keepdims=True)
    zc = z - mu
    var = jnp.mean(zc * zc, axis=-1, keepdims=True)
    return zc * lax.rsqrt(var + EPS) * g + b


def _rope(x, cos, sin_signed, lane):
    w = x.shape[1]
    nxt = pltpu.roll(x, w - 16, 1)
    prv = pltpu.roll(x, 16, 1)
    partner = jnp.where((lane % 32) < 16, nxt, prv)
    return x * cos + partner * sin_signed


def _tile_lanes(t, n):
    return t if n == 1 else jnp.concatenate([t] * n, axis=1)


def _mod_kernel(c_ref, w_ref, b_ref, o_ref):
    c = c_ref[...]
    s = c * jax.nn.sigmoid(c)
    o_ref[...] = jnp.dot(s, w_ref[...], precision=HIGHEST, preferred_element_type=F32) + b_ref[...]


def _modulation(c_rows, w_mod, b_mod):
    n_col = 6 * D_MODEL // 1024
    return pl.pallas_call(
        _mod_kernel,
        out_shape=jax.ShapeDtypeStruct((DEPTH, 8, 6 * D_MODEL), F32),
        grid=(DEPTH, n_col),
        in_specs=[
            pl.BlockSpec((8, D_MODEL), lambda l, j: (0, 0)),
            pl.BlockSpec((None, D_MODEL, 1024), lambda l, j: (l, 0, j)),
            pl.BlockSpec((None, 1, 1024), lambda l, j: (l, 0, j)),
        ],
        out_specs=pl.BlockSpec((None, 8, 1024), lambda l, j: (l, 0, j)),
        compiler_params=_cparams("arbitrary", "arbitrary"),
        name="modulation",
    )(c_rows, w_mod, b_mod.reshape(DEPTH, 1, 6 * D_MODEL))


def _attn_a_kernel(*refs, seq, tq, n_cache, rope, proj_rows):
    if rope:
        (x_ref, mod_ref, w_ref, gain_ref, bd_ref, rep_ref, cos_ref, sin_ref, ck_ref, cv_ref,
         y_ref, q_s, k_s, v_s) = refs
    else:
        (x_ref, mod_ref, w_ref, gain_ref, bd_ref, rep_ref,
         y_ref, nk_ref, nv_ref, q_s, k_s, v_s) = refs
    qi = pl.program_id(1)

    @pl.when(qi == 0)
    def _project():
        lane = lax.broadcasted_iota(jnp.int32, (1, 640), 1)
        for r0 in range(0, seq, proj_rows):
            rows = pl.ds(r0, proj_rows)
            h = _modulated(x_ref[rows, :], mod_ref, 0).astype(BF16)
            p = _dot(h, w_ref[...])
            qk = p[:, :640]
            sq = qk * qk
            hi = sq.astype(BF16)
            lo = (sq - hi.astype(F32)).astype(BF16)
            ms = _dot(hi, bd_ref[...]) + _dot(lo, bd_ref[...])
            qk = qk * lax.rsqrt(ms + EPS) * gain_ref[...]
            v = p[:, 640:768]
            if not rope:
                nk_ref[rows, :] = qk[:, 512:640]
                nv_ref[rows, :] = v
            else:
                cos = _tile_lanes(cos_ref[rows, :], 5)
                sin = _tile_lanes(sin_ref[rows, :], 5)
                qk = _rope(qk, cos, sin, lane)
            q_s[rows, :] = (qk[:, :512] * (A_HEAD_DIM ** -0.5)).astype(BF16)
            k_s[rows, :] = _dot(qk[:, 512:640].astype(BF16), rep_ref[...]).astype(BF16)
            v_s[rows, :] = _dot(v.astype(BF16), rep_ref[...]).astype(BF16)
        if n_cache:
            crow = pl.ds(seq, n_cache)
            k_s[crow, :] = _dot(ck_ref[...].astype(BF16), rep_ref[...]).astype(BF16)
            v_s[crow, :] = _dot(cv_ref[...].astype(BF16), rep_ref[...]).astype(BF16)

    head_of_lane = lax.broadcasted_iota(jnp.int32, (1, 256), 1) // A_HEAD_DIM
    qb = q_s[pl.ds(pl.multiple_of(qi * tq, tq), tq), :]
    for g in range(A_KV_HEADS):
        cols = slice(g * 256, (g + 1) * 256)
        qg = qb[:, cols]
        kg = k_s[:, cols]
        vg = v_s[:, cols]
        acc = jnp.zeros((tq, 256), F32)
        for r in range(A_HEADS // A_KV_HEADS):
            sel = head_of_lane == r
            qm = jnp.where(sel, qg, jnp.zeros_like(qg))
            s = _dot_nt(qm, kg)
            m = jnp.max(s, axis=-1, keepdims=True)
            p = jnp.exp(s - m)
            l = jnp.sum(p, axis=-1, keepdims=True)
            o = _dot(p.astype(BF16), vg)
            acc = acc + jnp.where(sel, o * (1.0 / l), 0.0)
        y_ref[:, cols] = acc.astype(BF16)


def _attn_a(x, mod, w_a, gain_row, bd, rep, *, n_seq, seq, tq, rope=None, cache=None, layer=0):
    n_q = seq // tq
    n_cache = 0 if cache is None else cache[0].shape[2]
    proj_rows = min(seq, 512)
    kern = functools.partial(_attn_a_kernel, seq=seq, tq=tq, n_cache=n_cache, rope=rope is not None,
                             proj_rows=proj_rows)
    const = lambda s, q: (0, 0)
    in_specs = [
        pl.BlockSpec((seq, D_MODEL), lambda s, q: (s, 0)),
        pl.BlockSpec((None, 1, 6 * D_MODEL), lambda s, q: (s if mod.shape[0] > 1 else 0, 0, 0)),
        pl.BlockSpec(w_a.shape, const),
        pl.BlockSpec(gain_row.shape, const),
        pl.BlockSpec(bd.shape, const),
        pl.BlockSpec(rep.shape, const),
    ]
    args = [x, mod, w_a, gain_row, bd, rep]
    n_tok = n_seq * seq
    y_spec = pl.BlockSpec((tq, BRANCH_WIDTH), lambda s, q: (s * n_q + q, 0))
    y_shape = jax.ShapeDtypeStruct((n_tok, BRANCH_WIDTH), BF16)
    if rope is not None:
        cos, sin = rope
        in_specs += [pl.BlockSpec(cos.shape, const), pl.BlockSpec(sin.shape, const),
                     pl.BlockSpec((None, None, n_cache, 128), lambda s, q: (s, layer, 0, 0)),
                     pl.BlockSpec((None, None, n_cache, 128), lambda s, q: (s, layer, 0, 0))]
        args += [cos, sin, cache[0], cache[1]]
        out_shape, out_specs = y_shape, y_spec
    else:
        kv_shape = jax.ShapeDtypeStruct((n_tok, 128), F32)
        kv_spec = pl.BlockSpec((seq, 128), lambda s, q: (s, 0))
        out_shape, out_specs = (y_shape, kv_shape, kv_shape), (y_spec, kv_spec, kv_spec)
    return pl.pallas_call(
        kern, out_shape=out_shape, grid=(n_seq, n_q), in_specs=in_specs, out_specs=out_specs,
        scratch_shapes=[pltpu.VMEM((seq, 512), BF16), pltpu.VMEM((seq + n_cache, 512), BF16),
                        pltpu.VMEM((seq + n_cache, 512), BF16)],
        compiler_params=_cparams("arbitrary", "arbitrary"),
        name="branch_a_lat" if rope is not None else "branch_a_ctx",
    )(*args)


def _attn_d_kernel(*refs, seq, tq, n_cache, rope, proj_rows, lam_init):
    if rope:
        (x_ref, mod_ref, w_ref, lam_ref, gn_ref, cos_ref, sin_ref, ck_ref, cv_ref,
         y_ref, q_s, k_s, v_s) = refs
    else:
        (x_ref, mod_ref, w_ref, lam_ref, gn_ref,
         y_ref, nk_ref, nv_ref, q_s, k_s, v_s) = refs
    qi = pl.program_id(1)

    @pl.when(qi == 0)
    def _project():
        lane = lax.broadcasted_iota(jnp.int32, (1, 512), 1)
        for r0 in range(0, seq, proj_rows):
            rows = pl.ds(r0, proj_rows)
            h = _modulated(x_ref[rows, :], mod_ref, 0).astype(BF16)
            p = _dot(h, w_ref[...])
            dq, dk, dv = p[:, :512], p[:, 512:1024], p[:, 1024:1536]
            if not rope:
                nk_ref[rows, :] = dk
                nv_ref[rows, :] = dv
            else:
                cos = _tile_lanes(cos_ref[rows, :], 4)
                sin = _tile_lanes(sin_ref[rows, :], 4)
                dq = _rope(dq, cos, sin, lane)
                dk = _rope(dk, cos, sin, lane)
            q_s[rows, :] = (dq * (D_HALF_DIM ** -0.5)).astype(BF16)
            k_s[rows, :] = dk.astype(BF16)
            v_s[rows, :] = dv.astype(BF16)
        if n_cache:
            crow = pl.ds(seq, n_cache)
            k_s[crow, :] = ck_ref[...].astype(BF16)
            v_s[crow, :] = cv_ref[...].astype(BF16)

    lv = lam_ref[...]
    lam = (jnp.exp(jnp.sum(lv[0:1] * lv[1:2], axis=-1, keepdims=True))
           - jnp.exp(jnp.sum(lv[2:3] * lv[3:4], axis=-1, keepdims=True)) + lam_init)
    half_of_lane = lax.broadcasted_iota(jnp.int32, (1, 128), 1) // D_HALF_DIM
    qb = q_s[pl.ds(pl.multiple_of(qi * tq, tq), tq), :]
    for hd in range(D_HEADS):
        cols = slice(hd * 128, (hd + 1) * 128)
        qh = qb[:, cols]
        kh = k_s[:, cols]
        vh = v_s[:, cols]
        probs = []
        for j in range(2):
            qm = jnp.where(half_of_lane == j, qh, jnp.zeros_like(qh))
            s = _dot_nt(qm, kh)
            m = jnp.max(s, axis=-1, keepdims=True)
            p = jnp.exp(s - m)
            l = jnp.sum(p, axis=-1, keepdims=True)
            probs.append(p * (1.0 / l))
        a = (probs[0] - lam * probs[1]).astype(BF16)
        o = _dot(a, vh)
        ms = jnp.mean(o * o, axis=-1, keepdims=True)
        o = o * lax.rsqrt(ms + EPS) * gn_ref[:, cols] * (1.0 - lam_init)
        y_ref[:, cols] = o.astype(BF16)


def _attn_d(x, mod, w_d, lam_params, gn_row, *, n_seq, seq, tq, lam_init, rope=None, cache=None, layer=0):
    n_q = seq // tq
    n_cache = 0 if cache is None else cache[0].shape[2]
    proj_rows = min(seq, 512)
    kern = functools.partial(_attn_d_kernel, seq=seq, tq=tq, n_cache=n_cache, rope=rope is not None,
                             proj_rows=proj_rows, lam_init=lam_init)
    const = lambda s, q: (0, 0)
    in_specs = [
        pl.BlockSpec((seq, D_MODEL), lambda s, q: (s, 0)),
        pl.BlockSpec((None, 1, 6 * D_MODEL), lambda s, q: (s if mod.shape[0] > 1 else 0, 0, 0)),
        pl.BlockSpec(w_d.shape, const),
        pl.BlockSpec(lam_params.shape, const),
        pl.BlockSpec(gn_row.shape, const),
    ]
    args = [x, mod, w_d, lam_params, gn_row]
    n_tok = n_seq * seq
    y_spec = pl.BlockSpec((tq, BRANCH_WIDTH), lambda s, q: (s * n_q + q, 0))
    y_shape = jax.ShapeDtypeStruct((n_tok, BRANCH_WIDTH), BF16)
    if rope is not None:
        cos, sin = rope
        in_specs += [pl.BlockSpec(cos.shape, const), pl.BlockSpec(sin.shape, const),
                     pl.BlockSpec((None, None, n_cache, 512), lambda s, q: (s, layer, 0, 0)),
                     pl.BlockSpec((None, None, n_cache, 512), lambda s, q: (s, layer, 0, 0))]
        args += [cos, sin, cache[0], cache[1]]
        out_shape, out_specs = y_shape, y_spec
    else:
        kv_shape = jax.ShapeDtypeStruct((n_tok, 512), F32)
        kv_spec = pl.BlockSpec((seq, 512), lambda s, q: (s, 0))
        out_shape, out_specs = (y_shape, kv_shape, kv_shape), (y_spec, kv_spec, kv_spec)
    return pl.pallas_call(
        kern, out_shape=out_shape, grid=(n_seq, n_q), in_specs=in_specs, out_specs=out_specs,
        scratch_shapes=[pltpu.VMEM((seq, 512), BF16), pltpu.VMEM((seq + n_cache, 512), BF16),
                        pltpu.VMEM((seq + n_cache, 512), BF16)],
        compiler_params=_cparams("arbitrary", "arbitrary"),
        name="branch_d_lat" if rope is not None else "branch_d_ctx",
    )(*args)


def _gmlp_kernel(x_ref, mod_ref, w_ref, ws_ref, bias_ref, y_ref, *, rows):
    h = _modulated(x_ref[...], mod_ref, 0).astype(BF16)
    p = _dot(h, w_ref[...])
    u, v = p[:, :BRANCH_WIDTH], p[:, BRANCH_WIDTH:]
    mu = jnp.mean(v, axis=-1, keepdims=True)
    vc = v - mu
    var = jnp.mean(vc * vc, axis=-1, keepdims=True)
    vn = (vc * lax.rsqrt(var + EPS)).astype(BF16)
    for c in range(rows // CHUNK):
        rs = slice(c * CHUNK, (c + 1) * CHUNK)
        for g in range(B_GROUPS):
            cs = slice(g * 128, (g + 1) * 128)
            s = _dot(ws_ref[g].astype(BF16), vn[rs, cs]) + bias_ref[:, cs]
            y_ref[rs, cs] = (u[rs, cs] * s).astype(BF16)


def _gmlp(x, mod, w_b, ws, bias_full, *, seq, rows):
    n_tok = x.shape[0]
    per_seq = seq // rows
    kern = functools.partial(_gmlp_kernel, rows=rows)
    return pl.pallas_call(
        kern, out_shape=jax.ShapeDtypeStruct((n_tok, BRANCH_WIDTH), BF16),
        grid=(n_tok // rows,),
        in_specs=[
            pl.BlockSpec((rows, D_MODEL), lambda i: (i, 0)),
            pl.BlockSpec((None, 1, 6 * D_MODEL), lambda i: (i // per_seq if mod.shape[0] > 1 else 0, 0, 0)),
            pl.BlockSpec(w_b.shape, lambda i: (0, 0)),
            pl.BlockSpec(ws.shape, lambda i: (0, 0, 0)),
            pl.BlockSpec(bias_full.shape, lambda i: (0, 0)),
        ],
        out_specs=pl.BlockSpec((rows, BRANCH_WIDTH), lambda i: (i, 0)),
        compiler_params=_cparams("arbitrary"),
        name="branch_b",
    )(x, mod, w_b, ws, bias_full)


def _mlstm_kernel(*refs, seq, has_init, proj_rows):
    if has_init:
        (x_ref, mod_ref, w_ref, gb_ref, gn_ref, tril_ref, triu_ref, c0_ref, n0_ref, m0_ref,
         y_ref, q_s, k_s, v_s, o_s, g_s, hf_s, hb_s, c_s, n_s, m_s) = refs
    else:
        (x_ref, mod_ref, w_ref, gb_ref, gn_ref, tril_ref, triu_ref,
         y_ref, cout_ref, nout_ref, mout_ref, q_s, k_s, v_s, o_s, g_s, hf_s, hb_s, c_s, n_s, m_s) = refs
    n_chunk = seq // CHUNK
    for r0 in range(0, seq, proj_rows):
        rows = pl.ds(r0, proj_rows)
        h = _modulated(x_ref[rows, :], mod_ref, 0).astype(BF16)
        p = _dot(h, w_ref[...])
        q_s[rows, :] = p[:, 0:512].astype(BF16)
        k_s[rows, :] = (p[:, 512:1024] * (C_HEAD_DIM ** -0.5)).astype(BF16)
        v_s[rows, :] = p[:, 1024:1536].astype(BF16)
        o_s[rows, :] = p[:, 1536:2048]
        g_s[rows, :] = p[:, 2048:2176] + gb_ref[...]
    if has_init:
        c_s[...] = c0_ref[...]
        n_s[...] = n0_ref[...]
        m_s[...] = m0_ref[...]
    else:
        c_s[...] = jnp.zeros_like(c_s)
        n_s[...] = jnp.zeros_like(n_s)
        m_s[...] = jnp.zeros_like(m_s)

    tril = tril_ref[...]
    triu = triu_ref[...]
    row_i = lax.broadcasted_iota(jnp.int32, (CHUNK, CHUNK), 0)
    col_i = lax.broadcasted_iota(jnp.int32, (CHUNK, CHUNK), 1)
    masks = (col_i <= row_i, col_i >= row_i)

    def chunk_step(c, carry):
        for direction in range(2):
            cc = c if direction == 0 else n_chunk - 1 - c
            rows = pl.ds(pl.multiple_of(cc * CHUNK, CHUNK), CHUNK)
            gates = g_s[rows, :]
            logf = jnp.minimum(gates, 0.0) - jnp.log1p(jnp.exp(-jnp.abs(gates)))
            tri_col = tril if direction == 0 else triu
            tri_row = triu if direction == 0 else tril
            b_col_all = jnp.dot(tri_col, logf, precision=HIGHEST, preferred_element_type=F32)
            gates_t = gates.T
            b_row_all = jnp.dot(logf.T, tri_row, precision=HIGHEST, preferred_element_type=F32)
            last = CHUNK - 1 if direction == 0 else 0
            h_out = hf_s if direction == 0 else hb_s
            for hd in range(C_HEADS):
                ic = direction * 8 + hd
                fc = direction * 8 + 4 + hd
                sidx = direction * 4 + hd
                cols = slice(hd * 128, (hd + 1) * 128)
                b_col = b_col_all[:, fc:fc + 1]
                b_row = b_row_all[fc:fc + 1, :]
                i_row = gates_t[ic:ic + 1, :]
                i_col = gates[:, ic:ic + 1]
                m_prev = m_s[sidx:sidx + 1, 0:1]
                qh = q_s[rows, cols]
                kh = k_s[rows, cols]
                vh = v_s[rows, cols]
                dlog = jnp.where(masks[direction], b_col - b_row + i_row, NEG_INF)
                m_t = jnp.maximum(jnp.max(dlog, axis=-1, keepdims=True), b_col + m_prev)
                w = _dot_nt(qh, kh) * jnp.exp(dlog - m_t)
                inter = jnp.exp(b_col + m_prev - m_t)
                c_prev = c_s[sidx]
                n_prev = n_s[sidx:sidx + 1, :]
                num = _dot(w.astype(BF16), vh) + inter * _dot(qh, c_prev.astype(BF16))
                den = (jnp.sum(w, axis=-1, keepdims=True)
                       + inter * jnp.sum(qh.astype(F32) * n_prev, axis=-1, keepdims=True))
                h_out[rows, cols] = num / jnp.maximum(jnp.abs(den), jnp.exp(-m_t))
                b_last = b_col[last:last + 1, :]
                g_col = b_last - b_col + i_col
                m_new = jnp.maximum(b_last + m_prev, jnp.max(g_col, axis=0, keepdims=True))
                decay = jnp.exp(b_last + m_prev - m_new)
                ksc = kh.astype(F32) * jnp.exp(g_col - m_new)
                c_s[sidx] = decay * c_prev + _dot(ksc.T.astype(BF16), vh)
                n_s[sidx:sidx + 1, :] = decay * n_prev + jnp.sum(ksc, axis=0, keepdims=True)
                m_s[sidx:sidx + 1, :] = jnp.broadcast_to(m_new, (1, 128))
        return carry

    lax.fori_loop(0, n_chunk, chunk_step, 0)

    for r0 in range(0, seq, proj_rows):
        rows = pl.ds(r0, proj_rows)
        hsum = hf_s[rows, :] + hb_s[rows, :]
        gate = jax.nn.sigmoid(o_s[rows, :])
        for hd in range(C_HEADS):
            cols = slice(hd * 128, (hd + 1) * 128)
            hh = hsum[:, cols]
            mu = jnp.mean(hh, axis=-1, keepdims=True)
            hc = hh - mu
            var = jnp.mean(hc * hc, axis=-1, keepdims=True)
            y_ref[rows, cols] = (hc * lax.rsqrt(var + EPS) * gn_ref[:, cols] * gate[:, cols]).astype(BF16)
    if not has_init:
        cout_ref[...] = c_s[...]
        nout_ref[...] = n_s[...]
        mout_ref[...] = m_s[...]


def _mlstm(x, mod, w_c, gate_bias_row, gn_row, tril, triu, *, n_seq, seq, init=None, layer=0):
    proj_rows = min(seq, 512)
    kern = functools.partial(_mlstm_kernel, seq=seq, has_init=init is not None, proj_rows=proj_rows)
    const = lambda s: (0, 0)
    in_specs = [
        pl.BlockSpec((seq, D_MODEL), lambda s: (s, 0)),
        pl.BlockSpec((None, 1, 6 * D_MODEL), lambda s: (s if mod.shape[0] > 1 else 0, 0, 0)),
        pl.BlockSpec(w_c.shape, const),
        pl.BlockSpec(gate_bias_row.shape, const),
        pl.BlockSpec(gn_row.shape, const),
        pl.BlockSpec(tril.shape, const),
        pl.BlockSpec(triu.shape, const),
    ]
    args = [x, mod, w_c, gate_bias_row, gn_row, tril, triu]
    n_tok = n_seq * seq
    y_shape = jax.ShapeDtypeStruct((n_tok, BRANCH_WIDTH), BF16)
    y_spec = pl.BlockSpec((seq, BRANCH_WIDTH), lambda s: (s, 0))
    if init is not None:
        c0, n0, m0 = init
        in_specs += [pl.BlockSpec((None, None, 8, 128, 128), lambda s: (s, layer, 0, 0, 0)),
                     pl.BlockSpec((None, None, 8, 128), lambda s: (s, layer, 0, 0)),
                     pl.BlockSpec((None, None, 8, 128), lambda s: (s, layer, 0, 0))]
        args += [c0, n0, m0]
        out_shape, out_specs = y_shape, y_spec
    else:
        out_shape = (y_shape, jax.ShapeDtypeStruct((n_seq, 8, 128, 128), F32),
                     jax.ShapeDtypeStruct((n_seq, 8, 128), F32), jax.ShapeDtypeStruct((n_seq, 8, 128), F32))
        out_specs = (y_spec, pl.BlockSpec((None, 8, 128, 128), lambda s: (s, 0, 0, 0)),
                     pl.BlockSpec((None, 8, 128), lambda s: (s, 0, 0)),
                     pl.BlockSpec((None, 8, 128), lambda s: (s, 0, 0)))
    return pl.pallas_call(
        kern, out_shape=out_shape, grid=(n_seq,), in_specs=in_specs, out_specs=out_specs,
        scratch_shapes=[pltpu.VMEM((seq, 512), BF16), pltpu.VMEM((seq, 512), BF16), pltpu.VMEM((seq, 512), BF16),
                        pltpu.VMEM((seq, 512), F32), pltpu.VMEM((seq, 128), F32),
                        pltpu.VMEM((seq, 512), F32), pltpu.VMEM((seq, 512), F32),
                        pltpu.VMEM((8, 128, 128), F32), pltpu.VMEM((8, 128), F32), pltpu.VMEM((8, 128), F32)],
        compiler_params=_cparams("arbitrary"),
        name="branch_c_lat" if init is not None else "branch_c_ctx",
    )(*args)


def _merge_kernel(x_ref, mod_ref, ya_ref, yb_ref, yc_ref, yd_ref, wg_ref, wbr_ref, wout_ref, lng_ref, lnb_ref,
                  o_ref):
    x = x_ref[...]
    h = _modulated(x, mod_ref, 0).astype(BF16)
    mix = None
    for n, y_ref in enumerate((ya_ref, yb_ref, yc_ref, yd_ref)):
        gate = jax.nn.sigmoid(_dot(h, wg_ref[:, n * D_MODEL:(n + 1) * D_MODEL]))
        term = gate * _dot(y_ref[...], wbr_ref[n])
        mix = term if mix is None else mix + term
    out = _dot(mix.astype(BF16), wout_ref[...])
    g1 = mod_ref[:, 2 * D_MODEL:3 * D_MODEL]
    o_ref[...] = _layer_norm_rows(ALPHA * x + g1 * out, lng_ref[...], lnb_ref[...])


def _merge(x, mod, ys, w_g, w_br, w_out, ln_g, ln_b, *, seq, rows):
    n_tok = x.shape[0]
    per_seq = seq // rows
    tok = lambda i: (i, 0)
    c2 = lambda i: (0, 0)
    return pl.pallas_call(
        _merge_kernel, out_shape=jax.ShapeDtypeStruct((n_tok, D_MODEL), F32),
        grid=(n_tok // rows,),
        in_specs=[
            pl.BlockSpec((rows, D_MODEL), tok),
            pl.BlockSpec((None, 1, 6 * D_MODEL), lambda i: (i // per_seq if mod.shape[0] > 1 else 0, 0, 0)),
            pl.BlockSpec((rows, BRANCH_WIDTH), tok), pl.BlockSpec((rows, BRANCH_WIDTH), tok),
            pl.BlockSpec((rows, BRANCH_WIDTH), tok), pl.BlockSpec((rows, BRANCH_WIDTH), tok),
            pl.BlockSpec(w_g.shape, c2), pl.BlockSpec(w_br.shape, lambda i: (0, 0, 0)),
            pl.BlockSpec(w_out.shape, c2), pl.BlockSpec(ln_g.shape, c2), pl.BlockSpec(ln_b.shape, c2),
        ],
        out_specs=pl.BlockSpec((rows, D_MODEL), tok),
        compiler_params=_cparams("arbitrary"),
        name="merge",
    )(x, mod, *ys, w_g, w_br, w_out, ln_g, ln_b)


_TAKEN = -(2.0 ** 127)


def _top16(s):
    cur = s
    vals = []
    for r in range(PEER_TOPK):
        mx = jnp.max(cur, axis=0, keepdims=True)
        cur = jnp.where(cur == mx, _TAKEN * (1.0 + r / 32.0), cur)
        vals.append(mx)
    rank = jnp.where(cur <= _TAKEN, cur * (32.0 / _TAKEN) - 31.0, float(PEER_TOPK + 1))
    return jnp.concatenate(vals, axis=0), rank


def _pair_tables():
    pairs = [(k1, k2) for k1 in range(PEER_TOPK) for k2 in range(PEER_TOPK // (k1 + 1))]
    n = 56
    sel_a = np.zeros((n, PEER_TOPK), np.float32)
    sel_b = np.zeros((n, PEER_TOPK), np.float32)
    pad = np.full((n, 1), NEG_INF, np.float32)
    for row, (k1, k2) in enumerate(pairs):
        sel_a[row, k1] = 1.0
        sel_b[row, k2] = 1.0
        pad[row, 0] = 0.0
    return jnp.asarray(sel_a), jnp.asarray(sel_b), jnp.asarray(pad), jnp.asarray(sel_a.T, BF16)


def _route_kernel(x_ref, mod_ref, wq_ref, keys_ref, sela_ref, selb_ref, pad_ref, ind_ref,
                  cnt_ref, e1_ref, r2_ref, e2_ref):
    h2 = _modulated(x_ref[...], mod_ref, 1).astype(BF16)
    q = _dot(h2, wq_ref[...])
    s1 = _dot_nt(keys_ref[0].astype(BF16), q[:, :128].astype(BF16))
    s2 = _dot_nt(keys_ref[1].astype(BF16), q[:, 128:].astype(BF16))
    a, rank1 = _top16(s1)
    b, rank2 = _top16(s2)
    ea = jnp.exp(a - a[0:1])
    eb = jnp.exp(b - b[0:1])
    pick = lambda sel_ref, v: jnp.dot(sel_ref[...], v, precision=HIGHEST, preferred_element_type=F32)
    cand = pick(sela_ref, a) + pick(selb_ref, b) + pad_ref[...]
    gate = pick(sela_ref, ea) * pick(selb_ref, eb)
    cur = cand
    thr = None
    for _ in range(PEER_TOPK):
        thr = jnp.max(cur, axis=0, keepdims=True)
        cur = jnp.where(cur == thr, NEG_INF, cur)
    chosen = cand >= thr
    z = jnp.sum(jnp.where(chosen, gate, 0.0), axis=0, keepdims=True)
    cnt_sorted = _dot(ind_ref[...], jnp.where(chosen, 1.0, 0.0).astype(BF16))
    cnt = jnp.zeros_like(s1)
    for r in range(PEER_TOPK):
        cnt = jnp.where(rank1 == float(r + 1), cnt_sorted[r:r + 1], cnt)
    cnt_ref[...] = cnt
    e1_ref[...] = jnp.where(rank1 <= float(PEER_TOPK), jnp.exp(s1 - a[0:1]) * (0.5 / z), 0.0)
    r2_ref[...] = rank2.astype(BF16)
    e2_ref[...] = jnp.where(rank2 <= float(PEER_TOPK), jnp.exp(s2 - b[0:1]), 0.0).astype(BF16)


def _route(x1, mod, wq, keys, *, seq, cols):
    n_tok = x1.shape[0]
    per_seq = seq // cols
    tables = _pair_tables()
    row_shape = jax.ShapeDtypeStruct((PEER_HEADS, PEER_NKEYS, n_tok), F32)
    col_shape = jax.ShapeDtypeStruct((PEER_HEADS, PEER_NKEYS, n_tok), BF16)
    spec = pl.BlockSpec((None, PEER_NKEYS, cols), lambda i, h: (h, 0, i))
    return pl.pallas_call(
        _route_kernel, out_shape=(row_shape, row_shape, col_shape, col_shape),
        grid=(n_tok // cols, PEER_HEADS),
        in_specs=[
            pl.BlockSpec((cols, D_MODEL), lambda i, h: (i, 0)),
            pl.BlockSpec((None, 1, 6 * D_MODEL), lambda i, h: (i // per_seq if mod.shape[0] > 1 else 0, 0, 0)),
            pl.BlockSpec((D_MODEL, PEER_QDIM), lambda i, h: (0, h)),
            pl.BlockSpec((None, 2, PEER_NKEYS, PEER_QDIM // 2), lambda i, h: (h, 0, 0, 0)),
        ] + [pl.BlockSpec(t.shape, lambda i, h: (0, 0)) for t in tables],
        out_specs=(spec, spec, spec, spec),
        compiler_params=_cparams("arbitrary", "arbitrary"),
        name="peer_route",
    )(x1, mod, wq, keys, *tables)


def _peer_kernel(x_ref, mod_ref, u_ref, vt_ref, cnt_ref, e1_ref, r2_ref, e2_ref, lng_ref, lnb_ref,
                 o_ref, h2t_s, acc_s, *, key_rows):
    e = pl.program_id(1)

    @pl.when(e == 0)
    def _init():
        h2 = _modulated(x_ref[...], mod_ref, 1)
        h2t_s[...] = h2.T.astype(BF16)
        acc_s[...] = jnp.zeros_like(acc_s)

    n_tok = h2t_s.shape[1]
    first_key = pl.multiple_of(e * key_rows, key_rows)
    zero = jnp.zeros((PEER_NKEYS // 16, 16, n_tok), BF16)
    act = jnp.dot(u_ref[...], h2t_s[...], preferred_element_type=ACT_DTYPE).astype(BF16)
    act = act * (1.0 + lax.erf(act * (2.0 ** -0.5)))
    pieces = []
    for r in range(key_rows):
        g = None
        for hd in range(PEER_HEADS):
            cnt_blk = cnt_ref[hd, pl.ds(first_key, key_rows), :]
            e1_blk = e1_ref[hd, pl.ds(first_key, key_rows), :]
            cnt_rows = jnp.broadcast_to(cnt_blk[r:r + 1, :], (16, n_tok)).astype(BF16)
            e1_rows = jnp.broadcast_to(e1_blk[r:r + 1, :], (16, n_tok)).astype(BF16)
            term = jnp.where(r2_ref[hd] <= cnt_rows[None], e2_ref[hd], zero) * e1_rows[None]
            g = term if g is None else g + term
        pieces.append(g.reshape(PEER_NKEYS, n_tok) * act[r * PEER_NKEYS:(r + 1) * PEER_NKEYS, :])
    acc_s[...] += _dot(vt_ref[...], jnp.concatenate(pieces, axis=0))

    @pl.when(e == pl.num_programs(1) - 1)
    def _finish():
        x = x_ref[...]
        g2 = mod_ref[:, 5 * D_MODEL:6 * D_MODEL]
        o_ref[...] = _layer_norm_rows(ALPHA * x + g2 * acc_s[...].T, lng_ref[...], lnb_ref[...])


ACT_DTYPE = F32


def _peer(x1, mod, u, vt, route, ln_g, ln_b, *, seq, cols, key_rows):
    n_tok = x1.shape[0]
    per_seq = seq // cols
    n_exp = key_rows * PEER_NKEYS
    assert key_rows % 8 == 0
    kern = functools.partial(_peer_kernel, key_rows=key_rows)
    cnt, e1, r2, e2 = route
    r2 = r2.reshape(PEER_HEADS, PEER_NKEYS // 16, 16, n_tok)
    e2 = e2.reshape(PEER_HEADS, PEER_NKEYS // 16, 16, n_tok)
    rspec = pl.BlockSpec((PEER_HEADS, PEER_NKEYS, cols), lambda i, e: (0, 0, i))
    cspec = pl.BlockSpec((PEER_HEADS, PEER_NKEYS // 16, 16, cols), lambda i, e: (0, 0, 0, i))
    return pl.pallas_call(
        kern, out_shape=jax.ShapeDtypeStruct((n_tok, D_MODEL), F32),
        grid=(n_tok // cols, PEER_EXPERTS // n_exp),
        in_specs=[
            pl.BlockSpec((cols, D_MODEL), lambda i, e: (i, 0)),
            pl.BlockSpec((None, 1, 6 * D_MODEL), lambda i, e: (i // per_seq if mod.shape[0] > 1 else 0, 0, 0)),
            pl.BlockSpec((n_exp, D_MODEL), lambda i, e: (e, 0)),
            pl.BlockSpec((D_MODEL, n_exp), lambda i, e: (0, e)),
            rspec, rspec, cspec, cspec,
            pl.BlockSpec(ln_g.shape, lambda i, e: (0, 0)), pl.BlockSpec(ln_b.shape, lambda i, e: (0, 0)),
        ],
        out_specs=pl.BlockSpec((cols, D_MODEL), lambda i, e: (i, 0)),
        scratch_shapes=[pltpu.VMEM((D_MODEL, cols), BF16), pltpu.VMEM((D_MODEL, cols), F32)],
        compiler_params=_cparams("arbitrary", "arbitrary"),
        name="peer_experts",
    )(x1, mod, u, vt, cnt, e1, r2, e2, ln_g, ln_b)


def _rope_tables(seq):
    t = np.arange(seq)
    pos = np.stack([t // GRID_W, t % GRID_W], axis=1).astype(np.float64)
    inv = ROPE_BASE ** (-np.arange(16, dtype=np.float64) / 16)
    lane = np.arange(64)
    ang = pos[:, lane // 32] * inv[lane % 16][None, :]
    sign = np.where((lane % 32) < 16, -1.0, 1.0)[None, :]
    cos = np.tile(np.cos(ang), (1, 2)).astype(np.float32)
    sin = np.tile(np.sin(ang) * sign, (1, 2)).astype(np.float32)
    return jnp.asarray(cos), jnp.asarray(sin)


def _static_tables():
    lane = np.arange(640)
    bd = (lane[:, None] // 64 == lane[None, :] // 64).astype(np.float32) / 64.0
    src = np.arange(128)
    dst = np.arange(512)
    rep = ((src[:, None] // 64 == dst[None, :] // 256) & (src[:, None] % 64 == dst[None, :] % 64))
    idx = np.arange(CHUNK)
    tril = (idx[None, :] <= idx[:, None]).astype(np.float32)
    triu = (idx[None, :] >= idx[:, None]).astype(np.float32)
    return (jnp.asarray(bd, BF16), jnp.asarray(rep.astype(np.float32), BF16), jnp.asarray(tril), jnp.asarray(triu))


def _layer_params(l, w_in, attn_qk_gain, gmlp_ws, gmlp_b, mlstm_gate_bias, mlstm_gn, diff_lambda, diff_gn,
                  w_branch, w_out, ln_g, ln_b, peer_wq, peer_keys, peer_u, peer_v):
    w = w_in[l]
    p = {}
    p["w_a"] = w[:, _OFF_A:_OFF_B].astype(BF16)
    p["w_b"] = w[:, _OFF_B:_OFF_C].astype(BF16)
    p["w_c"] = jnp.concatenate([w[:, _OFF_C:_OFF_D], jnp.zeros((D_MODEL, 112), F32)], axis=1).astype(BF16)
    p["w_d"] = w[:, _OFF_D:_OFF_G].astype(BF16)
    p["w_g"] = w[:, _OFF_G:].astype(BF16)
    gain = attn_qk_gain[l]
    p["gain_row"] = jnp.concatenate([jnp.tile(gain[0], A_HEADS), jnp.tile(gain[1], A_KV_HEADS)])[None, :]
    p["ws"] = gmlp_ws[l]
    p["bias_full"] = jnp.repeat(gmlp_b[l].T, 128, axis=1)
    p["gate_bias_row"] = jnp.concatenate([mlstm_gate_bias[l].reshape(16), jnp.zeros((112,), F32)])[None, :]
    p["mlstm_gn_row"] = mlstm_gn[l].reshape(1, BRANCH_WIDTH)
    p["lam"] = diff_lambda[l]
    p["diff_gn_row"] = diff_gn[l].reshape(1, BRANCH_WIDTH)
    p["w_br"] = w_branch[l].astype(BF16)
    p["w_out"] = w_out[l].astype(BF16)
    p["ln_g0"], p["ln_b0"] = ln_g[l, 0][None, :], ln_b[l, 0][None, :]
    p["ln_g1"], p["ln_b1"] = ln_g[l, 1][None, :], ln_b[l, 1][None, :]
    p["wq"] = peer_wq[l].astype(BF16)
    p["keys"] = peer_keys[l]
    p["u"] = peer_u[l].astype(BF16)
    p["vt"] = peer_v[l].T.astype(BF16)
    return p


def _trunk_layer(x, mod, p, tabs, *, l, n_seq, seq, cfg, ctx_cache=None):
    bd, rep, tril, triu = tabs
    lam_init = 0.8 - 0.6 * math.exp(-0.3 * l)
    state = None
    if ctx_cache is None:
        ya, nk, nv = _attn_a(x, mod, p["w_a"], p["gain_row"], bd, rep, n_seq=n_seq, seq=seq, tq=cfg["tq"])
        yd, ndk, ndv = _attn_d(x, mod, p["w_d"], p["lam"], p["diff_gn_row"], n_seq=n_seq, seq=seq, tq=cfg["tq"],
                               lam_init=lam_init)
        yc, c_new, n_new, m_new = _mlstm(x, mod, p["w_c"], p["gate_bias_row"], p["mlstm_gn_row"], tril, triu,
                                         n_seq=n_seq, seq=seq)
        state = (nk, nv, ndk, ndv, c_new, n_new, m_new)
    else:
        rope, cak, cav, cdk, cdv, c0, n0, m0 = ctx_cache
        ya = _attn_a(x, mod, p["w_a"], p["gain_row"], bd, rep, n_seq=n_seq, seq=seq, tq=cfg["tq"],
                     rope=rope, cache=(cak, cav), layer=l)
        yd = _attn_d(x, mod, p["w_d"], p["lam"], p["diff_gn_row"], n_seq=n_seq, seq=seq, tq=cfg["tq"],
                     lam_init=lam_init, rope=rope, cache=(cdk, cdv), layer=l)
        yc = _mlstm(x, mod, p["w_c"], p["gate_bias_row"], p["mlstm_gn_row"], tril, triu,
                    n_seq=n_seq, seq=seq, init=(c0, n0, m0), layer=l)
    yb = _gmlp(x, mod, p["w_b"], p["ws"], p["bias_full"], seq=seq, rows=cfg["rows"])
    x1 = _merge(x, mod, (ya, yb, yc, yd), p["w_g"], p["w_br"], p["w_out"], p["ln_g0"], p["ln_b0"],
                seq=seq, rows=cfg["rows"])
    route = _route(x1, mod, p["wq"], p["keys"], seq=seq, cols=cfg["cols"])
    x2 = _peer(x1, mod, p["u"], p["vt"], route, p["ln_g1"], p["ln_b1"], seq=seq, cols=cfg["cols"],
               key_rows=cfg["key_rows"])
    return x2, state


def kernel(x_prompt, x_sample, cache_a_k, cache_a_v, cache_d_k, cache_d_v, state_c_C, state_c_n, state_c_m,
           c, c_ctx, w_mod, b_mod, w_in, attn_qk_gain, gmlp_ws, gmlp_b, mlstm_gate_bias, mlstm_gn,
           diff_lambda, diff_gn, w_branch, w_out, ln_g, ln_b, peer_wq, peer_keys, peer_u, peer_v):
    batch, seq, _ = x_prompt.shape
    dec_batch, dec_seq, _ = x_sample.shape
    past = cache_a_k.shape[2]
    c_rows = jnp.concatenate([c_ctx[None, :], c, jnp.zeros((8 - 1 - dec_batch, D_MODEL), F32)], axis=0)
    mods = _modulation(c_rows, w_mod, b_mod)
    tabs = _static_tables()
    rope = _rope_tables(dec_seq)
    cak = cache_a_k.reshape(dec_batch, DEPTH, past, 128)
    cav = cache_a_v.reshape(dec_batch, DEPTH, past, 128)
    cdk = cache_d_k.reshape(dec_batch, DEPTH, past, 512)
    cdv = cache_d_v.reshape(dec_batch, DEPTH, past, 512)
    c0 = state_c_C.reshape(dec_batch, DEPTH, 8, 128, 128)
    n0 = state_c_n.reshape(dec_batch, DEPTH, 8, 128)
    m0 = jnp.broadcast_to(state_c_m.reshape(dec_batch, DEPTH, 8, 1), (dec_batch, DEPTH, 8, 128))
    cfg_ctx = dict(tq=seq, rows=min(512, seq), cols=512, key_rows=8)
    cfg_lat = dict(tq=min(256, dec_seq), rows=min(512, dec_seq), cols=512, key_rows=8)
    y_p = x_prompt.reshape(batch * seq, D_MODEL)
    y_s = x_sample.reshape(dec_batch * dec_seq, D_MODEL)
    states = []
    for l in range(DEPTH):
        p = _layer_params(l, w_in, attn_qk_gain, gmlp_ws, gmlp_b, mlstm_gate_bias, mlstm_gn, diff_lambda, diff_gn,
                          w_branch, w_out, ln_g, ln_b, peer_wq, peer_keys, peer_u, peer_v)
        mod_ctx = mods[l, 0:1].reshape(1, 1, 6 * D_MODEL)
        mod_lat = mods[l, 1:1 + dec_batch].reshape(dec_batch, 1, 6 * D_MODEL)
        y_p, st = _trunk_layer(y_p, mod_ctx, p, tabs, l=l, n_seq=batch, seq=seq, cfg=cfg_ctx)
        states.append(st)
        y_s, _ = _trunk_layer(y_s, mod_lat, p, tabs, l=l, n_seq=dec_batch, seq=dec_seq, cfg=cfg_lat,
                              ctx_cache=(rope, cak, cav, cdk, cdv, c0, n0, m0))
    nk = jnp.stack([s[0].reshape(batch, seq, A_KV_HEADS, A_HEAD_DIM) for s in states], axis=1)
    nv = jnp.stack([s[1].reshape(batch, seq, A_KV_HEADS, A_HEAD_DIM) for s in states], axis=1)
    ndk = jnp.stack([s[2].reshape(batch, seq, D_HEADS, 2, D_HALF_DIM) for s in states], axis=1)
    ndv = jnp.stack([s[3].reshape(batch, seq, D_HEADS, D_VDIM) for s in states], axis=1)
    nc = jnp.stack([s[4].reshape(batch, 2, C_HEADS, C_HEAD_DIM, C_HEAD_DIM) for s in states], axis=1)
    nn = jnp.stack([s[5].reshape(batch, 2, C_HEADS, C_HEAD_DIM) for s in states], axis=1)
    nm = jnp.stack([s[6][:, :, 0].reshape(batch, 2, C_HEADS) for s in states], axis=1)
    return (y_p.reshape(batch, seq, D_MODEL), y_s.reshape(dec_batch, dec_seq, D_MODEL), nk, nv, ndk, ndv, nc, nn, nm)
```

```python
import functools
import math

import numpy as np
import jax
import jax.numpy as jnp
from jax import lax
from jax.experimental import pallas as pl
from jax.experimental.pallas import tpu as pltpu

F32 = jnp.float32
BF16 = jnp.bfloat16
HIGHEST = lax.Precision.HIGHEST

D_MODEL = 1024
DEPTH = 4
GRID_W = 64
ROPE_BASE = 10000.0
EPS = 1e-6
BRANCH_WIDTH = D_MODEL // 2
A_HEAD_DIM = 64
A_HEADS = 8
A_KV_HEADS = 2
B_GROUPS = 4
CHUNK = 128
C_HEADS = 4
C_HEAD_DIM = 128
D_HEADS = 4
D_VDIM = 128
D_HALF_DIM = 64
PEER_HEADS = 8
PEER_NKEYS = 128
PEER_EXPERTS = PEER_NKEYS * PEER_NKEYS
PEER_QDIM = 256
PEER_TOPK = 16
ALPHA = (2 * DEPTH) ** 0.25

_OFF_A = 0
_OFF_B = 768
_OFF_C = 1792
_OFF_CG = 3840
_OFF_D = 3856
_OFF_G = 5392
_C_WIDTH = 4 * BRANCH_WIDTH + 128

VMEM_LIMIT_BYTES = 56 * 1024 * 1024
NEG_INF = float("-inf")


def _cparams(*sem, flags=None):
    return pltpu.CompilerParams(dimension_semantics=sem, vmem_limit_bytes=VMEM_LIMIT_BYTES, flags=flags)


def _dot(a, b):
    return jnp.dot(a, b, preferred_element_type=F32)


def _dot_nt(a, b):
    return lax.dot_general(a, b, (((1,), (1,)), ((), ())), preferred_element_type=F32)


def _modulated(x, mod_ref, which):
    base = 3 * D_MODEL * which
    sh = mod_ref[:, base:base + D_MODEL]
    sc = mod_ref[:, base + D_MODEL:base + 2 * D_MODEL]
    return x * (1.0 + sc) + sh


def _layer_norm_rows(z, g, b):
    mu = jnp.mean(z, axis=-1, keepdims=True)
    zc = z - mu
    var = jnp.mean(zc * zc, axis=-1, keepdims=True)
    return zc * lax.rsqrt(var + EPS) * g + b


def _rope(x, cos, sin_signed, lane):
    w = x.shape[1]
    nxt = pltpu.roll(x, w - 16, 1)
    prv = pltpu.roll(x, 16, 1)
    partner = jnp.where((lane % 32) < 16, nxt, prv)
    return x * cos + partner * sin_signed


def _tile_lanes(t, n):
    return t if n == 1 else jnp.concatenate([t] * n, axis=1)


def _mod_kernel(c_ref, w_ref, b_ref, o_ref):
    c = c_ref[...]
    s = c * jax.nn.sigmoid(c)
    o_ref[...] = jnp.dot(s, w_ref[...], precision=HIGHEST, preferred_element_type=F32) + b_ref[...]


def _modulation(c_rows, w_mod, b_mod):
    n_col = 6 * D_MODEL // 1024
    return pl.pallas_call(
        _mod_kernel,
        out_shape=jax.ShapeDtypeStruct((DEPTH, 8, 6 * D_MODEL), F32),
        grid=(DEPTH, n_col),
        in_specs=[
            pl.BlockSpec((8, D_MODEL), lambda l, j: (0, 0)),
            pl.BlockSpec((None, D_MODEL, 1024), lambda l, j: (l, 0, j)),
            pl.BlockSpec((None, 1, 1024), lambda l, j: (l, 0, j)),
        ],
        out_specs=pl.BlockSpec((None, 8, 1024), lambda l, j: (l, 0, j)),
        compiler_params=_cparams("arbitrary", "arbitrary"),
        name="modulation",
    )(c_rows, w_mod, b_mod.reshape(DEPTH, 1, 6 * D_MODEL))


def _attn_a_kernel(*refs, seq, tq, n_cache, rope, proj_rows):
    if rope:
        (x_ref, mod_ref, w_ref, gain_ref, bd_ref, rep_ref, cos_ref, sin_ref, ck_ref, cv_ref,
         y_ref, q_s, k_s, v_s) = refs
    else:
        (x_ref, mod_ref, w_ref, gain_ref, bd_ref, rep_ref) = refs[:6]
        (y_ref, nk_ref, nv_ref, q_s, k_s, v_s) = refs[-6:]
    qi = pl.program_id(1)

    @pl.when(qi == 0)
    def _project():
        lane = lax.broadcasted_iota(jnp.int32, (1, 640), 1)
        for r0 in range(0, seq, proj_rows):
            rows = pl.ds(r0, proj_rows)
            h = _modulated(x_ref[rows, :], mod_ref, 0).astype(BF16)
            p = _dot(h, w_ref[...])
            qk = p[:, :640]
            sq = qk * qk
            hi = sq.astype(BF16)
            lo = (sq - hi.astype(F32)).astype(BF16)
            ms = _dot(hi, bd_ref[...]) + _dot(lo, bd_ref[...])
            qk = qk * lax.rsqrt(ms + EPS) * gain_ref[...]
            v = p[:, 640:768]
            if not rope:
                nk_ref[rows, :] = qk[:, 512:640]
                nv_ref[rows, :] = v
            else:
                cos = _tile_lanes(cos_ref[rows, :], 5)
                sin = _tile_lanes(sin_ref[rows, :], 5)
                qk = _rope(qk, cos, sin, lane)
            q_s[rows, :] = (qk[:, :512] * (A_HEAD_DIM ** -0.5)).astype(BF16)
            k_s[rows, :] = _dot(qk[:, 512:640].astype(BF16), rep_ref[...]).astype(BF16)
            v_s[rows, :] = _dot(v.astype(BF16), rep_ref[...]).astype(BF16)
        if n_cache:
            crow = pl.ds(seq, n_cache)
            k_s[crow, :] = _dot(ck_ref[...].astype(BF16), rep_ref[...]).astype(BF16)
            v_s[crow, :] = _dot(cv_ref[...].astype(BF16), rep_ref[...]).astype(BF16)

    head_of_lane = lax.broadcasted_iota(jnp.int32, (1, 256), 1) // A_HEAD_DIM
    qb = q_s[pl.ds(pl.multiple_of(qi * tq, tq), tq), :]
    for g in range(A_KV_HEADS):
        cols = slice(g * 256, (g + 1) * 256)
        qg = qb[:, cols]
        kg = k_s[:, cols]
        vg = v_s[:, cols]
        acc = jnp.zeros((tq, 256), F32)
        for r in range(A_HEADS // A_KV_HEADS):
            sel = head_of_lane == r
            qm = jnp.where(sel, qg, jnp.zeros_like(qg))
            s = _dot_nt(qm, kg)
            m = jnp.max(s, axis=-1, keepdims=True)
            p = jnp.exp(s - m)
            l = jnp.sum(p, axis=-1, keepdims=True)
            o = _dot(p.astype(BF16), vg)
            acc = acc + jnp.where(sel, o * (1.0 / l), 0.0)
        y_ref[:, cols] = acc.astype(BF16)


def _carry_aliases(in_specs, args, carry, first_out):
    if carry is None:
        return {}
    aliases = {}
    for k, arr in enumerate(carry):
        aliases[len(args)] = first_out + k
        in_specs.append(pl.BlockSpec(memory_space=pl.ANY))
        args.append(arr)
    return aliases


def _attn_a(x, mod, w_a, gain_row, bd, rep, *, n_seq, seq, tq, rope=None, cache=None, layer=0, carry=None):
    n_q = seq // tq
    n_cache = 0 if cache is None else cache[0].shape[2]
    proj_rows = min(seq, 512)
    kern = functools.partial(_attn_a_kernel, seq=seq, tq=tq, n_cache=n_cache, rope=rope is not None,
                             proj_rows=proj_rows)
    const = lambda s, q: (0, 0)
    in_specs = [
        pl.BlockSpec((seq, D_MODEL), lambda s, q: (s, 0)),
        pl.BlockSpec((None, 1, 6 * D_MODEL), lambda s, q: (s if mod.shape[0] > 1 else 0, 0, 0)),
        pl.BlockSpec(w_a.shape, const),
        pl.BlockSpec(gain_row.shape, const),
        pl.BlockSpec(bd.shape, const),
        pl.BlockSpec(rep.shape, const),
    ]
    args = [x, mod, w_a, gain_row, bd, rep]
    n_tok = n_seq * seq
    y_spec = pl.BlockSpec((tq, BRANCH_WIDTH), lambda s, q: (s * n_q + q, 0))
    y_shape = jax.ShapeDtypeStruct((n_tok, BRANCH_WIDTH), BF16)
    if rope is not None:
        cos, sin = rope
        in_specs += [pl.BlockSpec(cos.shape, const), pl.BlockSpec(sin.shape, const),
                     pl.BlockSpec((None, None, n_cache, 128), lambda s, q: (s, layer, 0, 0)),
                     pl.BlockSpec((None, None, n_cache, 128), lambda s, q: (s, layer, 0, 0))]
        args += [cos, sin, cache[0], cache[1]]
        out_shape, out_specs, aliases = y_shape, y_spec, {}
    else:
        kv_shape = jax.ShapeDtypeStruct((n_seq, DEPTH, seq, 128), F32)
        kv_spec = pl.BlockSpec((None, None, seq, 128), lambda s, q: (s, layer, 0, 0))
        out_shape, out_specs = (y_shape, kv_shape, kv_shape), (y_spec, kv_spec, kv_spec)
        aliases = _carry_aliases(in_specs, args, carry, first_out=1)
    return pl.pallas_call(
        kern, out_shape=out_shape, grid=(n_seq, n_q), in_specs=in_specs, out_specs=out_specs,
        input_output_aliases=aliases,
        scratch_shapes=[pltpu.VMEM((seq, 512), BF16), pltpu.VMEM((seq + n_cache, 512), BF16),
                        pltpu.VMEM((seq + n_cache, 512), BF16)],
        compiler_params=_cparams("arbitrary", "arbitrary"),
        name="branch_a_lat" if rope is not None else "branch_a_ctx",
    )(*args)


def _attn_d_kernel(*refs, seq, tq, n_cache, rope, proj_rows, lam_init):
    if rope:
        (x_ref, mod_ref, w_ref, lam_ref, gn_ref, cos_ref, sin_ref, ck_ref, cv_ref,
         y_ref, q_s, k_s, v_s) = refs
    else:
        (x_ref, mod_ref, w_ref, lam_ref, gn_ref) = refs[:5]
        (y_ref, nk_ref, nv_ref, q_s, k_s, v_s) = refs[-6:]
    qi = pl.program_id(1)

    @pl.when(qi == 0)
    def _project():
        lane = lax.broadcasted_iota(jnp.int32, (1, 512), 1)
        for r0 in range(0, seq, proj_rows):
            rows = pl.ds(r0, proj_rows)
            h = _modulated(x_ref[rows, :], mod_ref, 0).astype(BF16)
            p = _dot(h, w_ref[...])
            dq, dk, dv = p[:, :512], p[:, 512:1024], p[:, 1024:1536]
            if not rope:
                nk_ref[rows, :] = dk
                nv_ref[rows, :] = dv
            else:
                cos = _tile_lanes(cos_ref[rows, :], 4)
                sin = _tile_lanes(sin_ref[rows, :], 4)
                dq = _rope(dq, cos, sin, lane)
                dk = _rope(dk, cos, sin, lane)
            q_s[rows, :] = (dq * (D_HALF_DIM ** -0.5)).astype(BF16)
            k_s[rows, :] = dk.astype(BF16)
            v_s[rows, :] = dv.astype(BF16)
        if n_cache:
            crow = pl.ds(seq, n_cache)
            k_s[crow, :] = ck_ref[...].astype(BF16)
            v_s[crow, :] = cv_ref[...].astype(BF16)

    lv = lam_ref[...]
    lam = (jnp.exp(jnp.sum(lv[0:1] * lv[1:2], axis=-1, keepdims=True))
           - jnp.exp(jnp.sum(lv[2:3] * lv[3:4], axis=-1, keepdims=True)) + lam_init)
    half_of_lane = lax.broadcasted_iota(jnp.int32, (1, 128), 1) // D_HALF_DIM
    qb = q_s[pl.ds(pl.multiple_of(qi * tq, tq), tq), :]
    for hd in range(D_HEADS):
        cols = slice(hd * 128, (hd + 1) * 128)
        qh = qb[:, cols]
        kh = k_s[:, cols]
        vh = v_s[:, cols]
        probs = []
        for j in range(2):
            qm = jnp.where(half_of_lane == j, qh, jnp.zeros_like(qh))
            s = _dot_nt(qm, kh)
            m = jnp.max(s, axis=-1, keepdims=True)
            p = jnp.exp(s - m)
            l = jnp.sum(p, axis=-1, keepdims=True)
            probs.append(p * (1.0 / l))
        a = (probs[0] - lam * probs[1]).astype(BF16)
        o = _dot(a, vh)
        ms = jnp.mean(o * o, axis=-1, keepdims=True)
        o = o * lax.rsqrt(ms + EPS) * gn_ref[:, cols] * (1.0 - lam_init)
        y_ref[:, cols] = o.astype(BF16)


def _attn_d(x, mod, w_d, lam_params, gn_row, *, n_seq, seq, tq, lam_init, rope=None, cache=None, layer=0,
            carry=None):
    n_q = seq // tq
    n_cache = 0 if cache is None else cache[0].shape[2]
    proj_rows = min(seq, 512)
    kern = functools.partial(_attn_d_kernel, seq=seq, tq=tq, n_cache=n_cache, rope=rope is not None,
                             proj_rows=proj_rows, lam_init=lam_init)
    const = lambda s, q: (0, 0)
    in_specs = [
        pl.BlockSpec((seq, D_MODEL), lambda s, q: (s, 0)),
        pl.BlockSpec((None, 1, 6 * D_MODEL), lambda s, q: (s if mod.shape[0] > 1 else 0, 0, 0)),
        pl.BlockSpec(w_d.shape, const),
        pl.BlockSpec(lam_params.shape, const),
        pl.BlockSpec(gn_row.shape, const),
    ]
    args = [x, mod, w_d, lam_params, gn_row]
    n_tok = n_seq * seq
    y_spec = pl.BlockSpec((tq, BRANCH_WIDTH), lambda s, q: (s * n_q + q, 0))
    y_shape = jax.ShapeDtypeStruct((n_tok, BRANCH_WIDTH), BF16)
    if rope is not None:
        cos, sin = rope
        in_specs += [pl.BlockSpec(cos.shape, const), pl.BlockSpec(sin.shape, const),
                     pl.BlockSpec((None, None, n_cache, 512), lambda s, q: (s, layer, 0, 0)),
                     pl.BlockSpec((None, None, n_cache, 512), lambda s, q: (s, layer, 0, 0))]
        args += [cos, sin, cache[0], cache[1]]
        out_shape, out_specs, aliases = y_shape, y_spec, {}
    else:
        kv_shape = jax.ShapeDtypeStruct((n_seq, DEPTH, seq, 512), F32)
        kv_spec = pl.BlockSpec((None, None, seq, 512), lambda s, q: (s, layer, 0, 0))
        out_shape, out_specs = (y_shape, kv_shape, kv_shape), (y_spec, kv_spec, kv_spec)
        aliases = _carry_aliases(in_specs, args, carry, first_out=1)
    return pl.pallas_call(
        kern, out_shape=out_shape, grid=(n_seq, n_q), in_specs=in_specs, out_specs=out_specs,
        input_output_aliases=aliases,
        scratch_shapes=[pltpu.VMEM((seq, 512), BF16), pltpu.VMEM((seq + n_cache, 512), BF16),
                        pltpu.VMEM((seq + n_cache, 512), BF16)],
        compiler_params=_cparams("arbitrary", "arbitrary"),
        name="branch_d_lat" if rope is not None else "branch_d_ctx",
    )(*args)


def _gmlp_kernel(x_ref, mod_ref, w_ref, ws_ref, bias_ref, y_ref, *, rows):
    h = _modulated(x_ref[...], mod_ref, 0).astype(BF16)
    p = _dot(h, w_ref[...])
    u, v = p[:, :BRANCH_WIDTH], p[:, BRANCH_WIDTH:]
    mu = jnp.mean(v, axis=-1, keepdims=True)
    vc = v - mu
    var = jnp.mean(vc * vc, axis=-1, keepdims=True)
    vn = (vc * lax.rsqrt(var + EPS)).astype(BF16)
    for c in range(rows // CHUNK):
        rs = slice(c * CHUNK, (c + 1) * CHUNK)
        for g in range(B_GROUPS):
            cs = slice(g * 128, (g + 1) * 128)
            s = _dot(ws_ref[g].astype(BF16), vn[rs, cs]) + bias_ref[:, cs]
            y_ref[rs, cs] = (u[rs, cs] * s).astype(BF16)


def _gmlp(x, mod, w_b, ws, bias_full, *, seq, rows):
    n_tok = x.shape[0]
    per_seq = seq // rows
    kern = functools.partial(_gmlp_kernel, rows=rows)
    return pl.pallas_call(
        kern, out_shape=jax.ShapeDtypeStruct((n_tok, BRANCH_WIDTH), BF16),
        grid=(n_tok // rows,),
        in_specs=[
            pl.BlockSpec((rows, D_MODEL), lambda i: (i, 0)),
            pl.BlockSpec((None, 1, 6 * D_MODEL), lambda i: (i // per_seq if mod.shape[0] > 1 else 0, 0, 0)),
            pl.BlockSpec(w_b.shape, lambda i: (0, 0)),
            pl.BlockSpec(ws.shape, lambda i: (0, 0, 0)),
            pl.BlockSpec(bias_full.shape, lambda i: (0, 0)),
        ],
        out_specs=pl.BlockSpec((rows, BRANCH_WIDTH), lambda i: (i, 0)),
        compiler_params=_cparams("arbitrary"),
        name="branch_b",
    )(x, mod, w_b, ws, bias_full)


def _mlstm_kernel(*refs, seq, has_init, proj_rows):
    if has_init:
        (x_ref, mod_ref, w_ref, gb_ref, gn_ref, tril_ref, triu_ref, c0_ref, n0_ref, m0_ref,
         y_ref, q_s, k_s, v_s, o_s, g_s, hf_s, hb_s, c_s, n_s, m_s) = refs
    else:
        (x_ref, mod_ref, w_ref, gb_ref, gn_ref, tril_ref, triu_ref) = refs[:7]
        (y_ref, cout_ref, nout_ref, mout_ref,
         q_s, k_s, v_s, o_s, g_s, hf_s, hb_s, c_s, n_s, m_s) = refs[-14:]
    n_chunk = seq // CHUNK
    for r0 in range(0, seq, proj_rows):
        rows = pl.ds(r0, proj_rows)
        h = _modulated(x_ref[rows, :], mod_ref, 0).astype(BF16)
        p = _dot(h, w_ref[...])
        q_s[rows, :] = p[:, 0:512].astype(BF16)
        k_s[rows, :] = (p[:, 512:1024] * (C_HEAD_DIM ** -0.5)).astype(BF16)
        v_s[rows, :] = p[:, 1024:1536].astype(BF16)
        o_s[rows, :] = p[:, 1536:2048]
        g_s[rows, :] = p[:, 2048:2176] + gb_ref[...]
    if has_init:
        c_s[...] = c0_ref[...]
        n_s[...] = n0_ref[...]
        m_s[...] = m0_ref[...]
    else:
        c_s[...] = jnp.zeros_like(c_s)
        n_s[...] = jnp.zeros_like(n_s)
        m_s[...] = jnp.zeros_like(m_s)

    tril = tril_ref[...]
    triu = triu_ref[...]
    row_i = lax.broadcasted_iota(jnp.int32, (CHUNK, CHUNK), 0)
    col_i = lax.broadcasted_iota(jnp.int32, (CHUNK, CHUNK), 1)
    masks = (col_i <= row_i, col_i >= row_i)

    def chunk_step(c, carry):
        for direction in range(2):
            cc = c if direction == 0 else n_chunk - 1 - c
            rows = pl.ds(pl.multiple_of(cc * CHUNK, CHUNK), CHUNK)
            gates = g_s[rows, :]
            logf = jnp.minimum(gates, 0.0) - jnp.log1p(jnp.exp(-jnp.abs(gates)))
            tri_col = tril if direction == 0 else triu
            tri_row = triu if direction == 0 else tril
            b_col_all = jnp.dot(tri_col, logf, precision=HIGHEST, preferred_element_type=F32)
            gates_t = gates.T
            b_row_all = jnp.dot(logf.T, tri_row, precision=HIGHEST, preferred_element_type=F32)
            last = CHUNK - 1 if direction == 0 else 0
            h_out = hf_s if direction == 0 else hb_s
            for hd in range(C_HEADS):
                ic = direction * 8 + hd
                fc = direction * 8 + 4 + hd
                sidx = direction * 4 + hd
                cols = slice(hd * 128, (hd + 1) * 128)
                b_col = b_col_all[:, fc:fc + 1]
                b_row = b_row_all[fc:fc + 1, :]
                i_row = gates_t[ic:ic + 1, :]
                i_col = gates[:, ic:ic + 1]
                m_prev = m_s[sidx:sidx + 1, 0:1]
                qh = q_s[rows, cols]
                kh = k_s[rows, cols]
                vh = v_s[rows, cols]
                dlog = jnp.where(masks[direction], b_col - b_row + i_row, NEG_INF)
                m_t = jnp.maximum(jnp.max(dlog, axis=-1, keepdims=True), b_col + m_prev)
                w = _dot_nt(qh, kh) * jnp.exp(dlog - m_t)
                inter = jnp.exp(b_col + m_prev - m_t)
                c_prev = c_s[sidx]
                n_prev = n_s[sidx:sidx + 1, :]
                num = _dot(w.astype(BF16), vh) + inter * _dot(qh, c_prev.astype(BF16))
                den = (jnp.sum(w, axis=-1, keepdims=True)
                       + inter * jnp.sum(qh.astype(F32) * n_prev, axis=-1, keepdims=True))
                h_out[rows, cols] = num / jnp.maximum(jnp.abs(den), jnp.exp(-m_t))
                b_last = b_col[last:last + 1, :]
                g_col = b_last - b_col + i_col
                m_new = jnp.maximum(b_last + m_prev, jnp.max(g_col, axis=0, keepdims=True))
                decay = jnp.exp(b_last + m_prev - m_new)
                ksc = kh.astype(F32) * jnp.exp(g_col - m_new)
                c_s[sidx] = decay * c_prev + _dot(ksc.T.astype(BF16), vh)
                n_s[sidx:sidx + 1, :] = decay * n_prev + jnp.sum(ksc, axis=0, keepdims=True)
                m_s[sidx:sidx + 1, :] = jnp.broadcast_to(m_new, (1, 128))
        return carry

    lax.fori_loop(0, n_chunk, chunk_step, 0)

    for r0 in range(0, seq, proj_rows):
        rows = pl.ds(r0, proj_rows)
        hsum = hf_s[rows, :] + hb_s[rows, :]
        gate = jax.nn.sigmoid(o_s[rows, :])
        for hd in range(C_HEADS):
            cols = slice(hd * 128, (hd + 1) * 128)
            hh = hsum[:, cols]
            mu = jnp.mean(hh, axis=-1, keepdims=True)
            hc = hh - mu
            var = jnp.mean(hc * hc, axis=-1, keepdims=True)
            y_ref[rows, cols] = (hc * lax.rsqrt(var + EPS) * gn_ref[:, cols] * gate[:, cols]).astype(BF16)
    if not has_init:
        cout_ref[...] = c_s[...]
        nout_ref[...] = n_s[...]
        mout_ref[...] = m_s[...]


def _mlstm(x, mod, w_c, gate_bias_row, gn_row, tril, triu, *, n_seq, seq, init=None, layer=0, carry=None):
    proj_rows = min(seq, 512)
    kern = functools.partial(_mlstm_kernel, seq=seq, has_init=init is not None, proj_rows=proj_rows)
    const = lambda s: (0, 0)
    in_specs = [
        pl.BlockSpec((seq, D_MODEL), lambda s: (s, 0)),
        pl.BlockSpec((None, 1, 6 * D_MODEL), lambda s: (s if mod.shape[0] > 1 else 0, 0, 0)),
        pl.BlockSpec(w_c.shape, const),
        pl.BlockSpec(gate_bias_row.shape, const),
        pl.BlockSpec(gn_row.shape, const),
        pl.BlockSpec(tril.shape, const),
        pl.BlockSpec(triu.shape, const),
    ]
    args = [x, mod, w_c, gate_bias_row, gn_row, tril, triu]
    n_tok = n_seq * seq
    y_shape = jax.ShapeDtypeStruct((n_tok, BRANCH_WIDTH), BF16)
    y_spec = pl.BlockSpec((seq, BRANCH_WIDTH), lambda s: (s, 0))
    if init is not None:
        c0, n0, m0 = init
        in_specs += [pl.BlockSpec((None, None, 8, 128, 128), lambda s: (s, layer, 0, 0, 0)),
                     pl.BlockSpec((None, None, 8, 128), lambda s: (s, layer, 0, 0)),
                     pl.BlockSpec((None, None, 8, 128), lambda s: (s, layer, 0, 0))]
        args += [c0, n0, m0]
        out_shape, out_specs, aliases = y_shape, y_spec, {}
    else:
        out_shape = (y_shape, jax.ShapeDtypeStruct((n_seq, DEPTH, 8, 128, 128), F32),
                     jax.ShapeDtypeStruct((n_seq, DEPTH, 8, 128), F32),
                     jax.ShapeDtypeStruct((n_seq, DEPTH, 8, 128), F32))
        out_specs = (y_spec, pl.BlockSpec((None, None, 8, 128, 128), lambda s: (s, layer, 0, 0, 0)),
                     pl.BlockSpec((None, None, 8, 128), lambda s: (s, layer, 0, 0)),
                     pl.BlockSpec((None, None, 8, 128), lambda s: (s, layer, 0, 0)))
        aliases = _carry_aliases(in_specs, args, carry, first_out=1)
    return pl.pallas_call(
        kern, out_shape=out_shape, grid=(n_seq,), in_specs=in_specs, out_specs=out_specs,
        input_output_aliases=aliases,
        scratch_shapes=[pltpu.VMEM((seq, 512), BF16), pltpu.VMEM((seq, 512), BF16), pltpu.VMEM((seq, 512), BF16),
                        pltpu.VMEM((seq, 512), F32), pltpu.VMEM((seq, 128), F32),
                        pltpu.VMEM((seq, 512), F32), pltpu.VMEM((seq, 512), F32),
                        pltpu.VMEM((8, 128, 128), F32), pltpu.VMEM((8, 128), F32), pltpu.VMEM((8, 128), F32)],
        compiler_params=_cparams("arbitrary"),
        name="branch_c_lat" if init is not None else "branch_c_ctx",
    )(*args)


def _merge_kernel(x_ref, mod_ref, ya_ref, yb_ref, yc_ref, yd_ref, wg_ref, wbr_ref, wout_ref, lng_ref, lnb_ref,
                  o_ref):
    x = x_ref[...]
    h = _modulated(x, mod_ref, 0).astype(BF16)
    mix = None
    for n, y_ref in enumerate((ya_ref, yb_ref, yc_ref, yd_ref)):
        gate = jax.nn.sigmoid(_dot(h, wg_ref[:, n * D_MODEL:(n + 1) * D_MODEL]))
        term = gate * _dot(y_ref[...], wbr_ref[n])
        mix = term if mix is None else mix + term
    out = _dot(mix.astype(BF16), wout_ref[...])
    g1 = mod_ref[:, 2 * D_MODEL:3 * D_MODEL]
    o_ref[...] = _layer_norm_rows(ALPHA * x + g1 * out, lng_ref[...], lnb_ref[...])


def _merge(x, mod, ys, w_g, w_br, w_out, ln_g, ln_b, *, seq, rows):
    n_tok = x.shape[0]
    per_seq = seq // rows
    tok = lambda i: (i, 0)
    c2 = lambda i: (0, 0)
    return pl.pallas_call(
        _merge_kernel, out_shape=jax.ShapeDtypeStruct((n_tok, D_MODEL), F32),
        grid=(n_tok // rows,),
        in_specs=[
            pl.BlockSpec((rows, D_MODEL), tok),
            pl.BlockSpec((None, 1, 6 * D_MODEL), lambda i: (i // per_seq if mod.shape[0] > 1 else 0, 0, 0)),
            pl.BlockSpec((rows, BRANCH_WIDTH), tok), pl.BlockSpec((rows, BRANCH_WIDTH), tok),
            pl.BlockSpec((rows, BRANCH_WIDTH), tok), pl.BlockSpec((rows, BRANCH_WIDTH), tok),
            pl.BlockSpec(w_g.shape, c2), pl.BlockSpec(w_br.shape, lambda i: (0, 0, 0)),
            pl.BlockSpec(w_out.shape, c2), pl.BlockSpec(ln_g.shape, c2), pl.BlockSpec(ln_b.shape, c2),
        ],
        out_specs=pl.BlockSpec((rows, D_MODEL), tok),
        compiler_params=_cparams("arbitrary"),
        name="merge",
    )(x, mod, *ys, w_g, w_br, w_out, ln_g, ln_b)


_TAKEN = -(2.0 ** 127)


def _top16(s):
    cur = s
    vals = []
    for r in range(PEER_TOPK):
        mx = jnp.max(cur, axis=0, keepdims=True)
        cur = jnp.where(cur == mx, _TAKEN * (1.0 + r / 32.0), cur)
        vals.append(mx)
    rank = jnp.where(cur <= _TAKEN, cur * (32.0 / _TAKEN) - 31.0, float(PEER_TOPK + 1))
    return jnp.concatenate(vals, axis=0), rank


def _pair_tables():
    pairs = [(k1, k2) for k1 in range(PEER_TOPK) for k2 in range(PEER_TOPK // (k1 + 1))]
    n = 56
    sel_a = np.zeros((n, PEER_TOPK), np.float32)
    sel_b = np.zeros((n, PEER_TOPK), np.float32)
    pad = np.full((n, 1), NEG_INF, np.float32)
    for row, (k1, k2) in enumerate(pairs):
        sel_a[row, k1] = 1.0
        sel_b[row, k2] = 1.0
        pad[row, 0] = 0.0
    return jnp.asarray(sel_a), jnp.asarray(sel_b), jnp.asarray(pad), jnp.asarray(sel_a.T, BF16)


def _route_kernel(x_ref, mod_ref, wq_ref, keys_ref, sela_ref, selb_ref, pad_ref, ind_ref,
                  cnt_ref, e1_ref, r2_ref, e2_ref):
    h2 = _modulated(x_ref[...], mod_ref, 1).astype(BF16)
    q = _dot(h2, wq_ref[...])
    s1 = _dot_nt(keys_ref[0].astype(BF16), q[:, :128].astype(BF16))
    s2 = _dot_nt(keys_ref[1].astype(BF16), q[:, 128:].astype(BF16))
    a, rank1 = _top16(s1)
    b, rank2 = _top16(s2)
    ea = jnp.exp(a - a[0:1])
    eb = jnp.exp(b - b[0:1])
    pick = lambda sel_ref, v: jnp.dot(sel_ref[...], v, precision=HIGHEST, preferred_element_type=F32)
    cand = pick(sela_ref, a) + pick(selb_ref, b) + pad_ref[...]
    gate = pick(sela_ref, ea) * pick(selb_ref, eb)
    cur = cand
    thr = None
    for _ in range(PEER_TOPK):
        thr = jnp.max(cur, axis=0, keepdims=True)
        cur = jnp.where(cur == thr, NEG_INF, cur)
    chosen = cand >= thr
    z = jnp.sum(jnp.where(chosen, gate, 0.0), axis=0, keepdims=True)
    cnt_sorted = _dot(ind_ref[...], jnp.where(chosen, 1.0, 0.0).astype(BF16))
    cnt = jnp.zeros_like(s1)
    for r in range(PEER_TOPK):
        cnt = jnp.where(rank1 == float(r + 1), cnt_sorted[r:r + 1], cnt)
    cnt_ref[...] = cnt
    e1_ref[...] = jnp.where(rank1 <= float(PEER_TOPK), jnp.exp(s1 - a[0:1]) * (0.5 / z), 0.0)
    r2_ref[...] = rank2.astype(BF16)
    e2_ref[...] = jnp.where(rank2 <= float(PEER_TOPK), jnp.exp(s2 - b[0:1]), 0.0).astype(BF16)


def _route(x1, mod, wq, keys, *, seq, cols):
    n_tok = x1.shape[0]
    per_seq = seq // cols
    tables = _pair_tables()
    row_shape = jax.ShapeDtypeStruct((PEER_HEADS, PEER_NKEYS, n_tok), F32)
    col_shape = jax.ShapeDtypeStruct((PEER_HEADS, PEER_NKEYS, n_tok), BF16)
    spec = pl.BlockSpec((None, PEER_NKEYS, cols), lambda i, h: (h, 0, i))
    return pl.pallas_call(
        _route_kernel, out_shape=(row_shape, row_shape, col_shape, col_shape),
        grid=(n_tok // cols, PEER_HEADS),
        in_specs=[
            pl.BlockSpec((cols, D_MODEL), lambda i, h: (i, 0)),
            pl.BlockSpec((None, 1, 6 * D_MODEL), lambda i, h: (i // per_seq if mod.shape[0] > 1 else 0, 0, 0)),
            pl.BlockSpec((D_MODEL, PEER_QDIM), lambda i, h: (0, h)),
            pl.BlockSpec((None, 2, PEER_NKEYS, PEER_QDIM // 2), lambda i, h: (h, 0, 0, 0)),
        ] + [pl.BlockSpec(t.shape, lambda i, h: (0, 0)) for t in tables],
        out_specs=(spec, spec, spec, spec),
        compiler_params=_cparams("arbitrary", "arbitrary"),
        name="peer_route",
    )(x1, mod, wq, keys, *tables)


def _peer_kernel(x_ref, mod_ref, u_ref, v_ref, cnt_ref, e1_ref, r2_ref, e2_ref, lng_ref, lnb_ref,
                 o_ref, h2t_s, acc_s, *, key_rows):
    e = pl.program_id(1)

    @pl.when(e == 0)
    def _init():
        h2 = _modulated(x_ref[...], mod_ref, 1)
        h2t_s[...] = h2.T.astype(BF16)
        acc_s[...] = jnp.zeros_like(acc_s)

    n_tok = h2t_s.shape[1]
    first_key = pl.multiple_of(e * key_rows, key_rows)
    zero = jnp.zeros((PEER_NKEYS // 16, 16, n_tok), BF16)
    act = jnp.dot(u_ref[...], h2t_s[...], preferred_element_type=ACT_DTYPE).astype(BF16)
    act = act * (1.0 + lax.erf(act * (2.0 ** -0.5)))
    pieces = []
    for r in range(key_rows):
        g = None
        for hd in range(PEER_HEADS):
            cnt_blk = cnt_ref[hd, pl.ds(first_key, key_rows), :]
            e1_blk = e1_ref[hd, pl.ds(first_key, key_rows), :]
            cnt_rows = jnp.broadcast_to(cnt_blk[r:r + 1, :], (16, n_tok)).astype(BF16)
            e1_rows = jnp.broadcast_to(e1_blk[r:r + 1, :], (16, n_tok)).astype(BF16)
            term = jnp.where(r2_ref[hd] <= cnt_rows[None], e2_ref[hd], zero) * e1_rows[None]
            g = term if g is None else g + term
        pieces.append(g.reshape(PEER_NKEYS, n_tok) * act[r * PEER_NKEYS:(r + 1) * PEER_NKEYS, :])
    acc_s[...] += lax.dot_general(v_ref[...], jnp.concatenate(pieces, axis=0), (((0,), (0,)), ((), ())),
                                  preferred_element_type=F32)

    @pl.when(e == pl.num_programs(1) - 1)
    def _finish():
        x = x_ref[...]
        g2 = mod_ref[:, 5 * D_MODEL:6 * D_MODEL]
        o_ref[...] = _layer_norm_rows(ALPHA * x + g2 * acc_s[...].T, lng_ref[...], lnb_ref[...])


ACT_DTYPE = F32


def _peer(x1, mod, u, v, route, ln_g, ln_b, *, seq, cols, key_rows):
    n_tok = x1.shape[0]
    per_seq = seq // cols
    n_exp = key_rows * PEER_NKEYS
    assert key_rows % 8 == 0
    kern = functools.partial(_peer_kernel, key_rows=key_rows)
    cnt, e1, r2, e2 = route
    r2 = r2.reshape(PEER_HEADS, PEER_NKEYS // 16, 16, n_tok)
    e2 = e2.reshape(PEER_HEADS, PEER_NKEYS // 16, 16, n_tok)
    rspec = pl.BlockSpec((PEER_HEADS, PEER_NKEYS, cols), lambda i, e: (0, 0, i))
    cspec = pl.BlockSpec((PEER_HEADS, PEER_NKEYS // 16, 16, cols), lambda i, e: (0, 0, 0, i))
    return pl.pallas_call(
        kern, out_shape=jax.ShapeDtypeStruct((n_tok, D_MODEL), F32),
        grid=(n_tok // cols, PEER_EXPERTS // n_exp),
        in_specs=[
            pl.BlockSpec((cols, D_MODEL), lambda i, e: (i, 0)),
            pl.BlockSpec((None, 1, 6 * D_MODEL), lambda i, e: (i // per_seq if mod.shape[0] > 1 else 0, 0, 0)),
            pl.BlockSpec((n_exp, D_MODEL), lambda i, e: (e, 0)),
            pl.BlockSpec((n_exp, D_MODEL), lambda i, e: (e, 0)),
            rspec, rspec, cspec, cspec,
            pl.BlockSpec(ln_g.shape, lambda i, e: (0, 0)), pl.BlockSpec(ln_b.shape, lambda i, e: (0, 0)),
        ],
        out_specs=pl.BlockSpec((cols, D_MODEL), lambda i, e: (i, 0)),
        scratch_shapes=[pltpu.VMEM((D_MODEL, cols), BF16), pltpu.VMEM((D_MODEL, cols), F32)],
        compiler_params=_cparams("arbitrary", "arbitrary"),
        name="peer_experts",
    )(x1, mod, u, v, cnt, e1, r2, e2, ln_g, ln_b)


def _rope_tables(seq):
    t = np.arange(seq)
    pos = np.stack([t // GRID_W, t % GRID_W], axis=1).astype(np.float64)
    inv = ROPE_BASE ** (-np.arange(16, dtype=np.float64) / 16)
    lane = np.arange(64)
    ang = pos[:, lane // 32] * inv[lane % 16][None, :]
    sign = np.where((lane % 32) < 16, -1.0, 1.0)[None, :]
    cos = np.tile(np.cos(ang), (1, 2)).astype(np.float32)
    sin = np.tile(np.sin(ang) * sign, (1, 2)).astype(np.float32)
    return jnp.asarray(cos), jnp.asarray(sin)


def _static_tables():
    lane = np.arange(640)
    bd = (lane[:, None] // 64 == lane[None, :] // 64).astype(np.float32) / 64.0
    src = np.arange(128)
    dst = np.arange(512)
    rep = ((src[:, None] // 64 == dst[None, :] // 256) & (src[:, None] % 64 == dst[None, :] % 64))
    idx = np.arange(CHUNK)
    tril = (idx[None, :] <= idx[:, None]).astype(np.float32)
    triu = (idx[None, :] >= idx[:, None]).astype(np.float32)
    return (jnp.asarray(bd, BF16), jnp.asarray(rep.astype(np.float32), BF16), jnp.asarray(tril), jnp.asarray(triu))


def _layer_params(l, w_in, attn_qk_gain, gmlp_ws, gmlp_b, mlstm_gate_bias, mlstm_gn, diff_lambda, diff_gn,
                  w_branch, w_out, ln_g, ln_b, peer_wq, peer_keys, peer_u, peer_v):
    w = w_in[l]
    p = {}
    p["w_a"] = w[:, _OFF_A:_OFF_B].astype(BF16)
    p["w_b"] = w[:, _OFF_B:_OFF_C].astype(BF16)
    p["w_c"] = jnp.concatenate([w[:, _OFF_C:_OFF_D], jnp.zeros((D_MODEL, 112), F32)], axis=1).astype(BF16)
    p["w_d"] = w[:, _OFF_D:_OFF_G].astype(BF16)
    p["w_g"] = w[:, _OFF_G:].astype(BF16)
    gain = attn_qk_gain[l]
    p["gain_row"] = jnp.concatenate([jnp.tile(gain[0], A_HEADS), jnp.tile(gain[1], A_KV_HEADS)])[None, :]
    p["ws"] = gmlp_ws[l]
    p["bias_full"] = jnp.repeat(gmlp_b[l].T, 128, axis=1)
    p["gate_bias_row"] = jnp.concatenate([mlstm_gate_bias[l].reshape(16), jnp.zeros((112,), F32)])[None, :]
    p["mlstm_gn_row"] = mlstm_gn[l].reshape(1, BRANCH_WIDTH)
    p["lam"] = diff_lambda[l]
    p["diff_gn_row"] = diff_gn[l].reshape(1, BRANCH_WIDTH)
    p["w_br"] = w_branch[l].astype(BF16)
    p["w_out"] = w_out[l].astype(BF16)
    p["ln_g0"], p["ln_b0"] = ln_g[l, 0][None, :], ln_b[l, 0][None, :]
    p["ln_g1"], p["ln_b1"] = ln_g[l, 1][None, :], ln_b[l, 1][None, :]
    p["wq"] = peer_wq[l].astype(BF16)
    p["keys"] = peer_keys[l]
    p["u"] = peer_u[l].astype(BF16)
    p["v"] = peer_v[l].astype(BF16)
    return p


def _trunk_layer(x, mod, p, tabs, *, l, n_seq, seq, cfg, ctx_cache=None, prev_state=None):
    bd, rep, tril, triu = tabs
    lam_init = 0.8 - 0.6 * math.exp(-0.3 * l)
    state = None
    if ctx_cache is None:
        prev = (None, None, None) if prev_state is None else (prev_state[0:2], prev_state[2:4], prev_state[4:7])
        ya, nk, nv = _attn_a(x, mod, p["w_a"], p["gain_row"], bd, rep, n_seq=n_seq, seq=seq, tq=cfg["tq"],
                             layer=l, carry=prev[0])
        yd, ndk, ndv = _attn_d(x, mod, p["w_d"], p["lam"], p["diff_gn_row"], n_seq=n_seq, seq=seq, tq=cfg["tq"],
                               lam_init=lam_init, layer=l, carry=prev[1])
        yc, c_new, n_new, m_new = _mlstm(x, mod, p["w_c"], p["gate_bias_row"], p["mlstm_gn_row"], tril, triu,
                                         n_seq=n_seq, seq=seq, layer=l, carry=prev[2])
        state = (nk, nv, ndk, ndv, c_new, n_new, m_new)
    else:
        rope, cak, cav, cdk, cdv, c0, n0, m0 = ctx_cache
        ya = _attn_a(x, mod, p["w_a"], p["gain_row"], bd, rep, n_seq=n_seq, seq=seq, tq=cfg["tq"],
                     rope=rope, cache=(cak, cav), layer=l)
        yd = _attn_d(x, mod, p["w_d"], p["lam"], p["diff_gn_row"], n_seq=n_seq, seq=seq, tq=cfg["tq"],
                     lam_init=lam_init, rope=rope, cache=(cdk, cdv), layer=l)
        yc = _mlstm(x, mod, p["w_c"], p["gate_bias_row"], p["mlstm_gn_row"], tril, triu,
                    n_seq=n_seq, seq=seq, init=(c0, n0, m0), layer=l)
    yb = _gmlp(x, mod, p["w_b"], p["ws"], p["bias_full"], seq=seq, rows=cfg["rows"])
    x1 = _merge(x, mod, (ya, yb, yc, yd), p["w_g"], p["w_br"], p["w_out"], p["ln_g0"], p["ln_b0"],
                seq=seq, rows=cfg["rows"])
    route = _route(x1, mod, p["wq"], p["keys"], seq=seq, cols=cfg["cols"])
    x2 = _peer(x1, mod, p["u"], p["v"], route, p["ln_g1"], p["ln_b1"], seq=seq, cols=cfg["cols"],
               key_rows=cfg["key_rows"])
    return x2, state


def kernel(x_prompt, x_sample, cache_a_k, cache_a_v, cache_d_k, cache_d_v, state_c_C, state_c_n, state_c_m,
           c, c_ctx, w_mod, b_mod, w_in, attn_qk_gain, gmlp_ws, gmlp_b, mlstm_gate_bias, mlstm_gn,
           diff_lambda, diff_gn, w_branch, w_out, ln_g, ln_b, peer_wq, peer_keys, peer_u, peer_v):
    batch, seq, _ = x_prompt.shape
    dec_batch, dec_seq, _ = x_sample.shape
    past = cache_a_k.shape[2]
    c_rows = jnp.concatenate([c_ctx[None, :], c, jnp.zeros((8 - 1 - dec_batch, D_MODEL), F32)], axis=0)
    mods = _modulation(c_rows, w_mod, b_mod)
    tabs = _static_tables()
    rope = _rope_tables(dec_seq)
    cak = cache_a_k.reshape(dec_batch, DEPTH, past, 128)
    cav = cache_a_v.reshape(dec_batch, DEPTH, past, 128)
    cdk = cache_d_k.reshape(dec_batch, DEPTH, past, 512)
    cdv = cache_d_v.reshape(dec_batch, DEPTH, past, 512)
    c0 = state_c_C.reshape(dec_batch, DEPTH, 8, 128, 128)
    n0 = state_c_n.reshape(dec_batch, DEPTH, 8, 128)
    m0 = jnp.broadcast_to(state_c_m.reshape(dec_batch, DEPTH, 8, 1), (dec_batch, DEPTH, 8, 128))
    cfg_ctx = dict(tq=seq, rows=min(512, seq), cols=512, key_rows=16)
    cfg_lat = dict(tq=min(256, dec_seq), rows=min(512, dec_seq), cols=512, key_rows=16)
    y_p = x_prompt.reshape(batch * seq, D_MODEL)
    y_s = x_sample.reshape(dec_batch * dec_seq, D_MODEL)
    state = None
    for l in range(DEPTH):
        p = _layer_params(l, w_in, attn_qk_gain, gmlp_ws, gmlp_b, mlstm_gate_bias, mlstm_gn, diff_lambda, diff_gn,
                          w_branch, w_out, ln_g, ln_b, peer_wq, peer_keys, peer_u, peer_v)
        mod_ctx = mods[l, 0:1].reshape(1, 1, 6 * D_MODEL)
        mod_lat = mods[l, 1:1 + dec_batch].reshape(dec_batch, 1, 6 * D_MODEL)
        y_p, state = _trunk_layer(y_p, mod_ctx, p, tabs, l=l, n_seq=batch, seq=seq, cfg=cfg_ctx, prev_state=state)
        y_s, _ = _trunk_layer(y_s, mod_lat, p, tabs, l=l, n_seq=dec_batch, seq=dec_seq, cfg=cfg_lat,
                              ctx_cache=(rope, cak, cav, cdk, cdv, c0, n0, m0))
    nk = state[0].reshape(batch, DEPTH, seq, A_KV_HEADS, A_HEAD_DIM)
    nv = state[1].reshape(batch, DEPTH, seq, A_KV_HEADS, A_HEAD_DIM)
    ndk = state[2].reshape(batch, DEPTH, seq, D_HEADS, 2, D_HALF_DIM)
    ndv = state[3].reshape(batch, DEPTH, seq, D_HEADS, D_VDIM)
    nc = state[4].reshape(batch, DEPTH, 2, C_HEADS, C_HEAD_DIM, C_HEAD_DIM)
    nn = state[5].reshape(batch, DEPTH, 2, C_HEADS, C_HEAD_DIM)
    nm = state[6][:, :, :, 0].reshape(batch, DEPTH, 2, C_HEADS)
    return (y_p.reshape(batch, seq, D_MODEL), y_s.reshape(dec_batch, dec_seq, D_MODEL), nk, nv, ndk, ndv, nc, nn, nm)
```

```python
import functools
import math

import numpy as np
import jax
import jax.numpy as jnp
from jax import lax
from jax.experimental import pallas as pl
from jax.experimental.pallas import tpu as pltpu

F32 = jnp.float32
BF16 = jnp.bfloat16
HIGHEST = lax.Precision.HIGHEST

D_MODEL = 1024
DEPTH = 4
GRID_W = 64
ROPE_BASE = 10000.0
EPS = 1e-6
BRANCH_WIDTH = D_MODEL // 2
A_HEAD_DIM = 64
A_HEADS = 8
A_KV_HEADS = 2
B_GROUPS = 4
CHUNK = 128
C_HEADS = 4
C_HEAD_DIM = 128
D_HEADS = 4
D_VDIM = 128
D_HALF_DIM = 64
PEER_HEADS = 8
PEER_NKEYS = 128
PEER_EXPERTS = PEER_NKEYS * PEER_NKEYS
PEER_QDIM = 256
PEER_TOPK = 16
ALPHA = (2 * DEPTH) ** 0.25

_OFF_A = 0
_OFF_B = 768
_OFF_C = 1792
_OFF_CG = 3840
_OFF_D = 3856
_OFF_G = 5392
_C_WIDTH = 4 * BRANCH_WIDTH + 128

VMEM_LIMIT_BYTES = 56 * 1024 * 1024
NEG_INF = float("-inf")


def _cparams(*sem, flags=None):
    return pltpu.CompilerParams(dimension_semantics=sem, vmem_limit_bytes=VMEM_LIMIT_BYTES, flags=flags)


def _dot(a, b):
    return jnp.dot(a, b, preferred_element_type=F32)


def _dot_nt(a, b):
    return lax.dot_general(a, b, (((1,), (1,)), ((), ())), preferred_element_type=F32)


def _modulated(x, mod_ref, which):
    base = 3 * D_MODEL * which
    sh = mod_ref[:, base:base + D_MODEL]
    sc = mod_ref[:, base + D_MODEL:base + 2 * D_MODEL]
    return x * (1.0 + sc) + sh


def _layer_norm_rows(z, g, b):
    mu = jnp.mean(z, axis=-1, keepdims=True)
    zc = z - mu
    var = jnp.mean(zc * zc, axis=-1, keepdims=True)
    return zc * lax.rsqrt(var + EPS) * g + b


def _rope(x, cos, sin_signed, lane):
    w = x.shape[1]
    nxt = pltpu.roll(x, w - 16, 1)
    prv = pltpu.roll(x, 16, 1)
    partner = jnp.where((lane % 32) < 16, nxt, prv)
    return x * cos + partner * sin_signed


def _tile_lanes(t, n):
    return t if n == 1 else jnp.concatenate([t] * n, axis=1)


def _mod_kernel(c_ref, w_ref, b_ref, o_ref):
    c = c_ref[...]
    s = c * jax.nn.sigmoid(c)
    o_ref[...] = jnp.dot(s, w_ref[...], precision=HIGHEST, preferred_element_type=F32) + b_ref[...]


def _modulation(c_rows, w_mod, b_mod):
    n_col = 6 * D_MODEL // 1024
    return pl.pallas_call(
        _mod_kernel,
        out_shape=jax.ShapeDtypeStruct((DEPTH, 8, 6 * D_MODEL), F32),
        grid=(DEPTH, n_col),
        in_specs=[
            pl.BlockSpec((8, D_MODEL), lambda l, j: (0, 0)),
            pl.BlockSpec((None, D_MODEL, 1024), lambda l, j: (l, 0, j)),
            pl.BlockSpec((None, 1, 1024), lambda l, j: (l, 0, j)),
        ],
        out_specs=pl.BlockSpec((None, 8, 1024), lambda l, j: (l, 0, j)),
        compiler_params=_cparams("arbitrary", "arbitrary"),
        name="modulation",
    )(c_rows, w_mod, b_mod.reshape(DEPTH, 1, 6 * D_MODEL))


def _attn_a_kernel(*refs, seq, tq, n_cache, rope, proj_rows):
    if rope:
        (x_ref, mod_ref, w_ref, gain_ref, bd_ref, rep_ref, cos_ref, sin_ref, ck_ref, cv_ref,
         y_ref, q_s, k_s, v_s) = refs
    else:
        (x_ref, mod_ref, w_ref, gain_ref, bd_ref, rep_ref) = refs[:6]
        (y_ref, nk_ref, nv_ref, q_s, k_s, v_s) = refs[-6:]
    qi = pl.program_id(1)

    @pl.when(qi == 0)
    def _project():
        lane = lax.broadcasted_iota(jnp.int32, (1, 640), 1)
        for r0 in range(0, seq, proj_rows):
            rows = pl.ds(r0, proj_rows)
            h = _modulated(x_ref[rows, :], mod_ref, 0).astype(BF16)
            p = _dot(h, w_ref[...])
            qk = p[:, :640]
            sq = qk * qk
            hi = sq.astype(BF16)
            lo = (sq - hi.astype(F32)).astype(BF16)
            ms = jnp.concatenate([_dot(hi[:, c:c + 128], bd_ref[...]) + _dot(lo[:, c:c + 128], bd_ref[...])
                                  for c in range(0, 640, 128)], axis=1)
            qk = qk * lax.rsqrt(ms + EPS) * gain_ref[...]
            v = p[:, 640:768]
            if not rope:
                nk_ref[rows, :] = qk[:, 512:640]
                nv_ref[rows, :] = v
            else:
                cos = _tile_lanes(cos_ref[rows, :], 5)
                sin = _tile_lanes(sin_ref[rows, :], 5)
                qk = _rope(qk, cos, sin, lane)
            q_s[rows, :] = (qk[:, :512] * (A_HEAD_DIM ** -0.5)).astype(BF16)
            k_s[rows, :] = _dot(qk[:, 512:640].astype(BF16), rep_ref[...]).astype(BF16)
            v_s[rows, :] = _dot(v.astype(BF16), rep_ref[...]).astype(BF16)
        if n_cache:
            crow = pl.ds(seq, n_cache)
            k_s[crow, :] = _dot(ck_ref[...].astype(BF16), rep_ref[...]).astype(BF16)
            v_s[crow, :] = _dot(cv_ref[...].astype(BF16), rep_ref[...]).astype(BF16)

    head_of_lane = lax.broadcasted_iota(jnp.int32, (1, 256), 1) // A_HEAD_DIM
    qb = q_s[pl.ds(pl.multiple_of(qi * tq, tq), tq), :]
    for g in range(A_KV_HEADS):
        cols = slice(g * 256, (g + 1) * 256)
        qg = qb[:, cols]
        kg = k_s[:, cols]
        vg = v_s[:, cols]
        acc = jnp.zeros((tq, 256), F32)
        for r in range(A_HEADS // A_KV_HEADS):
            sel = head_of_lane == r
            qm = jnp.where(sel, qg, jnp.zeros_like(qg))
            s = _dot_nt(qm, kg)
            m = jnp.max(s, axis=-1, keepdims=True)
            p = jnp.exp(s - m)
            l = jnp.sum(p, axis=-1, keepdims=True)
            o = _dot(p.astype(BF16), vg)
            acc = acc + jnp.where(sel, o * (1.0 / l), 0.0)
        y_ref[:, cols] = acc.astype(BF16)


def _carry_aliases(in_specs, args, carry, first_out):
    if carry is None:
        return {}
    aliases = {}
    for k, arr in enumerate(carry):
        aliases[len(args)] = first_out + k
        in_specs.append(pl.BlockSpec(memory_space=pl.ANY))
        args.append(arr)
    return aliases


def _attn_a(x, mod, w_a, gain_row, bd, rep, *, n_seq, seq, tq, rope=None, cache=None, layer=0, carry=None):
    n_q = seq // tq
    n_cache = 0 if cache is None else cache[0].shape[2]
    proj_rows = min(seq, 512)
    kern = functools.partial(_attn_a_kernel, seq=seq, tq=tq, n_cache=n_cache, rope=rope is not None,
                             proj_rows=proj_rows)
    const = lambda s, q: (0, 0)
    in_specs = [
        pl.BlockSpec((seq, D_MODEL), lambda s, q: (s, 0)),
        pl.BlockSpec((None, 1, 6 * D_MODEL), lambda s, q: (s if mod.shape[0] > 1 else 0, 0, 0)),
        pl.BlockSpec(w_a.shape, const),
        pl.BlockSpec(gain_row.shape, const),
        pl.BlockSpec(bd.shape, const),
        pl.BlockSpec(rep.shape, const),
    ]
    args = [x, mod, w_a, gain_row, bd, rep]
    n_tok = n_seq * seq
    y_spec = pl.BlockSpec((tq, BRANCH_WIDTH), lambda s, q: (s * n_q + q, 0))
    y_shape = jax.ShapeDtypeStruct((n_tok, BRANCH_WIDTH), BF16)
    if rope is not None:
        cos, sin = rope
        in_specs += [pl.BlockSpec(cos.shape, const), pl.BlockSpec(sin.shape, const),
                     pl.BlockSpec((None, None, n_cache, 128), lambda s, q: (s, layer, 0, 0)),
                     pl.BlockSpec((None, None, n_cache, 128), lambda s, q: (s, layer, 0, 0))]
        args += [cos, sin, cache[0], cache[1]]
        out_shape, out_specs, aliases = y_shape, y_spec, {}
    else:
        kv_shape = jax.ShapeDtypeStruct((n_seq, DEPTH, seq, 128), F32)
        kv_spec = pl.BlockSpec((None, None, seq, 128), lambda s, q: (s, layer, 0, 0))
        out_shape, out_specs = (y_shape, kv_shape, kv_shape), (y_spec, kv_spec, kv_spec)
        aliases = _carry_aliases(in_specs, args, carry, first_out=1)
    return pl.pallas_call(
        kern, out_shape=out_shape, grid=(n_seq, n_q), in_specs=in_specs, out_specs=out_specs,
        input_output_aliases=aliases,
        scratch_shapes=[pltpu.VMEM((seq, 512), BF16), pltpu.VMEM((seq + n_cache, 512), BF16),
                        pltpu.VMEM((seq + n_cache, 512), BF16)],
        compiler_params=_cparams("arbitrary", "arbitrary"),
        name="branch_a_lat" if rope is not None else "branch_a_ctx",
    )(*args)


def _attn_d_kernel(*refs, seq, tq, n_cache, rope, proj_rows, lam_init):
    if rope:
        (x_ref, mod_ref, w_ref, lam_ref, gn_ref, cos_ref, sin_ref, ck_ref, cv_ref,
         y_ref, q_s, k_s, v_s) = refs
    else:
        (x_ref, mod_ref, w_ref, lam_ref, gn_ref) = refs[:5]
        (y_ref, nk_ref, nv_ref, q_s, k_s, v_s) = refs[-6:]
    qi = pl.program_id(1)

    @pl.when(qi == 0)
    def _project():
        lane = lax.broadcasted_iota(jnp.int32, (1, 512), 1)
        for r0 in range(0, seq, proj_rows):
            rows = pl.ds(r0, proj_rows)
            h = _modulated(x_ref[rows, :], mod_ref, 0).astype(BF16)
            p = _dot(h, w_ref[...])
            dq, dk, dv = p[:, :512], p[:, 512:1024], p[:, 1024:1536]
            if not rope:
                nk_ref[rows, :] = dk
                nv_ref[rows, :] = dv
            else:
                cos = _tile_lanes(cos_ref[rows, :], 4)
                sin = _tile_lanes(sin_ref[rows, :], 4)
                dq = _rope(dq, cos, sin, lane)
                dk = _rope(dk, cos, sin, lane)
            q_s[rows, :] = (dq * (D_HALF_DIM ** -0.5)).astype(BF16)
            k_s[rows, :] = dk.astype(BF16)
            v_s[rows, :] = dv.astype(BF16)
        if n_cache:
            crow = pl.ds(seq, n_cache)
            k_s[crow, :] = ck_ref[...].astype(BF16)
            v_s[crow, :] = cv_ref[...].astype(BF16)

    lv = lam_ref[...]
    lam = (jnp.exp(jnp.sum(lv[0:1] * lv[1:2], axis=-1, keepdims=True))
           - jnp.exp(jnp.sum(lv[2:3] * lv[3:4], axis=-1, keepdims=True)) + lam_init)
    half_of_lane = lax.broadcasted_iota(jnp.int32, (1, 128), 1) // D_HALF_DIM
    qb = q_s[pl.ds(pl.multiple_of(qi * tq, tq), tq), :]
    for hd in range(D_HEADS):
        cols = slice(hd * 128, (hd + 1) * 128)
        qh = qb[:, cols]
        kh = k_s[:, cols]
        vh = v_s[:, cols]
        probs = []
        for j in range(2):
            qm = jnp.where(half_of_lane == j, qh, jnp.zeros_like(qh))
            s = _dot_nt(qm, kh)
            m = jnp.max(s, axis=-1, keepdims=True)
            p = jnp.exp(s - m)
            l = jnp.sum(p, axis=-1, keepdims=True)
            probs.append(p * (1.0 / l))
        a = (probs[0] - lam * probs[1]).astype(BF16)
        o = _dot(a, vh)
        ms = jnp.mean(o * o, axis=-1, keepdims=True)
        o = o * lax.rsqrt(ms + EPS) * gn_ref[:, cols] * (1.0 - lam_init)
        y_ref[:, cols] = o.astype(BF16)


def _attn_d(x, mod, w_d, lam_params, gn_row, *, n_seq, seq, tq, lam_init, rope=None, cache=None, layer=0,
            carry=None):
    n_q = seq // tq
    n_cache = 0 if cache is None else cache[0].shape[2]
    proj_rows = min(seq, 512)
    kern = functools.partial(_attn_d_kernel, seq=seq, tq=tq, n_cache=n_cache, rope=rope is not None,
                             proj_rows=proj_rows, lam_init=lam_init)
    const = lambda s, q: (0, 0)
    in_specs = [
        pl.BlockSpec((seq, D_MODEL), lambda s, q: (s, 0)),
        pl.BlockSpec((None, 1, 6 * D_MODEL), lambda s, q: (s if mod.shape[0] > 1 else 0, 0, 0)),
        pl.BlockSpec(w_d.shape, const),
        pl.BlockSpec(lam_params.shape, const),
        pl.BlockSpec(gn_row.shape, const),
    ]
    args = [x, mod, w_d, lam_params, gn_row]
    n_tok = n_seq * seq
    y_spec = pl.BlockSpec((tq, BRANCH_WIDTH), lambda s, q: (s * n_q + q, 0))
    y_shape = jax.ShapeDtypeStruct((n_tok, BRANCH_WIDTH), BF16)
    if rope is not None:
        cos, sin = rope
        in_specs += [pl.BlockSpec(cos.shape, const), pl.BlockSpec(sin.shape, const),
                     pl.BlockSpec((None, None, n_cache, 512), lambda s, q: (s, layer, 0, 0)),
                     pl.BlockSpec((None, None, n_cache, 512), lambda s, q: (s, layer, 0, 0))]
        args += [cos, sin, cache[0], cache[1]]
        out_shape, out_specs, aliases = y_shape, y_spec, {}
    else:
        kv_shape = jax.ShapeDtypeStruct((n_seq, DEPTH, seq, 512), F32)
        kv_spec = pl.BlockSpec((None, None, seq, 512), lambda s, q: (s, layer, 0, 0))
        out_shape, out_specs = (y_shape, kv_shape, kv_shape), (y_spec, kv_spec, kv_spec)
        aliases = _carry_aliases(in_specs, args, carry, first_out=1)
    return pl.pallas_call(
        kern, out_shape=out_shape, grid=(n_seq, n_q), in_specs=in_specs, out_specs=out_specs,
        input_output_aliases=aliases,
        scratch_shapes=[pltpu.VMEM((seq, 512), BF16), pltpu.VMEM((seq + n_cache, 512), BF16),
                        pltpu.VMEM((seq + n_cache, 512), BF16)],
        compiler_params=_cparams("arbitrary", "arbitrary"),
        name="branch_d_lat" if rope is not None else "branch_d_ctx",
    )(*args)


def _gmlp_kernel(x_ref, mod_ref, w_ref, ws_ref, bias_ref, y_ref, *, rows):
    h = _modulated(x_ref[...], mod_ref, 0).astype(BF16)
    p = _dot(h, w_ref[...])
    u, v = p[:, :BRANCH_WIDTH], p[:, BRANCH_WIDTH:]
    mu = jnp.mean(v, axis=-1, keepdims=True)
    vc = v - mu
    var = jnp.mean(vc * vc, axis=-1, keepdims=True)
    vn = (vc * lax.rsqrt(var + EPS)).astype(BF16)
    for c in range(rows // CHUNK):
        rs = slice(c * CHUNK, (c + 1) * CHUNK)
        for g in range(B_GROUPS):
            cs = slice(g * 128, (g + 1) * 128)
            s = _dot(ws_ref[g].astype(BF16), vn[rs, cs]) + bias_ref[:, cs]
            y_ref[rs, cs] = (u[rs, cs] * s).astype(BF16)


def _gmlp(x, mod, w_b, ws, bias_full, *, seq, rows):
    n_tok = x.shape[0]
    per_seq = seq // rows
    kern = functools.partial(_gmlp_kernel, rows=rows)
    return pl.pallas_call(
        kern, out_shape=jax.ShapeDtypeStruct((n_tok, BRANCH_WIDTH), BF16),
        grid=(n_tok // rows,),
        in_specs=[
            pl.BlockSpec((rows, D_MODEL), lambda i: (i, 0)),
            pl.BlockSpec((None, 1, 6 * D_MODEL), lambda i: (i // per_seq if mod.shape[0] > 1 else 0, 0, 0)),
            pl.BlockSpec(w_b.shape, lambda i: (0, 0)),
            pl.BlockSpec(ws.shape, lambda i: (0, 0, 0)),
            pl.BlockSpec(bias_full.shape, lambda i: (0, 0)),
        ],
        out_specs=pl.BlockSpec((rows, BRANCH_WIDTH), lambda i: (i, 0)),
        compiler_params=_cparams("arbitrary"),
        name="branch_b",
    )(x, mod, w_b, ws, bias_full)


def _mlstm_kernel(*refs, seq, has_init, proj_rows, group):
    if has_init:
        (x_ref, mod_ref, w_ref, gb_ref, gn_ref, tril_ref, triu_ref, c0_ref, n0_ref, m0_ref,
         y_ref, q_s, k_s, v_s, o_s, g_s, hf_s, hb_s, c_s, n_s, m_s) = refs
    else:
        (x_ref, mod_ref, w_ref, gb_ref, gn_ref, tril_ref, triu_ref) = refs[:7]
        (y_ref, cout_ref, nout_ref, mout_ref,
         q_s, k_s, v_s, o_s, g_s, hf_s, hb_s, c_s, n_s, m_s) = refs[-14:]
    n_chunk = seq // CHUNK
    for r0 in range(0, group * seq, proj_rows):
        rows = pl.ds(r0, proj_rows)
        h = _modulated(x_ref[rows, :], mod_ref, 0).astype(BF16)
        p = _dot(h, w_ref[...])
        q_s[rows, :] = p[:, 0:512].astype(BF16)
        k_s[rows, :] = (p[:, 512:1024] * (C_HEAD_DIM ** -0.5)).astype(BF16)
        v_s[rows, :] = p[:, 1024:1536].astype(BF16)
        o_s[rows, :] = p[:, 1536:2048]
        g_s[rows, :] = p[:, 2048:2176] + gb_ref[...]
    if has_init:
        c_s[...] = c0_ref[...]
        n_s[...] = n0_ref[...]
        m_s[...] = m0_ref[...]
    else:
        c_s[...] = jnp.zeros_like(c_s)
        n_s[...] = jnp.zeros_like(n_s)
        m_s[...] = jnp.zeros_like(m_s)

    tril = tril_ref[...]
    triu = triu_ref[...]
    row_i = lax.broadcasted_iota(jnp.int32, (CHUNK, CHUNK), 0)
    col_i = lax.broadcasted_iota(jnp.int32, (CHUNK, CHUNK), 1)
    masks = (col_i <= row_i, col_i >= row_i)

    def chunk_step(c, carry):
        for member, direction in [(g, d) for g in range(group) for d in range(2)]:
            cc = member * n_chunk + (c if direction == 0 else n_chunk - 1 - c)
            rows = pl.ds(pl.multiple_of(cc * CHUNK, CHUNK), CHUNK)
            gates = g_s[rows, :]
            logf = jnp.minimum(gates, 0.0) - jnp.log1p(jnp.exp(-jnp.abs(gates)))
            tri_col = tril if direction == 0 else triu
            tri_row = triu if direction == 0 else tril
            b_col_all = jnp.dot(tri_col, logf, precision=HIGHEST, preferred_element_type=F32)
            gates_t = gates.T
            b_row_all = jnp.dot(logf.T, tri_row, precision=HIGHEST, preferred_element_type=F32)
            last = CHUNK - 1 if direction == 0 else 0
            h_out = hf_s if direction == 0 else hb_s
            for hd in range(C_HEADS):
                ic = direction * 8 + hd
                fc = direction * 8 + 4 + hd
                sidx = member * 8 + direction * 4 + hd
                cols = slice(hd * 128, (hd + 1) * 128)
                b_col = b_col_all[:, fc:fc + 1]
                b_row = b_row_all[fc:fc + 1, :]
                i_row = gates_t[ic:ic + 1, :]
                i_col = gates[:, ic:ic + 1]
                m_prev = m_s[sidx:sidx + 1, 0:1]
                qh = q_s[rows, cols]
                kh = k_s[rows, cols]
                vh = v_s[rows, cols]
                dlog = jnp.where(masks[direction], b_col - b_row + i_row, NEG_INF)
                m_t = jnp.maximum(jnp.max(dlog, axis=-1, keepdims=True), b_col + m_prev)
                w = _dot_nt(qh, kh) * jnp.exp(dlog - m_t)
                inter = jnp.exp(b_col + m_prev - m_t)
                c_prev = c_s[sidx]
                n_prev = n_s[sidx:sidx + 1, :]
                num = _dot(w.astype(BF16), vh) + inter * _dot(qh, c_prev.astype(BF16))
                den = (jnp.sum(w, axis=-1, keepdims=True)
                       + inter * jnp.sum(qh.astype(F32) * n_prev, axis=-1, keepdims=True))
                h_out[rows, cols] = num / jnp.maximum(jnp.abs(den), jnp.exp(-m_t))
                b_last = b_col[last:last + 1, :]
                g_col = b_last - b_col + i_col
                m_new = jnp.maximum(b_last + m_prev, jnp.max(g_col, axis=0, keepdims=True))
                decay = jnp.exp(b_last + m_prev - m_new)
                ksc = kh.astype(F32) * jnp.exp(g_col - m_new)
                c_s[sidx] = decay * c_prev + _dot(ksc.T.astype(BF16), vh)
                n_s[sidx:sidx + 1, :] = decay * n_prev + jnp.sum(ksc, axis=0, keepdims=True)
                m_s[sidx:sidx + 1, :] = jnp.broadcast_to(m_new, (1, 128))
        return carry

    lax.fori_loop(0, n_chunk, chunk_step, 0)

    for r0 in range(0, group * seq, proj_rows):
        rows = pl.ds(r0, proj_rows)
        hsum = hf_s[rows, :] + hb_s[rows, :]
        gate = jax.nn.sigmoid(o_s[rows, :])
        for hd in range(C_HEADS):
            cols = slice(hd * 128, (hd + 1) * 128)
            hh = hsum[:, cols]
            mu = jnp.mean(hh, axis=-1, keepdims=True)
            hc = hh - mu
            var = jnp.mean(hc * hc, axis=-1, keepdims=True)
            y_ref[rows, cols] = (hc * lax.rsqrt(var + EPS) * gn_ref[:, cols] * gate[:, cols]).astype(BF16)
    if not has_init:
        for member in range(group):
            cout_ref[member] = c_s[member * 8:(member + 1) * 8]
            nout_ref[member] = n_s[member * 8:(member + 1) * 8, :]
            mout_ref[member] = m_s[member * 8:(member + 1) * 8, :]


def _mlstm(x, mod, w_c, gate_bias_row, gn_row, tril, triu, *, n_seq, seq, init=None, layer=0, carry=None, group=1):
    assert init is None or group == 1
    proj_rows = min(group * seq, 512)
    kern = functools.partial(_mlstm_kernel, seq=seq, has_init=init is not None, proj_rows=proj_rows, group=group)
    const = lambda s: (0, 0)
    in_specs = [
        pl.BlockSpec((group * seq, D_MODEL), lambda s: (s, 0)),
        pl.BlockSpec((None, 1, 6 * D_MODEL), lambda s: (s if mod.shape[0] > 1 else 0, 0, 0)),
        pl.BlockSpec(w_c.shape, const),
        pl.BlockSpec(gate_bias_row.shape, const),
        pl.BlockSpec(gn_row.shape, const),
        pl.BlockSpec(tril.shape, const),
        pl.BlockSpec(triu.shape, const),
    ]
    args = [x, mod, w_c, gate_bias_row, gn_row, tril, triu]
    n_tok = n_seq * seq
    y_shape = jax.ShapeDtypeStruct((n_tok, BRANCH_WIDTH), BF16)
    y_spec = pl.BlockSpec((group * seq, BRANCH_WIDTH), lambda s: (s, 0))
    if init is not None:
        c0, n0, m0 = init
        in_specs += [pl.BlockSpec((None, None, 8, 128, 128), lambda s: (s, layer, 0, 0, 0)),
                     pl.BlockSpec((None, None, 8, 128), lambda s: (s, layer, 0, 0)),
                     pl.BlockSpec((None, None, 8, 128), lambda s: (s, layer, 0, 0))]
        args += [c0, n0, m0]
        out_shape, out_specs, aliases = y_shape, y_spec, {}
    else:
        out_shape = (y_shape, jax.ShapeDtypeStruct((n_seq, DEPTH, 8, 128, 128), F32),
                     jax.ShapeDtypeStruct((n_seq, DEPTH, 8, 128), F32),
                     jax.ShapeDtypeStruct((n_seq, DEPTH, 8, 128), F32))
        out_specs = (y_spec, pl.BlockSpec((group, None, 8, 128, 128), lambda s: (s, layer, 0, 0, 0)),
                     pl.BlockSpec((group, None, 8, 128), lambda s: (s, layer, 0, 0)),
                     pl.BlockSpec((group, None, 8, 128), lambda s: (s, layer, 0, 0)))
        aliases = _carry_aliases(in_specs, args, carry, first_out=1)
    rows = group * seq
    return pl.pallas_call(
        kern, out_shape=out_shape, grid=(n_seq // group,), in_specs=in_specs, out_specs=out_specs,
        input_output_aliases=aliases,
        scratch_shapes=[pltpu.VMEM((rows, 512), BF16), pltpu.VMEM((rows, 512), BF16), pltpu.VMEM((rows, 512), BF16),
                        pltpu.VMEM((rows, 512), F32), pltpu.VMEM((rows, 128), F32),
                        pltpu.VMEM((rows, 512), F32), pltpu.VMEM((rows, 512), F32),
                        pltpu.VMEM((group * 8, 128, 128), F32), pltpu.VMEM((group * 8, 128), F32),
                        pltpu.VMEM((group * 8, 128), F32)],
        compiler_params=_cparams("arbitrary"),
        name="branch_c_lat" if init is not None else "branch_c_ctx",
    )(*args)


def _merge_kernel(x_ref, mod_ref, ya_ref, yb_ref, yc_ref, yd_ref, wg_ref, wbr_ref, wout_ref, lng_ref, lnb_ref,
                  o_ref):
    x = x_ref[...]
    h = _modulated(x, mod_ref, 0).astype(BF16)
    mix = None
    for n, y_ref in enumerate((ya_ref, yb_ref, yc_ref, yd_ref)):
        gate = jax.nn.sigmoid(_dot(h, wg_ref[:, n * D_MODEL:(n + 1) * D_MODEL]))
        term = gate * _dot(y_ref[...], wbr_ref[n])
        mix = term if mix is None else mix + term
    out = _dot(mix.astype(BF16), wout_ref[...])
    g1 = mod_ref[:, 2 * D_MODEL:3 * D_MODEL]
    o_ref[...] = _layer_norm_rows(ALPHA * x + g1 * out, lng_ref[...], lnb_ref[...])


def _merge(x, mod, ys, w_g, w_br, w_out, ln_g, ln_b, *, seq, rows):
    n_tok = x.shape[0]
    per_seq = seq // rows
    tok = lambda i: (i, 0)
    c2 = lambda i: (0, 0)
    return pl.pallas_call(
        _merge_kernel, out_shape=jax.ShapeDtypeStruct((n_tok, D_MODEL), F32),
        grid=(n_tok // rows,),
        in_specs=[
            pl.BlockSpec((rows, D_MODEL), tok),
            pl.BlockSpec((None, 1, 6 * D_MODEL), lambda i: (i // per_seq if mod.shape[0] > 1 else 0, 0, 0)),
            pl.BlockSpec((rows, BRANCH_WIDTH), tok), pl.BlockSpec((rows, BRANCH_WIDTH), tok),
            pl.BlockSpec((rows, BRANCH_WIDTH), tok), pl.BlockSpec((rows, BRANCH_WIDTH), tok),
            pl.BlockSpec(w_g.shape, c2), pl.BlockSpec(w_br.shape, lambda i: (0, 0, 0)),
            pl.BlockSpec(w_out.shape, c2), pl.BlockSpec(ln_g.shape, c2), pl.BlockSpec(ln_b.shape, c2),
        ],
        out_specs=pl.BlockSpec((rows, D_MODEL), tok),
        compiler_params=_cparams("arbitrary"),
        name="merge",
    )(x, mod, *ys, w_g, w_br, w_out, ln_g, ln_b)


_TAKEN = -(2.0 ** 127)


def _top16(s):
    cur = s
    vals = []
    for r in range(PEER_TOPK):
        mx = jnp.max(cur, axis=0, keepdims=True)
        cur = jnp.where(cur == mx, _TAKEN * (1.0 + r / 32.0), cur)
        vals.append(mx)
    rank = jnp.where(cur <= _TAKEN, cur * (32.0 / _TAKEN) - 31.0, float(PEER_TOPK + 1))
    return jnp.concatenate(vals, axis=0), rank


def _pair_tables():
    pairs = [(k1, k2) for k1 in range(PEER_TOPK) for k2 in range(PEER_TOPK // (k1 + 1))]
    n = 56
    sel_a = np.zeros((n, PEER_TOPK), np.float32)
    sel_b = np.zeros((n, PEER_TOPK), np.float32)
    pad = np.full((n, 1), NEG_INF, np.float32)
    for row, (k1, k2) in enumerate(pairs):
        sel_a[row, k1] = 1.0
        sel_b[row, k2] = 1.0
        pad[row, 0] = 0.0
    return jnp.asarray(sel_a), jnp.asarray(sel_b), jnp.asarray(pad), jnp.asarray(sel_a.T, BF16)


def _route_kernel(x_ref, mod_ref, wq_ref, keys_ref, sela_ref, selb_ref, pad_ref, ind_ref,
                  cnt_ref, e1_ref, r2_ref, e2_ref, h2_s, *, heads):
    @pl.when(pl.program_id(1) == 0)
    def _modulate():
        h2_s[...] = _modulated(x_ref[...], mod_ref, 1).astype(BF16)

    pick = lambda sel_ref, v: jnp.dot(sel_ref[...], v, precision=HIGHEST, preferred_element_type=F32)
    q_all = _dot(h2_s[...], wq_ref[...])
    for hd in range(heads):
        q = q_all[:, hd * PEER_QDIM:(hd + 1) * PEER_QDIM]
        s1 = _dot_nt(keys_ref[hd, 0].astype(BF16), q[:, :128].astype(BF16))
        s2 = _dot_nt(keys_ref[hd, 1].astype(BF16), q[:, 128:].astype(BF16))
        a, rank1 = _top16(s1)
        b, rank2 = _top16(s2)
        ea = jnp.exp(a - a[0:1])
        eb = jnp.exp(b - b[0:1])
        cand = pick(sela_ref, a) + pick(selb_ref, b) + pad_ref[...]
        gate = pick(sela_ref, ea) * pick(selb_ref, eb)
        cur = cand
        thr = None
        for _ in range(PEER_TOPK):
            thr = jnp.max(cur, axis=0, keepdims=True)
            cur = jnp.where(cur == thr, NEG_INF, cur)
        chosen = cand >= thr
        z = jnp.sum(jnp.where(chosen, gate, 0.0), axis=0, keepdims=True)
        cnt_sorted = _dot(ind_ref[...], jnp.where(chosen, 1.0, 0.0).astype(BF16))
        cnt = jnp.zeros_like(s1)
        for r in range(PEER_TOPK):
            cnt = jnp.where(rank1 == float(r + 1), cnt_sorted[r:r + 1], cnt)
        cnt_ref[hd] = cnt
        e1_ref[hd] = jnp.where(rank1 <= float(PEER_TOPK), jnp.exp(s1 - a[0:1]) * (0.5 / z), 0.0)
        r2_ref[hd] = rank2.astype(BF16)
        e2_ref[hd] = jnp.where(rank2 <= float(PEER_TOPK), jnp.exp(s2 - b[0:1]), 0.0).astype(BF16)


def _route(x1, mod, wq, keys, *, seq, cols, heads):
    n_tok = x1.shape[0]
    per_seq = seq // cols
    tables = _pair_tables()
    row_shape = jax.ShapeDtypeStruct((PEER_HEADS, PEER_NKEYS, n_tok), F32)
    col_shape = jax.ShapeDtypeStruct((PEER_HEADS, PEER_NKEYS, n_tok), BF16)
    spec = pl.BlockSpec((heads, PEER_NKEYS, cols), lambda i, h: (h, 0, i))
    return pl.pallas_call(
        functools.partial(_route_kernel, heads=heads), out_shape=(row_shape, row_shape, col_shape, col_shape),
        grid=(n_tok // cols, PEER_HEADS // heads),
        in_specs=[
            pl.BlockSpec((cols, D_MODEL), lambda i, h: (i, 0)),
            pl.BlockSpec((None, 1, 6 * D_MODEL), lambda i, h: (i // per_seq if mod.shape[0] > 1 else 0, 0, 0)),
            pl.BlockSpec((D_MODEL, heads * PEER_QDIM), lambda i, h: (0, h)),
            pl.BlockSpec((heads, 2, PEER_NKEYS, PEER_QDIM // 2), lambda i, h: (h, 0, 0, 0)),
        ] + [pl.BlockSpec(t.shape, lambda i, h: (0, 0)) for t in tables],
        out_specs=(spec, spec, spec, spec),
        scratch_shapes=[pltpu.VMEM((cols, D_MODEL), BF16)],
        compiler_params=_cparams("arbitrary", "arbitrary"),
        name="peer_route",
    )(x1, mod, wq, keys, *tables)


def _peer_kernel(x_ref, mod_ref, u_ref, v_ref, cnt_ref, e1_ref, r2_ref, e2_ref, lng_ref, lnb_ref,
                 o_ref, h2t_s, acc_s, *, key_rows):
    e = pl.program_id(1)

    @pl.when(e == 0)
    def _init():
        h2 = _modulated(x_ref[...], mod_ref, 1)
        h2t_s[...] = h2.T.astype(BF16)
        acc_s[...] = jnp.zeros_like(acc_s)

    n_tok = h2t_s.shape[1]
    first_key = pl.multiple_of(e * key_rows, key_rows)
    zero = jnp.zeros((PEER_NKEYS // 16, 16, n_tok), BF16)
    act = _dot(u_ref[...], h2t_s[...]).astype(BF16)
    act = act * (1.0 + lax.erf(act * (2.0 ** -0.5)))
    pieces = []
    for r in range(key_rows):
        g = None
        for hd in range(PEER_HEADS):
            cnt_blk = cnt_ref[hd, pl.ds(first_key, key_rows), :]
            e1_blk = e1_ref[hd, pl.ds(first_key, key_rows), :]
            cnt_rows = jnp.broadcast_to(cnt_blk[r:r + 1, :], (16, n_tok)).astype(BF16)
            e1_rows = jnp.broadcast_to(e1_blk[r:r + 1, :], (16, n_tok)).astype(BF16)
            term = jnp.where(r2_ref[hd] <= cnt_rows[None], e2_ref[hd], zero) * e1_rows[None]
            g = term if g is None else g + term
        pieces.append(g.reshape(PEER_NKEYS, n_tok) * act[r * PEER_NKEYS:(r + 1) * PEER_NKEYS, :])
    acc_s[...] += lax.dot_general(v_ref[...], jnp.concatenate(pieces, axis=0), (((0,), (0,)), ((), ())),
                                  preferred_element_type=F32)

    @pl.when(e == pl.num_programs(1) - 1)
    def _finish():
        x = x_ref[...]
        g2 = mod_ref[:, 5 * D_MODEL:6 * D_MODEL]
        o_ref[...] = _layer_norm_rows(ALPHA * x + g2 * acc_s[...].T, lng_ref[...], lnb_ref[...])


def _peer(x1, mod, u, v, route, ln_g, ln_b, *, seq, cols, key_rows):
    n_tok = x1.shape[0]
    per_seq = seq // cols
    n_exp = key_rows * PEER_NKEYS
    n_tiles = PEER_EXPERTS // n_exp
    assert key_rows % 8 == 0
    kern = functools.partial(_peer_kernel, key_rows=key_rows)
    cnt, e1, r2, e2 = route
    r2 = r2.reshape(PEER_HEADS, PEER_NKEYS // 16, 16, n_tok)
    e2 = e2.reshape(PEER_HEADS, PEER_NKEYS // 16, 16, n_tok)
    rspec = pl.BlockSpec((PEER_HEADS, PEER_NKEYS, cols), lambda i, e: (0, 0, i))
    cspec = pl.BlockSpec((PEER_HEADS, PEER_NKEYS // 16, 16, cols), lambda i, e: (0, 0, 0, i))
    return pl.pallas_call(
        kern, out_shape=jax.ShapeDtypeStruct((n_tok, D_MODEL), F32),
        grid=(n_tok // cols, n_tiles),
        in_specs=[
            pl.BlockSpec((cols, D_MODEL), lambda i, e: (i, 0)),
            pl.BlockSpec((None, 1, 6 * D_MODEL), lambda i, e: (i // per_seq if mod.shape[0] > 1 else 0, 0, 0)),
            pl.BlockSpec((n_exp, D_MODEL), lambda i, e: (e, 0)),
            pl.BlockSpec((n_exp, D_MODEL), lambda i, e: (e, 0)),
            rspec, rspec, cspec, cspec,
            pl.BlockSpec(ln_g.shape, lambda i, e: (0, 0)), pl.BlockSpec(ln_b.shape, lambda i, e: (0, 0)),
        ],
        out_specs=pl.BlockSpec((cols, D_MODEL), lambda i, e: (i, 0)),
        scratch_shapes=[pltpu.VMEM((D_MODEL, cols), BF16), pltpu.VMEM((D_MODEL, cols), F32)],
        compiler_params=_cparams("arbitrary", "arbitrary"),
        name="peer_experts",
    )(x1, mod, u, v, cnt, e1, r2, e2, ln_g, ln_b)


def _rope_tables(seq):
    t = np.arange(seq)
    pos = np.stack([t // GRID_W, t % GRID_W], axis=1).astype(np.float64)
    inv = ROPE_BASE ** (-np.arange(16, dtype=np.float64) / 16)
    lane = np.arange(64)
    ang = pos[:, lane // 32] * inv[lane % 16][None, :]
    sign = np.where((lane % 32) < 16, -1.0, 1.0)[None, :]
    cos = np.tile(np.cos(ang), (1, 2)).astype(np.float32)
    sin = np.tile(np.sin(ang) * sign, (1, 2)).astype(np.float32)
    return jnp.asarray(cos), jnp.asarray(sin)


def _static_tables():
    lane = np.arange(128)
    bd = (lane[:, None] // 64 == lane[None, :] // 64).astype(np.float32) / 64.0
    src = np.arange(128)
    dst = np.arange(512)
    rep = ((src[:, None] // 64 == dst[None, :] // 256) & (src[:, None] % 64 == dst[None, :] % 64))
    idx = np.arange(CHUNK)
    tril = (idx[None, :] <= idx[:, None]).astype(np.float32)
    triu = (idx[None, :] >= idx[:, None]).astype(np.float32)
    return (jnp.asarray(bd, BF16), jnp.asarray(rep.astype(np.float32), BF16), jnp.asarray(tril), jnp.asarray(triu))


def _layer_params(l, w_in, attn_qk_gain, gmlp_ws, gmlp_b, mlstm_gate_bias, mlstm_gn, diff_lambda, diff_gn,
                  w_branch, w_out, ln_g, ln_b, peer_wq, peer_keys, peer_u, peer_v):
    w = w_in[l]
    p = {}
    p["w_a"] = w[:, _OFF_A:_OFF_B].astype(BF16)
    p["w_b"] = w[:, _OFF_B:_OFF_C].astype(BF16)
    p["w_c"] = jnp.concatenate([w[:, _OFF_C:_OFF_D], jnp.zeros((D_MODEL, 112), F32)], axis=1).astype(BF16)
    p["w_d"] = w[:, _OFF_D:_OFF_G].astype(BF16)
    p["w_g"] = w[:, _OFF_G:].astype(BF16)
    gain = attn_qk_gain[l]
    p["gain_row"] = jnp.concatenate([jnp.tile(gain[0], A_HEADS), jnp.tile(gain[1], A_KV_HEADS)])[None, :]
    p["ws"] = gmlp_ws[l]
    p["bias_full"] = jnp.repeat(gmlp_b[l].T, 128, axis=1)
    p["gate_bias_row"] = jnp.concatenate([mlstm_gate_bias[l].reshape(16), jnp.zeros((112,), F32)])[None, :]
    p["mlstm_gn_row"] = mlstm_gn[l].reshape(1, BRANCH_WIDTH)
    p["lam"] = diff_lambda[l]
    p["diff_gn_row"] = diff_gn[l].reshape(1, BRANCH_WIDTH)
    p["w_br"] = w_branch[l].astype(BF16)
    p["w_out"] = w_out[l].astype(BF16)
    p["ln_g0"], p["ln_b0"] = ln_g[l, 0][None, :], ln_b[l, 0][None, :]
    p["ln_g1"], p["ln_b1"] = ln_g[l, 1][None, :], ln_b[l, 1][None, :]
    p["wq"] = peer_wq[l].astype(BF16)
    p["keys"] = peer_keys[l]
    p["u"] = peer_u[l].astype(BF16)
    p["v"] = peer_v[l].astype(BF16)
    return p


def _trunk_layer(x, mod, p, tabs, *, l, n_seq, seq, cfg, ctx_cache=None, prev_state=None):
    bd, rep, tril, triu = tabs
    lam_init = 0.8 - 0.6 * math.exp(-0.3 * l)
    state = None
    if ctx_cache is None:
        prev = (None, None, None) if prev_state is None else (prev_state[0:2], prev_state[2:4], prev_state[4:7])
        ya, nk, nv = _attn_a(x, mod, p["w_a"], p["gain_row"], bd, rep, n_seq=n_seq, seq=seq, tq=cfg["tq"],
                             layer=l, carry=prev[0])
        yd, ndk, ndv = _attn_d(x, mod, p["w_d"], p["lam"], p["diff_gn_row"], n_seq=n_seq, seq=seq, tq=cfg["tq"],
                               lam_init=lam_init, layer=l, carry=prev[1])
        yc, c_new, n_new, m_new = _mlstm(x, mod, p["w_c"], p["gate_bias_row"], p["mlstm_gn_row"], tril, triu,
                                         n_seq=n_seq, seq=seq, layer=l, carry=prev[2])
        state = (nk, nv, ndk, ndv, c_new, n_new, m_new)
    else:
        rope, cak, cav, cdk, cdv, c0, n0, m0 = ctx_cache
        ya = _attn_a(x, mod, p["w_a"], p["gain_row"], bd, rep, n_seq=n_seq, seq=seq, tq=cfg["tq"],
                     rope=rope, cache=(cak, cav), layer=l)
        yd = _attn_d(x, mod, p["w_d"], p["lam"], p["diff_gn_row"], n_seq=n_seq, seq=seq, tq=cfg["tq"],
                     lam_init=lam_init, rope=rope, cache=(cdk, cdv), layer=l)
        yc = _mlstm(x, mod, p["w_c"], p["gate_bias_row"], p["mlstm_gn_row"], tril, triu,
                    n_seq=n_seq, seq=seq, init=(c0, n0, m0), layer=l)
    yb = _gmlp(x, mod, p["w_b"], p["ws"], p["bias_full"], seq=seq, rows=cfg["rows"])
    x1 = _merge(x, mod, (ya, yb, yc, yd), p["w_g"], p["w_br"], p["w_out"], p["ln_g0"], p["ln_b0"],
                seq=seq, rows=cfg["rows"])
    route = _route(x1, mod, p["wq"], p["keys"], seq=seq, cols=cfg["route_cols"], heads=cfg["route_heads"])
    x2 = _peer(x1, mod, p["u"], p["v"], route, p["ln_g1"], p["ln_b1"], seq=seq, cols=cfg["cols"],
               key_rows=cfg["key_rows"])
    return x2, state


def kernel(x_prompt, x_sample, cache_a_k, cache_a_v, cache_d_k, cache_d_v, state_c_C, state_c_n, state_c_m,
           c, c_ctx, w_mod, b_mod, w_in, attn_qk_gain, gmlp_ws, gmlp_b, mlstm_gate_bias, mlstm_gn,
           diff_lambda, diff_gn, w_branch, w_out, ln_g, ln_b, peer_wq, peer_keys, peer_u, peer_v):
    batch, seq, _ = x_prompt.shape
    dec_batch, dec_seq, _ = x_sample.shape
    past = cache_a_k.shape[2]
    c_rows = jnp.concatenate([c_ctx[None, :], c, jnp.zeros((8 - 1 - dec_batch, D_MODEL), F32)], axis=0)
    mods = _modulation(c_rows, w_mod, b_mod)
    tabs = _static_tables()
    rope = _rope_tables(dec_seq)
    cak = cache_a_k.reshape(dec_batch, DEPTH, past, 128)
    cav = cache_a_v.reshape(dec_batch, DEPTH, past, 128)
    cdk = cache_d_k.reshape(dec_batch, DEPTH, past, 512)
    cdv = cache_d_v.reshape(dec_batch, DEPTH, past, 512)
    c0 = state_c_C.reshape(dec_batch, DEPTH, 8, 128, 128)
    n0 = state_c_n.reshape(dec_batch, DEPTH, 8, 128)
    m0 = jnp.broadcast_to(state_c_m.reshape(dec_batch, DEPTH, 8, 1), (dec_batch, DEPTH, 8, 128))
    cfg_ctx = dict(tq=seq, rows=512, cols=512, key_rows=16, route_cols=512, route_heads=4)
    cfg_lat = dict(tq=min(256, dec_seq), rows=min(512, dec_seq), cols=512, key_rows=16, route_cols=512,
                   route_heads=4)
    y_p = x_prompt.reshape(batch * seq, D_MODEL)
    y_s = x_sample.reshape(dec_batch * dec_seq, D_MODEL)
    state = None
    for l in range(DEPTH):
        p = _layer_params(l, w_in, attn_qk_gain, gmlp_ws, gmlp_b, mlstm_gate_bias, mlstm_gn, diff_lambda, diff_gn,
                          w_branch, w_out, ln_g, ln_b, peer_wq, peer_keys, peer_u, peer_v)
        mod_ctx = mods[l, 0:1].reshape(1, 1, 6 * D_MODEL)
        mod_lat = mods[l, 1:1 + dec_batch].reshape(dec_batch, 1, 6 * D_MODEL)
        y_p, state = _trunk_layer(y_p, mod_ctx, p, tabs, l=l, n_seq=batch, seq=seq, cfg=cfg_ctx, prev_state=state)
        y_s, _ = _trunk_layer(y_s, mod_lat, p, tabs, l=l, n_seq=dec_batch, seq=dec_seq, cfg=cfg_lat,
                              ctx_cache=(rope, cak, cav, cdk, cdv, c0, n0, m0))
    nk = state[0].reshape(batch, DEPTH, seq, A_KV_HEADS, A_HEAD_DIM)
    nv = state[1].reshape(batch, DEPTH, seq, A_KV_HEADS, A_HEAD_DIM)
    ndk = state[2].reshape(batch, DEPTH, seq, D_HEADS, 2, D_HALF_DIM)
    ndv = state[3].reshape(batch, DEPTH, seq, D_HEADS, D_VDIM)
    nc = state[4].reshape(batch, DEPTH, 2, C_HEADS, C_HEAD_DIM, C_HEAD_DIM)
    nn = state[5].reshape(batch, DEPTH, 2, C_HEADS, C_HEAD_DIM)
    nm = state[6][:, :, :, 0].reshape(batch, DEPTH, 2, C_HEADS)
    return (y_p.reshape(batch, seq, D_MODEL), y_s.reshape(dec_batch, dec_seq, D_MODEL), nk, nv, ndk, ndv, nc, nn, nm)
```

```python
import functools
import math

import numpy as np
import jax
import jax.numpy as jnp
from jax import lax
from jax.experimental import pallas as pl
from jax.experimental.pallas import tpu as pltpu

F32 = jnp.float32
BF16 = jnp.bfloat16
HIGHEST = lax.Precision.HIGHEST

D_MODEL = 1024
DEPTH = 4
GRID_W = 64
ROPE_BASE = 10000.0
EPS = 1e-6
BRANCH_WIDTH = D_MODEL // 2
A_HEAD_DIM = 64
A_HEADS = 8
A_KV_HEADS = 2
B_GROUPS = 4
CHUNK = 128
C_HEADS = 4
C_HEAD_DIM = 128
D_HEADS = 4
D_VDIM = 128
D_HALF_DIM = 64
PEER_HEADS = 8
PEER_NKEYS = 128
PEER_EXPERTS = PEER_NKEYS * PEER_NKEYS
PEER_QDIM = 256
PEER_TOPK = 16
ALPHA = (2 * DEPTH) ** 0.25

_OFF_A = 0
_OFF_B = 768
_OFF_C = 1792
_OFF_CG = 3840
_OFF_D = 3856
_OFF_G = 5392
_C_WIDTH = 4 * BRANCH_WIDTH + 128

VMEM_LIMIT_BYTES = 56 * 1024 * 1024
NEG_INF = float("-inf")


def _cparams(*sem, flags=None):
    return pltpu.CompilerParams(dimension_semantics=sem, vmem_limit_bytes=VMEM_LIMIT_BYTES, flags=flags)


def _dot(a, b):
    return jnp.dot(a, b, preferred_element_type=F32)


def _dot_nt(a, b):
    return lax.dot_general(a, b, (((1,), (1,)), ((), ())), preferred_element_type=F32)


def _modulated(x, mod_ref, which):
    base = 3 * D_MODEL * which
    sh = mod_ref[:, base:base + D_MODEL]
    sc = mod_ref[:, base + D_MODEL:base + 2 * D_MODEL]
    return x * (1.0 + sc) + sh


def _layer_norm_rows(z, g, b):
    mu = jnp.mean(z, axis=-1, keepdims=True)
    zc = z - mu
    var = jnp.mean(zc * zc, axis=-1, keepdims=True)
    return zc * lax.rsqrt(var + EPS) * g + b


def _rope(x, cos, sin_signed, lane):
    w = x.shape[1]
    nxt = pltpu.roll(x, w - 16, 1)
    prv = pltpu.roll(x, 16, 1)
    partner = jnp.where((lane % 32) < 16, nxt, prv)
    return x * cos + partner * sin_signed


def _tile_lanes(t, n):
    return t if n == 1 else jnp.concatenate([t] * n, axis=1)


def _mod_kernel(c_ref, w_ref, b_ref, o_ref):
    c = c_ref[...]
    s = c * jax.nn.sigmoid(c)
    o_ref[...] = jnp.dot(s, w_ref[...], precision=HIGHEST, preferred_element_type=F32) + b_ref[...]


def _modulation(c_rows, w_mod, b_mod):
    n_col = 6 * D_MODEL // 1024
    return pl.pallas_call(
        _mod_kernel,
        out_shape=jax.ShapeDtypeStruct((DEPTH, 8, 6 * D_MODEL), F32),
        grid=(DEPTH, n_col),
        in_specs=[
            pl.BlockSpec((8, D_MODEL), lambda l, j: (0, 0)),
            pl.BlockSpec((None, D_MODEL, 1024), lambda l, j: (l, 0, j)),
            pl.BlockSpec((None, 1, 1024), lambda l, j: (l, 0, j)),
        ],
        out_specs=pl.BlockSpec((None, 8, 1024), lambda l, j: (l, 0, j)),
        compiler_params=_cparams("arbitrary", "arbitrary"),
        name="modulation",
    )(c_rows, w_mod, b_mod.reshape(DEPTH, 1, 6 * D_MODEL))


def _attn_a_kernel(*refs, seq, tq, n_cache, rope, proj_rows):
    if rope:
        (x_ref, mod_ref, w_ref, gain_ref, bd_ref, rep_ref, cos_ref, sin_ref, ck_ref, cv_ref,
         y_ref, q_s, k_s, v_s) = refs
    else:
        (x_ref, mod_ref, w_ref, gain_ref, bd_ref, rep_ref) = refs[:6]
        (y_ref, nk_ref, nv_ref, q_s, k_s, v_s) = refs[-6:]
    qi = pl.program_id(1)

    @pl.when(qi == 0)
    def _project():
        lane = lax.broadcasted_iota(jnp.int32, (1, 640), 1)
        for r0 in range(0, seq, proj_rows):
            rows = pl.ds(r0, proj_rows)
            h = _modulated(x_ref[rows, :], mod_ref, 0).astype(BF16)
            p = _dot(h, w_ref[...])
            qk = p[:, :640]
            sq = qk * qk
            hi = sq.astype(BF16)
            lo = (sq - hi.astype(F32)).astype(BF16)
            ms = jnp.concatenate([_dot(hi[:, c:c + 128], bd_ref[...]) + _dot(lo[:, c:c + 128], bd_ref[...])
                                  for c in range(0, 640, 128)], axis=1)
            qk = qk * lax.rsqrt(ms + EPS) * gain_ref[...]
            v = p[:, 640:768]
            if not rope:
                nk_ref[rows, :] = qk[:, 512:640]
                nv_ref[rows, :] = v
            else:
                cos = _tile_lanes(cos_ref[rows, :], 5)
                sin = _tile_lanes(sin_ref[rows, :], 5)
                qk = _rope(qk, cos, sin, lane)
            q_s[rows, :] = (qk[:, :512] * (A_HEAD_DIM ** -0.5)).astype(BF16)
            k_s[rows, :] = _dot(qk[:, 512:640].astype(BF16), rep_ref[...]).astype(BF16)
            v_s[rows, :] = _dot(v.astype(BF16), rep_ref[...]).astype(BF16)
        if n_cache:
            crow = pl.ds(seq, n_cache)
            k_s[crow, :] = _dot(ck_ref[...].astype(BF16), rep_ref[...]).astype(BF16)
            v_s[crow, :] = _dot(cv_ref[...].astype(BF16), rep_ref[...]).astype(BF16)

    head_of_lane = lax.broadcasted_iota(jnp.int32, (1, 256), 1) // A_HEAD_DIM
    qb = q_s[pl.ds(pl.multiple_of(qi * tq, tq), tq), :]
    for g in range(A_KV_HEADS):
        cols = slice(g * 256, (g + 1) * 256)
        qg = qb[:, cols]
        kg = k_s[:, cols]
        vg = v_s[:, cols]
        acc = jnp.zeros((tq, 256), F32)
        for r in range(A_HEADS // A_KV_HEADS):
            sel = head_of_lane == r
            qm = jnp.where(sel, qg, jnp.zeros_like(qg))
            s = _dot_nt(qm, kg)
            m = jnp.max(s, axis=-1, keepdims=True)
            p = jnp.exp(s - m)
            l = jnp.sum(p, axis=-1, keepdims=True)
            o = _dot(p.astype(BF16), vg)
            acc = acc + jnp.where(sel, o * (1.0 / l), 0.0)
        y_ref[:, cols] = acc.astype(BF16)


def _carry_aliases(in_specs, args, carry, first_out):
    if carry is None:
        return {}
    aliases = {}
    for k, arr in enumerate(carry):
        aliases[len(args)] = first_out + k
        in_specs.append(pl.BlockSpec(memory_space=pl.ANY))
        args.append(arr)
    return aliases


def _attn_a(x, mod, w_a, gain_row, bd, rep, *, n_seq, seq, tq, rope=None, cache=None, layer=0, carry=None):
    n_q = seq // tq
    n_cache = 0 if cache is None else cache[0].shape[2]
    proj_rows = min(seq, 512)
    kern = functools.partial(_attn_a_kernel, seq=seq, tq=tq, n_cache=n_cache, rope=rope is not None,
                             proj_rows=proj_rows)
    const = lambda s, q: (0, 0)
    in_specs = [
        pl.BlockSpec((seq, D_MODEL), lambda s, q: (s, 0)),
        pl.BlockSpec((None, 1, 6 * D_MODEL), lambda s, q: (s if mod.shape[0] > 1 else 0, 0, 0)),
        pl.BlockSpec(w_a.shape, const),
        pl.BlockSpec(gain_row.shape, const),
        pl.BlockSpec(bd.shape, const),
        pl.BlockSpec(rep.shape, const),
    ]
    args = [x, mod, w_a, gain_row, bd, rep]
    n_tok = n_seq * seq
    y_spec = pl.BlockSpec((tq, BRANCH_WIDTH), lambda s, q: (s * n_q + q, 0))
    y_shape = jax.ShapeDtypeStruct((n_tok, BRANCH_WIDTH), BF16)
    if rope is not None:
        cos, sin = rope
        in_specs += [pl.BlockSpec(cos.shape, const), pl.BlockSpec(sin.shape, const),
                     pl.BlockSpec((None, None, n_cache, 128), lambda s, q: (s, layer, 0, 0)),
                     pl.BlockSpec((None, None, n_cache, 128), lambda s, q: (s, layer, 0, 0))]
        args += [cos, sin, cache[0], cache[1]]
        out_shape, out_specs, aliases = y_shape, y_spec, {}
    else:
        kv_shape = jax.ShapeDtypeStruct((n_seq, DEPTH, seq, 128), F32)
        kv_spec = pl.BlockSpec((None, None, seq, 128), lambda s, q: (s, layer, 0, 0))
        out_shape, out_specs = (y_shape, kv_shape, kv_shape), (y_spec, kv_spec, kv_spec)
        aliases = _carry_aliases(in_specs, args, carry, first_out=1)
    return pl.pallas_call(
        kern, out_shape=out_shape, grid=(n_seq, n_q), in_specs=in_specs, out_specs=out_specs,
        input_output_aliases=aliases,
        scratch_shapes=[pltpu.VMEM((seq, 512), BF16), pltpu.VMEM((seq + n_cache, 512), BF16),
                        pltpu.VMEM((seq + n_cache, 512), BF16)],
        compiler_params=_cparams("arbitrary", "arbitrary"),
        name="branch_a_lat" if rope is not None else "branch_a_ctx",
    )(*args)


def _attn_d_kernel(*refs, seq, tq, n_cache, rope, proj_rows, lam_init):
    if rope:
        (x_ref, mod_ref, w_ref, lam_ref, gn_ref, cos_ref, sin_ref, ck_ref, cv_ref,
         y_ref, q_s, k_s, v_s) = refs
    else:
        (x_ref, mod_ref, w_ref, lam_ref, gn_ref) = refs[:5]
        (y_ref, nk_ref, nv_ref, q_s, k_s, v_s) = refs[-6:]
    qi = pl.program_id(1)

    @pl.when(qi == 0)
    def _project():
        lane = lax.broadcasted_iota(jnp.int32, (1, 512), 1)
        for r0 in range(0, seq, proj_rows):
            rows = pl.ds(r0, proj_rows)
            h = _modulated(x_ref[rows, :], mod_ref, 0).astype(BF16)
            p = _dot(h, w_ref[...])
            dq, dk, dv = p[:, :512], p[:, 512:1024], p[:, 1024:1536]
            if not rope:
                nk_ref[rows, :] = dk
                nv_ref[rows, :] = dv
            else:
                cos = _tile_lanes(cos_ref[rows, :], 4)
                sin = _tile_lanes(sin_ref[rows, :], 4)
                dq = _rope(dq, cos, sin, lane)
                dk = _rope(dk, cos, sin, lane)
            q_s[rows, :] = (dq * (D_HALF_DIM ** -0.5)).astype(BF16)
            k_s[rows, :] = dk.astype(BF16)
            v_s[rows, :] = dv.astype(BF16)
        if n_cache:
            crow = pl.ds(seq, n_cache)
            k_s[crow, :] = ck_ref[...].astype(BF16)
            v_s[crow, :] = cv_ref[...].astype(BF16)

    lv = lam_ref[...]
    lam = (jnp.exp(jnp.sum(lv[0:1] * lv[1:2], axis=-1, keepdims=True))
           - jnp.exp(jnp.sum(lv[2:3] * lv[3:4], axis=-1, keepdims=True)) + lam_init)
    half_of_lane = lax.broadcasted_iota(jnp.int32, (1, 128), 1) // D_HALF_DIM
    qb = q_s[pl.ds(pl.multiple_of(qi * tq, tq), tq), :]
    for hd in range(D_HEADS):
        cols = slice(hd * 128, (hd + 1) * 128)
        qh = qb[:, cols]
        kh = k_s[:, cols]
        vh = v_s[:, cols]
        probs = []
        for j in range(2):
            qm = jnp.where(half_of_lane == j, qh, jnp.zeros_like(qh))
            s = _dot_nt(qm, kh)
            m = jnp.max(s, axis=-1, keepdims=True)
            p = jnp.exp(s - m)
            l = jnp.sum(p, axis=-1, keepdims=True)
            probs.append(p * ((1.0 if j == 0 else lam) / l))
        a = (probs[0] - probs[1]).astype(BF16)
        o = _dot(a, vh)
        ms = jnp.mean(o * o, axis=-1, keepdims=True)
        o = o * lax.rsqrt(ms + EPS) * gn_ref[:, cols] * (1.0 - lam_init)
        y_ref[:, cols] = o.astype(BF16)


def _attn_d(x, mod, w_d, lam_params, gn_row, *, n_seq, seq, tq, lam_init, rope=None, cache=None, layer=0,
            carry=None):
    n_q = seq // tq
    n_cache = 0 if cache is None else cache[0].shape[2]
    proj_rows = min(seq, 512)
    kern = functools.partial(_attn_d_kernel, seq=seq, tq=tq, n_cache=n_cache, rope=rope is not None,
                             proj_rows=proj_rows, lam_init=lam_init)
    const = lambda s, q: (0, 0)
    in_specs = [
        pl.BlockSpec((seq, D_MODEL), lambda s, q: (s, 0)),
        pl.BlockSpec((None, 1, 6 * D_MODEL), lambda s, q: (s if mod.shape[0] > 1 else 0, 0, 0)),
        pl.BlockSpec(w_d.shape, const),
        pl.BlockSpec(lam_params.shape, const),
        pl.BlockSpec(gn_row.shape, const),
    ]
    args = [x, mod, w_d, lam_params, gn_row]
    n_tok = n_seq * seq
    y_spec = pl.BlockSpec((tq, BRANCH_WIDTH), lambda s, q: (s * n_q + q, 0))
    y_shape = jax.ShapeDtypeStruct((n_tok, BRANCH_WIDTH), BF16)
    if rope is not None:
        cos, sin = rope
        in_specs += [pl.BlockSpec(cos.shape, const), pl.BlockSpec(sin.shape, const),
                     pl.BlockSpec((None, None, n_cache, 512), lambda s, q: (s, layer, 0, 0)),
                     pl.BlockSpec((None, None, n_cache, 512), lambda s, q: (s, layer, 0, 0))]
        args += [cos, sin, cache[0], cache[1]]
        out_shape, out_specs, aliases = y_shape, y_spec, {}
    else:
        kv_shape = jax.ShapeDtypeStruct((n_seq, DEPTH, seq, 512), F32)
        kv_spec = pl.BlockSpec((None, None, seq, 512), lambda s, q: (s, layer, 0, 0))
        out_shape, out_specs = (y_shape, kv_shape, kv_shape), (y_spec, kv_spec, kv_spec)
        aliases = _carry_aliases(in_specs, args, carry, first_out=1)
    return pl.pallas_call(
        kern, out_shape=out_shape, grid=(n_seq, n_q), in_specs=in_specs, out_specs=out_specs,
        input_output_aliases=aliases,
        scratch_shapes=[pltpu.VMEM((seq, 512), BF16), pltpu.VMEM((seq + n_cache, 512), BF16),
                        pltpu.VMEM((seq + n_cache, 512), BF16)],
        compiler_params=_cparams("arbitrary", "arbitrary"),
        name="branch_d_lat" if rope is not None else "branch_d_ctx",
    )(*args)


def _gmlp_kernel(x_ref, mod_ref, w_ref, ws_ref, bias_ref, y_ref, *, rows):
    h = _modulated(x_ref[...], mod_ref, 0).astype(BF16)
    p = _dot(h, w_ref[...])
    u, v = p[:, :BRANCH_WIDTH], p[:, BRANCH_WIDTH:]
    mu = jnp.mean(v, axis=-1, keepdims=True)
    vc = v - mu
    var = jnp.mean(vc * vc, axis=-1, keepdims=True)
    vn = (vc * lax.rsqrt(var + EPS)).astype(BF16)
    for c in range(rows // CHUNK):
        rs = slice(c * CHUNK, (c + 1) * CHUNK)
        for g in range(B_GROUPS):
            cs = slice(g * 128, (g + 1) * 128)
            s = _dot(ws_ref[g].astype(BF16), vn[rs, cs]) + bias_ref[:, cs]
            y_ref[rs, cs] = (u[rs, cs] * s).astype(BF16)


def _gmlp(x, mod, w_b, ws, bias_full, *, seq, rows):
    n_tok = x.shape[0]
    per_seq = seq // rows
    kern = functools.partial(_gmlp_kernel, rows=rows)
    return pl.pallas_call(
        kern, out_shape=jax.ShapeDtypeStruct((n_tok, BRANCH_WIDTH), BF16),
        grid=(n_tok // rows,),
        in_specs=[
            pl.BlockSpec((rows, D_MODEL), lambda i: (i, 0)),
            pl.BlockSpec((None, 1, 6 * D_MODEL), lambda i: (i // per_seq if mod.shape[0] > 1 else 0, 0, 0)),
            pl.BlockSpec(w_b.shape, lambda i: (0, 0)),
            pl.BlockSpec(ws.shape, lambda i: (0, 0, 0)),
            pl.BlockSpec(bias_full.shape, lambda i: (0, 0)),
        ],
        out_specs=pl.BlockSpec((rows, BRANCH_WIDTH), lambda i: (i, 0)),
        compiler_params=_cparams("arbitrary"),
        name="branch_b",
    )(x, mod, w_b, ws, bias_full)


def _mlstm_kernel(*refs, seq, has_init, proj_rows, group):
    if has_init:
        (x_ref, mod_ref, w_ref, gb_ref, gn_ref, tril_ref, triu_ref, c0_ref, n0_ref, m0_ref,
         y_ref, q_s, k_s, v_s, o_s, g_s, hf_s, hb_s, c_s, n_s, m_s) = refs
    else:
        (x_ref, mod_ref, w_ref, gb_ref, gn_ref, tril_ref, triu_ref) = refs[:7]
        (y_ref, cout_ref, nout_ref, mout_ref,
         q_s, k_s, v_s, o_s, g_s, hf_s, hb_s, c_s, n_s, m_s) = refs[-14:]
    n_chunk = seq // CHUNK
    for r0 in range(0, group * seq, proj_rows):
        rows = pl.ds(r0, proj_rows)
        h = _modulated(x_ref[rows, :], mod_ref, 0).astype(BF16)
        p = _dot(h, w_ref[...])
        q_s[rows, :] = p[:, 0:512].astype(BF16)
        k_s[rows, :] = (p[:, 512:1024] * (C_HEAD_DIM ** -0.5)).astype(BF16)
        v_s[rows, :] = p[:, 1024:1536].astype(BF16)
        o_s[rows, :] = p[:, 1536:2048]
        g_s[rows, :] = p[:, 2048:2176] + gb_ref[...]
    if has_init:
        c_s[...] = c0_ref[...]
        n_s[...] = n0_ref[...]
        m_s[...] = m0_ref[...]
    else:
        c_s[...] = jnp.zeros_like(c_s)
        n_s[...] = jnp.zeros_like(n_s)
        m_s[...] = jnp.zeros_like(m_s)

    tril = tril_ref[...]
    triu = triu_ref[...]
    row_i = lax.broadcasted_iota(jnp.int32, (CHUNK, CHUNK), 0)
    col_i = lax.broadcasted_iota(jnp.int32, (CHUNK, CHUNK), 1)
    masks = (col_i <= row_i, col_i >= row_i)

    def chunk_step(c, carry):
        for member, direction in [(g, d) for g in range(group) for d in range(2)]:
            cc = member * n_chunk + (c if direction == 0 else n_chunk - 1 - c)
            rows = pl.ds(pl.multiple_of(cc * CHUNK, CHUNK), CHUNK)
            gates = g_s[rows, :]
            logf = jnp.minimum(gates, 0.0) - jnp.log1p(jnp.exp(-jnp.abs(gates)))
            tri_col = tril if direction == 0 else triu
            tri_row = triu if direction == 0 else tril
            b_col_all = jnp.dot(tri_col, logf, precision=HIGHEST, preferred_element_type=F32)
            gates_t = gates.T
            b_row_all = jnp.dot(logf.T, tri_row, precision=HIGHEST, preferred_element_type=F32)
            last = CHUNK - 1 if direction == 0 else 0
            h_out = hf_s if direction == 0 else hb_s
            for hd in range(C_HEADS):
                ic = direction * 8 + hd
                fc = direction * 8 + 4 + hd
                sidx = member * 8 + direction * 4 + hd
                cols = slice(hd * 128, (hd + 1) * 128)
                b_col = b_col_all[:, fc:fc + 1]
                b_row = b_row_all[fc:fc + 1, :]
                i_row = gates_t[ic:ic + 1, :]
                i_col = gates[:, ic:ic + 1]
                m_prev = m_s[sidx:sidx + 1, 0:1]
                qh = q_s[rows, cols]
                kh = k_s[rows, cols]
                vh = v_s[rows, cols]
                dlog = jnp.where(masks[direction], b_col - b_row + i_row, NEG_INF)
                m_t = jnp.maximum(jnp.max(dlog, axis=-1, keepdims=True), b_col + m_prev)
                w = _dot_nt(qh, kh) * jnp.exp(dlog - m_t)
                inter = jnp.exp(b_col + m_prev - m_t)
                c_prev = c_s[sidx]
                n_prev = n_s[sidx:sidx + 1, :]
                num = _dot(w.astype(BF16), vh) + inter * _dot(qh, c_prev.astype(BF16))
                den = (jnp.sum(w, axis=-1, keepdims=True)
                       + inter * jnp.sum(qh.astype(F32) * n_prev, axis=-1, keepdims=True))
                h_out[rows, cols] = num / jnp.maximum(jnp.abs(den), jnp.exp(-m_t))
                b_last = b_col[last:last + 1, :]
                g_col = b_last - b_col + i_col
                m_new = jnp.maximum(b_last + m_prev, jnp.max(g_col, axis=0, keepdims=True))
                decay = jnp.exp(b_last + m_prev - m_new)
                ksc = kh.astype(F32) * jnp.exp(g_col - m_new)
                c_s[sidx] = decay * c_prev + _dot(ksc.T.astype(BF16), vh)
                n_s[sidx:sidx + 1, :] = decay * n_prev + jnp.sum(ksc, axis=0, keepdims=True)
                m_s[sidx:sidx + 1, :] = jnp.broadcast_to(m_new, (1, 128))
        return carry

    lax.fori_loop(0, n_chunk, chunk_step, 0)

    for r0 in range(0, group * seq, proj_rows):
        rows = pl.ds(r0, proj_rows)
        hsum = hf_s[rows, :] + hb_s[rows, :]
        gate = jax.nn.sigmoid(o_s[rows, :])
        for hd in range(C_HEADS):
            cols = slice(hd * 128, (hd + 1) * 128)
            hh = hsum[:, cols]
            mu = jnp.mean(hh, axis=-1, keepdims=True)
            hc = hh - mu
            var = jnp.mean(hc * hc, axis=-1, keepdims=True)
            y_ref[rows, cols] = (hc * lax.rsqrt(var + EPS) * gn_ref[:, cols] * gate[:, cols]).astype(BF16)
    if not has_init:
        for member in range(group):
            cout_ref[member] = c_s[member * 8:(member + 1) * 8]
            nout_ref[member] = n_s[member * 8:(member + 1) * 8, :]
            mout_ref[member] = m_s[member * 8:(member + 1) * 8, :]


def _mlstm(x, mod, w_c, gate_bias_row, gn_row, tril, triu, *, n_seq, seq, init=None, layer=0, carry=None, group=1):
    assert init is None or group == 1
    proj_rows = min(group * seq, 512)
    kern = functools.partial(_mlstm_kernel, seq=seq, has_init=init is not None, proj_rows=proj_rows, group=group)
    const = lambda s: (0, 0)
    in_specs = [
        pl.BlockSpec((group * seq, D_MODEL), lambda s: (s, 0)),
        pl.BlockSpec((None, 1, 6 * D_MODEL), lambda s: (s if mod.shape[0] > 1 else 0, 0, 0)),
        pl.BlockSpec(w_c.shape, const),
        pl.BlockSpec(gate_bias_row.shape, const),
        pl.BlockSpec(gn_row.shape, const),
        pl.BlockSpec(tril.shape, const),
        pl.BlockSpec(triu.shape, const),
    ]
    args = [x, mod, w_c, gate_bias_row, gn_row, tril, triu]
    n_tok = n_seq * seq
    y_shape = jax.ShapeDtypeStruct((n_tok, BRANCH_WIDTH), BF16)
    y_spec = pl.BlockSpec((group * seq, BRANCH_WIDTH), lambda s: (s, 0))
    if init is not None:
        c0, n0, m0 = init
        in_specs += [pl.BlockSpec((None, None, 8, 128, 128), lambda s: (s, layer, 0, 0, 0)),
                     pl.BlockSpec((None, None, 8, 128), lambda s: (s, layer, 0, 0)),
                     pl.BlockSpec((None, None, 8, 128), lambda s: (s, layer, 0, 0))]
        args += [c0, n0, m0]
        out_shape, out_specs, aliases = y_shape, y_spec, {}
    else:
        out_shape = (y_shape, jax.ShapeDtypeStruct((n_seq, DEPTH, 8, 128, 128), F32),
                     jax.ShapeDtypeStruct((n_seq, DEPTH, 8, 128), F32),
                     jax.ShapeDtypeStruct((n_seq, DEPTH, 8, 128), F32))
        out_specs = (y_spec, pl.BlockSpec((group, None, 8, 128, 128), lambda s: (s, layer, 0, 0, 0)),
                     pl.BlockSpec((group, None, 8, 128), lambda s: (s, layer, 0, 0)),
                     pl.BlockSpec((group, None, 8, 128), lambda s: (s, layer, 0, 0)))
        aliases = _carry_aliases(in_specs, args, carry, first_out=1)
    rows = group * seq
    return pl.pallas_call(
        kern, out_shape=out_shape, grid=(n_seq // group,), in_specs=in_specs, out_specs=out_specs,
        input_output_aliases=aliases,
        scratch_shapes=[pltpu.VMEM((rows, 512), BF16), pltpu.VMEM((rows, 512), BF16), pltpu.VMEM((rows, 512), BF16),
                        pltpu.VMEM((rows, 512), F32), pltpu.VMEM((rows, 128), F32),
                        pltpu.VMEM((rows, 512), F32), pltpu.VMEM((rows, 512), F32),
                        pltpu.VMEM((group * 8, 128, 128), F32), pltpu.VMEM((group * 8, 128), F32),
                        pltpu.VMEM((group * 8, 128), F32)],
        compiler_params=_cparams("arbitrary"),
        name="branch_c_lat" if init is not None else "branch_c_ctx",
    )(*args)


def _mlstm_t_kernel(*refs, seq, has_init, proj_rows):
    if has_init:
        (x_ref, mod_ref, wn_ref, wt_ref, gbr_ref, gbt_ref, gnt_ref, tril_ref, triu_ref, c0_ref, n0_ref, m0_ref,
         y_ref, q_s, k_s, qt_s, vt_s, ot_s, g_s, gt_s, hf_s, hb_s, st_s, m_s) = refs
    else:
        (x_ref, mod_ref, wn_ref, wt_ref, gbr_ref, gbt_ref, gnt_ref, tril_ref, triu_ref) = refs[:9]
        (y_ref, cout_ref, nout_ref, mout_ref,
         q_s, k_s, qt_s, vt_s, ot_s, g_s, gt_s, hf_s, hb_s, st_s, m_s) = refs[-15:]
    n_chunk = seq // CHUNK
    for r0 in range(0, seq, proj_rows):
        span = pl.ds(r0, proj_rows)
        h = _modulated(x_ref[span, :], mod_ref, 0).astype(BF16)
        p = _dot(h, wn_ref[...])
        q_s[span, :] = p[:, 0:512].astype(BF16)
        k_s[span, :] = (p[:, 512:1024] * (C_HEAD_DIM ** -0.5)).astype(BF16)
        g_s[span, :] = p[:, 1024:1152] + gbr_ref[...]
        pt = _dot_nt(wt_ref[...], h)
        qt_s[:, span] = pt[0:512].astype(BF16)
        vt_s[:, span] = pt[512:1024].astype(BF16)
        ot_s[:, span] = jax.nn.sigmoid(pt[1024:1536]).astype(BF16)
        gt_s[:, span] = pt[1536:1664] + _tile_lanes(gbt_ref[...], proj_rows // 128)

    first_row = lax.broadcasted_iota(jnp.int32, (CHUNK, CHUNK), 0) == 0
    for sidx in range(2 * C_HEADS):
        if has_init:
            st_s[sidx, 0:128, :] = c0_ref[sidx].T
            st_s[sidx, 128:256, :] = jnp.where(first_row, jnp.broadcast_to(n0_ref[sidx:sidx + 1, :], (CHUNK, 128)), 0.0)
        else:
            st_s[sidx] = jnp.zeros((2 * CHUNK, 128), F32)
    m_s[...] = m0_ref[...] if has_init else jnp.zeros_like(m_s)

    tril = tril_ref[...]
    triu = triu_ref[...]
    row_i = lax.broadcasted_iota(jnp.int32, (CHUNK, CHUNK), 0)
    col_i = lax.broadcasted_iota(jnp.int32, (CHUNK, CHUNK), 1)
    visible = (row_i <= col_i, row_i >= col_i)
    lane_i = lax.broadcasted_iota(jnp.int32, (1, CHUNK), 1)
    ones_blk = jnp.where(first_row, 1.0, 0.0).astype(BF16)

    def chunk_step(c, carry):
        for direction in range(2):
            cc = c if direction == 0 else n_chunk - 1 - c
            span = pl.ds(pl.multiple_of(cc * CHUNK, CHUNK), CHUNK)
            gates = g_s[span, :]
            gates_t = gt_s[:, span]
            logf = jnp.minimum(gates, 0.0) - jnp.log1p(jnp.exp(-jnp.abs(gates)))
            logf_t = jnp.minimum(gates_t, 0.0) - jnp.log1p(jnp.exp(-jnp.abs(gates_t)))
            tri_col = tril if direction == 0 else triu
            tri_row = triu if direction == 0 else tril
            b_col_all = jnp.dot(tri_col, logf, precision=HIGHEST, preferred_element_type=F32)
            b_row_all = jnp.dot(logf_t, tri_row, precision=HIGHEST, preferred_element_type=F32)
            r_col_all = gates - pltpu.roll(b_col_all, 128 - 4, 1)
            base = direction * 8
            i_rows = gates_t[base:base + 4, :]
            b_rows = b_row_all[base + 4:base + 8, :]
            last = CHUNK - 1 if direction == 0 else 0
            h_out = hf_s if direction == 0 else hb_s
            for hd in range(C_HEADS):
                sidx = direction * 4 + hd
                blk = slice(hd * 128, (hd + 1) * 128)
                r_col = r_col_all[:, base + hd:base + hd + 1]
                b_row = b_rows[hd:hd + 1, :]
                i_row = i_rows[hd:hd + 1, :]
                m_prev = m_s[sidx:sidx + 1, 0:1]
                r_wide = jnp.broadcast_to(r_col, (CHUNK, CHUNK))
                peak = jnp.max(jnp.where(visible[direction], r_wide, NEG_INF), axis=0, keepdims=True)
                m_row = b_row + jnp.maximum(peak, m_prev)
                qh = q_s[span, blk]
                kh = k_s[span, blk]
                arg = jnp.where(visible[direction], r_wide + (b_row - m_row), NEG_INF)
                w_t = (_dot_nt(kh, qh) * jnp.exp(arg)).astype(BF16)
                inter = jnp.exp(b_row + m_prev - m_row)
                v_one = jnp.concatenate([vt_s[blk, span], ones_blk], axis=0)
                state = st_s[sidx]
                both = _dot(v_one, w_t) + inter * _dot(state.astype(BF16), qt_s[blk, span])
                den = both[128:129, :]
                h_out[blk, span] = both[0:128, :] / jnp.maximum(jnp.abs(den), jnp.exp(-m_row))
                b_last = b_row[:, last:last + 1]
                g_row = b_last - b_row + i_row
                m_new = jnp.maximum(b_last + m_prev, jnp.max(g_row, axis=-1, keepdims=True))
                decay = jnp.exp(b_last + m_prev - m_new)
                scaled = v_one * jnp.exp(g_row - m_new).astype(BF16)
                st_s[sidx] = decay * state + _dot(scaled, kh)
                m_s[sidx:sidx + 1, :] = jnp.broadcast_to(m_new, (1, 128))
        return carry

    lax.fori_loop(0, n_chunk, chunk_step, 0)

    for r0 in range(0, seq, proj_rows):
        span = pl.ds(r0, proj_rows)
        for hd in range(C_HEADS):
            blk = slice(hd * 128, (hd + 1) * 128)
            hh = hf_s[blk, span] + hb_s[blk, span]
            mu = jnp.mean(hh, axis=0, keepdims=True)
            hc = hh - mu
            var = jnp.mean(hc * hc, axis=0, keepdims=True)
            gn = _tile_lanes(gnt_ref[blk, :], proj_rows // 128)
            y_t = hc * lax.rsqrt(var + EPS) * gn * ot_s[blk, span].astype(F32)
            y_ref[span, blk] = y_t.T.astype(BF16)
    if not has_init:
        for sidx in range(2 * C_HEADS):
            cout_ref[sidx] = st_s[sidx, 0:128, :].T
            nout_ref[sidx:sidx + 1, :] = st_s[sidx, 128:129, :]
        mout_ref[...] = m_s[...]


def _mlstm_t(x, mod, w_nat, w_t, gate_bias_row, gate_bias_t, gn_t, tril, triu, *, n_seq, seq, init=None, layer=0,
             carry=None):
    proj_rows = min(seq, 512)
    kern = functools.partial(_mlstm_t_kernel, seq=seq, has_init=init is not None, proj_rows=proj_rows)
    const = lambda s: (0, 0)
    in_specs = [
        pl.BlockSpec((seq, D_MODEL), lambda s: (s, 0)),
        pl.BlockSpec((None, 1, 6 * D_MODEL), lambda s: (s if mod.shape[0] > 1 else 0, 0, 0)),
    ] + [pl.BlockSpec(a.shape, const) for a in (w_nat, w_t, gate_bias_row, gate_bias_t, gn_t, tril, triu)]
    args = [x, mod, w_nat, w_t, gate_bias_row, gate_bias_t, gn_t, tril, triu]
    n_tok = n_seq * seq
    y_shape = jax.ShapeDtypeStruct((n_tok, BRANCH_WIDTH), BF16)
    y_spec = pl.BlockSpec((seq, BRANCH_WIDTH), lambda s: (s, 0))
    if init is not None:
        c0, n0, m0 = init
        in_specs += [pl.BlockSpec((None, None, 8, 128, 128), lambda s: (s, layer, 0, 0, 0)),
                     pl.BlockSpec((None, None, 8, 128), lambda s: (s, layer, 0, 0)),
                     pl.BlockSpec((None, None, 8, 128), lambda s: (s, layer, 0, 0))]
        args += [c0, n0, m0]
        out_shape, out_specs, aliases = y_shape, y_spec, {}
    else:
        out_shape = (y_shape, jax.ShapeDtypeStruct((n_seq, DEPTH, 8, 128, 128), F32),
                     jax.ShapeDtypeStruct((n_seq, DEPTH, 8, 128), F32),
                     jax.ShapeDtypeStruct((n_seq, DEPTH, 8, 128), F32))
        out_specs = (y_spec, pl.BlockSpec((None, None, 8, 128, 128), lambda s: (s, layer, 0, 0, 0)),
                     pl.BlockSpec((None, None, 8, 128), lambda s: (s, layer, 0, 0)),
                     pl.BlockSpec((None, None, 8, 128), lambda s: (s, layer, 0, 0)))
        aliases = _carry_aliases(in_specs, args, carry, first_out=1)
    return pl.pallas_call(
        kern, out_shape=out_shape, grid=(n_seq,), in_specs=in_specs, out_specs=out_specs,
        input_output_aliases=aliases,
        scratch_shapes=[pltpu.VMEM((seq, 512), BF16), pltpu.VMEM((seq, 512), BF16),
                        pltpu.VMEM((512, seq), BF16), pltpu.VMEM((512, seq), BF16),
                        pltpu.VMEM((512, seq), BF16),
                        pltpu.VMEM((seq, 128), F32), pltpu.VMEM((128, seq), F32),
                        pltpu.VMEM((512, seq), F32), pltpu.VMEM((512, seq), F32),
                        pltpu.VMEM((8, 256, 128), F32), pltpu.VMEM((8, 128), F32)],
        compiler_params=_cparams("arbitrary"),
        name="branch_c_lat" if init is not None else "branch_c_ctx",
    )(*args)


def _merge_kernel(x_ref, mod_ref, ya_ref, yb_ref, yc_ref, yd_ref, wg_ref, wbr_ref, wout_ref, lng_ref, lnb_ref,
                  o_ref):
    x = x_ref[...]
    h = _modulated(x, mod_ref, 0).astype(BF16)
    mix = None
    for n, y_ref in enumerate((ya_ref, yb_ref, yc_ref, yd_ref)):
        gate = jax.nn.sigmoid(_dot(h, wg_ref[:, n * D_MODEL:(n + 1) * D_MODEL]))
        term = gate * _dot(y_ref[...], wbr_ref[n])
        mix = term if mix is None else mix + term
    out = _dot(mix.astype(BF16), wout_ref[...])
    g1 = mod_ref[:, 2 * D_MODEL:3 * D_MODEL]
    o_ref[...] = _layer_norm_rows(ALPHA * x + g1 * out, lng_ref[...], lnb_ref[...])


def _merge(x, mod, ys, w_g, w_br, w_out, ln_g, ln_b, *, seq, rows):
    n_tok = x.shape[0]
    per_seq = seq // rows
    tok = lambda i: (i, 0)
    c2 = lambda i: (0, 0)
    return pl.pallas_call(
        _merge_kernel, out_shape=jax.ShapeDtypeStruct((n_tok, D_MODEL), F32),
        grid=(n_tok // rows,),
        in_specs=[
            pl.BlockSpec((rows, D_MODEL), tok),
            pl.BlockSpec((None, 1, 6 * D_MODEL), lambda i: (i // per_seq if mod.shape[0] > 1 else 0, 0, 0)),
            pl.BlockSpec((rows, BRANCH_WIDTH), tok), pl.BlockSpec((rows, BRANCH_WIDTH), tok),
            pl.BlockSpec((rows, BRANCH_WIDTH), tok), pl.BlockSpec((rows, BRANCH_WIDTH), tok),
            pl.BlockSpec(w_g.shape, c2), pl.BlockSpec(w_br.shape, lambda i: (0, 0, 0)),
            pl.BlockSpec(w_out.shape, c2), pl.BlockSpec(ln_g.shape, c2), pl.BlockSpec(ln_b.shape, c2),
        ],
        out_specs=pl.BlockSpec((rows, D_MODEL), tok),
        compiler_params=_cparams("arbitrary"),
        name="merge",
    )(x, mod, *ys, w_g, w_br, w_out, ln_g, ln_b)


_TAKEN = -(2.0 ** 127)


def _top16(s):
    cur = s
    vals = []
    for r in range(PEER_TOPK):
        mx = jnp.max(cur, axis=0, keepdims=True)
        cur = jnp.where(cur == mx, _TAKEN * (1.0 + r / 32.0), cur)
        vals.append(mx)
    rank = jnp.where(cur <= _TAKEN, cur * (32.0 / _TAKEN) - 31.0, float(PEER_TOPK + 1))
    return jnp.concatenate(vals, axis=0), rank


def _pair_tables():
    pairs = [(k1, k2) for k1 in range(PEER_TOPK) for k2 in range(PEER_TOPK // (k1 + 1))]
    n = 56
    sel_a = np.zeros((n, PEER_TOPK), np.float32)
    sel_b = np.zeros((n, PEER_TOPK), np.float32)
    pad = np.full((n, 1), NEG_INF, np.float32)
    for row, (k1, k2) in enumerate(pairs):
        sel_a[row, k1] = 1.0
        sel_b[row, k2] = 1.0
        pad[row, 0] = 0.0
    return jnp.asarray(sel_a), jnp.asarray(sel_b), jnp.asarray(pad), jnp.asarray(sel_a.T, BF16)


def _route_kernel(x_ref, mod_ref, wq_ref, keys_ref, sela_ref, selb_ref, pad_ref, ind_ref,
                  cnt_ref, e1_ref, r2_ref, e2_ref, h2_s, *, heads):
    @pl.when(pl.program_id(1) == 0)
    def _modulate():
        h2_s[...] = _modulated(x_ref[...], mod_ref, 1).astype(BF16)

    pick = lambda sel_ref, v: jnp.dot(sel_ref[...], v, precision=HIGHEST, preferred_element_type=F32)
    q_all = _dot(h2_s[...], wq_ref[...])
    for hd in range(heads):
        q = q_all[:, hd * PEER_QDIM:(hd + 1) * PEER_QDIM]
        s1 = _dot_nt(keys_ref[hd, 0].astype(BF16), q[:, :128].astype(BF16))
        s2 = _dot_nt(keys_ref[hd, 1].astype(BF16), q[:, 128:].astype(BF16))
        a, rank1 = _top16(s1)
        b, rank2 = _top16(s2)
        ea = jnp.exp(a - a[0:1])
        eb = jnp.exp(b - b[0:1])
        cand = pick(sela_ref, a) + pick(selb_ref, b) + pad_ref[...]
        gate = pick(sela_ref, ea) * pick(selb_ref, eb)
        cur = cand
        thr = None
        for _ in range(PEER_TOPK):
            thr = jnp.max(cur, axis=0, keepdims=True)
            cur = jnp.where(cur == thr, NEG_INF, cur)
        chosen = cand >= thr
        z = jnp.sum(jnp.where(chosen, gate, 0.0), axis=0, keepdims=True)
        cnt_sorted = _dot(ind_ref[...], jnp.where(chosen, 1.0, 0.0).astype(BF16))
        cnt = jnp.zeros_like(s1)
        for r in range(PEER_TOPK):
            cnt = jnp.where(rank1 == float(r + 1), cnt_sorted[r:r + 1], cnt)
        cnt_ref[hd] = cnt
        e1_ref[hd] = jnp.where(rank1 <= float(PEER_TOPK), jnp.exp(s1 - a[0:1]) * (0.5 / z), 0.0)
        packed = (PEER_NKEYS // 16, 16, s2.shape[1])
        r2_ref[hd] = rank2.astype(BF16).reshape(packed)
        e2_ref[hd] = jnp.where(rank2 <= float(PEER_TOPK), jnp.exp(s2 - b[0:1]), 0.0).astype(BF16).reshape(packed)


def _route(x1, mod, wq, keys, *, seq, cols, heads):
    n_tok = x1.shape[0]
    per_seq = seq // cols
    tables = _pair_tables()
    row_shape = jax.ShapeDtypeStruct((PEER_HEADS, PEER_NKEYS, n_tok), F32)
    col_shape = jax.ShapeDtypeStruct((PEER_HEADS, PEER_NKEYS // 16, 16, n_tok), BF16)
    col_spec = pl.BlockSpec((heads, PEER_NKEYS // 16, 16, cols), lambda i, h: (h, 0, 0, i))
    spec = pl.BlockSpec((heads, PEER_NKEYS, cols), lambda i, h: (h, 0, i))
    return pl.pallas_call(
        functools.partial(_route_kernel, heads=heads), out_shape=(row_shape, row_shape, col_shape, col_shape),
        grid=(n_tok // cols, PEER_HEADS // heads),
        in_specs=[
            pl.BlockSpec((cols, D_MODEL), lambda i, h: (i, 0)),
            pl.BlockSpec((None, 1, 6 * D_MODEL), lambda i, h: (i // per_seq if mod.shape[0] > 1 else 0, 0, 0)),
            pl.BlockSpec((D_MODEL, heads * PEER_QDIM), lambda i, h: (0, h)),
            pl.BlockSpec((heads, 2, PEER_NKEYS, PEER_QDIM // 2), lambda i, h: (h, 0, 0, 0)),
        ] + [pl.BlockSpec(t.shape, lambda i, h: (0, 0)) for t in tables],
        out_specs=(spec, spec, col_spec, col_spec),
        scratch_shapes=[pltpu.VMEM((cols, D_MODEL), BF16)],
        compiler_params=_cparams("arbitrary", "arbitrary"),
        name="peer_route",
    )(x1, mod, wq, keys, *tables)


def _peer_kernel(x_ref, mod_ref, u_ref, v_ref, cnt_ref, e1_ref, r2_ref, e2_ref, lng_ref, lnb_ref,
                 o_ref, h2t_s, acc_s, *, key_rows):
    e = pl.program_id(1)

    @pl.when(e == 0)
    def _init():
        h2 = _modulated(x_ref[...], mod_ref, 1)
        h2t_s[...] = h2.T.astype(BF16)
        acc_s[...] = jnp.zeros_like(acc_s)

    n_tok = h2t_s.shape[1]
    first_key = pl.multiple_of(e * key_rows, key_rows)
    zero = jnp.zeros((PEER_NKEYS // 16, 16, n_tok), BF16)
    act = _dot(u_ref[...], h2t_s[...]).astype(BF16)
    act = act * (1.0 + lax.erf(act * (2.0 ** -0.5)))
    pieces = []
    for r in range(key_rows):
        g = None
        for hd in range(PEER_HEADS):
            cnt_blk = cnt_ref[hd, pl.ds(first_key, key_rows), :]
            e1_blk = e1_ref[hd, pl.ds(first_key, key_rows), :]
            cnt_rows = jnp.broadcast_to(cnt_blk[r:r + 1, :], (16, n_tok)).astype(BF16)
            e1_rows = jnp.broadcast_to(e1_blk[r:r + 1, :], (16, n_tok)).astype(BF16)
            term = jnp.where(r2_ref[hd] <= cnt_rows[None], e2_ref[hd], zero) * e1_rows[None]
            g = term if g is None else g + term
        pieces.append(g.reshape(PEER_NKEYS, n_tok) * act[r * PEER_NKEYS:(r + 1) * PEER_NKEYS, :])
    acc_s[...] += lax.dot_general(v_ref[...], jnp.concatenate(pieces, axis=0), (((0,), (0,)), ((), ())),
                                  preferred_element_type=F32)

    @pl.when(e == pl.num_programs(1) - 1)
    def _finish():
        x = x_ref[...]
        g2 = mod_ref[:, 5 * D_MODEL:6 * D_MODEL]
        o_ref[...] = _layer_norm_rows(ALPHA * x + g2 * acc_s[...].T, lng_ref[...], lnb_ref[...])


def _peer(x1, mod, u, v, route, ln_g, ln_b, *, seq, cols, key_rows):
    n_tok = x1.shape[0]
    per_seq = seq // cols
    n_exp = key_rows * PEER_NKEYS
    n_tiles = PEER_EXPERTS // n_exp
    assert key_rows % 8 == 0
    kern = functools.partial(_peer_kernel, key_rows=key_rows)
    cnt, e1, r2, e2 = route
    rspec = pl.BlockSpec((PEER_HEADS, PEER_NKEYS, cols), lambda i, e: (0, 0, i))
    cspec = pl.BlockSpec((PEER_HEADS, PEER_NKEYS // 16, 16, cols), lambda i, e: (0, 0, 0, i))
    return pl.pallas_call(
        kern, out_shape=jax.ShapeDtypeStruct((n_tok, D_MODEL), F32),
        grid=(n_tok // cols, n_tiles),
        in_specs=[
            pl.BlockSpec((cols, D_MODEL), lambda i, e: (i, 0)),
            pl.BlockSpec((None, 1, 6 * D_MODEL), lambda i, e: (i // per_seq if mod.shape[0] > 1 else 0, 0, 0)),
            pl.BlockSpec((n_exp, D_MODEL), lambda i, e: (e, 0)),
            pl.BlockSpec((n_exp, D_MODEL), lambda i, e: (e, 0)),
            rspec, rspec, cspec, cspec,
            pl.BlockSpec(ln_g.shape, lambda i, e: (0, 0)), pl.BlockSpec(ln_b.shape, lambda i, e: (0, 0)),
        ],
        out_specs=pl.BlockSpec((cols, D_MODEL), lambda i, e: (i, 0)),
        scratch_shapes=[pltpu.VMEM((D_MODEL, cols), BF16), pltpu.VMEM((D_MODEL, cols), F32)],
        compiler_params=_cparams("arbitrary", "arbitrary"),
        name="peer_experts",
    )(x1, mod, u, v, cnt, e1, r2, e2, ln_g, ln_b)


def _rope_tables(seq):
    t = np.arange(seq)
    pos = np.stack([t // GRID_W, t % GRID_W], axis=1).astype(np.float64)
    inv = ROPE_BASE ** (-np.arange(16, dtype=np.float64) / 16)
    lane = np.arange(64)
    ang = pos[:, lane // 32] * inv[lane % 16][None, :]
    sign = np.where((lane % 32) < 16, -1.0, 1.0)[None, :]
    cos = np.tile(np.cos(ang), (1, 2)).astype(np.float32)
    sin = np.tile(np.sin(ang) * sign, (1, 2)).astype(np.float32)
    return jnp.asarray(cos), jnp.asarray(sin)


def _static_tables():
    lane = np.arange(128)
    bd = (lane[:, None] // 64 == lane[None, :] // 64).astype(np.float32) / 64.0
    src = np.arange(128)
    dst = np.arange(512)
    rep = ((src[:, None] // 64 == dst[None, :] // 256) & (src[:, None] % 64 == dst[None, :] % 64))
    idx = np.arange(CHUNK)
    tril = (idx[None, :] <= idx[:, None]).astype(np.float32)
    triu = (idx[None, :] >= idx[:, None]).astype(np.float32)
    return (jnp.asarray(bd, BF16), jnp.asarray(rep.astype(np.float32), BF16), jnp.asarray(tril), jnp.asarray(triu))


def _layer_params(l, w_in, attn_qk_gain, gmlp_ws, gmlp_b, mlstm_gate_bias, mlstm_gn, diff_lambda, diff_gn,
                  w_branch, w_out, ln_g, ln_b, peer_wq, peer_keys, peer_u, peer_v):
    w = w_in[l]
    p = {}
    p["w_a"] = w[:, _OFF_A:_OFF_B].astype(BF16)
    p["w_b"] = w[:, _OFF_B:_OFF_C].astype(BF16)
    cq, ck = w[:, _OFF_C:_OFF_C + 512], w[:, _OFF_C + 512:_OFF_C + 1024]
    cv, co = w[:, _OFF_C + 1024:_OFF_C + 1536], w[:, _OFF_C + 1536:_OFF_C + 2048]
    cg = jnp.concatenate([w[:, _OFF_CG:_OFF_D], jnp.zeros((D_MODEL, 112), F32)], axis=1)
    p["w_c_nat"] = jnp.concatenate([cq, ck, cg], axis=1).astype(BF16)
    p["w_c_t"] = jnp.concatenate([cq, cv, co, cg], axis=1).T.astype(BF16)
    p["w_d"] = w[:, _OFF_D:_OFF_G].astype(BF16)
    p["w_g"] = w[:, _OFF_G:].astype(BF16)
    gain = attn_qk_gain[l]
    p["gain_row"] = jnp.concatenate([jnp.tile(gain[0], A_HEADS), jnp.tile(gain[1], A_KV_HEADS)])[None, :]
    p["ws"] = gmlp_ws[l]
    p["bias_full"] = jnp.repeat(gmlp_b[l].T, 128, axis=1)
    p["gate_bias_row"] = jnp.concatenate([mlstm_gate_bias[l].reshape(16), jnp.zeros((112,), F32)])[None, :]
    p["gate_bias_t"] = jnp.broadcast_to(p["gate_bias_row"].reshape(128, 1), (128, 128))
    p["mlstm_gn_t"] = jnp.broadcast_to(mlstm_gn[l].reshape(BRANCH_WIDTH, 1), (BRANCH_WIDTH, 128))
    p["lam"] = diff_lambda[l]
    p["diff_gn_row"] = diff_gn[l].reshape(1, BRANCH_WIDTH)
    p["w_br"] = w_branch[l].astype(BF16)
    p["w_out"] = w_out[l].astype(BF16)
    p["ln_g0"], p["ln_b0"] = ln_g[l, 0][None, :], ln_b[l, 0][None, :]
    p["ln_g1"], p["ln_b1"] = ln_g[l, 1][None, :], ln_b[l, 1][None, :]
    p["wq"] = peer_wq[l].astype(BF16)
    p["keys"] = peer_keys[l]
    p["u"] = peer_u[l].astype(BF16)
    p["v"] = peer_v[l].astype(BF16)
    return p


def _trunk_layer(x, mod, p, tabs, *, l, n_seq, seq, cfg, ctx_cache=None, prev_state=None):
    bd, rep, tril, triu = tabs
    lam_init = 0.8 - 0.6 * math.exp(-0.3 * l)
    state = None
    if ctx_cache is None:
        prev = (None, None, None) if prev_state is None else (prev_state[0:2], prev_state[2:4], prev_state[4:7])
        ya, nk, nv = _attn_a(x, mod, p["w_a"], p["gain_row"], bd, rep, n_seq=n_seq, seq=seq, tq=cfg["tq"],
                             layer=l, carry=prev[0])
        yd, ndk, ndv = _attn_d(x, mod, p["w_d"], p["lam"], p["diff_gn_row"], n_seq=n_seq, seq=seq, tq=cfg["tq"],
                               lam_init=lam_init, layer=l, carry=prev[1])
        yc, c_new, n_new, m_new = _mlstm_t(x, mod, p["w_c_nat"], p["w_c_t"], p["gate_bias_row"], p["gate_bias_t"],
                                           p["mlstm_gn_t"], tril, triu, n_seq=n_seq, seq=seq, layer=l, carry=prev[2])
        state = (nk, nv, ndk, ndv, c_new, n_new, m_new)
    else:
        rope, cak, cav, cdk, cdv, c0, n0, m0 = ctx_cache
        ya = _attn_a(x, mod, p["w_a"], p["gain_row"], bd, rep, n_seq=n_seq, seq=seq, tq=cfg["tq"],
                     rope=rope, cache=(cak, cav), layer=l)
        yd = _attn_d(x, mod, p["w_d"], p["lam"], p["diff_gn_row"], n_seq=n_seq, seq=seq, tq=cfg["tq"],
                     lam_init=lam_init, rope=rope, cache=(cdk, cdv), layer=l)
        yc = _mlstm_t(x, mod, p["w_c_nat"], p["w_c_t"], p["gate_bias_row"], p["gate_bias_t"], p["mlstm_gn_t"],
                      tril, triu, n_seq=n_seq, seq=seq, init=(c0, n0, m0), layer=l)
    yb = _gmlp(x, mod, p["w_b"], p["ws"], p["bias_full"], seq=seq, rows=cfg["rows"])
    x1 = _merge(x, mod, (ya, yb, yc, yd), p["w_g"], p["w_br"], p["w_out"], p["ln_g0"], p["ln_b0"],
                seq=seq, rows=cfg["rows"])
    route = _route(x1, mod, p["wq"], p["keys"], seq=seq, cols=cfg["route_cols"], heads=cfg["route_heads"])
    x2 = _peer(x1, mod, p["u"], p["v"], route, p["ln_g1"], p["ln_b1"], seq=seq, cols=cfg["cols"],
               key_rows=cfg["key_rows"])
    return x2, state


def kernel(x_prompt, x_sample, cache_a_k, cache_a_v, cache_d_k, cache_d_v, state_c_C, state_c_n, state_c_m,
           c, c_ctx, w_mod, b_mod, w_in, attn_qk_gain, gmlp_ws, gmlp_b, mlstm_gate_bias, mlstm_gn,
           diff_lambda, diff_gn, w_branch, w_out, ln_g, ln_b, peer_wq, peer_keys, peer_u, peer_v):
    batch, seq, _ = x_prompt.shape
    dec_batch, dec_seq, _ = x_sample.shape
    past = cache_a_k.shape[2]
    c_rows = jnp.concatenate([c_ctx[None, :], c, jnp.zeros((8 - 1 - dec_batch, D_MODEL), F32)], axis=0)
    mods = _modulation(c_rows, w_mod, b_mod)
    tabs = _static_tables()
    rope = _rope_tables(dec_seq)
    cak = cache_a_k.reshape(dec_batch, DEPTH, past, 128)
    cav = cache_a_v.reshape(dec_batch, DEPTH, past, 128)
    cdk = cache_d_k.reshape(dec_batch, DEPTH, past, 512)
    cdv = cache_d_v.reshape(dec_batch, DEPTH, past, 512)
    c0 = state_c_C.reshape(dec_batch, DEPTH, 8, 128, 128)
    n0 = state_c_n.reshape(dec_batch, DEPTH, 8, 128)
    m0 = jnp.broadcast_to(state_c_m.reshape(dec_batch, DEPTH, 8, 1), (dec_batch, DEPTH, 8, 128))
    cfg_ctx = dict(tq=seq, rows=512, cols=512, key_rows=16, route_cols=512, route_heads=4)
    cfg_lat = dict(tq=min(256, dec_seq), rows=min(512, dec_seq), cols=512, key_rows=16, route_cols=512,
                   route_heads=4)
    y_p = x_prompt.reshape(batch * seq, D_MODEL)
    y_s = x_sample.reshape(dec_batch * dec_seq, D_MODEL)
    state = None
    for l in range(DEPTH):
        p = _layer_params(l, w_in, attn_qk_gain, gmlp_ws, gmlp_b, mlstm_gate_bias, mlstm_gn, diff_lambda, diff_gn,
                          w_branch, w_out, ln_g, ln_b, peer_wq, peer_keys, peer_u, peer_v)
        mod_ctx = mods[l, 0:1].reshape(1, 1, 6 * D_MODEL)
        mod_lat = mods[l, 1:1 + dec_batch].reshape(dec_batch, 1, 6 * D_MODEL)
        y_p, state = _trunk_layer(y_p, mod_ctx, p, tabs, l=l, n_seq=batch, seq=seq, cfg=cfg_ctx, prev_state=state)
        y_s, _ = _trunk_layer(y_s, mod_lat, p, tabs, l=l, n_seq=dec_batch, seq=dec_seq, cfg=cfg_lat,
                              ctx_cache=(rope, cak, cav, cdk, cdv, c0, n0, m0))
    nk = state[0].reshape(batch, DEPTH, seq, A_KV_HEADS, A_HEAD_DIM)
    nv = state[1].reshape(batch, DEPTH, seq, A_KV_HEADS, A_HEAD_DIM)
    ndk = state[2].reshape(batch, DEPTH, seq, D_HEADS, 2, D_HALF_DIM)
    ndv = state[3].reshape(batch, DEPTH, seq, D_HEADS, D_VDIM)
    nc = state[4].reshape(batch, DEPTH, 2, C_HEADS, C_HEAD_DIM, C_HEAD_DIM)
    nn = state[5].reshape(batch, DEPTH, 2, C_HEADS, C_HEAD_DIM)
    nm = state[6][:, :, :, 0].reshape(batch, DEPTH, 2, C_HEADS)
    return (y_p.reshape(batch, seq, D_MODEL), y_s.reshape(dec_batch, dec_seq, D_MODEL), nk, nv, ndk, ndv, nc, nn, nm)
```

```python
import functools
import math

import numpy as np
import jax
import jax.numpy as jnp
from jax import lax
from jax.experimental import pallas as pl
from jax.experimental.pallas import tpu as pltpu

F32 = jnp.float32
BF16 = jnp.bfloat16
HIGHEST = lax.Precision.HIGHEST

D_MODEL = 1024
DEPTH = 4
GRID_W = 64
ROPE_BASE = 10000.0
EPS = 1e-6
BRANCH_WIDTH = D_MODEL // 2
A_HEAD_DIM = 64
A_HEADS = 8
A_KV_HEADS = 2
B_GROUPS = 4
CHUNK = 128
C_HEADS = 4
C_HEAD_DIM = 128
D_HEADS = 4
D_VDIM = 128
D_HALF_DIM = 64
PEER_HEADS = 8
PEER_NKEYS = 128
PEER_EXPERTS = PEER_NKEYS * PEER_NKEYS
PEER_QDIM = 256
PEER_TOPK = 16
ALPHA = (2 * DEPTH) ** 0.25

_OFF_A = 0
_OFF_B = 768
_OFF_C = 1792
_OFF_CG = 3840
_OFF_D = 3856
_OFF_G = 5392

VMEM_LIMIT_BYTES = 56 * 1024 * 1024
NEG_INF = float("-inf")


def _cparams(*sem, flags=None):
    return pltpu.CompilerParams(dimension_semantics=sem, vmem_limit_bytes=VMEM_LIMIT_BYTES, flags=flags)


def _dot(a, b):
    return jnp.dot(a, b, preferred_element_type=F32)


def _dot_nt(a, b):
    return lax.dot_general(a, b, (((1,), (1,)), ((), ())), preferred_element_type=F32)


def _modulated(x, mod_ref, which):
    base = 3 * D_MODEL * which
    sh = mod_ref[:, base:base + D_MODEL]
    sc = mod_ref[:, base + D_MODEL:base + 2 * D_MODEL]
    return x * (1.0 + sc) + sh


def _layer_norm_rows(z, g, b):
    mu = jnp.mean(z, axis=-1, keepdims=True)
    zc = z - mu
    var = jnp.mean(zc * zc, axis=-1, keepdims=True)
    return zc * lax.rsqrt(var + EPS) * g + b


def _rope(x, cos, sin_signed, lane):
    w = x.shape[1]
    nxt = pltpu.roll(x, w - 16, 1)
    prv = pltpu.roll(x, 16, 1)
    partner = jnp.where((lane % 32) < 16, nxt, prv)
    return x * cos + partner * sin_signed


def _tile_lanes(t, n):
    return t if n == 1 else jnp.concatenate([t] * n, axis=1)


def _mod_kernel(c_ref, w_ref, b_ref, o_ref):
    c = c_ref[...]
    s = c * jax.nn.sigmoid(c)
    o_ref[...] = jnp.dot(s, w_ref[...], precision=HIGHEST, preferred_element_type=F32) + b_ref[...]


def _modulation(c_rows, w_mod, b_mod):
    n_col = 6 * D_MODEL // 1024
    return pl.pallas_call(
        _mod_kernel,
        out_shape=jax.ShapeDtypeStruct((DEPTH, 8, 6 * D_MODEL), F32),
        grid=(DEPTH, n_col),
        in_specs=[
            pl.BlockSpec((8, D_MODEL), lambda l, j: (0, 0)),
            pl.BlockSpec((None, D_MODEL, 1024), lambda l, j: (l, 0, j)),
            pl.BlockSpec((None, 1, 1024), lambda l, j: (l, 0, j)),
        ],
        out_specs=pl.BlockSpec((None, 8, 1024), lambda l, j: (l, 0, j)),
        compiler_params=_cparams("arbitrary", "arbitrary"),
        name="modulation",
    )(c_rows, w_mod, b_mod.reshape(DEPTH, 1, 6 * D_MODEL))


def _attn_a_kernel(*refs, seq, tq, n_cache, rope, proj_rows):
    if rope:
        (x_ref, mod_ref, w_ref, gain_ref, bd_ref, rep_ref, cos_ref, sin_ref, ck_ref, cv_ref,
         y_ref, q_s, k_s, v_s) = refs
    else:
        (x_ref, mod_ref, w_ref, gain_ref, bd_ref, rep_ref) = refs[:6]
        (y_ref, nk_ref, nv_ref, q_s, k_s, v_s) = refs[-6:]
    qi = pl.program_id(1)

    @pl.when(qi == 0)
    def _project():
        lane = lax.broadcasted_iota(jnp.int32, (1, 640), 1)
        for r0 in range(0, seq, proj_rows):
            rows = pl.ds(r0, proj_rows)
            h = _modulated(x_ref[rows, :], mod_ref, 0).astype(BF16)
            p = _dot(h, w_ref[...])
            qk = p[:, :640]
            sq = qk * qk
            hi = sq.astype(BF16)
            lo = (sq - hi.astype(F32)).astype(BF16)
            ms = jnp.concatenate([_dot(hi[:, c:c + 128], bd_ref[...]) + _dot(lo[:, c:c + 128], bd_ref[...])
                                  for c in range(0, 640, 128)], axis=1)
            qk = qk * lax.rsqrt(ms + EPS) * gain_ref[...]
            v = p[:, 640:768]
            if not rope:
                nk_ref[rows, :] = qk[:, 512:640]
                nv_ref[rows, :] = v
            else:
                cos = _tile_lanes(cos_ref[rows, :], 5)
                sin = _tile_lanes(sin_ref[rows, :], 5)
                qk = _rope(qk, cos, sin, lane)
            q_s[rows, :] = (qk[:, :512] * (A_HEAD_DIM ** -0.5)).astype(BF16)
            k_s[rows, :] = _dot(qk[:, 512:640].astype(BF16), rep_ref[...]).astype(BF16)
            v_s[rows, :] = _dot(v.astype(BF16), rep_ref[...]).astype(BF16)
        if n_cache:
            crow = pl.ds(seq, n_cache)
            k_s[crow, :] = _dot(ck_ref[...].astype(BF16), rep_ref[...]).astype(BF16)
            v_s[crow, :] = _dot(cv_ref[...].astype(BF16), rep_ref[...]).astype(BF16)

    head_of_lane = lax.broadcasted_iota(jnp.int32, (1, 256), 1) // A_HEAD_DIM
    qb = q_s[pl.ds(pl.multiple_of(qi * tq, tq), tq), :]
    for g in range(A_KV_HEADS):
        cols = slice(g * 256, (g + 1) * 256)
        qg = qb[:, cols]
        kg = k_s[:, cols]
        vg = v_s[:, cols]
        acc = jnp.zeros((tq, 256), F32)
        for r in range(A_HEADS // A_KV_HEADS):
            sel = head_of_lane == r
            qm = jnp.where(sel, qg, jnp.zeros_like(qg))
            s = _dot_nt(qm, kg)
            m = jnp.max(s, axis=-1, keepdims=True)
            p = jnp.exp(s - m)
            l = jnp.sum(p, axis=-1, keepdims=True)
            o = _dot(p.astype(BF16), vg)
            acc = acc + jnp.where(sel, o * (1.0 / l), 0.0)
        y_ref[:, cols] = acc.astype(BF16)


def _carry_aliases(in_specs, args, carry, first_out):
    if carry is None:
        return {}
    aliases = {}
    for k, arr in enumerate(carry):
        aliases[len(args)] = first_out + k
        in_specs.append(pl.BlockSpec(memory_space=pl.ANY))
        args.append(arr)
    return aliases


def _attn_a(x, mod, w_a, gain_row, bd, rep, *, n_seq, seq, tq, rope=None, cache=None, layer=0, carry=None):
    n_q = seq // tq
    n_cache = 0 if cache is None else cache[0].shape[2]
    proj_rows = min(seq, 512)
    kern = functools.partial(_attn_a_kernel, seq=seq, tq=tq, n_cache=n_cache, rope=rope is not None,
                             proj_rows=proj_rows)
    const = lambda s, q: (0, 0)
    in_specs = [
        pl.BlockSpec((seq, D_MODEL), lambda s, q: (s, 0)),
        pl.BlockSpec((None, 1, 6 * D_MODEL), lambda s, q: (s if mod.shape[0] > 1 else 0, 0, 0)),
        pl.BlockSpec(w_a.shape, const),
        pl.BlockSpec(gain_row.shape, const),
        pl.BlockSpec(bd.shape, const),
        pl.BlockSpec(rep.shape, const),
    ]
    args = [x, mod, w_a, gain_row, bd, rep]
    n_tok = n_seq * seq
    y_spec = pl.BlockSpec((tq, BRANCH_WIDTH), lambda s, q: (s * n_q + q, 0))
    y_shape = jax.ShapeDtypeStruct((n_tok, BRANCH_WIDTH), BF16)
    if rope is not None:
        cos, sin = rope
        in_specs += [pl.BlockSpec(cos.shape, const), pl.BlockSpec(sin.shape, const),
                     pl.BlockSpec((None, None, n_cache, 128), lambda s, q: (s, layer, 0, 0)),
                     pl.BlockSpec((None, None, n_cache, 128), lambda s, q: (s, layer, 0, 0))]
        args += [cos, sin, cache[0], cache[1]]
        out_shape, out_specs, aliases = y_shape, y_spec, {}
    else:
        kv_shape = jax.ShapeDtypeStruct((n_seq, DEPTH, seq, 128), F32)
        kv_spec = pl.BlockSpec((None, None, seq, 128), lambda s, q: (s, layer, 0, 0))
        out_shape, out_specs = (y_shape, kv_shape, kv_shape), (y_spec, kv_spec, kv_spec)
        aliases = _carry_aliases(in_specs, args, carry, first_out=1)
    return pl.pallas_call(
        kern, out_shape=out_shape, grid=(n_seq, n_q), in_specs=in_specs, out_specs=out_specs,
        input_output_aliases=aliases,
        scratch_shapes=[pltpu.VMEM((seq, 512), BF16), pltpu.VMEM((seq + n_cache, 512), BF16),
                        pltpu.VMEM((seq + n_cache, 512), BF16)],
        compiler_params=_cparams("arbitrary", "arbitrary"),
        name="branch_a_lat" if rope is not None else "branch_a_ctx",
    )(*args)


def _attn_d_kernel(*refs, seq, tq, n_cache, rope, proj_rows, lam_init):
    if rope:
        (x_ref, mod_ref, w_ref, lam_ref, gn_ref, cos_ref, sin_ref, ck_ref, cv_ref,
         y_ref, q_s, k_s, v_s) = refs
    else:
        (x_ref, mod_ref, w_ref, lam_ref, gn_ref) = refs[:5]
        (y_ref, nk_ref, nv_ref, q_s, k_s, v_s) = refs[-6:]
    qi = pl.program_id(1)

    @pl.when(qi == 0)
    def _project():
        lane = lax.broadcasted_iota(jnp.int32, (1, 512), 1)
        for r0 in range(0, seq, proj_rows):
            rows = pl.ds(r0, proj_rows)
            h = _modulated(x_ref[rows, :], mod_ref, 0).astype(BF16)
            p = _dot(h, w_ref[...])
            dq, dk, dv = p[:, :512], p[:, 512:1024], p[:, 1024:1536]
            if not rope:
                nk_ref[rows, :] = dk
                nv_ref[rows, :] = dv
            else:
                cos = _tile_lanes(cos_ref[rows, :], 4)
                sin = _tile_lanes(sin_ref[rows, :], 4)
                dq = _rope(dq, cos, sin, lane)
                dk = _rope(dk, cos, sin, lane)
            q_s[rows, :] = (dq * (D_HALF_DIM ** -0.5)).astype(BF16)
            k_s[rows, :] = dk.astype(BF16)
            v_s[rows, :] = dv.astype(BF16)
        if n_cache:
            crow = pl.ds(seq, n_cache)
            k_s[crow, :] = ck_ref[...].astype(BF16)
            v_s[crow, :] = cv_ref[...].astype(BF16)

    lv = lam_ref[...]
    lam = (jnp.exp(jnp.sum(lv[0:1] * lv[1:2], axis=-1, keepdims=True))
           - jnp.exp(jnp.sum(lv[2:3] * lv[3:4], axis=-1, keepdims=True)) + lam_init)
    half_of_lane = lax.broadcasted_iota(jnp.int32, (1, 128), 1) // D_HALF_DIM
    qb = q_s[pl.ds(pl.multiple_of(qi * tq, tq), tq), :]
    for hd in range(D_HEADS):
        cols = slice(hd * 128, (hd + 1) * 128)
        qh = qb[:, cols]
        kh = k_s[:, cols]
        vh = v_s[:, cols]
        probs = []
        for j in range(2):
            qm = jnp.where(half_of_lane == j, qh, jnp.zeros_like(qh))
            s = _dot_nt(qm, kh)
            m = jnp.max(s, axis=-1, keepdims=True)
            p = jnp.exp(s - m)
            l = jnp.sum(p, axis=-1, keepdims=True)
            probs.append(p * (1.0 / l))
        a = (probs[0] - lam * probs[1]).astype(BF16)
        o = _dot(a, vh)
        ms = jnp.mean(o * o, axis=-1, keepdims=True)
        o = o * lax.rsqrt(ms + EPS) * gn_ref[:, cols] * (1.0 - lam_init)
        y_ref[:, cols] = o.astype(BF16)


def _attn_d(x, mod, w_d, lam_params, gn_row, *, n_seq, seq, tq, lam_init, rope=None, cache=None, layer=0,
            carry=None):
    n_q = seq // tq
    n_cache = 0 if cache is None else cache[0].shape[2]
    proj_rows = min(seq, 512)
    kern = functools.partial(_attn_d_kernel, seq=seq, tq=tq, n_cache=n_cache, rope=rope is not None,
                             proj_rows=proj_rows, lam_init=lam_init)
    const = lambda s, q: (0, 0)
    in_specs = [
        pl.BlockSpec((seq, D_MODEL), lambda s, q: (s, 0)),
        pl.BlockSpec((None, 1, 6 * D_MODEL), lambda s, q: (s if mod.shape[0] > 1 else 0, 0, 0)),
        pl.BlockSpec(w_d.shape, const),
        pl.BlockSpec(lam_params.shape, const),
        pl.BlockSpec(gn_row.shape, const),
    ]
    args = [x, mod, w_d, lam_params, gn_row]
    n_tok = n_seq * seq
    y_spec = pl.BlockSpec((tq, BRANCH_WIDTH), lambda s, q: (s * n_q + q, 0))
    y_shape = jax.ShapeDtypeStruct((n_tok, BRANCH_WIDTH), BF16)
    if rope is not None:
        cos, sin = rope
        in_specs += [pl.BlockSpec(cos.shape, const), pl.BlockSpec(sin.shape, const),
                     pl.BlockSpec((None, None, n_cache, 512), lambda s, q: (s, layer, 0, 0)),
                     pl.BlockSpec((None, None, n_cache, 512), lambda s, q: (s, layer, 0, 0))]
        args += [cos, sin, cache[0], cache[1]]
        out_shape, out_specs, aliases = y_shape, y_spec, {}
    else:
        kv_shape = jax.ShapeDtypeStruct((n_seq, DEPTH, seq, 512), F32)
        kv_spec = pl.BlockSpec((None, None, seq, 512), lambda s, q: (s, layer, 0, 0))
        out_shape, out_specs = (y_shape, kv_shape, kv_shape), (y_spec, kv_spec, kv_spec)
        aliases = _carry_aliases(in_specs, args, carry, first_out=1)
    return pl.pallas_call(
        kern, out_shape=out_shape, grid=(n_seq, n_q), in_specs=in_specs, out_specs=out_specs,
        input_output_aliases=aliases,
        scratch_shapes=[pltpu.VMEM((seq, 512), BF16), pltpu.VMEM((seq + n_cache, 512), BF16),
                        pltpu.VMEM((seq + n_cache, 512), BF16)],
        compiler_params=_cparams("arbitrary", "arbitrary"),
        name="branch_d_lat" if rope is not None else "branch_d_ctx",
    )(*args)


def _gmlp_kernel(x_ref, mod_ref, w_ref, ws_ref, bias_ref, y_ref, *, rows):
    h = _modulated(x_ref[...], mod_ref, 0).astype(BF16)
    p = _dot(h, w_ref[...])
    u, v = p[:, :BRANCH_WIDTH], p[:, BRANCH_WIDTH:]
    mu = jnp.mean(v, axis=-1, keepdims=True)
    vc = v - mu
    var = jnp.mean(vc * vc, axis=-1, keepdims=True)
    vn = (vc * lax.rsqrt(var + EPS)).astype(BF16)
    for c in range(rows // CHUNK):
        rs = slice(c * CHUNK, (c + 1) * CHUNK)
        for g in range(B_GROUPS):
            cs = slice(g * 128, (g + 1) * 128)
            s = _dot(ws_ref[g].astype(BF16), vn[rs, cs]) + bias_ref[:, cs]
            y_ref[rs, cs] = (u[rs, cs] * s).astype(BF16)


def _gmlp(x, mod, w_b, ws, bias_full, *, seq, rows):
    n_tok = x.shape[0]
    per_seq = seq // rows
    kern = functools.partial(_gmlp_kernel, rows=rows)
    return pl.pallas_call(
        kern, out_shape=jax.ShapeDtypeStruct((n_tok, BRANCH_WIDTH), BF16),
        grid=(n_tok // rows,),
        in_specs=[
            pl.BlockSpec((rows, D_MODEL), lambda i: (i, 0)),
            pl.BlockSpec((None, 1, 6 * D_MODEL), lambda i: (i // per_seq if mod.shape[0] > 1 else 0, 0, 0)),
            pl.BlockSpec(w_b.shape, lambda i: (0, 0)),
            pl.BlockSpec(ws.shape, lambda i: (0, 0, 0)),
            pl.BlockSpec(bias_full.shape, lambda i: (0, 0)),
        ],
        out_specs=pl.BlockSpec((rows, BRANCH_WIDTH), lambda i: (i, 0)),
        compiler_params=_cparams("arbitrary"),
        name="branch_b",
    )(x, mod, w_b, ws, bias_full)


def _mlstm_t_kernel(*refs, seq, has_init, proj_rows):
    if has_init:
        (x_ref, mod_ref, wn_ref, wt_ref, gbr_ref, gbt_ref, gnt_ref, tril_ref, triu_ref, c0_ref, n0_ref, m0_ref,
         y_ref, q_s, k_s, qt_s, vt_s, ot_s, g_s, gt_s, hf_s, hb_s, st_s, m_s) = refs
    else:
        (x_ref, mod_ref, wn_ref, wt_ref, gbr_ref, gbt_ref, gnt_ref, tril_ref, triu_ref) = refs[:9]
        (y_ref, cout_ref, nout_ref, mout_ref,
         q_s, k_s, qt_s, vt_s, ot_s, g_s, gt_s, hf_s, hb_s, st_s, m_s) = refs[-15:]
    n_chunk = seq // CHUNK
    for r0 in range(0, seq, proj_rows):
        span = pl.ds(r0, proj_rows)
        h = _modulated(x_ref[span, :], mod_ref, 0).astype(BF16)
        p = _dot(h, wn_ref[...])
        q_s[span, :] = p[:, 0:512].astype(BF16)
        k_s[span, :] = (p[:, 512:1024] * (C_HEAD_DIM ** -0.5)).astype(BF16)
        g_s[span, :] = p[:, 1024:1152] + gbr_ref[...]
        pt = _dot_nt(wt_ref[...], h)
        qt_s[:, span] = pt[0:512].astype(BF16)
        vt_s[:, span] = pt[512:1024].astype(BF16)
        ot_s[:, span] = jax.nn.sigmoid(pt[1024:1536]).astype(BF16)
        gt_s[:, span] = pt[1536:1664] + _tile_lanes(gbt_ref[...], proj_rows // 128)

    first_row = lax.broadcasted_iota(jnp.int32, (CHUNK, CHUNK), 0) == 0
    for sidx in range(2 * C_HEADS):
        if has_init:
            st_s[sidx, 0:128, :] = c0_ref[sidx].T
            st_s[sidx, 128:256, :] = jnp.where(first_row, jnp.broadcast_to(n0_ref[sidx:sidx + 1, :], (CHUNK, 128)), 0.0)
        else:
            st_s[sidx] = jnp.zeros((2 * CHUNK, 128), F32)
    m_s[...] = m0_ref[...] if has_init else jnp.zeros_like(m_s)

    tril = tril_ref[...]
    triu = triu_ref[...]
    row_i = lax.broadcasted_iota(jnp.int32, (CHUNK, CHUNK), 0)
    col_i = lax.broadcasted_iota(jnp.int32, (CHUNK, CHUNK), 1)
    visible = (row_i <= col_i, row_i >= col_i)
    ones_blk = jnp.where(first_row, 1.0, 0.0).astype(BF16)

    def chunk_step(c, carry):
        for direction in range(2):
            cc = c if direction == 0 else n_chunk - 1 - c
            span = pl.ds(pl.multiple_of(cc * CHUNK, CHUNK), CHUNK)
            gates = g_s[span, :]
            gates_t = gt_s[:, span]
            logf = jnp.minimum(gates, 0.0) - jnp.log1p(jnp.exp(-jnp.abs(gates)))
            logf_t = jnp.minimum(gates_t, 0.0) - jnp.log1p(jnp.exp(-jnp.abs(gates_t)))
            tri_col = tril if direction == 0 else triu
            tri_row = triu if direction == 0 else tril
            b_col_all = jnp.dot(tri_col, logf, precision=HIGHEST, preferred_element_type=F32)
            b_row_all = jnp.dot(logf_t, tri_row, precision=HIGHEST, preferred_element_type=F32)
            r_col_all = gates - pltpu.roll(b_col_all, 128 - 4, 1)
            base = direction * 8
            i_rows = gates_t[base:base + 4, :]
            b_rows = b_row_all[base + 4:base + 8, :]
            last = CHUNK - 1 if direction == 0 else 0
            h_out = hf_s if direction == 0 else hb_s
            for hd in range(C_HEADS):
                sidx = direction * 4 + hd
                blk = slice(hd * 128, (hd + 1) * 128)
                r_col = r_col_all[:, base + hd:base + hd + 1]
                b_row = b_rows[hd:hd + 1, :]
                i_row = i_rows[hd:hd + 1, :]
                m_prev = m_s[sidx:sidx + 1, 0:1]
                r_wide = jnp.broadcast_to(r_col, (CHUNK, CHUNK))
                peak = jnp.max(jnp.where(visible[direction], r_wide, NEG_INF), axis=0, keepdims=True)
                m_row = b_row + jnp.maximum(peak, m_prev)
                qh = q_s[span, blk]
                kh = k_s[span, blk]
                arg = jnp.where(visible[direction], r_wide + (b_row - m_row), NEG_INF)
                w_t = (_dot_nt(kh, qh) * jnp.exp(arg)).astype(BF16)
                inter = jnp.exp(b_row + m_prev - m_row)
                v_one = jnp.concatenate([vt_s[blk, span], ones_blk], axis=0)
                state = st_s[sidx]
                both = _dot(v_one, w_t) + inter * _dot(state.astype(BF16), qt_s[blk, span])
                den = both[128:129, :]
                h_out[blk, span] = both[0:128, :] / jnp.maximum(jnp.abs(den), jnp.exp(-m_row))
                b_last = b_row[:, last:last + 1]
                g_row = b_last - b_row + i_row
                m_new = jnp.maximum(b_last + m_prev, jnp.max(g_row, axis=-1, keepdims=True))
                decay = jnp.exp(b_last + m_prev - m_new)
                scaled = v_one * jnp.exp(g_row - m_new).astype(BF16)
                st_s[sidx] = decay * state + _dot(scaled, kh)
                m_s[sidx:sidx + 1, :] = jnp.broadcast_to(m_new, (1, 128))
        return carry

    lax.fori_loop(0, n_chunk, chunk_step, 0)

    for r0 in range(0, seq, proj_rows):
        span = pl.ds(r0, proj_rows)
        for hd in range(C_HEADS):
            blk = slice(hd * 128, (hd + 1) * 128)
            hh = hf_s[blk, span] + hb_s[blk, span]
            mu = jnp.mean(hh, axis=0, keepdims=True)
            hc = hh - mu
            var = jnp.mean(hc * hc, axis=0, keepdims=True)
            gn = _tile_lanes(gnt_ref[blk, :], proj_rows // 128)
            y_t = hc * lax.rsqrt(var + EPS) * gn * ot_s[blk, span].astype(F32)
            y_ref[span, blk] = y_t.T.astype(BF16)
    if not has_init:
        for sidx in range(2 * C_HEADS):
            cout_ref[sidx] = st_s[sidx, 0:128, :].T
            nout_ref[sidx:sidx + 1, :] = st_s[sidx, 128:129, :]
        mout_ref[...] = m_s[...]


def _mlstm_t(x, mod, w_nat, w_t, gate_bias_row, gate_bias_t, gn_t, tril, triu, *, n_seq, seq, init=None, layer=0,
             carry=None):
    proj_rows = min(seq, 512)
    kern = functools.partial(_mlstm_t_kernel, seq=seq, has_init=init is not None, proj_rows=proj_rows)
    const = lambda s: (0, 0)
    in_specs = [
        pl.BlockSpec((seq, D_MODEL), lambda s: (s, 0)),
        pl.BlockSpec((None, 1, 6 * D_MODEL), lambda s: (s if mod.shape[0] > 1 else 0, 0, 0)),
    ] + [pl.BlockSpec(a.shape, const) for a in (w_nat, w_t, gate_bias_row, gate_bias_t, gn_t, tril, triu)]
    args = [x, mod, w_nat, w_t, gate_bias_row, gate_bias_t, gn_t, tril, triu]
    n_tok = n_seq * seq
    y_shape = jax.ShapeDtypeStruct((n_tok, BRANCH_WIDTH), BF16)
    y_spec = pl.BlockSpec((seq, BRANCH_WIDTH), lambda s: (s, 0))
    if init is not None:
        c0, n0, m0 = init
        in_specs += [pl.BlockSpec((None, None, 8, 128, 128), lambda s: (s, layer, 0, 0, 0)),
                     pl.BlockSpec((None, None, 8, 128), lambda s: (s, layer, 0, 0)),
                     pl.BlockSpec((None, None, 8, 128), lambda s: (s, layer, 0, 0))]
        args += [c0, n0, m0]
        out_shape, out_specs, aliases = y_shape, y_spec, {}
    else:
        out_shape = (y_shape, jax.ShapeDtypeStruct((n_seq, DEPTH, 8, 128, 128), F32),
                     jax.ShapeDtypeStruct((n_seq, DEPTH, 8, 128), F32),
                     jax.ShapeDtypeStruct((n_seq, DEPTH, 8, 128), F32))
        out_specs = (y_spec, pl.BlockSpec((None, None, 8, 128, 128), lambda s: (s, layer, 0, 0, 0)),
                     pl.BlockSpec((None, None, 8, 128), lambda s: (s, layer, 0, 0)),
                     pl.BlockSpec((None, None, 8, 128), lambda s: (s, layer, 0, 0)))
        aliases = _carry_aliases(in_specs, args, carry, first_out=1)
    return pl.pallas_call(
        kern, out_shape=out_shape, grid=(n_seq,), in_specs=in_specs, out_specs=out_specs,
        input_output_aliases=aliases,
        scratch_shapes=[pltpu.VMEM((seq, 512), BF16), pltpu.VMEM((seq, 512), BF16),
                        pltpu.VMEM((512, seq), BF16), pltpu.VMEM((512, seq), BF16),
                        pltpu.VMEM((512, seq), BF16),
                        pltpu.VMEM((seq, 128), F32), pltpu.VMEM((128, seq), F32),
                        pltpu.VMEM((512, seq), F32), pltpu.VMEM((512, seq), F32),
                        pltpu.VMEM((8, 256, 128), F32), pltpu.VMEM((8, 128), F32)],
        compiler_params=_cparams("arbitrary"),
        name="branch_c_lat" if init is not None else "branch_c_ctx",
    )(*args)


def _merge_kernel(x_ref, mod_ref, ya_ref, yb_ref, yc_ref, yd_ref, wg_ref, wbr_ref, wout_ref, lng_ref, lnb_ref,
                  o_ref):
    x = x_ref[...]
    h = _modulated(x, mod_ref, 0).astype(BF16)
    mix = None
    for n, y_ref in enumerate((ya_ref, yb_ref, yc_ref, yd_ref)):
        gate = jax.nn.sigmoid(_dot(h, wg_ref[:, n * D_MODEL:(n + 1) * D_MODEL]))
        term = gate * _dot(y_ref[...], wbr_ref[n])
        mix = term if mix is None else mix + term
    out = _dot(mix.astype(BF16), wout_ref[...])
    g1 = mod_ref[:, 2 * D_MODEL:3 * D_MODEL]
    o_ref[...] = _layer_norm_rows(ALPHA * x + g1 * out, lng_ref[...], lnb_ref[...])


def _merge(x, mod, ys, w_g, w_br, w_out, ln_g, ln_b, *, seq, rows):
    n_tok = x.shape[0]
    per_seq = seq // rows
    tok = lambda i: (i, 0)
    c2 = lambda i: (0, 0)
    return pl.pallas_call(
        _merge_kernel, out_shape=jax.ShapeDtypeStruct((n_tok, D_MODEL), F32),
        grid=(n_tok // rows,),
        in_specs=[
            pl.BlockSpec((rows, D_MODEL), tok),
            pl.BlockSpec((None, 1, 6 * D_MODEL), lambda i: (i // per_seq if mod.shape[0] > 1 else 0, 0, 0)),
            pl.BlockSpec((rows, BRANCH_WIDTH), tok), pl.BlockSpec((rows, BRANCH_WIDTH), tok),
            pl.BlockSpec((rows, BRANCH_WIDTH), tok), pl.BlockSpec((rows, BRANCH_WIDTH), tok),
            pl.BlockSpec(w_g.shape, c2), pl.BlockSpec(w_br.shape, lambda i: (0, 0, 0)),
            pl.BlockSpec(w_out.shape, c2), pl.BlockSpec(ln_g.shape, c2), pl.BlockSpec(ln_b.shape, c2),
        ],
        out_specs=pl.BlockSpec((rows, D_MODEL), tok),
        compiler_params=_cparams("arbitrary"),
        name="merge",
    )(x, mod, *ys, w_g, w_br, w_out, ln_g, ln_b)


_TAKEN = -(2.0 ** 127)


def _top16(s):
    cur = s
    vals = []
    for r in range(PEER_TOPK):
        mx = jnp.max(cur, axis=0, keepdims=True)
        cur = jnp.where(cur == mx, _TAKEN * (1.0 + r / 32.0), cur)
        vals.append(mx)
    rank = jnp.where(cur <= _TAKEN, cur * (32.0 / _TAKEN) - 31.0, float(PEER_TOPK + 1))
    return jnp.concatenate(vals, axis=0), rank


def _pair_tables():
    pairs = [(k1, k2) for k1 in range(PEER_TOPK) for k2 in range(PEER_TOPK // (k1 + 1))]
    n = 56
    sel_a = np.zeros((n, PEER_TOPK), np.float32)
    sel_b = np.zeros((n, PEER_TOPK), np.float32)
    pad = np.full((n, 1), NEG_INF, np.float32)
    for row, (k1, k2) in enumerate(pairs):
        sel_a[row, k1] = 1.0
        sel_b[row, k2] = 1.0
        pad[row, 0] = 0.0
    return jnp.asarray(sel_a), jnp.asarray(sel_b), jnp.asarray(pad), jnp.asarray(sel_a.T, BF16)


def _route_kernel(x_ref, mod_ref, wq_ref, keys_ref, sela_ref, selb_ref, pad_ref, ind_ref,
                  cnt_ref, e1_ref, r2_ref, e2_ref, h2_s, *, heads):
    @pl.when(pl.program_id(1) == 0)
    def _modulate():
        h2_s[...] = _modulated(x_ref[...], mod_ref, 1).astype(BF16)

    pick = lambda sel_ref, v: jnp.dot(sel_ref[...], v, precision=HIGHEST, preferred_element_type=F32)
    q_all = _dot(h2_s[...], wq_ref[...])
    for hd in range(heads):
        q = q_all[:, hd * PEER_QDIM:(hd + 1) * PEER_QDIM]
        s1 = _dot_nt(keys_ref[hd, 0].astype(BF16), q[:, :128].astype(BF16))
        s2 = _dot_nt(keys_ref[hd, 1].astype(BF16), q[:, 128:].astype(BF16))
        a, rank1 = _top16(s1)
        b, rank2 = _top16(s2)
        ea = jnp.exp(a - a[0:1])
        eb = jnp.exp(b - b[0:1])
        cand = pick(sela_ref, a) + pick(selb_ref, b) + pad_ref[...]
        gate = pick(sela_ref, ea) * pick(selb_ref, eb)
        cur = cand
        thr = None
        for _ in range(PEER_TOPK):
            thr = jnp.max(cur, axis=0, keepdims=True)
            cur = jnp.where(cur == thr, NEG_INF, cur)
        chosen = cand >= thr
        z = jnp.sum(jnp.where(chosen, gate, 0.0), axis=0, keepdims=True)
        cnt_sorted = _dot(ind_ref[...], jnp.where(chosen, 1.0, 0.0).astype(BF16))
        cnt = jnp.zeros_like(s1)
        for r in range(PEER_TOPK):
            cnt = jnp.where(rank1 == float(r + 1), cnt_sorted[r:r + 1], cnt)
        cnt_ref[hd] = cnt
        e1_ref[hd] = jnp.where(rank1 <= float(PEER_TOPK), jnp.exp(s1 - a[0:1]) * (0.5 / z), 0.0)
        packed = (PEER_NKEYS // 16, 16, s2.shape[1])
        r2_ref[hd] = rank2.astype(BF16).reshape(packed)
        e2_ref[hd] = jnp.where(rank2 <= float(PEER_TOPK), jnp.exp(s2 - b[0:1]), 0.0).astype(BF16).reshape(packed)


def _route(x1, mod, wq, keys, *, seq, cols, heads):
    n_tok = x1.shape[0]
    per_seq = seq // cols
    tables = _pair_tables()
    row_shape = jax.ShapeDtypeStruct((PEER_HEADS, PEER_NKEYS, n_tok), F32)
    col_shape = jax.ShapeDtypeStruct((PEER_HEADS, PEER_NKEYS // 16, 16, n_tok), BF16)
    col_spec = pl.BlockSpec((heads, PEER_NKEYS // 16, 16, cols), lambda i, h: (h, 0, 0, i))
    spec = pl.BlockSpec((heads, PEER_NKEYS, cols), lambda i, h: (h, 0, i))
    return pl.pallas_call(
        functools.partial(_route_kernel, heads=heads), out_shape=(row_shape, row_shape, col_shape, col_shape),
        grid=(n_tok // cols, PEER_HEADS // heads),
        in_specs=[
            pl.BlockSpec((cols, D_MODEL), lambda i, h: (i, 0)),
            pl.BlockSpec((None, 1, 6 * D_MODEL), lambda i, h: (i // per_seq if mod.shape[0] > 1 else 0, 0, 0)),
            pl.BlockSpec((D_MODEL, heads * PEER_QDIM), lambda i, h: (0, h)),
            pl.BlockSpec((heads, 2, PEER_NKEYS, PEER_QDIM // 2), lambda i, h: (h, 0, 0, 0)),
        ] + [pl.BlockSpec(t.shape, lambda i, h: (0, 0)) for t in tables],
        out_specs=(spec, spec, col_spec, col_spec),
        scratch_shapes=[pltpu.VMEM((cols, D_MODEL), BF16)],
        compiler_params=_cparams("arbitrary", "arbitrary"),
        name="peer_route",
    )(x1, mod, wq, keys, *tables)


def _peer_kernel(x_ref, mod_ref, u_ref, v_ref, cnt_ref, e1_ref, r2_ref, e2_ref, lng_ref, lnb_ref,
                 o_ref, h2t_s, acc_s, *, key_rows):
    e = pl.program_id(1)

    @pl.when(e == 0)
    def _init():
        h2 = _modulated(x_ref[...], mod_ref, 1)
        h2t_s[...] = h2.T.astype(BF16)
        acc_s[...] = jnp.zeros_like(acc_s)

    n_tok = h2t_s.shape[1]
    first_key = pl.multiple_of(e * key_rows, key_rows)
    zero = jnp.zeros((PEER_NKEYS // 16, 16, n_tok), BF16)
    act = _dot(u_ref[...], h2t_s[...]).astype(BF16)
    act = act * (1.0 + lax.erf(act * (2.0 ** -0.5)))
    pieces = []
    for r in range(key_rows):
        g = None
        for hd in range(PEER_HEADS):
            cnt_blk = cnt_ref[hd, pl.ds(first_key, key_rows), :]
            e1_blk = e1_ref[hd, pl.ds(first_key, key_rows), :]
            cnt_rows = jnp.broadcast_to(cnt_blk[r:r + 1, :], (16, n_tok)).astype(BF16)
            e1_rows = jnp.broadcast_to(e1_blk[r:r + 1, :], (16, n_tok)).astype(BF16)
            term = jnp.where(r2_ref[hd] <= cnt_rows[None], e2_ref[hd], zero) * e1_rows[None]
            g = term if g is None else g + term
        pieces.append(g.reshape(PEER_NKEYS, n_tok) * act[r * PEER_NKEYS:(r + 1) * PEER_NKEYS, :])
    acc_s[...] += lax.dot_general(v_ref[...], jnp.concatenate(pieces, axis=0), (((0,), (0,)), ((), ())),
                                  preferred_element_type=F32)

    @pl.when(e == pl.num_programs(1) - 1)
    def _finish():
        x = x_ref[...]
        g2 = mod_ref[:, 5 * D_MODEL:6 * D_MODEL]
        o_ref[...] = _layer_norm_rows(ALPHA * x + g2 * acc_s[...].T, lng_ref[...], lnb_ref[...])


def _peer(x1, mod, u, v, route, ln_g, ln_b, *, seq, cols, key_rows):
    n_tok = x1.shape[0]
    per_seq = seq // cols
    n_exp = key_rows * PEER_NKEYS
    n_tiles = PEER_EXPERTS // n_exp
    assert key_rows % 8 == 0
    kern = functools.partial(_peer_kernel, key_rows=key_rows)
    cnt, e1, r2, e2 = route
    rspec = pl.BlockSpec((PEER_HEADS, PEER_NKEYS, cols), lambda i, e: (0, 0, i))
    cspec = pl.BlockSpec((PEER_HEADS, PEER_NKEYS // 16, 16, cols), lambda i, e: (0, 0, 0, i))
    return pl.pallas_call(
        kern, out_shape=jax.ShapeDtypeStruct((n_tok, D_MODEL), F32),
        grid=(n_tok // cols, n_tiles),
        in_specs=[
            pl.BlockSpec((cols, D_MODEL), lambda i, e: (i, 0)),
            pl.BlockSpec((None, 1, 6 * D_MODEL), lambda i, e: (i // per_seq if mod.shape[0] > 1 else 0, 0, 0)),
            pl.BlockSpec((n_exp, D_MODEL), lambda i, e: (e, 0)),
            pl.BlockSpec((n_exp, D_MODEL), lambda i, e: (e, 0)),
            rspec, rspec, cspec, cspec,
            pl.BlockSpec(ln_g.shape, lambda i, e: (0, 0)), pl.BlockSpec(ln_b.shape, lambda i, e: (0, 0)),
        ],
        out_specs=pl.BlockSpec((cols, D_MODEL), lambda i, e: (i, 0)),
        scratch_shapes=[pltpu.VMEM((D_MODEL, cols), BF16), pltpu.VMEM((D_MODEL, cols), F32)],
        compiler_params=_cparams("arbitrary", "arbitrary"),
        name="peer_experts",
    )(x1, mod, u, v, cnt, e1, r2, e2, ln_g, ln_b)


def _rope_tables(seq):
    t = np.arange(seq)
    pos = np.stack([t // GRID_W, t % GRID_W], axis=1).astype(np.float64)
    inv = ROPE_BASE ** (-np.arange(16, dtype=np.float64) / 16)
    lane = np.arange(64)
    ang = pos[:, lane // 32] * inv[lane % 16][None, :]
    sign = np.where((lane % 32) < 16, -1.0, 1.0)[None, :]
    cos = np.tile(np.cos(ang), (1, 2)).astype(np.float32)
    sin = np.tile(np.sin(ang) * sign, (1, 2)).astype(np.float32)
    return jnp.asarray(cos), jnp.asarray(sin)


def _static_tables():
    lane = np.arange(128)
    bd = (lane[:, None] // 64 == lane[None, :] // 64).astype(np.float32) / 64.0
    src = np.arange(128)
    dst = np.arange(512)
    rep = ((src[:, None] // 64 == dst[None, :] // 256) & (src[:, None] % 64 == dst[None, :] % 64))
    idx = np.arange(CHUNK)
    tril = (idx[None, :] <= idx[:, None]).astype(np.float32)
    triu = (idx[None, :] >= idx[:, None]).astype(np.float32)
    return (jnp.asarray(bd, BF16), jnp.asarray(rep.astype(np.float32), BF16), jnp.asarray(tril), jnp.asarray(triu))


def _layer_params(l, w_in, attn_qk_gain, gmlp_ws, gmlp_b, mlstm_gate_bias, mlstm_gn, diff_lambda, diff_gn,
                  w_branch, w_out, ln_g, ln_b, peer_wq, peer_keys, peer_u, peer_v):
    w = w_in[l]
    p = {}
    p["w_a"] = w[:, _OFF_A:_OFF_B].astype(BF16)
    p["w_b"] = w[:, _OFF_B:_OFF_C].astype(BF16)
    cq, ck = w[:, _OFF_C:_OFF_C + 512], w[:, _OFF_C + 512:_OFF_C + 1024]
    cv, co = w[:, _OFF_C + 1024:_OFF_C + 1536], w[:, _OFF_C + 1536:_OFF_C + 2048]
    cg = jnp.concatenate([w[:, _OFF_CG:_OFF_D], jnp.zeros((D_MODEL, 112), F32)], axis=1)
    p["w_c_nat"] = jnp.concatenate([cq, ck, cg], axis=1).astype(BF16)
    p["w_c_t"] = jnp.concatenate([cq, cv, co, cg], axis=1).T.astype(BF16)
    p["w_d"] = w[:, _OFF_D:_OFF_G].astype(BF16)
    p["w_g"] = w[:, _OFF_G:].astype(BF16)
    gain = attn_qk_gain[l]
    p["gain_row"] = jnp.concatenate([jnp.tile(gain[0], A_HEADS), jnp.tile(gain[1], A_KV_HEADS)])[None, :]
    p["ws"] = gmlp_ws[l]
    p["bias_full"] = jnp.repeat(gmlp_b[l].T, 128, axis=1)
    p["gate_bias_row"] = jnp.concatenate([mlstm_gate_bias[l].reshape(16), jnp.zeros((112,), F32)])[None, :]
    p["gate_bias_t"] = jnp.broadcast_to(p["gate_bias_row"].reshape(128, 1), (128, 128))
    p["mlstm_gn_t"] = jnp.broadcast_to(mlstm_gn[l].reshape(BRANCH_WIDTH, 1), (BRANCH_WIDTH, 128))
    p["lam"] = diff_lambda[l]
    p["diff_gn_row"] = diff_gn[l].reshape(1, BRANCH_WIDTH)
    p["w_br"] = w_branch[l].astype(BF16)
    p["w_out"] = w_out[l].astype(BF16)
    p["ln_g0"], p["ln_b0"] = ln_g[l, 0][None, :], ln_b[l, 0][None, :]
    p["ln_g1"], p["ln_b1"] = ln_g[l, 1][None, :], ln_b[l, 1][None, :]
    p["wq"] = peer_wq[l].astype(BF16)
    p["keys"] = peer_keys[l]
    p["u"] = peer_u[l].astype(BF16)
    p["v"] = peer_v[l].astype(BF16)
    return p


def _trunk_layer(x, mod, p, tabs, *, l, n_seq, seq, cfg, ctx_cache=None, prev_state=None):
    bd, rep, tril, triu = tabs
    lam_init = 0.8 - 0.6 * math.exp(-0.3 * l)
    state = None
    if ctx_cache is None:
        prev = (None, None, None) if prev_state is None else (prev_state[0:2], prev_state[2:4], prev_state[4:7])
        ya, nk, nv = _attn_a(x, mod, p["w_a"], p["gain_row"], bd, rep, n_seq=n_seq, seq=seq, tq=cfg["tq"],
                             layer=l, carry=prev[0])
        yd, ndk, ndv = _attn_d(x, mod, p["w_d"], p["lam"], p["diff_gn_row"], n_seq=n_seq, seq=seq, tq=cfg["tq"],
                               lam_init=lam_init, layer=l, carry=prev[1])
        yc, c_new, n_new, m_new = _mlstm_t(x, mod, p["w_c_nat"], p["w_c_t"], p["gate_bias_row"], p["gate_bias_t"],
                                           p["mlstm_gn_t"], tril, triu, n_seq=n_seq, seq=seq, layer=l, carry=prev[2])
        state = (nk, nv, ndk, ndv, c_new, n_new, m_new)
    else:
        rope, cak, cav, cdk, cdv, c0, n0, m0 = ctx_cache
        ya = _attn_a(x, mod, p["w_a"], p["gain_row"], bd, rep, n_seq=n_seq, seq=seq, tq=cfg["tq"],
                     rope=rope, cache=(cak, cav), layer=l)
        yd = _attn_d(x, mod, p["w_d"], p["lam"], p["diff_gn_row"], n_seq=n_seq, seq=seq, tq=cfg["tq"],
                     lam_init=lam_init, rope=rope, cache=(cdk, cdv), layer=l)
        yc = _mlstm_t(x, mod, p["w_c_nat"], p["w_c_t"], p["gate_bias_row"], p["gate_bias_t"], p["mlstm_gn_t"],
                      tril, triu, n_seq=n_seq, seq=seq, init=(c0, n0, m0), layer=l)
    yb = _gmlp(x, mod, p["w_b"], p["ws"], p["bias_full"], seq=seq, rows=cfg["rows"])
    x1 = _merge(x, mod, (ya, yb, yc, yd), p["w_g"], p["w_br"], p["w_out"], p["ln_g0"], p["ln_b0"],
                seq=seq, rows=cfg["rows"])
    route = _route(x1, mod, p["wq"], p["keys"], seq=seq, cols=cfg["route_cols"], heads=cfg["route_heads"])
    x2 = _peer(x1, mod, p["u"], p["v"], route, p["ln_g1"], p["ln_b1"], seq=seq, cols=cfg["cols"],
               key_rows=cfg["key_rows"])
    return x2, state


def kernel(x_prompt, x_sample, cache_a_k, cache_a_v, cache_d_k, cache_d_v, state_c_C, state_c_n, state_c_m,
           c, c_ctx, w_mod, b_mod, w_in, attn_qk_gain, gmlp_ws, gmlp_b, mlstm_gate_bias, mlstm_gn,
           diff_lambda, diff_gn, w_branch, w_out, ln_g, ln_b, peer_wq, peer_keys, peer_u, peer_v):
    batch, seq, _ = x_prompt.shape
    dec_batch, dec_seq, _ = x_sample.shape
    past = cache_a_k.shape[2]
    c_rows = jnp.concatenate([c_ctx[None, :], c, jnp.zeros((8 - 1 - dec_batch, D_MODEL), F32)], axis=0)
    mods = _modulation(c_rows, w_mod, b_mod)
    tabs = _static_tables()
    rope = _rope_tables(dec_seq)
    cak = cache_a_k.reshape(dec_batch, DEPTH, past, 128)
    cav = cache_a_v.reshape(dec_batch, DEPTH, past, 128)
    cdk = cache_d_k.reshape(dec_batch, DEPTH, past, 512)
    cdv = cache_d_v.reshape(dec_batch, DEPTH, past, 512)
    c0 = state_c_C.reshape(dec_batch, DEPTH, 8, 128, 128)
    n0 = state_c_n.reshape(dec_batch, DEPTH, 8, 128)
    m0 = jnp.broadcast_to(state_c_m.reshape(dec_batch, DEPTH, 8, 1), (dec_batch, DEPTH, 8, 128))
    cfg_ctx = dict(tq=seq, rows=512, cols=512, key_rows=16, route_cols=min(1024, batch * seq), route_heads=4)
    cfg_lat = dict(tq=min(256, dec_seq), rows=min(512, dec_seq), cols=512, key_rows=16,
                   route_cols=min(1024, dec_seq), route_heads=4)
    y_p = x_prompt.reshape(batch * seq, D_MODEL)
    y_s = x_sample.reshape(dec_batch * dec_seq, D_MODEL)
    state = None
    for l in range(DEPTH):
        p = _layer_params(l, w_in, attn_qk_gain, gmlp_ws, gmlp_b, mlstm_gate_bias, mlstm_gn, diff_lambda, diff_gn,
                          w_branch, w_out, ln_g, ln_b, peer_wq, peer_keys, peer_u, peer_v)
        mod_ctx = mods[l, 0:1].reshape(1, 1, 6 * D_MODEL)
        mod_lat = mods[l, 1:1 + dec_batch].reshape(dec_batch, 1, 6 * D_MODEL)
        y_p, state = _trunk_layer(y_p, mod_ctx, p, tabs, l=l, n_seq=batch, seq=seq, cfg=cfg_ctx, prev_state=state)
        y_s, _ = _trunk_layer(y_s, mod_lat, p, tabs, l=l, n_seq=dec_batch, seq=dec_seq, cfg=cfg_lat,
                              ctx_cache=(rope, cak, cav, cdk, cdv, c0, n0, m0))
    nk = state[0].reshape(batch, DEPTH, seq, A_KV_HEADS, A_HEAD_DIM)
    nv = state[1].reshape(batch, DEPTH, seq, A_KV_HEADS, A_HEAD_DIM)
    ndk = state[2].reshape(batch, DEPTH, seq, D_HEADS, 2, D_HALF_DIM)
    ndv = state[3].reshape(batch, DEPTH, seq, D_HEADS, D_VDIM)
    nc = state[4].reshape(batch, DEPTH, 2, C_HEADS, C_HEAD_DIM, C_HEAD_DIM)
    nn = state[5].reshape(batch, DEPTH, 2, C_HEADS, C_HEAD_DIM)
    nm = state[6][:, :, :, 0].reshape(batch, DEPTH, 2, C_HEADS)
    return (y_p.reshape(batch, seq, D_MODEL), y_s.reshape(dec_batch, dec_seq, D_MODEL), nk, nv, ndk, ndv, nc, nn, nm)
```

```python
import functools
import math

import numpy as np
import jax
import jax.numpy as jnp
from jax import lax
from jax.experimental import pallas as pl
from jax.experimental.pallas import tpu as pltpu

F32 = jnp.float32
BF16 = jnp.bfloat16
HIGHEST = lax.Precision.HIGHEST

D_MODEL = 1024
DEPTH = 4
GRID_W = 64
ROPE_BASE = 10000.0
EPS = 1e-6
BRANCH_WIDTH = D_MODEL // 2
A_HEAD_DIM = 64
A_HEADS = 8
A_KV_HEADS = 2
B_GROUPS = 4
CHUNK = 128
C_HEADS = 4
C_HEAD_DIM = 128
D_HEADS = 4
D_VDIM = 128
D_HALF_DIM = 64
PEER_HEADS = 8
PEER_NKEYS = 128
PEER_EXPERTS = PEER_NKEYS * PEER_NKEYS
PEER_QDIM = 256
PEER_TOPK = 16
ALPHA = (2 * DEPTH) ** 0.25

_OFF_A = 0
_OFF_B = 768
_OFF_C = 1792
_OFF_CG = 3840
_OFF_D = 3856
_OFF_G = 5392

VMEM_LIMIT_BYTES = 56 * 1024 * 1024
NEG_INF = float("-inf")


def _cparams(*sem, flags=None):
    return pltpu.CompilerParams(dimension_semantics=sem, vmem_limit_bytes=VMEM_LIMIT_BYTES, flags=flags)


def _dot(a, b):
    return jnp.dot(a, b, preferred_element_type=F32)


def _dot_nt(a, b):
    return lax.dot_general(a, b, (((1,), (1,)), ((), ())), preferred_element_type=F32)


def _modulated(x, mod_ref, which):
    base = 3 * D_MODEL * which
    sh = mod_ref[:, base:base + D_MODEL]
    sc = mod_ref[:, base + D_MODEL:base + 2 * D_MODEL]
    return x * (1.0 + sc) + sh


def _layer_norm_rows(z, g, b):
    mu = jnp.mean(z, axis=-1, keepdims=True)
    zc = z - mu
    var = jnp.mean(zc * zc, axis=-1, keepdims=True)
    return zc * lax.rsqrt(var + EPS) * g + b


def _rope(x, cos, sin_signed, lane):
    w = x.shape[1]
    nxt = pltpu.roll(x, w - 16, 1)
    prv = pltpu.roll(x, 16, 1)
    partner = jnp.where((lane % 32) < 16, nxt, prv)
    return x * cos + partner * sin_signed


def _tile_lanes(t, n):
    return t if n == 1 else jnp.concatenate([t] * n, axis=1)


def _mod_kernel(c_ref, w_ref, b_ref, o_ref):
    c = c_ref[...]
    s = c * jax.nn.sigmoid(c)
    o_ref[...] = jnp.dot(s, w_ref[...], precision=HIGHEST, preferred_element_type=F32) + b_ref[...]


def _modulation(c_rows, w_mod, b_mod):
    n_col = 6 * D_MODEL // 1024
    return pl.pallas_call(
        _mod_kernel,
        out_shape=jax.ShapeDtypeStruct((DEPTH, 8, 6 * D_MODEL), F32),
        grid=(DEPTH, n_col),
        in_specs=[
            pl.BlockSpec((8, D_MODEL), lambda l, j: (0, 0)),
            pl.BlockSpec((None, D_MODEL, 1024), lambda l, j: (l, 0, j)),
            pl.BlockSpec((None, 1, 1024), lambda l, j: (l, 0, j)),
        ],
        out_specs=pl.BlockSpec((None, 8, 1024), lambda l, j: (l, 0, j)),
        compiler_params=_cparams("arbitrary", "arbitrary"),
        name="modulation",
    )(c_rows, w_mod, b_mod.reshape(DEPTH, 1, 6 * D_MODEL))


def _attn_a_kernel(*refs, seq, tq, n_cache, rope, proj_rows):
    if rope:
        (x_ref, mod_ref, w_ref, gain_ref, bd_ref, rep_ref, cos_ref, sin_ref, ck_ref, cv_ref,
         y_ref, q_s, k_s, v_s) = refs
    else:
        (x_ref, mod_ref, w_ref, gain_ref, bd_ref, rep_ref) = refs[:6]
        (y_ref, nk_ref, nv_ref, q_s, k_s, v_s) = refs[-6:]
    qi = pl.program_id(1)

    @pl.when(qi == 0)
    def _project():
        lane = lax.broadcasted_iota(jnp.int32, (1, 640), 1)
        for r0 in range(0, seq, proj_rows):
            rows = pl.ds(r0, proj_rows)
            h = _modulated(x_ref[rows, :], mod_ref, 0).astype(BF16)
            p = _dot(h, w_ref[...])
            qk = p[:, :640]
            sq = qk * qk
            hi = sq.astype(BF16)
            lo = (sq - hi.astype(F32)).astype(BF16)
            ms = jnp.concatenate([_dot(hi[:, c:c + 128], bd_ref[...]) + _dot(lo[:, c:c + 128], bd_ref[...])
                                  for c in range(0, 640, 128)], axis=1)
            qk = qk * lax.rsqrt(ms + EPS) * gain_ref[...]
            v = p[:, 640:768]
            if not rope:
                nk_ref[rows, :] = qk[:, 512:640]
                nv_ref[rows, :] = v
            else:
                cos = _tile_lanes(cos_ref[rows, :], 5)
                sin = _tile_lanes(sin_ref[rows, :], 5)
                qk = _rope(qk, cos, sin, lane)
            q_s[rows, :] = (qk[:, :512] * (A_HEAD_DIM ** -0.5)).astype(BF16)
            k_s[rows, :] = _dot(qk[:, 512:640].astype(BF16), rep_ref[...]).astype(BF16)
            v_s[rows, :] = _dot(v.astype(BF16), rep_ref[...]).astype(BF16)
        if n_cache:
            crow = pl.ds(seq, n_cache)
            k_s[crow, :] = _dot(ck_ref[...].astype(BF16), rep_ref[...]).astype(BF16)
            v_s[crow, :] = _dot(cv_ref[...].astype(BF16), rep_ref[...]).astype(BF16)

    head_of_lane = lax.broadcasted_iota(jnp.int32, (1, 256), 1) // A_HEAD_DIM
    qb = q_s[pl.ds(pl.multiple_of(qi * tq, tq), tq), :]
    for g in range(A_KV_HEADS):
        cols = slice(g * 256, (g + 1) * 256)
        qg = qb[:, cols]
        kg = k_s[:, cols]
        vg = v_s[:, cols]
        acc = jnp.zeros((tq, 256), F32)
        for r in range(A_HEADS // A_KV_HEADS):
            sel = head_of_lane == r
            qm = jnp.where(sel, qg, jnp.zeros_like(qg))
            s = _dot_nt(qm, kg)
            m = jnp.max(s, axis=-1, keepdims=True)
            p = jnp.exp(s - m)
            l = jnp.sum(p, axis=-1, keepdims=True)
            o = _dot(p.astype(BF16), vg)
            acc = acc + jnp.where(sel, o * (1.0 / l), 0.0)
        y_ref[:, cols] = acc.astype(BF16)


def _carry_aliases(in_specs, args, carry, first_out):
    if carry is None:
        return {}
    aliases = {}
    for k, arr in enumerate(carry):
        aliases[len(args)] = first_out + k
        in_specs.append(pl.BlockSpec(memory_space=pl.ANY))
        args.append(arr)
    return aliases


def _attn_a(x, mod, w_a, gain_row, bd, rep, *, n_seq, seq, tq, rope=None, cache=None, layer=0, carry=None):
    n_q = seq // tq
    n_cache = 0 if cache is None else cache[0].shape[2]
    proj_rows = min(seq, 512)
    kern = functools.partial(_attn_a_kernel, seq=seq, tq=tq, n_cache=n_cache, rope=rope is not None,
                             proj_rows=proj_rows)
    const = lambda s, q: (0, 0)
    in_specs = [
        pl.BlockSpec((seq, D_MODEL), lambda s, q: (s, 0)),
        pl.BlockSpec((None, 1, 6 * D_MODEL), lambda s, q: (s if mod.shape[0] > 1 else 0, 0, 0)),
        pl.BlockSpec(w_a.shape, const),
        pl.BlockSpec(gain_row.shape, const),
        pl.BlockSpec(bd.shape, const),
        pl.BlockSpec(rep.shape, const),
    ]
    args = [x, mod, w_a, gain_row, bd, rep]
    n_tok = n_seq * seq
    y_spec = pl.BlockSpec((tq, BRANCH_WIDTH), lambda s, q: (s * n_q + q, 0))
    y_shape = jax.ShapeDtypeStruct((n_tok, BRANCH_WIDTH), BF16)
    if rope is not None:
        cos, sin = rope
        in_specs += [pl.BlockSpec(cos.shape, const), pl.BlockSpec(sin.shape, const),
                     pl.BlockSpec((None, None, n_cache, 128), lambda s, q: (s, layer, 0, 0)),
                     pl.BlockSpec((None, None, n_cache, 128), lambda s, q: (s, layer, 0, 0))]
        args += [cos, sin, cache[0], cache[1]]
        out_shape, out_specs, aliases = y_shape, y_spec, {}
    else:
        kv_shape = jax.ShapeDtypeStruct((n_seq, DEPTH, seq, 128), F32)
        kv_spec = pl.BlockSpec((None, None, seq, 128), lambda s, q: (s, layer, 0, 0))
        out_shape, out_specs = (y_shape, kv_shape, kv_shape), (y_spec, kv_spec, kv_spec)
        aliases = _carry_aliases(in_specs, args, carry, first_out=1)
    return pl.pallas_call(
        kern, out_shape=out_shape, grid=(n_seq, n_q), in_specs=in_specs, out_specs=out_specs,
        input_output_aliases=aliases,
        scratch_shapes=[pltpu.VMEM((seq, 512), BF16), pltpu.VMEM((seq + n_cache, 512), BF16),
                        pltpu.VMEM((seq + n_cache, 512), BF16)],
        compiler_params=_cparams("arbitrary", "arbitrary"),
        name="branch_a_lat" if rope is not None else "branch_a_ctx",
    )(*args)


def _attn_d_kernel(*refs, seq, tq, n_cache, rope, proj_rows, lam_init):
    if rope:
        (x_ref, mod_ref, w_ref, lam_ref, gn_ref, cos_ref, sin_ref, ck_ref, cv_ref,
         y_ref, q_s, k_s, v_s) = refs
    else:
        (x_ref, mod_ref, w_ref, lam_ref, gn_ref) = refs[:5]
        (y_ref, nk_ref, nv_ref, q_s, k_s, v_s) = refs[-6:]
    qi = pl.program_id(1)

    @pl.when(qi == 0)
    def _project():
        lane = lax.broadcasted_iota(jnp.int32, (1, 512), 1)
        for r0 in range(0, seq, proj_rows):
            rows = pl.ds(r0, proj_rows)
            h = _modulated(x_ref[rows, :], mod_ref, 0).astype(BF16)
            p = _dot(h, w_ref[...])
            dq, dk, dv = p[:, :512], p[:, 512:1024], p[:, 1024:1536]
            if not rope:
                nk_ref[rows, :] = dk
                nv_ref[rows, :] = dv
            else:
                cos = _tile_lanes(cos_ref[rows, :], 4)
                sin = _tile_lanes(sin_ref[rows, :], 4)
                dq = _rope(dq, cos, sin, lane)
                dk = _rope(dk, cos, sin, lane)
            q_s[rows, :] = (dq * (D_HALF_DIM ** -0.5)).astype(BF16)
            k_s[rows, :] = dk.astype(BF16)
            v_s[rows, :] = dv.astype(BF16)
        if n_cache:
            crow = pl.ds(seq, n_cache)
            k_s[crow, :] = ck_ref[...].astype(BF16)
            v_s[crow, :] = cv_ref[...].astype(BF16)

    lv = lam_ref[...]
    lam = (jnp.exp(jnp.sum(lv[0:1] * lv[1:2], axis=-1, keepdims=True))
           - jnp.exp(jnp.sum(lv[2:3] * lv[3:4], axis=-1, keepdims=True)) + lam_init)
    half_of_lane = lax.broadcasted_iota(jnp.int32, (1, 128), 1) // D_HALF_DIM
    qb = q_s[pl.ds(pl.multiple_of(qi * tq, tq), tq), :]
    for hd in range(D_HEADS):
        cols = slice(hd * 128, (hd + 1) * 128)
        qh = qb[:, cols]
        kh = k_s[:, cols]
        vh = v_s[:, cols]
        probs = []
        for j in range(2):
            qm = jnp.where(half_of_lane == j, qh, jnp.zeros_like(qh))
            s = _dot_nt(qm, kh)
            m = jnp.max(s, axis=-1, keepdims=True)
            p = jnp.exp(s - m)
            l = jnp.sum(p, axis=-1, keepdims=True)
            probs.append(p * (1.0 / l))
        a = (probs[0] - lam * probs[1]).astype(BF16)
        o = _dot(a, vh)
        ms = jnp.mean(o * o, axis=-1, keepdims=True)
        o = o * lax.rsqrt(ms + EPS) * gn_ref[:, cols] * (1.0 - lam_init)
        y_ref[:, cols] = o.astype(BF16)


def _attn_d(x, mod, w_d, lam_params, gn_row, *, n_seq, seq, tq, lam_init, rope=None, cache=None, layer=0,
            carry=None):
    n_q = seq // tq
    n_cache = 0 if cache is None else cache[0].shape[2]
    proj_rows = min(seq, 512)
    kern = functools.partial(_attn_d_kernel, seq=seq, tq=tq, n_cache=n_cache, rope=rope is not None,
                             proj_rows=proj_rows, lam_init=lam_init)
    const = lambda s, q: (0, 0)
    in_specs = [
        pl.BlockSpec((seq, D_MODEL), lambda s, q: (s, 0)),
        pl.BlockSpec((None, 1, 6 * D_MODEL), lambda s, q: (s if mod.shape[0] > 1 else 0, 0, 0)),
        pl.BlockSpec(w_d.shape, const),
        pl.BlockSpec(lam_params.shape, const),
        pl.BlockSpec(gn_row.shape, const),
    ]
    args = [x, mod, w_d, lam_params, gn_row]
    n_tok = n_seq * seq
    y_spec = pl.BlockSpec((tq, BRANCH_WIDTH), lambda s, q: (s * n_q + q, 0))
    y_shape = jax.ShapeDtypeStruct((n_tok, BRANCH_WIDTH), BF16)
    if rope is not None:
        cos, sin = rope
        in_specs += [pl.BlockSpec(cos.shape, const), pl.BlockSpec(sin.shape, const),
                     pl.BlockSpec((None, None, n_cache, 512), lambda s, q: (s, layer, 0, 0)),
                     pl.BlockSpec((None, None, n_cache, 512), lambda s, q: (s, layer, 0, 0))]
        args += [cos, sin, cache[0], cache[1]]
        out_shape, out_specs, aliases = y_shape, y_spec, {}
    else:
        kv_shape = jax.ShapeDtypeStruct((n_seq, DEPTH, seq, 512), F32)
        kv_spec = pl.BlockSpec((None, None, seq, 512), lambda s, q: (s, layer, 0, 0))
        out_shape, out_specs = (y_shape, kv_shape, kv_shape), (y_spec, kv_spec, kv_spec)
        aliases = _carry_aliases(in_specs, args, carry, first_out=1)
    return pl.pallas_call(
        kern, out_shape=out_shape, grid=(n_seq, n_q), in_specs=in_specs, out_specs=out_specs,
        input_output_aliases=aliases,
        scratch_shapes=[pltpu.VMEM((seq, 512), BF16), pltpu.VMEM((seq + n_cache, 512), BF16),
                        pltpu.VMEM((seq + n_cache, 512), BF16)],
        compiler_params=_cparams("arbitrary", "arbitrary"),
        name="branch_d_lat" if rope is not None else "branch_d_ctx",
    )(*args)


def _gmlp_kernel(x_ref, mod_ref, w_ref, ws_ref, bias_ref, y_ref, *, rows):
    h = _modulated(x_ref[...], mod_ref, 0).astype(BF16)
    p = _dot(h, w_ref[...])
    u, v = p[:, :BRANCH_WIDTH], p[:, BRANCH_WIDTH:]
    mu = jnp.mean(v, axis=-1, keepdims=True)
    vc = v - mu
    var = jnp.mean(vc * vc, axis=-1, keepdims=True)
    vn = (vc * lax.rsqrt(var + EPS)).astype(BF16)
    for c in range(rows // CHUNK):
        rs = slice(c * CHUNK, (c + 1) * CHUNK)
        for g in range(B_GROUPS):
            cs = slice(g * 128, (g + 1) * 128)
            s = _dot(ws_ref[g].astype(BF16), vn[rs, cs]) + bias_ref[:, cs]
            y_ref[rs, cs] = (u[rs, cs] * s).astype(BF16)


def _gmlp(x, mod, w_b, ws, bias_full, *, seq, rows):
    n_tok = x.shape[0]
    per_seq = seq // rows
    kern = functools.partial(_gmlp_kernel, rows=rows)
    return pl.pallas_call(
        kern, out_shape=jax.ShapeDtypeStruct((n_tok, BRANCH_WIDTH), BF16),
        grid=(n_tok // rows,),
        in_specs=[
            pl.BlockSpec((rows, D_MODEL), lambda i: (i, 0)),
            pl.BlockSpec((None, 1, 6 * D_MODEL), lambda i: (i // per_seq if mod.shape[0] > 1 else 0, 0, 0)),
            pl.BlockSpec(w_b.shape, lambda i: (0, 0)),
            pl.BlockSpec(ws.shape, lambda i: (0, 0, 0)),
            pl.BlockSpec(bias_full.shape, lambda i: (0, 0)),
        ],
        out_specs=pl.BlockSpec((rows, BRANCH_WIDTH), lambda i: (i, 0)),
        compiler_params=_cparams("arbitrary"),
        name="branch_b",
    )(x, mod, w_b, ws, bias_full)


def _mlstm_t_kernel(*refs, seq, has_init, proj_rows):
    if has_init:
        (x_ref, mod_ref, wn_ref, wt_ref, gbr_ref, gbt_ref, gnt_ref, tril_ref, triu_ref, c0_ref, n0_ref, m0_ref,
         y_ref, q_s, k_s, qt_s, vt_s, ot_s, g_s, gt_s, hf_s, hb_s, st_s, m_s) = refs
    else:
        (x_ref, mod_ref, wn_ref, wt_ref, gbr_ref, gbt_ref, gnt_ref, tril_ref, triu_ref) = refs[:9]
        (y_ref, cout_ref, nout_ref, mout_ref,
         q_s, k_s, qt_s, vt_s, ot_s, g_s, gt_s, hf_s, hb_s, st_s, m_s) = refs[-15:]
    n_chunk = seq // CHUNK
    for r0 in range(0, seq, proj_rows):
        span = pl.ds(r0, proj_rows)
        h = _modulated(x_ref[span, :], mod_ref, 0).astype(BF16)
        p = _dot(h, wn_ref[...])
        q_s[span, :] = p[:, 0:512].astype(BF16)
        k_s[span, :] = (p[:, 512:1024] * (C_HEAD_DIM ** -0.5)).astype(BF16)
        g_s[span, :] = p[:, 1024:1152] + gbr_ref[...]
        pt = _dot_nt(wt_ref[...], h)
        qt_s[:, span] = pt[0:512].astype(BF16)
        vt_s[:, span] = pt[512:1024].astype(BF16)
        ot_s[:, span] = jax.nn.sigmoid(pt[1024:1536]).astype(BF16)
        gt_s[:, span] = pt[1536:1664] + _tile_lanes(gbt_ref[...], proj_rows // 128)

    first_row = lax.broadcasted_iota(jnp.int32, (CHUNK, CHUNK), 0) == 0
    for sidx in range(2 * C_HEADS):
        if has_init:
            st_s[sidx, 0:128, :] = c0_ref[sidx].T
            st_s[sidx, 128:256, :] = jnp.where(first_row, jnp.broadcast_to(n0_ref[sidx:sidx + 1, :], (CHUNK, 128)), 0.0)
        else:
            st_s[sidx] = jnp.zeros((2 * CHUNK, 128), F32)
    m_s[...] = m0_ref[...] if has_init else jnp.zeros_like(m_s)

    tril = tril_ref[...]
    triu = triu_ref[...]
    row_i = lax.broadcasted_iota(jnp.int32, (CHUNK, CHUNK), 0)
    col_i = lax.broadcasted_iota(jnp.int32, (CHUNK, CHUNK), 1)
    visible = (row_i <= col_i, row_i >= col_i)
    ones_blk = jnp.where(first_row, 1.0, 0.0).astype(BF16)

    def chunk_step(c, carry):
        prep = []
        for direction in range(2):
            cc = c if direction == 0 else n_chunk - 1 - c
            span = pl.ds(pl.multiple_of(cc * CHUNK, CHUNK), CHUNK)
            gates = g_s[span, :]
            gates_t = gt_s[:, span]
            logf = jnp.minimum(gates, 0.0) - jnp.log1p(jnp.exp(-jnp.abs(gates)))
            logf_t = jnp.minimum(gates_t, 0.0) - jnp.log1p(jnp.exp(-jnp.abs(gates_t)))
            tri_col = tril if direction == 0 else triu
            tri_row = triu if direction == 0 else tril
            b_col_all = jnp.dot(tri_col, logf, precision=HIGHEST, preferred_element_type=F32)
            b_row_all = jnp.dot(logf_t, tri_row, precision=HIGHEST, preferred_element_type=F32)
            r_col_all = gates - pltpu.roll(b_col_all, 128 - 4, 1)
            base = direction * 8
            prep.append((span, r_col_all, gates_t[base:base + 4, :], b_row_all[base + 4:base + 8, :]))
        for hd in range(C_HEADS):
            for direction in range(2):
                span, r_col_all, i_rows, b_rows = prep[direction]
                base = direction * 8
                last = CHUNK - 1 if direction == 0 else 0
                h_out = hf_s if direction == 0 else hb_s
                sidx = direction * 4 + hd
                blk = slice(hd * 128, (hd + 1) * 128)
                r_col = r_col_all[:, base + hd:base + hd + 1]
                b_row = b_rows[hd:hd + 1, :]
                i_row = i_rows[hd:hd + 1, :]
                m_prev = m_s[sidx:sidx + 1, 0:1]
                r_wide = jnp.broadcast_to(r_col, (CHUNK, CHUNK))
                peak = jnp.max(jnp.where(visible[direction], r_wide, NEG_INF), axis=0, keepdims=True)
                m_row = b_row + jnp.maximum(peak, m_prev)
                qh = q_s[span, blk]
                kh = k_s[span, blk]
                arg = jnp.where(visible[direction], r_wide + (b_row - m_row), NEG_INF)
                w_t = (_dot_nt(kh, qh) * jnp.exp(arg)).astype(BF16)
                inter = jnp.exp(b_row + m_prev - m_row)
                v_one = jnp.concatenate([vt_s[blk, span], ones_blk], axis=0)
                state = st_s[sidx]
                both = _dot(v_one, w_t) + inter * _dot(state.astype(BF16), qt_s[blk, span])
                den = both[128:129, :]
                h_out[blk, span] = both[0:128, :] / jnp.maximum(jnp.abs(den), jnp.exp(-m_row))
                b_last = b_row[:, last:last + 1]
                g_row = b_last - b_row + i_row
                m_new = jnp.maximum(b_last + m_prev, jnp.max(g_row, axis=-1, keepdims=True))
                decay = jnp.exp(b_last + m_prev - m_new)
                scaled = v_one * jnp.exp(g_row - m_new).astype(BF16)
                st_s[sidx] = decay * state + _dot(scaled, kh)
                m_s[sidx:sidx + 1, :] = jnp.broadcast_to(m_new, (1, 128))
        return carry

    lax.fori_loop(0, n_chunk, chunk_step, 0)

    for r0 in range(0, seq, proj_rows):
        span = pl.ds(r0, proj_rows)
        for hd in range(C_HEADS):
            blk = slice(hd * 128, (hd + 1) * 128)
            hh = hf_s[blk, span] + hb_s[blk, span]
            mu = jnp.mean(hh, axis=0, keepdims=True)
            hc = hh - mu
            var = jnp.mean(hc * hc, axis=0, keepdims=True)
            gn = _tile_lanes(gnt_ref[blk, :], proj_rows // 128)
            y_t = hc * lax.rsqrt(var + EPS) * gn * ot_s[blk, span].astype(F32)
            y_ref[span, blk] = y_t.T.astype(BF16)
    if not has_init:
        for sidx in range(2 * C_HEADS):
            cout_ref[sidx] = st_s[sidx, 0:128, :].T
            nout_ref[sidx:sidx + 1, :] = st_s[sidx, 128:129, :]
        mout_ref[...] = m_s[...]


def _mlstm_t(x, mod, w_nat, w_t, gate_bias_row, gate_bias_t, gn_t, tril, triu, *, n_seq, seq, init=None, layer=0,
             carry=None):
    proj_rows = min(seq, 512)
    kern = functools.partial(_mlstm_t_kernel, seq=seq, has_init=init is not None, proj_rows=proj_rows)
    const = lambda s: (0, 0)
    in_specs = [
        pl.BlockSpec((seq, D_MODEL), lambda s: (s, 0)),
        pl.BlockSpec((None, 1, 6 * D_MODEL), lambda s: (s if mod.shape[0] > 1 else 0, 0, 0)),
    ] + [pl.BlockSpec(a.shape, const) for a in (w_nat, w_t, gate_bias_row, gate_bias_t, gn_t, tril, triu)]
    args = [x, mod, w_nat, w_t, gate_bias_row, gate_bias_t, gn_t, tril, triu]
    n_tok = n_seq * seq
    y_shape = jax.ShapeDtypeStruct((n_tok, BRANCH_WIDTH), BF16)
    y_spec = pl.BlockSpec((seq, BRANCH_WIDTH), lambda s: (s, 0))
    if init is not None:
        c0, n0, m0 = init
        in_specs += [pl.BlockSpec((None, None, 8, 128, 128), lambda s: (s, layer, 0, 0, 0)),
                     pl.BlockSpec((None, None, 8, 128), lambda s: (s, layer, 0, 0)),
                     pl.BlockSpec((None, None, 8, 128), lambda s: (s, layer, 0, 0))]
        args += [c0, n0, m0]
        out_shape, out_specs, aliases = y_shape, y_spec, {}
    else:
        out_shape = (y_shape, jax.ShapeDtypeStruct((n_seq, DEPTH, 8, 128, 128), F32),
                     jax.ShapeDtypeStruct((n_seq, DEPTH, 8, 128), F32),
                     jax.ShapeDtypeStruct((n_seq, DEPTH, 8, 128), F32))
        out_specs = (y_spec, pl.BlockSpec((None, None, 8, 128, 128), lambda s: (s, layer, 0, 0, 0)),
                     pl.BlockSpec((None, None, 8, 128), lambda s: (s, layer, 0, 0)),
                     pl.BlockSpec((None, None, 8, 128), lambda s: (s, layer, 0, 0)))
        aliases = _carry_aliases(in_specs, args, carry, first_out=1)
    return pl.pallas_call(
        kern, out_shape=out_shape, grid=(n_seq,), in_specs=in_specs, out_specs=out_specs,
        input_output_aliases=aliases,
        scratch_shapes=[pltpu.VMEM((seq, 512), BF16), pltpu.VMEM((seq, 512), BF16),
                        pltpu.VMEM((512, seq), BF16), pltpu.VMEM((512, seq), BF16),
                        pltpu.VMEM((512, seq), BF16),
                        pltpu.VMEM((seq, 128), F32), pltpu.VMEM((128, seq), F32),
                        pltpu.VMEM((512, seq), F32), pltpu.VMEM((512, seq), F32),
                        pltpu.VMEM((8, 256, 128), F32), pltpu.VMEM((8, 128), F32)],
        compiler_params=_cparams("arbitrary"),
        name="branch_c_lat" if init is not None else "branch_c_ctx",
    )(*args)


def _merge_kernel(x_ref, mod_ref, ya_ref, yb_ref, yc_ref, yd_ref, wg_ref, wbr_ref, wout_ref, lng_ref, lnb_ref,
                  o_ref):
    x = x_ref[...]
    h = _modulated(x, mod_ref, 0).astype(BF16)
    mix = None
    for n, y_ref in enumerate((ya_ref, yb_ref, yc_ref, yd_ref)):
        gate = jax.nn.sigmoid(_dot(h, wg_ref[:, n * D_MODEL:(n + 1) * D_MODEL]))
        term = gate * _dot(y_ref[...], wbr_ref[n])
        mix = term if mix is None else mix + term
    out = _dot(mix.astype(BF16), wout_ref[...])
    g1 = mod_ref[:, 2 * D_MODEL:3 * D_MODEL]
    o_ref[...] = _layer_norm_rows(ALPHA * x + g1 * out, lng_ref[...], lnb_ref[...])


def _merge(x, mod, ys, w_g, w_br, w_out, ln_g, ln_b, *, seq, rows):
    n_tok = x.shape[0]
    per_seq = seq // rows
    tok = lambda i: (i, 0)
    c2 = lambda i: (0, 0)
    return pl.pallas_call(
        _merge_kernel, out_shape=jax.ShapeDtypeStruct((n_tok, D_MODEL), F32),
        grid=(n_tok // rows,),
        in_specs=[
            pl.BlockSpec((rows, D_MODEL), tok),
            pl.BlockSpec((None, 1, 6 * D_MODEL), lambda i: (i // per_seq if mod.shape[0] > 1 else 0, 0, 0)),
            pl.BlockSpec((rows, BRANCH_WIDTH), tok), pl.BlockSpec((rows, BRANCH_WIDTH), tok),
            pl.BlockSpec((rows, BRANCH_WIDTH), tok), pl.BlockSpec((rows, BRANCH_WIDTH), tok),
            pl.BlockSpec(w_g.shape, c2), pl.BlockSpec(w_br.shape, lambda i: (0, 0, 0)),
            pl.BlockSpec(w_out.shape, c2), pl.BlockSpec(ln_g.shape, c2), pl.BlockSpec(ln_b.shape, c2),
        ],
        out_specs=pl.BlockSpec((rows, D_MODEL), tok),
        compiler_params=_cparams("arbitrary"),
        name="merge",
    )(x, mod, *ys, w_g, w_br, w_out, ln_g, ln_b)


_TAKEN = -(2.0 ** 127)


def _top16(s):
    cur = s
    vals = []
    for r in range(PEER_TOPK):
        mx = jnp.max(cur, axis=0, keepdims=True)
        cur = jnp.where(cur == mx, _TAKEN * (1.0 + r / 32.0), cur)
        vals.append(mx)
    rank = jnp.where(cur <= _TAKEN, cur * (32.0 / _TAKEN) - 31.0, float(PEER_TOPK + 1))
    return jnp.concatenate(vals, axis=0), rank


def _pair_tables():
    pairs = [(k1, k2) for k1 in range(PEER_TOPK) for k2 in range(PEER_TOPK // (k1 + 1))]
    n = 56
    sel_a = np.zeros((n, PEER_TOPK), np.float32)
    sel_b = np.zeros((n, PEER_TOPK), np.float32)
    pad = np.full((n, 1), NEG_INF, np.float32)
    for row, (k1, k2) in enumerate(pairs):
        sel_a[row, k1] = 1.0
        sel_b[row, k2] = 1.0
        pad[row, 0] = 0.0
    return jnp.asarray(sel_a), jnp.asarray(sel_b), jnp.asarray(pad), jnp.asarray(sel_a.T, BF16)


def _route_kernel(x_ref, mod_ref, wq_ref, keys_ref, sela_ref, selb_ref, pad_ref, ind_ref,
                  cnt_ref, e1_ref, r2_ref, e2_ref, h2_s, *, heads):
    @pl.when(pl.program_id(1) == 0)
    def _modulate():
        h2_s[...] = _modulated(x_ref[...], mod_ref, 1).astype(BF16)

    pick = lambda sel_ref, v: jnp.dot(sel_ref[...], v, precision=HIGHEST, preferred_element_type=F32)
    q_all = _dot(h2_s[...], wq_ref[...])
    for hd in range(heads):
        q = q_all[:, hd * PEER_QDIM:(hd + 1) * PEER_QDIM]
        s1 = _dot_nt(keys_ref[hd, 0].astype(BF16), q[:, :128].astype(BF16))
        s2 = _dot_nt(keys_ref[hd, 1].astype(BF16), q[:, 128:].astype(BF16))
        a, rank1 = _top16(s1)
        b, rank2 = _top16(s2)
        ea = jnp.exp(a - a[0:1])
        eb = jnp.exp(b - b[0:1])
        cand = pick(sela_ref, a) + pick(selb_ref, b) + pad_ref[...]
        gate = pick(sela_ref, ea) * pick(selb_ref, eb)
        cur = cand
        thr = None
        for _ in range(PEER_TOPK):
            thr = jnp.max(cur, axis=0, keepdims=True)
            cur = jnp.where(cur == thr, NEG_INF, cur)
        chosen = cand >= thr
        z = jnp.sum(jnp.where(chosen, gate, 0.0), axis=0, keepdims=True)
        cnt_sorted = _dot(ind_ref[...], jnp.where(chosen, 1.0, 0.0).astype(BF16))
        cnt = jnp.zeros_like(s1)
        for r in range(PEER_TOPK):
            cnt = jnp.where(rank1 == float(r + 1), cnt_sorted[r:r + 1], cnt)
        cnt_ref[hd] = cnt
        e1_ref[hd] = jnp.where(rank1 <= float(PEER_TOPK), jnp.exp(s1 - a[0:1]) * (0.5 / z), 0.0)
        packed = (PEER_NKEYS // 16, 16, s2.shape[1])
        r2_ref[hd] = rank2.astype(BF16).reshape(packed)
        e2_ref[hd] = jnp.where(rank2 <= float(PEER_TOPK), jnp.exp(s2 - b[0:1]), 0.0).astype(BF16).reshape(packed)


def _route(x1, mod, wq, keys, *, seq, cols, heads):
    n_tok = x1.shape[0]
    per_seq = seq // cols
    tables = _pair_tables()
    row_shape = jax.ShapeDtypeStruct((PEER_HEADS, PEER_NKEYS, n_tok), F32)
    col_shape = jax.ShapeDtypeStruct((PEER_HEADS, PEER_NKEYS // 16, 16, n_tok), BF16)
    col_spec = pl.BlockSpec((heads, PEER_NKEYS // 16, 16, cols), lambda i, h: (h, 0, 0, i))
    spec = pl.BlockSpec((heads, PEER_NKEYS, cols), lambda i, h: (h, 0, i))
    return pl.pallas_call(
        functools.partial(_route_kernel, heads=heads), out_shape=(row_shape, row_shape, col_shape, col_shape),
        grid=(n_tok // cols, PEER_HEADS // heads),
        in_specs=[
            pl.BlockSpec((cols, D_MODEL), lambda i, h: (i, 0)),
            pl.BlockSpec((None, 1, 6 * D_MODEL), lambda i, h: (i // per_seq if mod.shape[0] > 1 else 0, 0, 0)),
            pl.BlockSpec((D_MODEL, heads * PEER_QDIM), lambda i, h: (0, h)),
            pl.BlockSpec((heads, 2, PEER_NKEYS, PEER_QDIM // 2), lambda i, h: (h, 0, 0, 0)),
        ] + [pl.BlockSpec(t.shape, lambda i, h: (0, 0)) for t in tables],
        out_specs=(spec, spec, col_spec, col_spec),
        scratch_shapes=[pltpu.VMEM((cols, D_MODEL), BF16)],
        compiler_params=_cparams("arbitrary", "arbitrary"),
        name="peer_route",
    )(x1, mod, wq, keys, *tables)


def _peer_kernel(x_ref, mod_ref, u_ref, v_ref, cnt_ref, e1_ref, r2_ref, e2_ref, lng_ref, lnb_ref,
                 o_ref, h2t_s, acc_s, *, key_rows):
    e = pl.program_id(1)

    @pl.when(e == 0)
    def _init():
        h2 = _modulated(x_ref[...], mod_ref, 1)
        h2t_s[...] = h2.T.astype(BF16)
        acc_s[...] = jnp.zeros_like(acc_s)

    n_tok = h2t_s.shape[1]
    first_key = pl.multiple_of(e * key_rows, key_rows)
    zero = jnp.zeros((PEER_NKEYS // 16, 16, n_tok), BF16)
    act = _dot(u_ref[...], h2t_s[...]).astype(BF16)
    act = act * (1.0 + lax.erf(act * (2.0 ** -0.5)))
    pieces = []
    for r in range(key_rows):
        g = None
        for hd in range(PEER_HEADS):
            cnt_blk = cnt_ref[hd, pl.ds(first_key, key_rows), :]
            e1_blk = e1_ref[hd, pl.ds(first_key, key_rows), :]
            cnt_rows = jnp.broadcast_to(cnt_blk[r:r + 1, :], (16, n_tok)).astype(BF16)
            e1_rows = jnp.broadcast_to(e1_blk[r:r + 1, :], (16, n_tok)).astype(BF16)
            term = jnp.where(r2_ref[hd] <= cnt_rows[None], e2_ref[hd], zero) * e1_rows[None]
            g = term if g is None else g + term
        pieces.append(g.reshape(PEER_NKEYS, n_tok) * act[r * PEER_NKEYS:(r + 1) * PEER_NKEYS, :])
    acc_s[...] += lax.dot_general(v_ref[...], jnp.concatenate(pieces, axis=0), (((0,), (0,)), ((), ())),
                                  preferred_element_type=F32)

    @pl.when(e == pl.num_programs(1) - 1)
    def _finish():
        x = x_ref[...]
        g2 = mod_ref[:, 5 * D_MODEL:6 * D_MODEL]
        o_ref[...] = _layer_norm_rows(ALPHA * x + g2 * acc_s[...].T, lng_ref[...], lnb_ref[...])


def _peer(x1, mod, u, v, route, ln_g, ln_b, *, seq, cols, key_rows):
    n_tok = x1.shape[0]
    per_seq = seq // cols
    n_exp = key_rows * PEER_NKEYS
    n_tiles = PEER_EXPERTS // n_exp
    assert key_rows % 8 == 0
    kern = functools.partial(_peer_kernel, key_rows=key_rows)
    cnt, e1, r2, e2 = route
    rspec = pl.BlockSpec((PEER_HEADS, PEER_NKEYS, cols), lambda i, e: (0, 0, i))
    cspec = pl.BlockSpec((PEER_HEADS, PEER_NKEYS // 16, 16, cols), lambda i, e: (0, 0, 0, i))
    return pl.pallas_call(
        kern, out_shape=jax.ShapeDtypeStruct((n_tok, D_MODEL), F32),
        grid=(n_tok // cols, n_tiles),
        in_specs=[
            pl.BlockSpec((cols, D_MODEL), lambda i, e: (i, 0)),
            pl.BlockSpec((None, 1, 6 * D_MODEL), lambda i, e: (i // per_seq if mod.shape[0] > 1 else 0, 0, 0)),
            pl.BlockSpec((n_exp, D_MODEL), lambda i, e: (e, 0)),
            pl.BlockSpec((n_exp, D_MODEL), lambda i, e: (e, 0)),
            rspec, rspec, cspec, cspec,
            pl.BlockSpec(ln_g.shape, lambda i, e: (0, 0)), pl.BlockSpec(ln_b.shape, lambda i, e: (0, 0)),
        ],
        out_specs=pl.BlockSpec((cols, D_MODEL), lambda i, e: (i, 0)),
        scratch_shapes=[pltpu.VMEM((D_MODEL, cols), BF16), pltpu.VMEM((D_MODEL, cols), F32)],
        compiler_params=_cparams("arbitrary", "arbitrary"),
        name="peer_experts",
    )(x1, mod, u, v, cnt, e1, r2, e2, ln_g, ln_b)


def _rope_tables(seq):
    t = np.arange(seq)
    pos = np.stack([t // GRID_W, t % GRID_W], axis=1).astype(np.float64)
    inv = ROPE_BASE ** (-np.arange(16, dtype=np.float64) / 16)
    lane = np.arange(64)
    ang = pos[:, lane // 32] * inv[lane % 16][None, :]
    sign = np.where((lane % 32) < 16, -1.0, 1.0)[None, :]
    cos = np.tile(np.cos(ang), (1, 2)).astype(np.float32)
    sin = np.tile(np.sin(ang) * sign, (1, 2)).astype(np.float32)
    return jnp.asarray(cos), jnp.asarray(sin)


def _static_tables():
    lane = np.arange(128)
    bd = (lane[:, None] // 64 == lane[None, :] // 64).astype(np.float32) / 64.0
    src = np.arange(128)
    dst = np.arange(512)
    rep = ((src[:, None] // 64 == dst[None, :] // 256) & (src[:, None] % 64 == dst[None, :] % 64))
    idx = np.arange(CHUNK)
    tril = (idx[None, :] <= idx[:, None]).astype(np.float32)
    triu = (idx[None, :] >= idx[:, None]).astype(np.float32)
    return (jnp.asarray(bd, BF16), jnp.asarray(rep.astype(np.float32), BF16), jnp.asarray(tril), jnp.asarray(triu))


def _layer_params(l, w_in, attn_qk_gain, gmlp_ws, gmlp_b, mlstm_gate_bias, mlstm_gn, diff_lambda, diff_gn,
                  w_branch, w_out, ln_g, ln_b, peer_wq, peer_keys, peer_u, peer_v):
    w = w_in[l]
    p = {}
    p["w_a"] = w[:, _OFF_A:_OFF_B].astype(BF16)
    p["w_b"] = w[:, _OFF_B:_OFF_C].astype(BF16)
    cq, ck = w[:, _OFF_C:_OFF_C + 512], w[:, _OFF_C + 512:_OFF_C + 1024]
    cv, co = w[:, _OFF_C + 1024:_OFF_C + 1536], w[:, _OFF_C + 1536:_OFF_C + 2048]
    cg = jnp.concatenate([w[:, _OFF_CG:_OFF_D], jnp.zeros((D_MODEL, 112), F32)], axis=1)
    p["w_c_nat"] = jnp.concatenate([cq, ck, cg], axis=1).astype(BF16)
    p["w_c_t"] = jnp.concatenate([cq, cv, co, cg], axis=1).T.astype(BF16)
    p["w_d"] = w[:, _OFF_D:_OFF_G].astype(BF16)
    p["w_g"] = w[:, _OFF_G:].astype(BF16)
    gain = attn_qk_gain[l]
    p["gain_row"] = jnp.concatenate([jnp.tile(gain[0], A_HEADS), jnp.tile(gain[1], A_KV_HEADS)])[None, :]
    p["ws"] = gmlp_ws[l]
    p["bias_full"] = jnp.repeat(gmlp_b[l].T, 128, axis=1)
    p["gate_bias_row"] = jnp.concatenate([mlstm_gate_bias[l].reshape(16), jnp.zeros((112,), F32)])[None, :]
    p["gate_bias_t"] = jnp.broadcast_to(p["gate_bias_row"].reshape(128, 1), (128, 128))
    p["mlstm_gn_t"] = jnp.broadcast_to(mlstm_gn[l].reshape(BRANCH_WIDTH, 1), (BRANCH_WIDTH, 128))
    p["lam"] = diff_lambda[l]
    p["diff_gn_row"] = diff_gn[l].reshape(1, BRANCH_WIDTH)
    p["w_br"] = w_branch[l].astype(BF16)
    p["w_out"] = w_out[l].astype(BF16)
    p["ln_g0"], p["ln_b0"] = ln_g[l, 0][None, :], ln_b[l, 0][None, :]
    p["ln_g1"], p["ln_b1"] = ln_g[l, 1][None, :], ln_b[l, 1][None, :]
    p["wq"] = peer_wq[l].astype(BF16)
    p["keys"] = peer_keys[l]
    p["u"] = peer_u[l].astype(BF16)
    p["v"] = peer_v[l].astype(BF16)
    return p


def _trunk_layer(x, mod, p, tabs, *, l, n_seq, seq, cfg, ctx_cache=None, prev_state=None):
    bd, rep, tril, triu = tabs
    lam_init = 0.8 - 0.6 * math.exp(-0.3 * l)
    state = None
    if ctx_cache is None:
        prev = (None, None, None) if prev_state is None else (prev_state[0:2], prev_state[2:4], prev_state[4:7])
        ya, nk, nv = _attn_a(x, mod, p["w_a"], p["gain_row"], bd, rep, n_seq=n_seq, seq=seq, tq=cfg["tq"],
                             layer=l, carry=prev[0])
        yd, ndk, ndv = _attn_d(x, mod, p["w_d"], p["lam"], p["diff_gn_row"], n_seq=n_seq, seq=seq, tq=cfg["tq"],
                               lam_init=lam_init, layer=l, carry=prev[1])
        yc, c_new, n_new, m_new = _mlstm_t(x, mod, p["w_c_nat"], p["w_c_t"], p["gate_bias_row"], p["gate_bias_t"],
                                           p["mlstm_gn_t"], tril, triu, n_seq=n_seq, seq=seq, layer=l, carry=prev[2])
        state = (nk, nv, ndk, ndv, c_new, n_new, m_new)
    else:
        rope, cak, cav, cdk, cdv, c0, n0, m0 = ctx_cache
        ya = _attn_a(x, mod, p["w_a"], p["gain_row"], bd, rep, n_seq=n_seq, seq=seq, tq=cfg["tq"],
                     rope=rope, cache=(cak, cav), layer=l)
        yd = _attn_d(x, mod, p["w_d"], p["lam"], p["diff_gn_row"], n_seq=n_seq, seq=seq, tq=cfg["tq"],
                     lam_init=lam_init, rope=rope, cache=(cdk, cdv), layer=l)
        yc = _mlstm_t(x, mod, p["w_c_nat"], p["w_c_t"], p["gate_bias_row"], p["gate_bias_t"], p["mlstm_gn_t"],
                      tril, triu, n_seq=n_seq, seq=seq, init=(c0, n0, m0), layer=l)
    yb = _gmlp(x, mod, p["w_b"], p["ws"], p["bias_full"], seq=seq, rows=cfg["rows"])
    x1 = _merge(x, mod, (ya, yb, yc, yd), p["w_g"], p["w_br"], p["w_out"], p["ln_g0"], p["ln_b0"],
                seq=seq, rows=cfg["rows"])
    route = _route(x1, mod, p["wq"], p["keys"], seq=seq, cols=cfg["route_cols"], heads=cfg["route_heads"])
    x2 = _peer(x1, mod, p["u"], p["v"], route, p["ln_g1"], p["ln_b1"], seq=seq, cols=cfg["cols"],
               key_rows=cfg["key_rows"])
    return x2, state


def kernel(x_prompt, x_sample, cache_a_k, cache_a_v, cache_d_k, cache_d_v, state_c_C, state_c_n, state_c_m,
           c, c_ctx, w_mod, b_mod, w_in, attn_qk_gain, gmlp_ws, gmlp_b, mlstm_gate_bias, mlstm_gn,
           diff_lambda, diff_gn, w_branch, w_out, ln_g, ln_b, peer_wq, peer_keys, peer_u, peer_v):
    batch, seq, _ = x_prompt.shape
    dec_batch, dec_seq, _ = x_sample.shape
    past = cache_a_k.shape[2]
    c_rows = jnp.concatenate([c_ctx[None, :], c, jnp.zeros((8 - 1 - dec_batch, D_MODEL), F32)], axis=0)
    mods = _modulation(c_rows, w_mod, b_mod)
    tabs = _static_tables()
    rope = _rope_tables(dec_seq)
    cak = cache_a_k.reshape(dec_batch, DEPTH, past, 128)
    cav = cache_a_v.reshape(dec_batch, DEPTH, past, 128)
    cdk = cache_d_k.reshape(dec_batch, DEPTH, past, 512)
    cdv = cache_d_v.reshape(dec_batch, DEPTH, past, 512)
    c0 = state_c_C.reshape(dec_batch, DEPTH, 8, 128, 128)
    n0 = state_c_n.reshape(dec_batch, DEPTH, 8, 128)
    m0 = jnp.broadcast_to(state_c_m.reshape(dec_batch, DEPTH, 8, 1), (dec_batch, DEPTH, 8, 128))
    cfg_ctx = dict(tq=seq, rows=512, cols=512, key_rows=16, route_cols=min(1024, batch * seq), route_heads=4)
    cfg_lat = dict(tq=min(256, dec_seq), rows=min(512, dec_seq), cols=512, key_rows=16,
                   route_cols=min(1024, dec_seq), route_heads=4)
    y_p = x_prompt.reshape(batch * seq, D_MODEL)
    y_s = x_sample.reshape(dec_batch * dec_seq, D_MODEL)
    state = None
    for l in range(DEPTH):
        p = _layer_params(l, w_in, attn_qk_gain, gmlp_ws, gmlp_b, mlstm_gate_bias, mlstm_gn, diff_lambda, diff_gn,
                          w_branch, w_out, ln_g, ln_b, peer_wq, peer_keys, peer_u, peer_v)
        mod_ctx = mods[l, 0:1].reshape(1, 1, 6 * D_MODEL)
        mod_lat = mods[l, 1:1 + dec_batch].reshape(dec_batch, 1, 6 * D_MODEL)
        y_p, state = _trunk_layer(y_p, mod_ctx, p, tabs, l=l, n_seq=batch, seq=seq, cfg=cfg_ctx, prev_state=state)
        y_s, _ = _trunk_layer(y_s, mod_lat, p, tabs, l=l, n_seq=dec_batch, seq=dec_seq, cfg=cfg_lat,
                              ctx_cache=(rope, cak, cav, cdk, cdv, c0, n0, m0))
    nk = state[0].reshape(batch, DEPTH, seq, A_KV_HEADS, A_HEAD_DIM)
    nv = state[1].reshape(batch, DEPTH, seq, A_KV_HEADS, A_HEAD_DIM)
    ndk = state[2].reshape(batch, DEPTH, seq, D_HEADS, 2, D_HALF_DIM)
    ndv = state[3].reshape(batch, DEPTH, seq, D_HEADS, D_VDIM)
    nc = state[4].reshape(batch, DEPTH, 2, C_HEADS, C_HEAD_DIM, C_HEAD_DIM)
    nn = state[5].reshape(batch, DEPTH, 2, C_HEADS, C_HEAD_DIM)
    nm = state[6][:, :, :, 0].reshape(batch, DEPTH, 2, C_HEADS)
    return (y_p.reshape(batch, seq, D_MODEL), y_s.reshape(dec_batch, dec_seq, D_MODEL), nk, nv, ndk, ndv, nc, nn, nm)
```

```python
import functools
import math

import numpy as np
import jax
import jax.numpy as jnp
from jax import lax
from jax.experimental import pallas as pl
from jax.experimental.pallas import tpu as pltpu

F32 = jnp.float32
BF16 = jnp.bfloat16
HIGHEST = lax.Precision.HIGHEST

D_MODEL = 1024
DEPTH = 4
GRID_W = 64
ROPE_BASE = 10000.0
EPS = 1e-6
BRANCH_WIDTH = D_MODEL // 2
A_HEAD_DIM = 64
A_HEADS = 8
A_KV_HEADS = 2
B_GROUPS = 4
CHUNK = 128
C_HEADS = 4
C_HEAD_DIM = 128
D_HEADS = 4
D_VDIM = 128
D_HALF_DIM = 64
PEER_HEADS = 8
PEER_NKEYS = 128
PEER_EXPERTS = PEER_NKEYS * PEER_NKEYS
PEER_QDIM = 256
PEER_TOPK = 16
ALPHA = (2 * DEPTH) ** 0.25

_OFF_A = 0
_OFF_B = 768
_OFF_C = 1792
_OFF_CG = 3840
_OFF_D = 3856
_OFF_G = 5392

VMEM_LIMIT_BYTES = 56 * 1024 * 1024
NEG_INF = float("-inf")


def _cparams(*sem):
    return pltpu.CompilerParams(dimension_semantics=sem, vmem_limit_bytes=VMEM_LIMIT_BYTES)


def _dot(a, b):
    return jnp.dot(a, b, preferred_element_type=F32)


def _dot_nt(a, b):
    return lax.dot_general(a, b, (((1,), (1,)), ((), ())), preferred_element_type=F32)


def _modulated(x, mod_ref, which):
    base = 3 * D_MODEL * which
    sh = mod_ref[:, base:base + D_MODEL]
    sc = mod_ref[:, base + D_MODEL:base + 2 * D_MODEL]
    return x * (1.0 + sc) + sh


def _layer_norm_rows(z, g, b):
    mu = jnp.mean(z, axis=-1, keepdims=True)
    zc = z - mu
    var = jnp.mean(zc * zc, axis=-1, keepdims=True)
    return zc * lax.rsqrt(var + EPS) * g + b


def _rope(x, cos, sin_signed, lane):
    w = x.shape[1]
    nxt = pltpu.roll(x, w - 16, 1)
    prv = pltpu.roll(x, 16, 1)
    partner = jnp.where((lane % 32) < 16, nxt, prv)
    return x * cos + partner * sin_signed


def _tile_lanes(t, n):
    return t if n == 1 else jnp.concatenate([t] * n, axis=1)


def _mod_kernel(c_ref, w_ref, b_ref, o_ref):
    c = c_ref[...]
    s = c * jax.nn.sigmoid(c)
    o_ref[...] = jnp.dot(s, w_ref[...], precision=HIGHEST, preferred_element_type=F32) + b_ref[...]


def _modulation(c_rows, w_mod, b_mod):
    n_col = 6 * D_MODEL // 1024
    return pl.pallas_call(
        _mod_kernel,
        out_shape=jax.ShapeDtypeStruct((DEPTH, 8, 6 * D_MODEL), F32),
        grid=(DEPTH, n_col),
        in_specs=[
            pl.BlockSpec((8, D_MODEL), lambda l, j: (0, 0)),
            pl.BlockSpec((None, D_MODEL, 1024), lambda l, j: (l, 0, j)),
            pl.BlockSpec((None, 1, 1024), lambda l, j: (l, 0, j)),
        ],
        out_specs=pl.BlockSpec((None, 8, 1024), lambda l, j: (l, 0, j)),
        compiler_params=_cparams("arbitrary", "arbitrary"),
        name="modulation",
    )(c_rows, w_mod, b_mod.reshape(DEPTH, 1, 6 * D_MODEL))


def _attn_a_kernel(*refs, seq, tq, n_cache, rope, proj_rows):
    if rope:
        (x_ref, mod_ref, w_ref, gain_ref, bd_ref, rep_ref, cos_ref, sin_ref, ck_ref, cv_ref,
         y_ref, q_s, k_s, v_s) = refs
    else:
        (x_ref, mod_ref, w_ref, gain_ref, bd_ref, rep_ref) = refs[:6]
        (y_ref, nk_ref, nv_ref, q_s, k_s, v_s) = refs[-6:]
    qi = pl.program_id(1)

    @pl.when(qi == 0)
    def _project():
        lane = lax.broadcasted_iota(jnp.int32, (1, 640), 1)
        for r0 in range(0, seq, proj_rows):
            rows = pl.ds(r0, proj_rows)
            h = _modulated(x_ref[rows, :], mod_ref, 0).astype(BF16)
            p = _dot(h, w_ref[...])
            qk = p[:, :640]
            sq = qk * qk
            hi = sq.astype(BF16)
            lo = (sq - hi.astype(F32)).astype(BF16)
            ms = jnp.concatenate([_dot(hi[:, c:c + 128], bd_ref[...]) + _dot(lo[:, c:c + 128], bd_ref[...])
                                  for c in range(0, 640, 128)], axis=1)
            qk = qk * lax.rsqrt(ms + EPS) * gain_ref[...]
            v = p[:, 640:768]
            if not rope:
                nk_ref[rows, :] = qk[:, 512:640]
                nv_ref[rows, :] = v
            else:
                cos = _tile_lanes(cos_ref[rows, :], 5)
                sin = _tile_lanes(sin_ref[rows, :], 5)
                qk = _rope(qk, cos, sin, lane)
            q_s[rows, :] = (qk[:, :512] * (A_HEAD_DIM ** -0.5)).astype(BF16)
            k_s[rows, :] = _dot(qk[:, 512:640].astype(BF16), rep_ref[...]).astype(BF16)
            v_s[rows, :] = _dot(v.astype(BF16), rep_ref[...]).astype(BF16)
        if n_cache:
            crow = pl.ds(seq, n_cache)
            k_s[crow, :] = _dot(ck_ref[...].astype(BF16), rep_ref[...]).astype(BF16)
            v_s[crow, :] = _dot(cv_ref[...].astype(BF16), rep_ref[...]).astype(BF16)

    head_of_lane = lax.broadcasted_iota(jnp.int32, (1, 256), 1) // A_HEAD_DIM
    qb = q_s[pl.ds(pl.multiple_of(qi * tq, tq), tq), :]
    for g in range(A_KV_HEADS):
        cols = slice(g * 256, (g + 1) * 256)
        qg = qb[:, cols]
        kg = k_s[:, cols]
        vg = v_s[:, cols]
        acc = jnp.zeros((tq, 256), F32)
        for r in range(A_HEADS // A_KV_HEADS):
            sel = head_of_lane == r
            qm = jnp.where(sel, qg, jnp.zeros_like(qg))
            s = _dot_nt(qm, kg)
            m = jnp.max(s, axis=-1, keepdims=True)
            p = jnp.exp(s - m)
            l = jnp.sum(p, axis=-1, keepdims=True)
            o = _dot(p.astype(BF16), vg)
            acc = acc + jnp.where(sel, o * (1.0 / l), 0.0)
        y_ref[:, cols] = acc.astype(BF16)


def _carry_aliases(in_specs, args, carry, first_out):
    if carry is None:
        return {}
    aliases = {}
    for k, arr in enumerate(carry):
        aliases[len(args)] = first_out + k
        in_specs.append(pl.BlockSpec(memory_space=pl.ANY))
        args.append(arr)
    return aliases


def _attn_a(x, mod, w_a, gain_row, bd, rep, *, n_seq, seq, tq, rope=None, cache=None, layer=0, carry=None):
    n_q = seq // tq
    n_cache = 0 if cache is None else cache[0].shape[2]
    proj_rows = min(seq, 512)
    kern = functools.partial(_attn_a_kernel, seq=seq, tq=tq, n_cache=n_cache, rope=rope is not None,
                             proj_rows=proj_rows)
    const = lambda s, q: (0, 0)
    in_specs = [
        pl.BlockSpec((seq, D_MODEL), lambda s, q: (s, 0)),
        pl.BlockSpec((None, 1, 6 * D_MODEL), lambda s, q: (s if mod.shape[0] > 1 else 0, 0, 0)),
        pl.BlockSpec(w_a.shape, const),
        pl.BlockSpec(gain_row.shape, const),
        pl.BlockSpec(bd.shape, const),
        pl.BlockSpec(rep.shape, const),
    ]
    args = [x, mod, w_a, gain_row, bd, rep]
    n_tok = n_seq * seq
    y_spec = pl.BlockSpec((tq, BRANCH_WIDTH), lambda s, q: (s * n_q + q, 0))
    y_shape = jax.ShapeDtypeStruct((n_tok, BRANCH_WIDTH), BF16)
    if rope is not None:
        cos, sin = rope
        in_specs += [pl.BlockSpec(cos.shape, const), pl.BlockSpec(sin.shape, const),
                     pl.BlockSpec((None, None, n_cache, 128), lambda s, q: (s, layer, 0, 0)),
                     pl.BlockSpec((None, None, n_cache, 128), lambda s, q: (s, layer, 0, 0))]
        args += [cos, sin, cache[0], cache[1]]
        out_shape, out_specs, aliases = y_shape, y_spec, {}
    else:
        kv_shape = jax.ShapeDtypeStruct((n_seq, DEPTH, seq, 128), F32)
        kv_spec = pl.BlockSpec((None, None, seq, 128), lambda s, q: (s, layer, 0, 0))
        out_shape, out_specs = (y_shape, kv_shape, kv_shape), (y_spec, kv_spec, kv_spec)
        aliases = _carry_aliases(in_specs, args, carry, first_out=1)
    return pl.pallas_call(
        kern, out_shape=out_shape, grid=(n_seq, n_q), in_specs=in_specs, out_specs=out_specs,
        input_output_aliases=aliases,
        scratch_shapes=[pltpu.VMEM((seq, 512), BF16), pltpu.VMEM((seq + n_cache, 512), BF16),
                        pltpu.VMEM((seq + n_cache, 512), BF16)],
        compiler_params=_cparams("arbitrary", "arbitrary"),
        name="branch_a_lat" if rope is not None else "branch_a_ctx",
    )(*args)


def _attn_d_kernel(*refs, seq, tq, n_cache, rope, proj_rows, lam_init):
    if rope:
        (x_ref, mod_ref, w_ref, lam_ref, gn_ref, cos_ref, sin_ref, ck_ref, cv_ref,
         y_ref, q_s, k_s, v_s) = refs
    else:
        (x_ref, mod_ref, w_ref, lam_ref, gn_ref) = refs[:5]
        (y_ref, nk_ref, nv_ref, q_s, k_s, v_s) = refs[-6:]
    qi = pl.program_id(1)

    @pl.when(qi == 0)
    def _project():
        lane = lax.broadcasted_iota(jnp.int32, (1, 512), 1)
        for r0 in range(0, seq, proj_rows):
            rows = pl.ds(r0, proj_rows)
            h = _modulated(x_ref[rows, :], mod_ref, 0).astype(BF16)
            p = _dot(h, w_ref[...])
            dq, dk, dv = p[:, :512], p[:, 512:1024], p[:, 1024:1536]
            if not rope:
                nk_ref[rows, :] = dk
                nv_ref[rows, :] = dv
            else:
                cos = _tile_lanes(cos_ref[rows, :], 4)
                sin = _tile_lanes(sin_ref[rows, :], 4)
                dq = _rope(dq, cos, sin, lane)
                dk = _rope(dk, cos, sin, lane)
            q_s[rows, :] = (dq * (D_HALF_DIM ** -0.5)).astype(BF16)
            k_s[rows, :] = dk.astype(BF16)
            v_s[rows, :] = dv.astype(BF16)
        if n_cache:
            crow = pl.ds(seq, n_cache)
            k_s[crow, :] = ck_ref[...].astype(BF16)
            v_s[crow, :] = cv_ref[...].astype(BF16)

    lv = lam_ref[...]
    lam = (jnp.exp(jnp.sum(lv[0:1] * lv[1:2], axis=-1, keepdims=True))
           - jnp.exp(jnp.sum(lv[2:3] * lv[3:4], axis=-1, keepdims=True)) + lam_init)
    half_of_lane = lax.broadcasted_iota(jnp.int32, (1, 128), 1) // D_HALF_DIM
    qb = q_s[pl.ds(pl.multiple_of(qi * tq, tq), tq), :]
    for hd in range(D_HEADS):
        cols = slice(hd * 128, (hd + 1) * 128)
        qh = qb[:, cols]
        kh = k_s[:, cols]
        vh = v_s[:, cols]
        probs = []
        for j in range(2):
            qm = jnp.where(half_of_lane == j, qh, jnp.zeros_like(qh))
            s = _dot_nt(qm, kh)
            m = jnp.max(s, axis=-1, keepdims=True)
            p = jnp.exp(s - m)
            l = jnp.sum(p, axis=-1, keepdims=True)
            probs.append(p * (1.0 / l))
        a = (probs[0] - lam * probs[1]).astype(BF16)
        o = _dot(a, vh)
        ms = jnp.mean(o * o, axis=-1, keepdims=True)
        o = o * lax.rsqrt(ms + EPS) * gn_ref[:, cols] * (1.0 - lam_init)
        y_ref[:, cols] = o.astype(BF16)


def _attn_d(x, mod, w_d, lam_params, gn_row, *, n_seq, seq, tq, lam_init, rope=None, cache=None, layer=0,
            carry=None):
    n_q = seq // tq
    n_cache = 0 if cache is None else cache[0].shape[2]
    proj_rows = min(seq, 512)
    kern = functools.partial(_attn_d_kernel, seq=seq, tq=tq, n_cache=n_cache, rope=rope is not None,
                             proj_rows=proj_rows, lam_init=lam_init)
    const = lambda s, q: (0, 0)
    in_specs = [
        pl.BlockSpec((seq, D_MODEL), lambda s, q: (s, 0)),
        pl.BlockSpec((None, 1, 6 * D_MODEL), lambda s, q: (s if mod.shape[0] > 1 else 0, 0, 0)),
        pl.BlockSpec(w_d.shape, const),
        pl.BlockSpec(lam_params.shape, const),
        pl.BlockSpec(gn_row.shape, const),
    ]
    args = [x, mod, w_d, lam_params, gn_row]
    n_tok = n_seq * seq
    y_spec = pl.BlockSpec((tq, BRANCH_WIDTH), lambda s, q: (s * n_q + q, 0))
    y_shape = jax.ShapeDtypeStruct((n_tok, BRANCH_WIDTH), BF16)
    if rope is not None:
        cos, sin = rope
        in_specs += [pl.BlockSpec(cos.shape, const), pl.BlockSpec(sin.shape, const),
                     pl.BlockSpec((None, None, n_cache, 512), lambda s, q: (s, layer, 0, 0)),
                     pl.BlockSpec((None, None, n_cache, 512), lambda s, q: (s, layer, 0, 0))]
        args += [cos, sin, cache[0], cache[1]]
        out_shape, out_specs, aliases = y_shape, y_spec, {}
    else:
        kv_shape = jax.ShapeDtypeStruct((n_seq, DEPTH, seq, 512), F32)
        kv_spec = pl.BlockSpec((None, None, seq, 512), lambda s, q: (s, layer, 0, 0))
        out_shape, out_specs = (y_shape, kv_shape, kv_shape), (y_spec, kv_spec, kv_spec)
        aliases = _carry_aliases(in_specs, args, carry, first_out=1)
    return pl.pallas_call(
        kern, out_shape=out_shape, grid=(n_seq, n_q), in_specs=in_specs, out_specs=out_specs,
        input_output_aliases=aliases,
        scratch_shapes=[pltpu.VMEM((seq, 512), BF16), pltpu.VMEM((seq + n_cache, 512), BF16),
                        pltpu.VMEM((seq + n_cache, 512), BF16)],
        compiler_params=_cparams("arbitrary", "arbitrary"),
        name="branch_d_lat" if rope is not None else "branch_d_ctx",
    )(*args)


def _gmlp_kernel(x_ref, mod_ref, w_ref, ws_ref, bias_ref, y_ref, *, rows):
    h = _modulated(x_ref[...], mod_ref, 0).astype(BF16)
    p = _dot(h, w_ref[...])
    u, v = p[:, :BRANCH_WIDTH], p[:, BRANCH_WIDTH:]
    mu = jnp.mean(v, axis=-1, keepdims=True)
    vc = v - mu
    var = jnp.mean(vc * vc, axis=-1, keepdims=True)
    vn = (vc * lax.rsqrt(var + EPS)).astype(BF16)
    for c in range(rows // CHUNK):
        rs = slice(c * CHUNK, (c + 1) * CHUNK)
        for g in range(B_GROUPS):
            cs = slice(g * 128, (g + 1) * 128)
            s = _dot(ws_ref[g].astype(BF16), vn[rs, cs]) + bias_ref[:, cs]
            y_ref[rs, cs] = (u[rs, cs] * s).astype(BF16)


def _gmlp(x, mod, w_b, ws, bias_full, *, seq, rows):
    n_tok = x.shape[0]
    per_seq = seq // rows
    kern = functools.partial(_gmlp_kernel, rows=rows)
    return pl.pallas_call(
        kern, out_shape=jax.ShapeDtypeStruct((n_tok, BRANCH_WIDTH), BF16),
        grid=(n_tok // rows,),
        in_specs=[
            pl.BlockSpec((rows, D_MODEL), lambda i: (i, 0)),
            pl.BlockSpec((None, 1, 6 * D_MODEL), lambda i: (i // per_seq if mod.shape[0] > 1 else 0, 0, 0)),
            pl.BlockSpec(w_b.shape, lambda i: (0, 0)),
            pl.BlockSpec(ws.shape, lambda i: (0, 0, 0)),
            pl.BlockSpec(bias_full.shape, lambda i: (0, 0)),
        ],
        out_specs=pl.BlockSpec((rows, BRANCH_WIDTH), lambda i: (i, 0)),
        compiler_params=_cparams("arbitrary"),
        name="branch_b",
    )(x, mod, w_b, ws, bias_full)


def _mlstm_t_kernel(*refs, seq, has_init, proj_rows):
    if has_init:
        (x_ref, mod_ref, wn_ref, wt_ref, gbr_ref, gbt_ref, gnt_ref, tril_ref, triu_ref, c0_ref, n0_ref, m0_ref,
         y_ref, q_s, k_s, qt_s, vt_s, ot_s, g_s, gt_s, hf_s, hb_s, st_s, m_s) = refs
    else:
        (x_ref, mod_ref, wn_ref, wt_ref, gbr_ref, gbt_ref, gnt_ref, tril_ref, triu_ref) = refs[:9]
        (y_ref, cout_ref, nout_ref, mout_ref,
         q_s, k_s, qt_s, vt_s, ot_s, g_s, gt_s, hf_s, hb_s, st_s, m_s) = refs[-15:]
    n_chunk = seq // CHUNK
    for r0 in range(0, seq, proj_rows):
        span = pl.ds(r0, proj_rows)
        h = _modulated(x_ref[span, :], mod_ref, 0).astype(BF16)
        p = _dot(h, wn_ref[...])
        q_s[span, :] = p[:, 0:512].astype(BF16)
        k_s[span, :] = (p[:, 512:1024] * (C_HEAD_DIM ** -0.5)).astype(BF16)
        g_s[span, :] = p[:, 1024:1152] + gbr_ref[...]
        pt = _dot_nt(wt_ref[...], h)
        qt_s[:, span] = pt[0:512].astype(BF16)
        vt_s[:, span] = pt[512:1024].astype(BF16)
        ot_s[:, span] = jax.nn.sigmoid(pt[1024:1536]).astype(BF16)
        gt_s[:, span] = pt[1536:1664] + _tile_lanes(gbt_ref[...], proj_rows // 128)

    first_row = lax.broadcasted_iota(jnp.int32, (CHUNK, CHUNK), 0) == 0
    for sidx in range(2 * C_HEADS):
        if has_init:
            st_s[sidx, 0:128, :] = c0_ref[sidx].T
            st_s[sidx, 128:256, :] = jnp.where(first_row, jnp.broadcast_to(n0_ref[sidx:sidx + 1, :], (CHUNK, 128)), 0.0)
        else:
            st_s[sidx] = jnp.zeros((2 * CHUNK, 128), F32)
    m_s[...] = m0_ref[...] if has_init else jnp.zeros_like(m_s)

    tril = tril_ref[...]
    triu = triu_ref[...]
    row_i = lax.broadcasted_iota(jnp.int32, (CHUNK, CHUNK), 0)
    col_i = lax.broadcasted_iota(jnp.int32, (CHUNK, CHUNK), 1)
    visible = (row_i <= col_i, row_i >= col_i)
    ones_blk = jnp.where(first_row, 1.0, 0.0).astype(BF16)

    def chunk_step(c, carry):
        prep = []
        for direction in range(2):
            cc = c if direction == 0 else n_chunk - 1 - c
            span = pl.ds(pl.multiple_of(cc * CHUNK, CHUNK), CHUNK)
            gates = g_s[span, :]
            gates_t = gt_s[:, span]
            logf = jnp.minimum(gates, 0.0) - jnp.log1p(jnp.exp(-jnp.abs(gates)))
            logf_t = jnp.minimum(gates_t, 0.0) - jnp.log1p(jnp.exp(-jnp.abs(gates_t)))
            tri_col = tril if direction == 0 else triu
            tri_row = triu if direction == 0 else tril
            b_col_all = jnp.dot(tri_col, logf, precision=HIGHEST, preferred_element_type=F32)
            b_row_all = jnp.dot(logf_t, tri_row, precision=HIGHEST, preferred_element_type=F32)
            r_col_all = gates - pltpu.roll(b_col_all, 128 - 4, 1)
            base = direction * 8
            prep.append((span, r_col_all, gates_t[base:base + 4, :], b_row_all[base + 4:base + 8, :]))
        for hd in range(C_HEADS):
            for direction in range(2):
                span, r_col_all, i_rows, b_rows = prep[direction]
                base = direction * 8
                last = CHUNK - 1 if direction == 0 else 0
                h_out = hf_s if direction == 0 else hb_s
                sidx = direction * 4 + hd
                blk = slice(hd * 128, (hd + 1) * 128)
                r_col = r_col_all[:, base + hd:base + hd + 1]
                b_row = b_rows[hd:hd + 1, :]
                i_row = i_rows[hd:hd + 1, :]
                m_prev = m_s[sidx:sidx + 1, 0:1]
                r_wide = jnp.broadcast_to(r_col, (CHUNK, CHUNK))
                peak = jnp.max(jnp.where(visible[direction], r_wide, NEG_INF), axis=0, keepdims=True)
                m_row = b_row + jnp.maximum(peak, m_prev)
                qh = q_s[span, blk]
                kh = k_s[span, blk]
                arg = jnp.where(visible[direction], r_wide + (b_row - m_row), NEG_INF)
                w_t = (_dot_nt(kh, qh) * jnp.exp(arg)).astype(BF16)
                inter = jnp.exp(b_row + m_prev - m_row)
                v_one = jnp.concatenate([vt_s[blk, span], ones_blk], axis=0)
                state = st_s[sidx]
                both = _dot(v_one, w_t) + inter * _dot(state.astype(BF16), qt_s[blk, span])
                den = both[128:129, :]
                h_out[blk, span] = both[0:128, :] / jnp.maximum(jnp.abs(den), jnp.exp(-m_row))
                b_last = b_row[:, last:last + 1]
                g_row = b_last - b_row + i_row
                m_new = jnp.maximum(b_last + m_prev, jnp.max(g_row, axis=-1, keepdims=True))
                decay = jnp.exp(b_last + m_prev - m_new)
                scaled = v_one * jnp.exp(g_row - m_new).astype(BF16)
                st_s[sidx] = decay * state + _dot(scaled, kh)
                m_s[sidx:sidx + 1, :] = jnp.broadcast_to(m_new, (1, 128))
        return carry

    lax.fori_loop(0, n_chunk, chunk_step, 0)

    for r0 in range(0, seq, proj_rows):
        span = pl.ds(r0, proj_rows)
        for hd in range(C_HEADS):
            blk = slice(hd * 128, (hd + 1) * 128)
            hh = hf_s[blk, span] + hb_s[blk, span]
            mu = jnp.mean(hh, axis=0, keepdims=True)
            hc = hh - mu
            var = jnp.mean(hc * hc, axis=0, keepdims=True)
            gn = _tile_lanes(gnt_ref[blk, :], proj_rows // 128)
            y_t = hc * lax.rsqrt(var + EPS) * gn * ot_s[blk, span].astype(F32)
            y_ref[span, blk] = y_t.T.astype(BF16)
    if not has_init:
        for sidx in range(2 * C_HEADS):
            cout_ref[sidx] = st_s[sidx, 0:128, :].T
            nout_ref[sidx:sidx + 1, :] = st_s[sidx, 128:129, :]
        mout_ref[...] = m_s[...]


def _mlstm_t(x, mod, w_nat, w_t, gate_bias_row, gate_bias_t, gn_t, tril, triu, *, n_seq, seq, init=None, layer=0,
             carry=None):
    proj_rows = min(seq, 512)
    kern = functools.partial(_mlstm_t_kernel, seq=seq, has_init=init is not None, proj_rows=proj_rows)
    const = lambda s: (0, 0)
    in_specs = [
        pl.BlockSpec((seq, D_MODEL), lambda s: (s, 0)),
        pl.BlockSpec((None, 1, 6 * D_MODEL), lambda s: (s if mod.shape[0] > 1 else 0, 0, 0)),
    ] + [pl.BlockSpec(a.shape, const) for a in (w_nat, w_t, gate_bias_row, gate_bias_t, gn_t, tril, triu)]
    args = [x, mod, w_nat, w_t, gate_bias_row, gate_bias_t, gn_t, tril, triu]
    n_tok = n_seq * seq
    y_shape = jax.ShapeDtypeStruct((n_tok, BRANCH_WIDTH), BF16)
    y_spec = pl.BlockSpec((seq, BRANCH_WIDTH), lambda s: (s, 0))
    if init is not None:
        c0, n0, m0 = init
        in_specs += [pl.BlockSpec((None, None, 8, 128, 128), lambda s: (s, layer, 0, 0, 0)),
                     pl.BlockSpec((None, None, 8, 128), lambda s: (s, layer, 0, 0)),
                     pl.BlockSpec((None, None, 8, 128), lambda s: (s, layer, 0, 0))]
        args += [c0, n0, m0]
        out_shape, out_specs, aliases = y_shape, y_spec, {}
    else:
        out_shape = (y_shape, jax.ShapeDtypeStruct((n_seq, DEPTH, 8, 128, 128), F32),
                     jax.ShapeDtypeStruct((n_seq, DEPTH, 8, 128), F32),
                     jax.ShapeDtypeStruct((n_seq, DEPTH, 8, 128), F32))
        out_specs = (y_spec, pl.BlockSpec((None, None, 8, 128, 128), lambda s: (s, layer, 0, 0, 0)),
                     pl.BlockSpec((None, None, 8, 128), lambda s: (s, layer, 0, 0)),
                     pl.BlockSpec((None, None, 8, 128), lambda s: (s, layer, 0, 0)))
        aliases = _carry_aliases(in_specs, args, carry, first_out=1)
    return pl.pallas_call(
        kern, out_shape=out_shape, grid=(n_seq,), in_specs=in_specs, out_specs=out_specs,
        input_output_aliases=aliases,
        scratch_shapes=[pltpu.VMEM((seq, 512), BF16), pltpu.VMEM((seq, 512), BF16),
                        pltpu.VMEM((512, seq), BF16), pltpu.VMEM((512, seq), BF16),
                        pltpu.VMEM((512, seq), BF16),
                        pltpu.VMEM((seq, 128), F32), pltpu.VMEM((128, seq), F32),
                        pltpu.VMEM((512, seq), F32), pltpu.VMEM((512, seq), F32),
                        pltpu.VMEM((8, 256, 128), F32), pltpu.VMEM((8, 128), F32)],
        compiler_params=_cparams("arbitrary"),
        name="branch_c_lat" if init is not None else "branch_c_ctx",
    )(*args)


def _merge_kernel(x_ref, mod_ref, ya_ref, yb_ref, yc_ref, yd_ref, wg_ref, wbr_ref, wout_ref, lng_ref, lnb_ref,
                  o_ref):
    x = x_ref[...]
    h = _modulated(x, mod_ref, 0).astype(BF16)
    mix = None
    for n, y_ref in enumerate((ya_ref, yb_ref, yc_ref, yd_ref)):
        gate = jax.nn.sigmoid(_dot(h, wg_ref[:, n * D_MODEL:(n + 1) * D_MODEL]))
        term = gate * _dot(y_ref[...], wbr_ref[n])
        mix = term if mix is None else mix + term
    out = _dot(mix.astype(BF16), wout_ref[...])
    g1 = mod_ref[:, 2 * D_MODEL:3 * D_MODEL]
    o_ref[...] = _layer_norm_rows(ALPHA * x + g1 * out, lng_ref[...], lnb_ref[...])


def _merge(x, mod, ys, w_g, w_br, w_out, ln_g, ln_b, *, seq, rows):
    n_tok = x.shape[0]
    per_seq = seq // rows
    tok = lambda i: (i, 0)
    c2 = lambda i: (0, 0)
    return pl.pallas_call(
        _merge_kernel, out_shape=jax.ShapeDtypeStruct((n_tok, D_MODEL), F32),
        grid=(n_tok // rows,),
        in_specs=[
            pl.BlockSpec((rows, D_MODEL), tok),
            pl.BlockSpec((None, 1, 6 * D_MODEL), lambda i: (i // per_seq if mod.shape[0] > 1 else 0, 0, 0)),
            pl.BlockSpec((rows, BRANCH_WIDTH), tok), pl.BlockSpec((rows, BRANCH_WIDTH), tok),
            pl.BlockSpec((rows, BRANCH_WIDTH), tok), pl.BlockSpec((rows, BRANCH_WIDTH), tok),
            pl.BlockSpec(w_g.shape, c2), pl.BlockSpec(w_br.shape, lambda i: (0, 0, 0)),
            pl.BlockSpec(w_out.shape, c2), pl.BlockSpec(ln_g.shape, c2), pl.BlockSpec(ln_b.shape, c2),
        ],
        out_specs=pl.BlockSpec((rows, D_MODEL), tok),
        compiler_params=_cparams("arbitrary"),
        name="merge",
    )(x, mod, *ys, w_g, w_br, w_out, ln_g, ln_b)


_TAKEN = -(2.0 ** 127)


def _top16(s):
    cur = s
    vals = []
    for r in range(PEER_TOPK):
        mx = jnp.max(cur, axis=0, keepdims=True)
        cur = jnp.where(cur == mx, _TAKEN * (1.0 + r / 32.0), cur)
        vals.append(mx)
    rank = jnp.where(cur <= _TAKEN, cur * (32.0 / _TAKEN) - 31.0, float(PEER_TOPK + 1))
    return jnp.concatenate(vals, axis=0), rank


def _pair_tables():
    pairs = [(k1, k2) for k1 in range(PEER_TOPK) for k2 in range(PEER_TOPK // (k1 + 1))]
    n = 56
    sel_a = np.zeros((n, PEER_TOPK), np.float32)
    sel_b = np.zeros((n, PEER_TOPK), np.float32)
    pad = np.full((n, 1), NEG_INF, np.float32)
    for row, (k1, k2) in enumerate(pairs):
        sel_a[row, k1] = 1.0
        sel_b[row, k2] = 1.0
        pad[row, 0] = 0.0
    return jnp.asarray(sel_a), jnp.asarray(sel_b), jnp.asarray(pad), jnp.asarray(sel_a.T, BF16)


def _route_kernel(x_ref, mod_ref, wq_ref, keys_ref, sela_ref, selb_ref, pad_ref, ind_ref,
                  cnt_ref, e1_ref, r2_ref, e2_ref, h2_s, *, heads):
    @pl.when(pl.program_id(1) == 0)
    def _modulate():
        h2_s[...] = _modulated(x_ref[...], mod_ref, 1).astype(BF16)

    pick = lambda sel_ref, v: jnp.dot(sel_ref[...], v, precision=HIGHEST, preferred_element_type=F32)
    q_all = _dot(h2_s[...], wq_ref[...])
    for hd in range(heads):
        q = q_all[:, hd * PEER_QDIM:(hd + 1) * PEER_QDIM]
        s1 = _dot_nt(keys_ref[hd, 0].astype(BF16), q[:, :128].astype(BF16))
        s2 = _dot_nt(keys_ref[hd, 1].astype(BF16), q[:, 128:].astype(BF16))
        a, rank1 = _top16(s1)
        b, rank2 = _top16(s2)
        ea = jnp.exp(a - a[0:1])
        eb = jnp.exp(b - b[0:1])
        cand = pick(sela_ref, a) + pick(selb_ref, b) + pad_ref[...]
        gate = pick(sela_ref, ea) * pick(selb_ref, eb)
        cur = cand
        thr = None
        for _ in range(PEER_TOPK):
            thr = jnp.max(cur, axis=0, keepdims=True)
            cur = jnp.where(cur == thr, NEG_INF, cur)
        chosen = cand >= thr
        z = jnp.sum(jnp.where(chosen, gate, 0.0), axis=0, keepdims=True)
        cnt_sorted = _dot(ind_ref[...], jnp.where(chosen, 1.0, 0.0).astype(BF16))
        rank1_b = rank1.astype(BF16)
        counts_b = cnt_sorted.astype(BF16)
        cnt = jnp.zeros(s1.shape, BF16)
        for r in range(PEER_TOPK):
            cnt = jnp.where(rank1_b == float(r + 1), counts_b[r:r + 1], cnt)
        cnt_ref[hd] = cnt.astype(F32)
        e1_ref[hd] = jnp.where(rank1 <= float(PEER_TOPK), jnp.exp(s1 - a[0:1]) * (0.5 / z), 0.0)
        packed = (PEER_NKEYS // 16, 16, s2.shape[1])
        r2_ref[hd] = rank2.astype(BF16).reshape(packed)
        e2_ref[hd] = jnp.where(rank2 <= float(PEER_TOPK), jnp.exp(s2 - b[0:1]), 0.0).astype(BF16).reshape(packed)


def _route(x1, mod, wq, keys, *, seq, cols, heads):
    n_tok = x1.shape[0]
    per_seq = seq // cols
    tables = _pair_tables()
    row_shape = jax.ShapeDtypeStruct((PEER_HEADS, PEER_NKEYS, n_tok), F32)
    col_shape = jax.ShapeDtypeStruct((PEER_HEADS, PEER_NKEYS // 16, 16, n_tok), BF16)
    col_spec = pl.BlockSpec((heads, PEER_NKEYS // 16, 16, cols), lambda i, h: (h, 0, 0, i))
    spec = pl.BlockSpec((heads, PEER_NKEYS, cols), lambda i, h: (h, 0, i))
    return pl.pallas_call(
        functools.partial(_route_kernel, heads=heads), out_shape=(row_shape, row_shape, col_shape, col_shape),
        grid=(n_tok // cols, PEER_HEADS // heads),
        in_specs=[
            pl.BlockSpec((cols, D_MODEL), lambda i, h: (i, 0)),
            pl.BlockSpec((None, 1, 6 * D_MODEL), lambda i, h: (i // per_seq if mod.shape[0] > 1 else 0, 0, 0)),
            pl.BlockSpec((D_MODEL, heads * PEER_QDIM), lambda i, h: (0, h)),
            pl.BlockSpec((heads, 2, PEER_NKEYS, PEER_QDIM // 2), lambda i, h: (h, 0, 0, 0)),
        ] + [pl.BlockSpec(t.shape, lambda i, h: (0, 0)) for t in tables],
        out_specs=(spec, spec, col_spec, col_spec),
        scratch_shapes=[pltpu.VMEM((cols, D_MODEL), BF16)],
        compiler_params=_cparams("arbitrary", "arbitrary"),
        name="peer_route",
    )(x1, mod, wq, keys, *tables)


def _peer_kernel(x_ref, mod_ref, u_ref, v_ref, cnt_ref, e1_ref, r2_ref, e2_ref, lng_ref, lnb_ref,
                 o_ref, h2t_s, acc_s, *, key_rows):
    e = pl.program_id(1)

    @pl.when(e == 0)
    def _init():
        h2 = _modulated(x_ref[...], mod_ref, 1)
        h2t_s[...] = h2.T.astype(BF16)
        acc_s[...] = jnp.zeros_like(acc_s)

    n_tok = h2t_s.shape[1]
    first_key = pl.multiple_of(e * key_rows, key_rows)
    zero = jnp.zeros((PEER_NKEYS // 16, 16, n_tok), BF16)
    act = _dot(u_ref[...], h2t_s[...]).astype(BF16)
    act = act * (1.0 + lax.erf(act * (2.0 ** -0.5)))
    pieces = []
    for r in range(key_rows):
        g = None
        for hd in range(PEER_HEADS):
            cnt_blk = cnt_ref[hd, pl.ds(first_key, key_rows), :]
            e1_blk = e1_ref[hd, pl.ds(first_key, key_rows), :]
            cnt_rows = jnp.broadcast_to(cnt_blk[r:r + 1, :], (16, n_tok)).astype(BF16)
            e1_rows = jnp.broadcast_to(e1_blk[r:r + 1, :], (16, n_tok)).astype(BF16)
            term = jnp.where(r2_ref[hd] <= cnt_rows[None], e2_ref[hd], zero) * e1_rows[None]
            g = term if g is None else g + term
        pieces.append(g.reshape(PEER_NKEYS, n_tok) * act[r * PEER_NKEYS:(r + 1) * PEER_NKEYS, :])
    acc_s[...] += lax.dot_general(v_ref[...], jnp.concatenate(pieces, axis=0), (((0,), (0,)), ((), ())),
                                  preferred_element_type=F32)

    @pl.when(e == pl.num_programs(1) - 1)
    def _finish():
        x = x_ref[...]
        g2 = mod_ref[:, 5 * D_MODEL:6 * D_MODEL]
        o_ref[...] = _layer_norm_rows(ALPHA * x + g2 * acc_s[...].T, lng_ref[...], lnb_ref[...])


def _peer(x1, mod, u, v, route, ln_g, ln_b, *, seq, cols, key_rows):
    n_tok = x1.shape[0]
    per_seq = seq // cols
    n_exp = key_rows * PEER_NKEYS
    n_tiles = PEER_EXPERTS // n_exp
    assert key_rows % 8 == 0
    kern = functools.partial(_peer_kernel, key_rows=key_rows)
    cnt, e1, r2, e2 = route
    rspec = pl.BlockSpec((PEER_HEADS, PEER_NKEYS, cols), lambda i, e: (0, 0, i))
    cspec = pl.BlockSpec((PEER_HEADS, PEER_NKEYS // 16, 16, cols), lambda i, e: (0, 0, 0, i))
    return pl.pallas_call(
        kern, out_shape=jax.ShapeDtypeStruct((n_tok, D_MODEL), F32),
        grid=(n_tok // cols, n_tiles),
        in_specs=[
            pl.BlockSpec((cols, D_MODEL), lambda i, e: (i, 0)),
            pl.BlockSpec((None, 1, 6 * D_MODEL), lambda i, e: (i // per_seq if mod.shape[0] > 1 else 0, 0, 0)),
            pl.BlockSpec((n_exp, D_MODEL), lambda i, e: (e, 0)),
            pl.BlockSpec((n_exp, D_MODEL), lambda i, e: (e, 0)),
            rspec, rspec, cspec, cspec,
            pl.BlockSpec(ln_g.shape, lambda i, e: (0, 0)), pl.BlockSpec(ln_b.shape, lambda i, e: (0, 0)),
        ],
        out_specs=pl.BlockSpec((cols, D_MODEL), lambda i, e: (i, 0)),
        scratch_shapes=[pltpu.VMEM((D_MODEL, cols), BF16), pltpu.VMEM((D_MODEL, cols), F32)],
        compiler_params=_cparams("arbitrary", "arbitrary"),
        name="peer_experts",
    )(x1, mod, u, v, cnt, e1, r2, e2, ln_g, ln_b)


def _rope_tables(seq):
    t = np.arange(seq)
    pos = np.stack([t // GRID_W, t % GRID_W], axis=1).astype(np.float64)
    inv = ROPE_BASE ** (-np.arange(16, dtype=np.float64) / 16)
    lane = np.arange(64)
    ang = pos[:, lane // 32] * inv[lane % 16][None, :]
    sign = np.where((lane % 32) < 16, -1.0, 1.0)[None, :]
    cos = np.tile(np.cos(ang), (1, 2)).astype(np.float32)
    sin = np.tile(np.sin(ang) * sign, (1, 2)).astype(np.float32)
    return jnp.asarray(cos), jnp.asarray(sin)


def _static_tables():
    lane = np.arange(128)
    bd = (lane[:, None] // 64 == lane[None, :] // 64).astype(np.float32) / 64.0
    src = np.arange(128)
    dst = np.arange(512)
    rep = ((src[:, None] // 64 == dst[None, :] // 256) & (src[:, None] % 64 == dst[None, :] % 64))
    idx = np.arange(CHUNK)
    tril = (idx[None, :] <= idx[:, None]).astype(np.float32)
    triu = (idx[None, :] >= idx[:, None]).astype(np.float32)
    return (jnp.asarray(bd, BF16), jnp.asarray(rep.astype(np.float32), BF16), jnp.asarray(tril), jnp.asarray(triu))


def _layer_params(l, w_in, attn_qk_gain, gmlp_ws, gmlp_b, mlstm_gate_bias, mlstm_gn, diff_lambda, diff_gn,
                  w_branch, w_out, ln_g, ln_b, peer_wq, peer_keys, peer_u, peer_v):
    w = w_in[l]
    p = {}
    p["w_a"] = w[:, _OFF_A:_OFF_B].astype(BF16)
    p["w_b"] = w[:, _OFF_B:_OFF_C].astype(BF16)
    cq, ck = w[:, _OFF_C:_OFF_C + 512], w[:, _OFF_C + 512:_OFF_C + 1024]
    cv, co = w[:, _OFF_C + 1024:_OFF_C + 1536], w[:, _OFF_C + 1536:_OFF_C + 2048]
    cg = jnp.concatenate([w[:, _OFF_CG:_OFF_D], jnp.zeros((D_MODEL, 112), F32)], axis=1)
    p["w_c_nat"] = jnp.concatenate([cq, ck, cg], axis=1).astype(BF16)
    p["w_c_t"] = jnp.concatenate([cq, cv, co, cg], axis=1).T.astype(BF16)
    p["w_d"] = w[:, _OFF_D:_OFF_G].astype(BF16)
    p["w_g"] = w[:, _OFF_G:].astype(BF16)
    gain = attn_qk_gain[l]
    p["gain_row"] = jnp.concatenate([jnp.tile(gain[0], A_HEADS), jnp.tile(gain[1], A_KV_HEADS)])[None, :]
    p["ws"] = gmlp_ws[l]
    p["bias_full"] = jnp.repeat(gmlp_b[l].T, 128, axis=1)
    p["gate_bias_row"] = jnp.concatenate([mlstm_gate_bias[l].reshape(16), jnp.zeros((112,), F32)])[None, :]
    p["gate_bias_t"] = jnp.broadcast_to(p["gate_bias_row"].reshape(128, 1), (128, 128))
    p["mlstm_gn_t"] = jnp.broadcast_to(mlstm_gn[l].reshape(BRANCH_WIDTH, 1), (BRANCH_WIDTH, 128))
    p["lam"] = diff_lambda[l]
    p["diff_gn_row"] = diff_gn[l].reshape(1, BRANCH_WIDTH)
    p["w_br"] = w_branch[l].astype(BF16)
    p["w_out"] = w_out[l].astype(BF16)
    p["ln_g0"], p["ln_b0"] = ln_g[l, 0][None, :], ln_b[l, 0][None, :]
    p["ln_g1"], p["ln_b1"] = ln_g[l, 1][None, :], ln_b[l, 1][None, :]
    p["wq"] = peer_wq[l].astype(BF16)
    p["keys"] = peer_keys[l]
    p["u"] = peer_u[l].astype(BF16)
    p["v"] = peer_v[l].astype(BF16)
    return p


def _trunk_layer(x, mod, p, tabs, *, l, n_seq, seq, cfg, ctx_cache=None, prev_state=None):
    bd, rep, tril, triu = tabs
    lam_init = 0.8 - 0.6 * math.exp(-0.3 * l)
    state = None
    if ctx_cache is None:
        prev = (None, None, None) if prev_state is None else (prev_state[0:2], prev_state[2:4], prev_state[4:7])
        ya, nk, nv = _attn_a(x, mod, p["w_a"], p["gain_row"], bd, rep, n_seq=n_seq, seq=seq, tq=cfg["tq"],
                             layer=l, carry=prev[0])
        yd, ndk, ndv = _attn_d(x, mod, p["w_d"], p["lam"], p["diff_gn_row"], n_seq=n_seq, seq=seq, tq=cfg["tq"],
                               lam_init=lam_init, layer=l, carry=prev[1])
        yc, c_new, n_new, m_new = _mlstm_t(x, mod, p["w_c_nat"], p["w_c_t"], p["gate_bias_row"], p["gate_bias_t"],
                                           p["mlstm_gn_t"], tril, triu, n_seq=n_seq, seq=seq, layer=l, carry=prev[2])
        state = (nk, nv, ndk, ndv, c_new, n_new, m_new)
    else:
        rope, cak, cav, cdk, cdv, c0, n0, m0 = ctx_cache
        ya = _attn_a(x, mod, p["w_a"], p["gain_row"], bd, rep, n_seq=n_seq, seq=seq, tq=cfg["tq"],
                     rope=rope, cache=(cak, cav), layer=l)
        yd = _attn_d(x, mod, p["w_d"], p["lam"], p["diff_gn_row"], n_seq=n_seq, seq=seq, tq=cfg["tq"],
                     lam_init=lam_init, rope=rope, cache=(cdk, cdv), layer=l)
        yc = _mlstm_t(x, mod, p["w_c_nat"], p["w_c_t"], p["gate_bias_row"], p["gate_bias_t"], p["mlstm_gn_t"],
                      tril, triu, n_seq=n_seq, seq=seq, init=(c0, n0, m0), layer=l)
    yb = _gmlp(x, mod, p["w_b"], p["ws"], p["bias_full"], seq=seq, rows=cfg["rows"])
    x1 = _merge(x, mod, (ya, yb, yc, yd), p["w_g"], p["w_br"], p["w_out"], p["ln_g0"], p["ln_b0"],
                seq=seq, rows=cfg["rows"])
    route = _route(x1, mod, p["wq"], p["keys"], seq=seq, cols=cfg["route_cols"], heads=cfg["route_heads"])
    x2 = _peer(x1, mod, p["u"], p["v"], route, p["ln_g1"], p["ln_b1"], seq=seq, cols=cfg["cols"],
               key_rows=cfg["key_rows"])
    return x2, state


def kernel(x_prompt, x_sample, cache_a_k, cache_a_v, cache_d_k, cache_d_v, state_c_C, state_c_n, state_c_m,
           c, c_ctx, w_mod, b_mod, w_in, attn_qk_gain, gmlp_ws, gmlp_b, mlstm_gate_bias, mlstm_gn,
           diff_lambda, diff_gn, w_branch, w_out, ln_g, ln_b, peer_wq, peer_keys, peer_u, peer_v):
    batch, seq, _ = x_prompt.shape
    dec_batch, dec_seq, _ = x_sample.shape
    past = cache_a_k.shape[2]
    c_rows = jnp.concatenate([c_ctx[None, :], c, jnp.zeros((8 - 1 - dec_batch, D_MODEL), F32)], axis=0)
    mods = _modulation(c_rows, w_mod, b_mod)
    tabs = _static_tables()
    rope = _rope_tables(dec_seq)
    cak = cache_a_k.reshape(dec_batch, DEPTH, past, 128)
    cav = cache_a_v.reshape(dec_batch, DEPTH, past, 128)
    cdk = cache_d_k.reshape(dec_batch, DEPTH, past, 512)
    cdv = cache_d_v.reshape(dec_batch, DEPTH, past, 512)
    c0 = state_c_C.reshape(dec_batch, DEPTH, 8, 128, 128)
    n0 = state_c_n.reshape(dec_batch, DEPTH, 8, 128)
    m0 = jnp.broadcast_to(state_c_m.reshape(dec_batch, DEPTH, 8, 1), (dec_batch, DEPTH, 8, 128))
    cfg_ctx = dict(tq=seq, rows=512, cols=512, key_rows=16, route_cols=min(1024, batch * seq), route_heads=4)
    cfg_lat = dict(tq=min(256, dec_seq), rows=min(512, dec_seq), cols=512, key_rows=16,
                   route_cols=min(1024, dec_seq), route_heads=4)
    y_p = x_prompt.reshape(batch * seq, D_MODEL)
    y_s = x_sample.reshape(dec_batch * dec_seq, D_MODEL)
    state = None
    for l in range(DEPTH):
        p = _layer_params(l, w_in, attn_qk_gain, gmlp_ws, gmlp_b, mlstm_gate_bias, mlstm_gn, diff_lambda, diff_gn,
                          w_branch, w_out, ln_g, ln_b, peer_wq, peer_keys, peer_u, peer_v)
        mod_ctx = mods[l, 0:1].reshape(1, 1, 6 * D_MODEL)
        mod_lat = mods[l, 1:1 + dec_batch].reshape(dec_batch, 1, 6 * D_MODEL)
        y_p, state = _trunk_layer(y_p, mod_ctx, p, tabs, l=l, n_seq=batch, seq=seq, cfg=cfg_ctx, prev_state=state)
        y_s, _ = _trunk_layer(y_s, mod_lat, p, tabs, l=l, n_seq=dec_batch, seq=dec_seq, cfg=cfg_lat,
                              ctx_cache=(rope, cak, cav, cdk, cdv, c0, n0, m0))
    nk = state[0].reshape(batch, DEPTH, seq, A_KV_HEADS, A_HEAD_DIM)
    nv = state[1].reshape(batch, DEPTH, seq, A_KV_HEADS, A_HEAD_DIM)
    ndk = state[2].reshape(batch, DEPTH, seq, D_HEADS, 2, D_HALF_DIM)
    ndv = state[3].reshape(batch, DEPTH, seq, D_HEADS, D_VDIM)
    nc = state[4].reshape(batch, DEPTH, 2, C_HEADS, C_HEAD_DIM, C_HEAD_DIM)
    nn = state[5].reshape(batch, DEPTH, 2, C_HEADS, C_HEAD_DIM)
    nm = state[6][:, :, :, 0].reshape(batch, DEPTH, 2, C_HEADS)
    return (y_p.reshape(batch, seq, D_MODEL), y_s.reshape(dec_batch, dec_seq, D_MODEL), nk, nv, ndk, ndv, nc, nn, nm)
```

```python
import functools
import math

import numpy as np
import jax
import jax.numpy as jnp
from jax import lax
from jax.experimental import pallas as pl
from jax.experimental.pallas import tpu as pltpu

F32 = jnp.float32
BF16 = jnp.bfloat16
HIGHEST = lax.Precision.HIGHEST

D_MODEL = 1024
DEPTH = 4
GRID_W = 64
ROPE_BASE = 10000.0
EPS = 1e-6
BRANCH_WIDTH = D_MODEL // 2
A_HEAD_DIM = 64
A_HEADS = 8
A_KV_HEADS = 2
B_GROUPS = 4
CHUNK = 128
C_HEADS = 4
C_HEAD_DIM = 128
D_HEADS = 4
D_VDIM = 128
D_HALF_DIM = 64
PEER_HEADS = 8
PEER_NKEYS = 128
PEER_EXPERTS = PEER_NKEYS * PEER_NKEYS
PEER_QDIM = 256
PEER_TOPK = 16
ALPHA = (2 * DEPTH) ** 0.25

_OFF_A = 0
_OFF_B = 768
_OFF_C = 1792
_OFF_CG = 3840
_OFF_D = 3856
_OFF_G = 5392

VMEM_LIMIT_BYTES = 56 * 1024 * 1024
NEG_INF = float("-inf")


def _cparams(*sem):
    return pltpu.CompilerParams(dimension_semantics=sem, vmem_limit_bytes=VMEM_LIMIT_BYTES)


def _dot(a, b):
    return jnp.dot(a, b, preferred_element_type=F32)


def _dot_nt(a, b):
    return lax.dot_general(a, b, (((1,), (1,)), ((), ())), preferred_element_type=F32)


def _modulated(x, mod_ref, which):
    base = 3 * D_MODEL * which
    sh = mod_ref[:, base:base + D_MODEL]
    sc = mod_ref[:, base + D_MODEL:base + 2 * D_MODEL]
    return x * (1.0 + sc) + sh


def _layer_norm_rows(z, g, b):
    mu = jnp.mean(z, axis=-1, keepdims=True)
    zc = z - mu
    var = jnp.mean(zc * zc, axis=-1, keepdims=True)
    return zc * lax.rsqrt(var + EPS) * g + b


def _rope(x, cos, sin_signed, lane):
    w = x.shape[1]
    nxt = pltpu.roll(x, w - 16, 1)
    prv = pltpu.roll(x, 16, 1)
    partner = jnp.where((lane % 32) < 16, nxt, prv)
    return x * cos + partner * sin_signed


def _tile_lanes(t, n):
    return t if n == 1 else jnp.concatenate([t] * n, axis=1)


def _mod_kernel(c_ref, w_ref, b_ref, o_ref):
    c = c_ref[...]
    s = c * jax.nn.sigmoid(c)
    o_ref[...] = jnp.dot(s, w_ref[...], precision=HIGHEST, preferred_element_type=F32) + b_ref[...]


def _modulation(c_rows, w_mod, b_mod):
    n_col = 6 * D_MODEL // 1024
    return pl.pallas_call(
        _mod_kernel,
        out_shape=jax.ShapeDtypeStruct((DEPTH, 8, 6 * D_MODEL), F32),
        grid=(DEPTH, n_col),
        in_specs=[
            pl.BlockSpec((8, D_MODEL), lambda l, j: (0, 0)),
            pl.BlockSpec((None, D_MODEL, 1024), lambda l, j: (l, 0, j)),
            pl.BlockSpec((None, 1, 1024), lambda l, j: (l, 0, j)),
        ],
        out_specs=pl.BlockSpec((None, 8, 1024), lambda l, j: (l, 0, j)),
        compiler_params=_cparams("arbitrary", "arbitrary"),
        name="modulation",
    )(c_rows, w_mod, b_mod.reshape(DEPTH, 1, 6 * D_MODEL))


def _attn_a_kernel(*refs, seq, tq, n_cache, rope, proj_rows):
    if rope:
        (x_ref, mod_ref, w_ref, gain_ref, bd_ref, rep_ref, cos_ref, sin_ref, ck_ref, cv_ref,
         y_ref, q_s, k_s, v_s) = refs
    else:
        (x_ref, mod_ref, w_ref, gain_ref, bd_ref, rep_ref) = refs[:6]
        (y_ref, nk_ref, nv_ref, q_s, k_s, v_s) = refs[-6:]
    qi = pl.program_id(1)

    @pl.when(qi == 0)
    def _project():
        lane = lax.broadcasted_iota(jnp.int32, (1, 640), 1)
        for r0 in range(0, seq, proj_rows):
            rows = pl.ds(r0, proj_rows)
            h = _modulated(x_ref[rows, :], mod_ref, 0).astype(BF16)
            p = _dot(h, w_ref[...])
            qk = p[:, :640]
            sq = qk * qk
            hi = sq.astype(BF16)
            lo = (sq - hi.astype(F32)).astype(BF16)
            ms = jnp.concatenate([_dot(hi[:, c:c + 128], bd_ref[...]) + _dot(lo[:, c:c + 128], bd_ref[...])
                                  for c in range(0, 640, 128)], axis=1)
            qk = qk * lax.rsqrt(ms + EPS) * gain_ref[...]
            v = p[:, 640:768]
            if not rope:
                nk_ref[rows, :] = qk[:, 512:640]
                nv_ref[rows, :] = v
            else:
                cos = _tile_lanes(cos_ref[rows, :], 5)
                sin = _tile_lanes(sin_ref[rows, :], 5)
                qk = _rope(qk, cos, sin, lane)
            q_s[rows, :] = (qk[:, :512] * (A_HEAD_DIM ** -0.5)).astype(BF16)
            k_s[rows, :] = _dot(qk[:, 512:640].astype(BF16), rep_ref[...]).astype(BF16)
            v_s[rows, :] = _dot(v.astype(BF16), rep_ref[...]).astype(BF16)
        if n_cache:
            crow = pl.ds(seq, n_cache)
            k_s[crow, :] = _dot(ck_ref[...].astype(BF16), rep_ref[...]).astype(BF16)
            v_s[crow, :] = _dot(cv_ref[...].astype(BF16), rep_ref[...]).astype(BF16)

    head_of_lane = lax.broadcasted_iota(jnp.int32, (1, 256), 1) // A_HEAD_DIM
    qb = q_s[pl.ds(pl.multiple_of(qi * tq, tq), tq), :]
    for g in range(A_KV_HEADS):
        cols = slice(g * 256, (g + 1) * 256)
        qg = qb[:, cols]
        kg = k_s[:, cols]
        vg = v_s[:, cols]
        acc = jnp.zeros((tq, 256), F32)
        for r in range(A_HEADS // A_KV_HEADS):
            sel = head_of_lane == r
            qm = jnp.where(sel, qg, jnp.zeros_like(qg))
            s = _dot_nt(qm, kg)
            m = jnp.max(s, axis=-1, keepdims=True)
            p = jnp.exp(s - m)
            l = jnp.sum(p, axis=-1, keepdims=True)
            o = _dot(p.astype(BF16), vg)
            acc = acc + jnp.where(sel, o * (1.0 / l), 0.0)
        y_ref[:, cols] = acc.astype(BF16)


def _carry_aliases(in_specs, args, carry, first_out):
    if carry is None:
        return {}
    aliases = {}
    for k, arr in enumerate(carry):
        aliases[len(args)] = first_out + k
        in_specs.append(pl.BlockSpec(memory_space=pl.ANY))
        args.append(arr)
    return aliases


def _attn_a(x, mod, w_a, gain_row, bd, rep, *, n_seq, seq, tq, rope=None, cache=None, layer=0, carry=None):
    n_q = seq // tq
    n_cache = 0 if cache is None else cache[0].shape[2]
    proj_rows = min(seq, 512)
    kern = functools.partial(_attn_a_kernel, seq=seq, tq=tq, n_cache=n_cache, rope=rope is not None,
                             proj_rows=proj_rows)
    const = lambda s, q: (0, 0)
    in_specs = [
        pl.BlockSpec((seq, D_MODEL), lambda s, q: (s, 0)),
        pl.BlockSpec((None, 1, 6 * D_MODEL), lambda s, q: (s if mod.shape[0] > 1 else 0, 0, 0)),
        pl.BlockSpec(w_a.shape, const),
        pl.BlockSpec(gain_row.shape, const),
        pl.BlockSpec(bd.shape, const),
        pl.BlockSpec(rep.shape, const),
    ]
    args = [x, mod, w_a, gain_row, bd, rep]
    n_tok = n_seq * seq
    y_spec = pl.BlockSpec((tq, BRANCH_WIDTH), lambda s, q: (s * n_q + q, 0))
    y_shape = jax.ShapeDtypeStruct((n_tok, BRANCH_WIDTH), BF16)
    if rope is not None:
        cos, sin = rope
        in_specs += [pl.BlockSpec(cos.shape, const), pl.BlockSpec(sin.shape, const),
                     pl.BlockSpec((None, None, n_cache, 128), lambda s, q: (s, layer, 0, 0)),
                     pl.BlockSpec((None, None, n_cache, 128), lambda s, q: (s, layer, 0, 0))]
        args += [cos, sin, cache[0], cache[1]]
        out_shape, out_specs, aliases = y_shape, y_spec, {}
    else:
        kv_shape = jax.ShapeDtypeStruct((n_seq, DEPTH, seq, 128), F32)
        kv_spec = pl.BlockSpec((None, None, seq, 128), lambda s, q: (s, layer, 0, 0))
        out_shape, out_specs = (y_shape, kv_shape, kv_shape), (y_spec, kv_spec, kv_spec)
        aliases = _carry_aliases(in_specs, args, carry, first_out=1)
    return pl.pallas_call(
        kern, out_shape=out_shape, grid=(n_seq, n_q), in_specs=in_specs, out_specs=out_specs,
        input_output_aliases=aliases,
        scratch_shapes=[pltpu.VMEM((seq, 512), BF16), pltpu.VMEM((seq + n_cache, 512), BF16),
                        pltpu.VMEM((seq + n_cache, 512), BF16)],
        compiler_params=_cparams("arbitrary", "arbitrary"),
        name="branch_a_lat" if rope is not None else "branch_a_ctx",
    )(*args)


def _attn_d_kernel(*refs, seq, tq, n_cache, rope, proj_rows, lam_init):
    if rope:
        (x_ref, mod_ref, w_ref, lam_ref, gn_ref, cos_ref, sin_ref, ck_ref, cv_ref,
         y_ref, q_s, k_s, v_s) = refs
    else:
        (x_ref, mod_ref, w_ref, lam_ref, gn_ref) = refs[:5]
        (y_ref, nk_ref, nv_ref, q_s, k_s, v_s) = refs[-6:]
    qi = pl.program_id(1)

    @pl.when(qi == 0)
    def _project():
        lane = lax.broadcasted_iota(jnp.int32, (1, 512), 1)
        for r0 in range(0, seq, proj_rows):
            rows = pl.ds(r0, proj_rows)
            h = _modulated(x_ref[rows, :], mod_ref, 0).astype(BF16)
            p = _dot(h, w_ref[...])
            dq, dk, dv = p[:, :512], p[:, 512:1024], p[:, 1024:1536]
            if not rope:
                nk_ref[rows, :] = dk
                nv_ref[rows, :] = dv
            else:
                cos = _tile_lanes(cos_ref[rows, :], 4)
                sin = _tile_lanes(sin_ref[rows, :], 4)
                dq = _rope(dq, cos, sin, lane)
                dk = _rope(dk, cos, sin, lane)
            q_s[rows, :] = (dq * (D_HALF_DIM ** -0.5)).astype(BF16)
            k_s[rows, :] = dk.astype(BF16)
            v_s[rows, :] = dv.astype(BF16)
        if n_cache:
            crow = pl.ds(seq, n_cache)
            k_s[crow, :] = ck_ref[...].astype(BF16)
            v_s[crow, :] = cv_ref[...].astype(BF16)

    lv = lam_ref[...]
    lam = (jnp.exp(jnp.sum(lv[0:1] * lv[1:2], axis=-1, keepdims=True))
           - jnp.exp(jnp.sum(lv[2:3] * lv[3:4], axis=-1, keepdims=True)) + lam_init)
    half_of_lane = lax.broadcasted_iota(jnp.int32, (1, 128), 1) // D_HALF_DIM
    qb = q_s[pl.ds(pl.multiple_of(qi * tq, tq), tq), :]
    for hd in range(D_HEADS):
        cols = slice(hd * 128, (hd + 1) * 128)
        qh = qb[:, cols]
        kh = k_s[:, cols]
        vh = v_s[:, cols]
        probs = []
        for j in range(2):
            qm = jnp.where(half_of_lane == j, qh, jnp.zeros_like(qh))
            s = _dot_nt(qm, kh)
            m = jnp.max(s, axis=-1, keepdims=True)
            p = jnp.exp(s - m)
            l = jnp.sum(p, axis=-1, keepdims=True)
            probs.append(p * (1.0 / l))
        a = (probs[0] - lam * probs[1]).astype(BF16)
        o = _dot(a, vh)
        ms = jnp.mean(o * o, axis=-1, keepdims=True)
        o = o * lax.rsqrt(ms + EPS) * gn_ref[:, cols] * (1.0 - lam_init)
        y_ref[:, cols] = o.astype(BF16)


def _attn_d(x, mod, w_d, lam_params, gn_row, *, n_seq, seq, tq, lam_init, rope=None, cache=None, layer=0,
            carry=None):
    n_q = seq // tq
    n_cache = 0 if cache is None else cache[0].shape[2]
    proj_rows = min(seq, 512)
    kern = functools.partial(_attn_d_kernel, seq=seq, tq=tq, n_cache=n_cache, rope=rope is not None,
                             proj_rows=proj_rows, lam_init=lam_init)
    const = lambda s, q: (0, 0)
    in_specs = [
        pl.BlockSpec((seq, D_MODEL), lambda s, q: (s, 0)),
        pl.BlockSpec((None, 1, 6 * D_MODEL), lambda s, q: (s if mod.shape[0] > 1 else 0, 0, 0)),
        pl.BlockSpec(w_d.shape, const),
        pl.BlockSpec(lam_params.shape, const),
        pl.BlockSpec(gn_row.shape, const),
    ]
    args = [x, mod, w_d, lam_params, gn_row]
    n_tok = n_seq * seq
    y_spec = pl.BlockSpec((tq, BRANCH_WIDTH), lambda s, q: (s * n_q + q, 0))
    y_shape = jax.ShapeDtypeStruct((n_tok, BRANCH_WIDTH), BF16)
    if rope is not None:
        cos, sin = rope
        in_specs += [pl.BlockSpec(cos.shape, const), pl.BlockSpec(sin.shape, const),
                     pl.BlockSpec((None, None, n_cache, 512), lambda s, q: (s, layer, 0, 0)),
                     pl.BlockSpec((None, None, n_cache, 512), lambda s, q: (s, layer, 0, 0))]
        args += [cos, sin, cache[0], cache[1]]
        out_shape, out_specs, aliases = y_shape, y_spec, {}
    else:
        kv_shape = jax.ShapeDtypeStruct((n_seq, DEPTH, seq, 512), F32)
        kv_spec = pl.BlockSpec((None, None, seq, 512), lambda s, q: (s, layer, 0, 0))
        out_shape, out_specs = (y_shape, kv_shape, kv_shape), (y_spec, kv_spec, kv_spec)
        aliases = _carry_aliases(in_specs, args, carry, first_out=1)
    return pl.pallas_call(
        kern, out_shape=out_shape, grid=(n_seq, n_q), in_specs=in_specs, out_specs=out_specs,
        input_output_aliases=aliases,
        scratch_shapes=[pltpu.VMEM((seq, 512), BF16), pltpu.VMEM((seq + n_cache, 512), BF16),
                        pltpu.VMEM((seq + n_cache, 512), BF16)],
        compiler_params=_cparams("arbitrary", "arbitrary"),
        name="branch_d_lat" if rope is not None else "branch_d_ctx",
    )(*args)


def _gmlp_kernel(x_ref, mod_ref, w_ref, ws_ref, bias_ref, y_ref, *, rows):
    h = _modulated(x_ref[...], mod_ref, 0).astype(BF16)
    p = _dot(h, w_ref[...])
    u, v = p[:, :BRANCH_WIDTH], p[:, BRANCH_WIDTH:]
    mu = jnp.mean(v, axis=-1, keepdims=True)
    vc = v - mu
    var = jnp.mean(vc * vc, axis=-1, keepdims=True)
    vn = (vc * lax.rsqrt(var + EPS)).astype(BF16)
    for c in range(rows // CHUNK):
        rs = slice(c * CHUNK, (c + 1) * CHUNK)
        for g in range(B_GROUPS):
            cs = slice(g * 128, (g + 1) * 128)
            s = _dot(ws_ref[g].astype(BF16), vn[rs, cs]) + bias_ref[:, cs]
            y_ref[rs, cs] = (u[rs, cs] * s).astype(BF16)


def _gmlp(x, mod, w_b, ws, bias_full, *, seq, rows):
    n_tok = x.shape[0]
    per_seq = seq // rows
    kern = functools.partial(_gmlp_kernel, rows=rows)
    return pl.pallas_call(
        kern, out_shape=jax.ShapeDtypeStruct((n_tok, BRANCH_WIDTH), BF16),
        grid=(n_tok // rows,),
        in_specs=[
            pl.BlockSpec((rows, D_MODEL), lambda i: (i, 0)),
            pl.BlockSpec((None, 1, 6 * D_MODEL), lambda i: (i // per_seq if mod.shape[0] > 1 else 0, 0, 0)),
            pl.BlockSpec(w_b.shape, lambda i: (0, 0)),
            pl.BlockSpec(ws.shape, lambda i: (0, 0, 0)),
            pl.BlockSpec(bias_full.shape, lambda i: (0, 0)),
        ],
        out_specs=pl.BlockSpec((rows, BRANCH_WIDTH), lambda i: (i, 0)),
        compiler_params=_cparams("arbitrary"),
        name="branch_b",
    )(x, mod, w_b, ws, bias_full)


def _mlstm_t_kernel(*refs, seq, has_init, proj_rows):
    if has_init:
        (x_ref, mod_ref, wn_ref, wt_ref, gbr_ref, gbt_ref, gnt_ref, tril_ref, triu_ref, c0_ref, n0_ref, m0_ref,
         y_ref, q_s, k_s, qt_s, vt_s, ot_s, g_s, gt_s, hf_s, hb_s, st_s, m_s) = refs
    else:
        (x_ref, mod_ref, wn_ref, wt_ref, gbr_ref, gbt_ref, gnt_ref, tril_ref, triu_ref) = refs[:9]
        (y_ref, cout_ref, nout_ref, mout_ref,
         q_s, k_s, qt_s, vt_s, ot_s, g_s, gt_s, hf_s, hb_s, st_s, m_s) = refs[-15:]
    n_chunk = seq // CHUNK
    for r0 in range(0, seq, proj_rows):
        span = pl.ds(r0, proj_rows)
        h = _modulated(x_ref[span, :], mod_ref, 0).astype(BF16)
        p = _dot(h, wn_ref[...])
        q_s[span, :] = p[:, 0:512].astype(BF16)
        k_s[span, :] = (p[:, 512:1024] * (C_HEAD_DIM ** -0.5)).astype(BF16)
        g_s[span, :] = p[:, 1024:1152] + gbr_ref[...]
        pt = _dot_nt(wt_ref[...], h)
        qt_s[:, span] = pt[0:512].astype(BF16)
        vt_s[:, span] = pt[512:1024].astype(BF16)
        ot_s[:, span] = jax.nn.sigmoid(pt[1024:1536]).astype(BF16)
        gt_s[:, span] = pt[1536:1664] + _tile_lanes(gbt_ref[...], proj_rows // 128)

    first_row = lax.broadcasted_iota(jnp.int32, (CHUNK, CHUNK), 0) == 0
    for sidx in range(2 * C_HEADS):
        if has_init:
            st_s[sidx, 0:128, :] = c0_ref[sidx].T
            st_s[sidx, 128:256, :] = jnp.where(first_row, jnp.broadcast_to(n0_ref[sidx:sidx + 1, :], (CHUNK, 128)), 0.0)
        else:
            st_s[sidx] = jnp.zeros((2 * CHUNK, 128), F32)
    m_s[...] = m0_ref[...] if has_init else jnp.zeros_like(m_s)

    tril = tril_ref[...]
    triu = triu_ref[...]
    row_i = lax.broadcasted_iota(jnp.int32, (CHUNK, CHUNK), 0)
    col_i = lax.broadcasted_iota(jnp.int32, (CHUNK, CHUNK), 1)
    visible = (row_i <= col_i, row_i >= col_i)
    ones_blk = jnp.where(first_row, 1.0, 0.0).astype(BF16)

    def chunk_step(c, carry):
        prep = []
        for direction in range(2):
            cc = c if direction == 0 else n_chunk - 1 - c
            span = pl.ds(pl.multiple_of(cc * CHUNK, CHUNK), CHUNK)
            gates = g_s[span, :]
            gates_t = gt_s[:, span]
            logf = jnp.minimum(gates, 0.0) - jnp.log1p(jnp.exp(-jnp.abs(gates)))
            logf_t = jnp.minimum(gates_t, 0.0) - jnp.log1p(jnp.exp(-jnp.abs(gates_t)))
            tri_col = tril if direction == 0 else triu
            tri_row = triu if direction == 0 else tril
            b_col_all = jnp.dot(tri_col, logf, precision=HIGHEST, preferred_element_type=F32)
            b_row_all = jnp.dot(logf_t, tri_row, precision=HIGHEST, preferred_element_type=F32)
            r_col_all = gates - pltpu.roll(b_col_all, 128 - 4, 1)
            base = direction * 8
            prep.append((span, r_col_all, gates_t[base:base + 4, :], b_row_all[base + 4:base + 8, :]))
        for hd in range(C_HEADS):
            for direction in range(2):
                span, r_col_all, i_rows, b_rows = prep[direction]
                base = direction * 8
                last = CHUNK - 1 if direction == 0 else 0
                h_out = hf_s if direction == 0 else hb_s
                sidx = direction * 4 + hd
                blk = slice(hd * 128, (hd + 1) * 128)
                r_col = r_col_all[:, base + hd:base + hd + 1]
                b_row = b_rows[hd:hd + 1, :]
                i_row = i_rows[hd:hd + 1, :]
                m_prev = m_s[sidx:sidx + 1, 0:1]
                r_wide = jnp.broadcast_to(r_col, (CHUNK, CHUNK))
                peak = jnp.max(jnp.where(visible[direction], r_wide, NEG_INF), axis=0, keepdims=True)
                m_row = b_row + jnp.maximum(peak, m_prev)
                qh = q_s[span, blk]
                kh = k_s[span, blk]
                arg = jnp.where(visible[direction], r_wide + (b_row - m_row), NEG_INF)
                w_t = (_dot_nt(kh, qh) * jnp.exp(arg)).astype(BF16)
                inter = jnp.exp(b_row + m_prev - m_row)
                v_one = jnp.concatenate([vt_s[blk, span], ones_blk], axis=0)
                state = st_s[sidx]
                both = _dot(v_one, w_t) + inter * _dot(state.astype(BF16), qt_s[blk, span])
                den = both[128:129, :]
                h_out[blk, span] = both[0:128, :] / jnp.maximum(jnp.abs(den), jnp.exp(-m_row))
                b_last = b_row[:, last:last + 1]
                g_row = b_last - b_row + i_row
                m_new = jnp.maximum(b_last + m_prev, jnp.max(g_row, axis=-1, keepdims=True))
                decay = jnp.exp(b_last + m_prev - m_new)
                scaled = v_one * jnp.exp(g_row - m_new).astype(BF16)
                st_s[sidx] = decay * state + _dot(scaled, kh)
                m_s[sidx:sidx + 1, :] = jnp.broadcast_to(m_new, (1, 128))
        return carry

    lax.fori_loop(0, n_chunk, chunk_step, 0)

    for r0 in range(0, seq, proj_rows):
        span = pl.ds(r0, proj_rows)
        for hd in range(C_HEADS):
            blk = slice(hd * 128, (hd + 1) * 128)
            hh = hf_s[blk, span] + hb_s[blk, span]
            mu = jnp.mean(hh, axis=0, keepdims=True)
            hc = hh - mu
            var = jnp.mean(hc * hc, axis=0, keepdims=True)
            gn = _tile_lanes(gnt_ref[blk, :], proj_rows // 128)
            y_t = hc * lax.rsqrt(var + EPS) * gn * ot_s[blk, span].astype(F32)
            y_ref[span, blk] = y_t.T.astype(BF16)
    if not has_init:
        for sidx in range(2 * C_HEADS):
            cout_ref[sidx] = st_s[sidx, 0:128, :].T
            nout_ref[sidx:sidx + 1, :] = st_s[sidx, 128:129, :]
        mout_ref[...] = m_s[...]


def _mlstm_t(x, mod, w_nat, w_t, gate_bias_row, gate_bias_t, gn_t, tril, triu, *, n_seq, seq, init=None, layer=0,
             carry=None):
    proj_rows = min(seq, 512)
    kern = functools.partial(_mlstm_t_kernel, seq=seq, has_init=init is not None, proj_rows=proj_rows)
    const = lambda s: (0, 0)
    in_specs = [
        pl.BlockSpec((seq, D_MODEL), lambda s: (s, 0)),
        pl.BlockSpec((None, 1, 6 * D_MODEL), lambda s: (s if mod.shape[0] > 1 else 0, 0, 0)),
    ] + [pl.BlockSpec(a.shape, const) for a in (w_nat, w_t, gate_bias_row, gate_bias_t, gn_t, tril, triu)]
    args = [x, mod, w_nat, w_t, gate_bias_row, gate_bias_t, gn_t, tril, triu]
    n_tok = n_seq * seq
    y_shape = jax.ShapeDtypeStruct((n_tok, BRANCH_WIDTH), BF16)
    y_spec = pl.BlockSpec((seq, BRANCH_WIDTH), lambda s: (s, 0))
    if init is not None:
        c0, n0, m0 = init
        in_specs += [pl.BlockSpec((None, None, 8, 128, 128), lambda s: (s, layer, 0, 0, 0)),
                     pl.BlockSpec((None, None, 8, 128), lambda s: (s, layer, 0, 0)),
                     pl.BlockSpec((None, None, 8, 128), lambda s: (s, layer, 0, 0))]
        args += [c0, n0, m0]
        out_shape, out_specs, aliases = y_shape, y_spec, {}
    else:
        out_shape = (y_shape, jax.ShapeDtypeStruct((n_seq, DEPTH, 8, 128, 128), F32),
                     jax.ShapeDtypeStruct((n_seq, DEPTH, 8, 128), F32),
                     jax.ShapeDtypeStruct((n_seq, DEPTH, 8, 128), F32))
        out_specs = (y_spec, pl.BlockSpec((None, None, 8, 128, 128), lambda s: (s, layer, 0, 0, 0)),
                     pl.BlockSpec((None, None, 8, 128), lambda s: (s, layer, 0, 0)),
                     pl.BlockSpec((None, None, 8, 128), lambda s: (s, layer, 0, 0)))
        aliases = _carry_aliases(in_specs, args, carry, first_out=1)
    return pl.pallas_call(
        kern, out_shape=out_shape, grid=(n_seq,), in_specs=in_specs, out_specs=out_specs,
        input_output_aliases=aliases,
        scratch_shapes=[pltpu.VMEM((seq, 512), BF16), pltpu.VMEM((seq, 512), BF16),
                        pltpu.VMEM((512, seq), BF16), pltpu.VMEM((512, seq), BF16),
                        pltpu.VMEM((512, seq), BF16),
                        pltpu.VMEM((seq, 128), F32), pltpu.VMEM((128, seq), F32),
                        pltpu.VMEM((512, seq), F32), pltpu.VMEM((512, seq), F32),
                        pltpu.VMEM((8, 256, 128), F32), pltpu.VMEM((8, 128), F32)],
        compiler_params=_cparams("arbitrary"),
        name="branch_c_lat" if init is not None else "branch_c_ctx",
    )(*args)


def _merge_kernel(x_ref, mod_ref, ya_ref, yb_ref, yc_ref, yd_ref, wg_ref, wbr_ref, wout_ref, lng_ref, lnb_ref,
                  o_ref):
    x = x_ref[...]
    h = _modulated(x, mod_ref, 0).astype(BF16)
    mix = None
    for n, y_ref in enumerate((ya_ref, yb_ref, yc_ref, yd_ref)):
        gate = jax.nn.sigmoid(_dot(h, wg_ref[:, n * D_MODEL:(n + 1) * D_MODEL]))
        term = gate * _dot(y_ref[...], wbr_ref[n])
        mix = term if mix is None else mix + term
    out = _dot(mix.astype(BF16), wout_ref[...])
    g1 = mod_ref[:, 2 * D_MODEL:3 * D_MODEL]
    o_ref[...] = _layer_norm_rows(ALPHA * x + g1 * out, lng_ref[...], lnb_ref[...])


def _merge(x, mod, ys, w_g, w_br, w_out, ln_g, ln_b, *, seq, rows):
    n_tok = x.shape[0]
    per_seq = seq // rows
    tok = lambda i: (i, 0)
    c2 = lambda i: (0, 0)
    return pl.pallas_call(
        _merge_kernel, out_shape=jax.ShapeDtypeStruct((n_tok, D_MODEL), F32),
        grid=(n_tok // rows,),
        in_specs=[
            pl.BlockSpec((rows, D_MODEL), tok),
            pl.BlockSpec((None, 1, 6 * D_MODEL), lambda i: (i // per_seq if mod.shape[0] > 1 else 0, 0, 0)),
            pl.BlockSpec((rows, BRANCH_WIDTH), tok), pl.BlockSpec((rows, BRANCH_WIDTH), tok),
            pl.BlockSpec((rows, BRANCH_WIDTH), tok), pl.BlockSpec((rows, BRANCH_WIDTH), tok),
            pl.BlockSpec(w_g.shape, c2), pl.BlockSpec(w_br.shape, lambda i: (0, 0, 0)),
            pl.BlockSpec(w_out.shape, c2), pl.BlockSpec(ln_g.shape, c2), pl.BlockSpec(ln_b.shape, c2),
        ],
        out_specs=pl.BlockSpec((rows, D_MODEL), tok),
        compiler_params=_cparams("arbitrary"),
        name="merge",
    )(x, mod, *ys, w_g, w_br, w_out, ln_g, ln_b)


_TAKEN = -(2.0 ** 127)


def _top16(s):
    cur = s
    vals = []
    for r in range(PEER_TOPK):
        mx = jnp.max(cur, axis=0, keepdims=True)
        cur = jnp.where(cur == mx, _TAKEN * (1.0 + r / 32.0), cur)
        vals.append(mx)
    rank = jnp.where(cur <= _TAKEN, cur * (32.0 / _TAKEN) - 31.0, float(PEER_TOPK + 1))
    return jnp.concatenate(vals, axis=0), rank


def _pair_tables():
    pairs = [(k1, k2) for k1 in range(PEER_TOPK) for k2 in range(PEER_TOPK // (k1 + 1))]
    n = 56
    sel_a = np.zeros((n, PEER_TOPK), np.float32)
    sel_b = np.zeros((n, PEER_TOPK), np.float32)
    pad = np.full((n, 1), NEG_INF, np.float32)
    for row, (k1, k2) in enumerate(pairs):
        sel_a[row, k1] = 1.0
        sel_b[row, k2] = 1.0
        pad[row, 0] = 0.0
    return jnp.asarray(sel_a), jnp.asarray(sel_b), jnp.asarray(pad), jnp.asarray(sel_a.T, BF16)


def _route_kernel(x_ref, mod_ref, wq_ref, keys_ref, sela_ref, selb_ref, pad_ref, ind_ref,
                  cnt_ref, e1_ref, r2_ref, e2_ref, h2_s, *, heads):
    @pl.when(pl.program_id(1) == 0)
    def _modulate():
        h2_s[...] = _modulated(x_ref[...], mod_ref, 1).astype(BF16)

    pick = lambda sel_ref, v: jnp.dot(sel_ref[...], v, precision=HIGHEST, preferred_element_type=F32)
    q_all = _dot(h2_s[...], wq_ref[...])
    for hd in range(heads):
        q = q_all[:, hd * PEER_QDIM:(hd + 1) * PEER_QDIM]
        s1 = _dot_nt(keys_ref[hd, 0].astype(BF16), q[:, :128].astype(BF16))
        s2 = _dot_nt(keys_ref[hd, 1].astype(BF16), q[:, 128:].astype(BF16))
        a, rank1 = _top16(s1)
        b, rank2 = _top16(s2)
        ea = jnp.exp(a - a[0:1])
        eb = jnp.exp(b - b[0:1])
        cand = pick(sela_ref, a) + pick(selb_ref, b) + pad_ref[...]
        gate = pick(sela_ref, ea) * pick(selb_ref, eb)
        cur = cand
        thr = None
        for _ in range(PEER_TOPK):
            thr = jnp.max(cur, axis=0, keepdims=True)
            cur = jnp.where(cur == thr, NEG_INF, cur)
        chosen = cand >= thr
        z = jnp.sum(jnp.where(chosen, gate, 0.0), axis=0, keepdims=True)
        cnt_sorted = _dot(ind_ref[...], jnp.where(chosen, 1.0, 0.0).astype(BF16))
        rank1_b = rank1.astype(BF16)
        counts_b = cnt_sorted.astype(BF16)
        cnt = jnp.zeros(s1.shape, BF16)
        for r in range(PEER_TOPK):
            cnt = jnp.where(rank1_b == float(r + 1), counts_b[r:r + 1], cnt)
        cnt_ref[hd] = cnt.astype(F32)
        e1_ref[hd] = jnp.where(rank1 <= float(PEER_TOPK), jnp.exp(s1 - a[0:1]) * (0.5 / z), 0.0)
        packed = (PEER_NKEYS // 16, 16, s2.shape[1])
        r2_ref[hd] = rank2.astype(BF16).reshape(packed)
        e2_ref[hd] = jnp.where(rank2 <= float(PEER_TOPK), jnp.exp(s2 - b[0:1]), 0.0).astype(BF16).reshape(packed)


def _route(x1, mod, wq, keys, *, seq, cols, heads):
    n_tok = x1.shape[0]
    per_seq = seq // cols
    tables = _pair_tables()
    row_shape = jax.ShapeDtypeStruct((PEER_HEADS, PEER_NKEYS, n_tok), F32)
    col_shape = jax.ShapeDtypeStruct((PEER_HEADS, PEER_NKEYS // 16, 16, n_tok), BF16)
    col_spec = pl.BlockSpec((heads, PEER_NKEYS // 16, 16, cols), lambda i, h: (h, 0, 0, i))
    spec = pl.BlockSpec((heads, PEER_NKEYS, cols), lambda i, h: (h, 0, i))
    return pl.pallas_call(
        functools.partial(_route_kernel, heads=heads), out_shape=(row_shape, row_shape, col_shape, col_shape),
        grid=(n_tok // cols, PEER_HEADS // heads),
        in_specs=[
            pl.BlockSpec((cols, D_MODEL), lambda i, h: (i, 0)),
            pl.BlockSpec((None, 1, 6 * D_MODEL), lambda i, h: (i // per_seq if mod.shape[0] > 1 else 0, 0, 0)),
            pl.BlockSpec((D_MODEL, heads * PEER_QDIM), lambda i, h: (0, h)),
            pl.BlockSpec((heads, 2, PEER_NKEYS, PEER_QDIM // 2), lambda i, h: (h, 0, 0, 0)),
        ] + [pl.BlockSpec(t.shape, lambda i, h: (0, 0)) for t in tables],
        out_specs=(spec, spec, col_spec, col_spec),
        scratch_shapes=[pltpu.VMEM((cols, D_MODEL), BF16)],
        compiler_params=_cparams("arbitrary", "arbitrary"),
        name="peer_route",
    )(x1, mod, wq, keys, *tables)


def _peer_kernel(x_ref, mod_ref, u_ref, v_ref, cnt_ref, e1_ref, r2_ref, e2_ref, lng_ref, lnb_ref,
                 o_ref, h2t_s, acc_s, *, key_rows):
    e = pl.program_id(1)

    @pl.when(e == 0)
    def _init():
        h2 = _modulated(x_ref[...], mod_ref, 1)
        h2t_s[...] = h2.T.astype(BF16)
        acc_s[...] = jnp.zeros_like(acc_s)

    n_tok = h2t_s.shape[1]
    first_key = pl.multiple_of(e * key_rows, key_rows)
    zero = jnp.zeros((PEER_NKEYS // 16, 16, n_tok), BF16)
    act = _dot(u_ref[...], h2t_s[...]).astype(BF16)
    act = act * (1.0 + lax.erf(act * (2.0 ** -0.5)))
    pieces = []
    for r in range(key_rows):
        g = None
        for hd in range(PEER_HEADS):
            cnt_blk = cnt_ref[hd, pl.ds(first_key, key_rows), :]
            e1_blk = e1_ref[hd, pl.ds(first_key, key_rows), :]
            cnt_rows = jnp.broadcast_to(cnt_blk[r:r + 1, :], (16, n_tok)).astype(BF16)
            e1_rows = jnp.broadcast_to(e1_blk[r:r + 1, :], (16, n_tok)).astype(BF16)
            term = jnp.where(r2_ref[hd] <= cnt_rows[None], e2_ref[hd], zero) * e1_rows[None]
            g = term if g is None else g + term
        pieces.append(g.reshape(PEER_NKEYS, n_tok) * act[r * PEER_NKEYS:(r + 1) * PEER_NKEYS, :])
    acc_s[...] += lax.dot_general(v_ref[...], jnp.concatenate(pieces, axis=0), (((0,), (0,)), ((), ())),
                                  preferred_element_type=F32)

    @pl.when(e == pl.num_programs(1) - 1)
    def _finish():
        x = x_ref[...]
        g2 = mod_ref[:, 5 * D_MODEL:6 * D_MODEL]
        o_ref[...] = _layer_norm_rows(ALPHA * x + g2 * acc_s[...].T, lng_ref[...], lnb_ref[...])


def _peer(x1, mod, u, v, route, ln_g, ln_b, *, seq, cols, key_rows, layer):
    n_tok = x1.shape[0]
    per_seq = seq // cols
    n_exp = key_rows * PEER_NKEYS
    n_tiles = PEER_EXPERTS // n_exp
    assert key_rows % 8 == 0
    kern = functools.partial(_peer_kernel, key_rows=key_rows)
    cnt, e1, r2, e2 = route
    rspec = pl.BlockSpec((PEER_HEADS, PEER_NKEYS, cols), lambda i, e: (0, 0, i))
    cspec = pl.BlockSpec((PEER_HEADS, PEER_NKEYS // 16, 16, cols), lambda i, e: (0, 0, 0, i))
    return pl.pallas_call(
        kern, out_shape=jax.ShapeDtypeStruct((n_tok, D_MODEL), F32),
        grid=(n_tok // cols, n_tiles),
        in_specs=[
            pl.BlockSpec((cols, D_MODEL), lambda i, e: (i, 0)),
            pl.BlockSpec((None, 1, 6 * D_MODEL), lambda i, e: (i // per_seq if mod.shape[0] > 1 else 0, 0, 0)),
            pl.BlockSpec((None, n_exp, D_MODEL), lambda i, e: (layer, e, 0)),
            pl.BlockSpec((None, n_exp, D_MODEL), lambda i, e: (layer, e, 0)),
            rspec, rspec, cspec, cspec,
            pl.BlockSpec(ln_g.shape, lambda i, e: (0, 0)), pl.BlockSpec(ln_b.shape, lambda i, e: (0, 0)),
        ],
        out_specs=pl.BlockSpec((cols, D_MODEL), lambda i, e: (i, 0)),
        scratch_shapes=[pltpu.VMEM((D_MODEL, cols), BF16), pltpu.VMEM((D_MODEL, cols), F32)],
        compiler_params=_cparams("arbitrary", "arbitrary"),
        name="peer_experts",
    )(x1, mod, u, v, cnt, e1, r2, e2, ln_g, ln_b)


def _rope_tables(seq):
    t = np.arange(seq)
    pos = np.stack([t // GRID_W, t % GRID_W], axis=1).astype(np.float64)
    inv = ROPE_BASE ** (-np.arange(16, dtype=np.float64) / 16)
    lane = np.arange(64)
    ang = pos[:, lane // 32] * inv[lane % 16][None, :]
    sign = np.where((lane % 32) < 16, -1.0, 1.0)[None, :]
    cos = np.tile(np.cos(ang), (1, 2)).astype(np.float32)
    sin = np.tile(np.sin(ang) * sign, (1, 2)).astype(np.float32)
    return jnp.asarray(cos), jnp.asarray(sin)


def _static_tables():
    lane = np.arange(128)
    bd = (lane[:, None] // 64 == lane[None, :] // 64).astype(np.float32) / 64.0
    src = np.arange(128)
    dst = np.arange(512)
    rep = ((src[:, None] // 64 == dst[None, :] // 256) & (src[:, None] % 64 == dst[None, :] % 64))
    idx = np.arange(CHUNK)
    tril = (idx[None, :] <= idx[:, None]).astype(np.float32)
    triu = (idx[None, :] >= idx[:, None]).astype(np.float32)
    return (jnp.asarray(bd, BF16), jnp.asarray(rep.astype(np.float32), BF16), jnp.asarray(tril), jnp.asarray(triu))


def _layer_params(l, w_in, attn_qk_gain, gmlp_ws, gmlp_b, mlstm_gate_bias, mlstm_gn, diff_lambda, diff_gn,
                  w_branch, w_out, ln_g, ln_b, peer_wq, peer_keys, peer_u, peer_v):
    w = w_in[l]
    p = {}
    p["w_a"] = w[:, _OFF_A:_OFF_B].astype(BF16)
    p["w_b"] = w[:, _OFF_B:_OFF_C].astype(BF16)
    cq, ck = w[:, _OFF_C:_OFF_C + 512], w[:, _OFF_C + 512:_OFF_C + 1024]
    cv, co = w[:, _OFF_C + 1024:_OFF_C + 1536], w[:, _OFF_C + 1536:_OFF_C + 2048]
    cg = jnp.concatenate([w[:, _OFF_CG:_OFF_D], jnp.zeros((D_MODEL, 112), F32)], axis=1)
    p["w_c_nat"] = jnp.concatenate([cq, ck, cg], axis=1).astype(BF16)
    p["w_c_t"] = jnp.concatenate([cq, cv, co, cg], axis=1).T.astype(BF16)
    p["w_d"] = w[:, _OFF_D:_OFF_G].astype(BF16)
    p["w_g"] = w[:, _OFF_G:].astype(BF16)
    gain = attn_qk_gain[l]
    p["gain_row"] = jnp.concatenate([jnp.tile(gain[0], A_HEADS), jnp.tile(gain[1], A_KV_HEADS)])[None, :]
    p["ws"] = gmlp_ws[l]
    p["bias_full"] = jnp.repeat(gmlp_b[l].T, 128, axis=1)
    p["gate_bias_row"] = jnp.concatenate([mlstm_gate_bias[l].reshape(16), jnp.zeros((112,), F32)])[None, :]
    p["gate_bias_t"] = jnp.broadcast_to(p["gate_bias_row"].reshape(128, 1), (128, 128))
    p["mlstm_gn_t"] = jnp.broadcast_to(mlstm_gn[l].reshape(BRANCH_WIDTH, 1), (BRANCH_WIDTH, 128))
    p["lam"] = diff_lambda[l]
    p["diff_gn_row"] = diff_gn[l].reshape(1, BRANCH_WIDTH)
    p["w_br"] = w_branch[l].astype(BF16)
    p["w_out"] = w_out[l].astype(BF16)
    p["ln_g0"], p["ln_b0"] = ln_g[l, 0][None, :], ln_b[l, 0][None, :]
    p["ln_g1"], p["ln_b1"] = ln_g[l, 1][None, :], ln_b[l, 1][None, :]
    p["wq"] = peer_wq[l].astype(BF16)
    p["keys"] = peer_keys[l]
    p["u"], p["v"] = peer_u, peer_v
    return p


def _trunk_layer(x, mod, p, tabs, *, l, n_seq, seq, cfg, ctx_cache=None, prev_state=None):
    bd, rep, tril, triu = tabs
    lam_init = 0.8 - 0.6 * math.exp(-0.3 * l)
    state = None
    if ctx_cache is None:
        prev = (None, None, None) if prev_state is None else (prev_state[0:2], prev_state[2:4], prev_state[4:7])
        ya, nk, nv = _attn_a(x, mod, p["w_a"], p["gain_row"], bd, rep, n_seq=n_seq, seq=seq, tq=cfg["tq"],
                             layer=l, carry=prev[0])
        yd, ndk, ndv = _attn_d(x, mod, p["w_d"], p["lam"], p["diff_gn_row"], n_seq=n_seq, seq=seq, tq=cfg["tq"],
                               lam_init=lam_init, layer=l, carry=prev[1])
        yc, c_new, n_new, m_new = _mlstm_t(x, mod, p["w_c_nat"], p["w_c_t"], p["gate_bias_row"], p["gate_bias_t"],
                                           p["mlstm_gn_t"], tril, triu, n_seq=n_seq, seq=seq, layer=l, carry=prev[2])
        state = (nk, nv, ndk, ndv, c_new, n_new, m_new)
    else:
        rope, cak, cav, cdk, cdv, c0, n0, m0 = ctx_cache
        ya = _attn_a(x, mod, p["w_a"], p["gain_row"], bd, rep, n_seq=n_seq, seq=seq, tq=cfg["tq"],
                     rope=rope, cache=(cak, cav), layer=l)
        yd = _attn_d(x, mod, p["w_d"], p["lam"], p["diff_gn_row"], n_seq=n_seq, seq=seq, tq=cfg["tq"],
                     lam_init=lam_init, rope=rope, cache=(cdk, cdv), layer=l)
        yc = _mlstm_t(x, mod, p["w_c_nat"], p["w_c_t"], p["gate_bias_row"], p["gate_bias_t"], p["mlstm_gn_t"],
                      tril, triu, n_seq=n_seq, seq=seq, init=(c0, n0, m0), layer=l)
    yb = _gmlp(x, mod, p["w_b"], p["ws"], p["bias_full"], seq=seq, rows=cfg["rows"])
    x1 = _merge(x, mod, (ya, yb, yc, yd), p["w_g"], p["w_br"], p["w_out"], p["ln_g0"], p["ln_b0"],
                seq=seq, rows=cfg["rows"])
    route = _route(x1, mod, p["wq"], p["keys"], seq=seq, cols=cfg["route_cols"], heads=cfg["route_heads"])
    x2 = _peer(x1, mod, p["u"], p["v"], route, p["ln_g1"], p["ln_b1"], seq=seq, cols=cfg["cols"], layer=l,
               key_rows=cfg["key_rows"])
    return x2, state


def kernel(x_prompt, x_sample, cache_a_k, cache_a_v, cache_d_k, cache_d_v, state_c_C, state_c_n, state_c_m,
           c, c_ctx, w_mod, b_mod, w_in, attn_qk_gain, gmlp_ws, gmlp_b, mlstm_gate_bias, mlstm_gn,
           diff_lambda, diff_gn, w_branch, w_out, ln_g, ln_b, peer_wq, peer_keys, peer_u, peer_v):
    batch, seq, _ = x_prompt.shape
    dec_batch, dec_seq, _ = x_sample.shape
    past = cache_a_k.shape[2]
    c_rows = jnp.concatenate([c_ctx[None, :], c, jnp.zeros((8 - 1 - dec_batch, D_MODEL), F32)], axis=0)
    mods = _modulation(c_rows, w_mod, b_mod)
    tabs = _static_tables()
    rope = _rope_tables(dec_seq)
    cak = cache_a_k.reshape(dec_batch, DEPTH, past, 128)
    cav = cache_a_v.reshape(dec_batch, DEPTH, past, 128)
    cdk = cache_d_k.reshape(dec_batch, DEPTH, past, 512)
    cdv = cache_d_v.reshape(dec_batch, DEPTH, past, 512)
    c0 = state_c_C.reshape(dec_batch, DEPTH, 8, 128, 128)
    n0 = state_c_n.reshape(dec_batch, DEPTH, 8, 128)
    m0 = jnp.broadcast_to(state_c_m.reshape(dec_batch, DEPTH, 8, 1), (dec_batch, DEPTH, 8, 128))
    cfg_ctx = dict(tq=seq, rows=512, cols=512, key_rows=16, route_cols=min(1024, batch * seq), route_heads=4)
    cfg_lat = dict(tq=min(256, dec_seq), rows=min(512, dec_seq), cols=512, key_rows=16,
                   route_cols=min(1024, dec_seq), route_heads=4)
    y_p = x_prompt.reshape(batch * seq, D_MODEL)
    y_s = x_sample.reshape(dec_batch * dec_seq, D_MODEL)
    state = None
    u_all, v_all = peer_u.astype(BF16), peer_v.astype(BF16)
    for l in range(DEPTH):
        p = _layer_params(l, w_in, attn_qk_gain, gmlp_ws, gmlp_b, mlstm_gate_bias, mlstm_gn, diff_lambda, diff_gn,
                          w_branch, w_out, ln_g, ln_b, peer_wq, peer_keys, u_all, v_all)
        mod_ctx = mods[l, 0:1].reshape(1, 1, 6 * D_MODEL)
        mod_lat = mods[l, 1:1 + dec_batch].reshape(dec_batch, 1, 6 * D_MODEL)
        y_p, state = _trunk_layer(y_p, mod_ctx, p, tabs, l=l, n_seq=batch, seq=seq, cfg=cfg_ctx, prev_state=state)
        y_s, _ = _trunk_layer(y_s, mod_lat, p, tabs, l=l, n_seq=dec_batch, seq=dec_seq, cfg=cfg_lat,
                              ctx_cache=(rope, cak, cav, cdk, cdv, c0, n0, m0))
    nk = state[0].reshape(batch, DEPTH, seq, A_KV_HEADS, A_HEAD_DIM)
    nv = state[1].reshape(batch, DEPTH, seq, A_KV_HEADS, A_HEAD_DIM)
    ndk = state[2].reshape(batch, DEPTH, seq, D_HEADS, 2, D_HALF_DIM)
    ndv = state[3].reshape(batch, DEPTH, seq, D_HEADS, D_VDIM)
    nc = state[4].reshape(batch, DEPTH, 2, C_HEADS, C_HEAD_DIM, C_HEAD_DIM)
    nn = state[5].reshape(batch, DEPTH, 2, C_HEADS, C_HEAD_DIM)
    nm = state[6][:, :, :, 0].reshape(batch, DEPTH, 2, C_HEADS)
    return (y_p.reshape(batch, seq, D_MODEL), y_s.reshape(dec_batch, dec_seq, D_MODEL), nk, nv, ndk, ndv, nc, nn, nm)
```

```python
import functools
import math

import numpy as np
import jax
import jax.numpy as jnp
from jax import lax
from jax.experimental import pallas as pl
from jax.experimental.pallas import tpu as pltpu

F32 = jnp.float32
BF16 = jnp.bfloat16
HIGHEST = lax.Precision.HIGHEST

D_MODEL = 1024
DEPTH = 4
GRID_W = 64
ROPE_BASE = 10000.0
EPS = 1e-6
BRANCH_WIDTH = D_MODEL // 2
A_HEAD_DIM = 64
A_HEADS = 8
A_KV_HEADS = 2
B_GROUPS = 4
CHUNK = 128
C_HEADS = 4
C_HEAD_DIM = 128
D_HEADS = 4
D_VDIM = 128
D_HALF_DIM = 64
PEER_HEADS = 8
PEER_NKEYS = 128
PEER_EXPERTS = PEER_NKEYS * PEER_NKEYS
PEER_QDIM = 256
PEER_TOPK = 16
ALPHA = (2 * DEPTH) ** 0.25

_OFF_A = 0
_OFF_B = 768
_OFF_C = 1792
_OFF_CG = 3840
_OFF_D = 3856
_OFF_G = 5392

VMEM_LIMIT_BYTES = 56 * 1024 * 1024
NEG_INF = float("-inf")


def _cparams(*sem):
    return pltpu.CompilerParams(dimension_semantics=sem, vmem_limit_bytes=VMEM_LIMIT_BYTES)


def _dot(a, b):
    return jnp.dot(a, b, preferred_element_type=F32)


def _dot_nt(a, b):
    return lax.dot_general(a, b, (((1,), (1,)), ((), ())), preferred_element_type=F32)


def _modulated(x, mod_ref, which):
    base = 3 * D_MODEL * which
    sh = mod_ref[:, base:base + D_MODEL]
    sc = mod_ref[:, base + D_MODEL:base + 2 * D_MODEL]
    return x * (1.0 + sc) + sh


def _layer_norm_rows(z, g, b):
    mu = jnp.mean(z, axis=-1, keepdims=True)
    zc = z - mu
    var = jnp.mean(zc * zc, axis=-1, keepdims=True)
    return zc * lax.rsqrt(var + EPS) * g + b


def _rope(x, cos, sin_signed, lane):
    w = x.shape[1]
    nxt = pltpu.roll(x, w - 16, 1)
    prv = pltpu.roll(x, 16, 1)
    partner = jnp.where((lane % 32) < 16, nxt, prv)
    return x * cos + partner * sin_signed


def _tile_lanes(t, n):
    return t if n == 1 else jnp.concatenate([t] * n, axis=1)


def _mod_kernel(c_ref, w_ref, b_ref, o_ref):
    c = c_ref[...]
    s = c * jax.nn.sigmoid(c)
    o_ref[...] = jnp.dot(s, w_ref[...], precision=HIGHEST, preferred_element_type=F32) + b_ref[...]


def _modulation(c_rows, w_mod, b_mod):
    n_col = 6 * D_MODEL // 1024
    return pl.pallas_call(
        _mod_kernel,
        out_shape=jax.ShapeDtypeStruct((DEPTH, 8, 6 * D_MODEL), F32),
        grid=(DEPTH, n_col),
        in_specs=[
            pl.BlockSpec((8, D_MODEL), lambda l, j: (0, 0)),
            pl.BlockSpec((None, D_MODEL, 1024), lambda l, j: (l, 0, j)),
            pl.BlockSpec((None, 1, 1024), lambda l, j: (l, 0, j)),
        ],
        out_specs=pl.BlockSpec((None, 8, 1024), lambda l, j: (l, 0, j)),
        compiler_params=_cparams("arbitrary", "arbitrary"),
        name="modulation",
    )(c_rows, w_mod, b_mod.reshape(DEPTH, 1, 6 * D_MODEL))


def _attn_a_kernel(*refs, seq, tq, n_cache, rope, proj_rows):
    if rope:
        (x_ref, mod_ref, w_ref, gain_ref, bd_ref, rep_ref, cos_ref, sin_ref, ck_ref, cv_ref,
         y_ref, q_s, k_s, v_s) = refs
    else:
        (x_ref, mod_ref, w_ref, gain_ref, bd_ref, rep_ref) = refs[:6]
        (y_ref, nk_ref, nv_ref, q_s, k_s, v_s) = refs[-6:]
    qi = pl.program_id(1)

    @pl.when(qi == 0)
    def _project():
        lane = lax.broadcasted_iota(jnp.int32, (1, 640), 1)
        for r0 in range(0, seq, proj_rows):
            rows = pl.ds(r0, proj_rows)
            h = _modulated(x_ref[rows, :], mod_ref, 0).astype(BF16)
            p = _dot(h, w_ref[...])
            qk = p[:, :640]
            sq = qk * qk
            hi = sq.astype(BF16)
            lo = (sq - hi.astype(F32)).astype(BF16)
            ms = jnp.concatenate([_dot(hi[:, c:c + 128], bd_ref[...]) + _dot(lo[:, c:c + 128], bd_ref[...])
                                  for c in range(0, 640, 128)], axis=1)
            qk = qk * lax.rsqrt(ms + EPS) * gain_ref[...]
            v = p[:, 640:768]
            if not rope:
                nk_ref[rows, :] = qk[:, 512:640]
                nv_ref[rows, :] = v
            else:
                cos = _tile_lanes(cos_ref[rows, :], 5)
                sin = _tile_lanes(sin_ref[rows, :], 5)
                qk = _rope(qk, cos, sin, lane)
            q_s[rows, :] = (qk[:, :512] * (A_HEAD_DIM ** -0.5)).astype(BF16)
            k_s[rows, :] = _dot(qk[:, 512:640].astype(BF16), rep_ref[...]).astype(BF16)
            v_s[rows, :] = _dot(v.astype(BF16), rep_ref[...]).astype(BF16)
        if n_cache:
            crow = pl.ds(seq, n_cache)
            k_s[crow, :] = _dot(ck_ref[...].astype(BF16), rep_ref[...]).astype(BF16)
            v_s[crow, :] = _dot(cv_ref[...].astype(BF16), rep_ref[...]).astype(BF16)

    head_of_lane = lax.broadcasted_iota(jnp.int32, (1, 256), 1) // A_HEAD_DIM
    qb = q_s[pl.ds(pl.multiple_of(qi * tq, tq), tq), :]
    for g in range(A_KV_HEADS):
        cols = slice(g * 256, (g + 1) * 256)
        qg = qb[:, cols]
        kg = k_s[:, cols]
        vg = v_s[:, cols]
        acc = jnp.zeros((tq, 256), F32)
        for r in range(A_HEADS // A_KV_HEADS):
            sel = head_of_lane == r
            qm = jnp.where(sel, qg, jnp.zeros_like(qg))
            s = _dot_nt(qm, kg)
            m = jnp.max(s, axis=-1, keepdims=True)
            p = jnp.exp(s - m)
            l = jnp.sum(p, axis=-1, keepdims=True)
            o = _dot(p.astype(BF16), vg)
            acc = acc + jnp.where(sel, o * (1.0 / l), 0.0)
        y_ref[:, cols] = acc.astype(BF16)


def _carry_aliases(in_specs, args, carry, first_out):
    if carry is None:
        return {}
    aliases = {}
    for k, arr in enumerate(carry):
        aliases[len(args)] = first_out + k
        in_specs.append(pl.BlockSpec(memory_space=pl.ANY))
        args.append(arr)
    return aliases


def _attn_a(x, mod, w_a, gain_row, bd, rep, *, n_seq, seq, tq, rope=None, cache=None, layer=0, carry=None):
    n_q = seq // tq
    n_cache = 0 if cache is None else cache[0].shape[2]
    proj_rows = min(seq, 512)
    kern = functools.partial(_attn_a_kernel, seq=seq, tq=tq, n_cache=n_cache, rope=rope is not None,
                             proj_rows=proj_rows)
    const = lambda s, q: (0, 0)
    in_specs = [
        pl.BlockSpec((seq, D_MODEL), lambda s, q: (s, 0)),
        pl.BlockSpec((None, 1, 6 * D_MODEL), lambda s, q: (s if mod.shape[0] > 1 else 0, 0, 0)),
        pl.BlockSpec(w_a.shape, const),
        pl.BlockSpec(gain_row.shape, const),
        pl.BlockSpec(bd.shape, const),
        pl.BlockSpec(rep.shape, const),
    ]
    args = [x, mod, w_a, gain_row, bd, rep]
    n_tok = n_seq * seq
    y_spec = pl.BlockSpec((tq, BRANCH_WIDTH), lambda s, q: (s * n_q + q, 0))
    y_shape = jax.ShapeDtypeStruct((n_tok, BRANCH_WIDTH), BF16)
    if rope is not None:
        cos, sin = rope
        in_specs += [pl.BlockSpec(cos.shape, const), pl.BlockSpec(sin.shape, const),
                     pl.BlockSpec((None, None, n_cache, 128), lambda s, q: (s, layer, 0, 0)),
                     pl.BlockSpec((None, None, n_cache, 128), lambda s, q: (s, layer, 0, 0))]
        args += [cos, sin, cache[0], cache[1]]
        out_shape, out_specs, aliases = y_shape, y_spec, {}
    else:
        kv_shape = jax.ShapeDtypeStruct((n_seq, DEPTH, seq, 128), F32)
        kv_spec = pl.BlockSpec((None, None, seq, 128), lambda s, q: (s, layer, 0, 0))
        out_shape, out_specs = (y_shape, kv_shape, kv_shape), (y_spec, kv_spec, kv_spec)
        aliases = _carry_aliases(in_specs, args, carry, first_out=1)
    return pl.pallas_call(
        kern, out_shape=out_shape, grid=(n_seq, n_q), in_specs=in_specs, out_specs=out_specs,
        input_output_aliases=aliases,
        scratch_shapes=[pltpu.VMEM((seq, 512), BF16), pltpu.VMEM((seq + n_cache, 512), BF16),
                        pltpu.VMEM((seq + n_cache, 512), BF16)],
        compiler_params=_cparams("arbitrary", "arbitrary"),
        name="branch_a_lat" if rope is not None else "branch_a_ctx",
    )(*args)


def _attn_d_kernel(*refs, seq, tq, n_cache, rope, proj_rows, lam_init):
    if rope:
        (x_ref, mod_ref, w_ref, lam_ref, gn_ref, cos_ref, sin_ref, ck_ref, cv_ref,
         y_ref, q_s, k_s, v_s) = refs
    else:
        (x_ref, mod_ref, w_ref, lam_ref, gn_ref) = refs[:5]
        (y_ref, nk_ref, nv_ref, q_s, k_s, v_s) = refs[-6:]
    qi = pl.program_id(1)

    @pl.when(qi == 0)
    def _project():
        lane = lax.broadcasted_iota(jnp.int32, (1, 512), 1)
        for r0 in range(0, seq, proj_rows):
            rows = pl.ds(r0, proj_rows)
            h = _modulated(x_ref[rows, :], mod_ref, 0).astype(BF16)
            p = _dot(h, w_ref[...])
            dq, dk, dv = p[:, :512], p[:, 512:1024], p[:, 1024:1536]
            if not rope:
                nk_ref[rows, :] = dk
                nv_ref[rows, :] = dv
            else:
                cos = _tile_lanes(cos_ref[rows, :], 4)
                sin = _tile_lanes(sin_ref[rows, :], 4)
                dq = _rope(dq, cos, sin, lane)
                dk = _rope(dk, cos, sin, lane)
            q_s[rows, :] = (dq * (D_HALF_DIM ** -0.5)).astype(BF16)
            k_s[rows, :] = dk.astype(BF16)
            v_s[rows, :] = dv.astype(BF16)
        if n_cache:
            crow = pl.ds(seq, n_cache)
            k_s[crow, :] = ck_ref[...].astype(BF16)
            v_s[crow, :] = cv_ref[...].astype(BF16)

    lv = lam_ref[...]
    lam = (jnp.exp(jnp.sum(lv[0:1] * lv[1:2], axis=-1, keepdims=True))
           - jnp.exp(jnp.sum(lv[2:3] * lv[3:4], axis=-1, keepdims=True)) + lam_init)
    half_of_lane = lax.broadcasted_iota(jnp.int32, (1, 128), 1) // D_HALF_DIM
    qb = q_s[pl.ds(pl.multiple_of(qi * tq, tq), tq), :]
    for hd in range(D_HEADS):
        cols = slice(hd * 128, (hd + 1) * 128)
        qh = qb[:, cols]
        kh = k_s[:, cols]
        vh = v_s[:, cols]
        probs = []
        for j in range(2):
            qm = jnp.where(half_of_lane == j, qh, jnp.zeros_like(qh))
            s = _dot_nt(qm, kh)
            m = jnp.max(s, axis=-1, keepdims=True)
            p = jnp.exp(s - m)
            l = jnp.sum(p, axis=-1, keepdims=True)
            probs.append(p * (1.0 / l))
        a = (probs[0] - lam * probs[1]).astype(BF16)
        o = _dot(a, vh)
        ms = jnp.mean(o * o, axis=-1, keepdims=True)
        o = o * lax.rsqrt(ms + EPS) * gn_ref[:, cols] * (1.0 - lam_init)
        y_ref[:, cols] = o.astype(BF16)


def _attn_d(x, mod, w_d, lam_params, gn_row, *, n_seq, seq, tq, lam_init, rope=None, cache=None, layer=0,
            carry=None):
    n_q = seq // tq
    n_cache = 0 if cache is None else cache[0].shape[2]
    proj_rows = min(seq, 512)
    kern = functools.partial(_attn_d_kernel, seq=seq, tq=tq, n_cache=n_cache, rope=rope is not None,
                             proj_rows=proj_rows, lam_init=lam_init)
    const = lambda s, q: (0, 0)
    in_specs = [
        pl.BlockSpec((seq, D_MODEL), lambda s, q: (s, 0)),
        pl.BlockSpec((None, 1, 6 * D_MODEL), lambda s, q: (s if mod.shape[0] > 1 else 0, 0, 0)),
        pl.BlockSpec(w_d.shape, const),
        pl.BlockSpec(lam_params.shape, const),
        pl.BlockSpec(gn_row.shape, const),
    ]
    args = [x, mod, w_d, lam_params, gn_row]
    n_tok = n_seq * seq
    y_spec = pl.BlockSpec((tq, BRANCH_WIDTH), lambda s, q: (s * n_q + q, 0))
    y_shape = jax.ShapeDtypeStruct((n_tok, BRANCH_WIDTH), BF16)
    if rope is not None:
        cos, sin = rope
        in_specs += [pl.BlockSpec(cos.shape, const), pl.BlockSpec(sin.shape, const),
                     pl.BlockSpec((None, None, n_cache, 512), lambda s, q: (s, layer, 0, 0)),
                     pl.BlockSpec((None, None, n_cache, 512), lambda s, q: (s, layer, 0, 0))]
        args += [cos, sin, cache[0], cache[1]]
        out_shape, out_specs, aliases = y_shape, y_spec, {}
    else:
        kv_shape = jax.ShapeDtypeStruct((n_seq, DEPTH, seq, 512), F32)
        kv_spec = pl.BlockSpec((None, None, seq, 512), lambda s, q: (s, layer, 0, 0))
        out_shape, out_specs = (y_shape, kv_shape, kv_shape), (y_spec, kv_spec, kv_spec)
        aliases = _carry_aliases(in_specs, args, carry, first_out=1)
    return pl.pallas_call(
        kern, out_shape=out_shape, grid=(n_seq, n_q), in_specs=in_specs, out_specs=out_specs,
        input_output_aliases=aliases,
        scratch_shapes=[pltpu.VMEM((seq, 512), BF16), pltpu.VMEM((seq + n_cache, 512), BF16),
                        pltpu.VMEM((seq + n_cache, 512), BF16)],
        compiler_params=_cparams("arbitrary", "arbitrary"),
        name="branch_d_lat" if rope is not None else "branch_d_ctx",
    )(*args)


def _gmlp_kernel(x_ref, mod_ref, w_ref, ws_ref, bias_ref, y_ref, *, rows):
    h = _modulated(x_ref[...], mod_ref, 0).astype(BF16)
    p = _dot(h, w_ref[...])
    u, v = p[:, :BRANCH_WIDTH], p[:, BRANCH_WIDTH:]
    mu = jnp.mean(v, axis=-1, keepdims=True)
    vc = v - mu
    var = jnp.mean(vc * vc, axis=-1, keepdims=True)
    vn = (vc * lax.rsqrt(var + EPS)).astype(BF16)
    for c in range(rows // CHUNK):
        rs = slice(c * CHUNK, (c + 1) * CHUNK)
        for g in range(B_GROUPS):
            cs = slice(g * 128, (g + 1) * 128)
            s = _dot(ws_ref[g].astype(BF16), vn[rs, cs]) + bias_ref[:, cs]
            y_ref[rs, cs] = (u[rs, cs] * s).astype(BF16)


def _gmlp(x, mod, w_b, ws, bias_full, *, seq, rows):
    n_tok = x.shape[0]
    per_seq = seq // rows
    kern = functools.partial(_gmlp_kernel, rows=rows)
    return pl.pallas_call(
        kern, out_shape=jax.ShapeDtypeStruct((n_tok, BRANCH_WIDTH), BF16),
        grid=(n_tok // rows,),
        in_specs=[
            pl.BlockSpec((rows, D_MODEL), lambda i: (i, 0)),
            pl.BlockSpec((None, 1, 6 * D_MODEL), lambda i: (i // per_seq if mod.shape[0] > 1 else 0, 0, 0)),
            pl.BlockSpec(w_b.shape, lambda i: (0, 0)),
            pl.BlockSpec(ws.shape, lambda i: (0, 0, 0)),
            pl.BlockSpec(bias_full.shape, lambda i: (0, 0)),
        ],
        out_specs=pl.BlockSpec((rows, BRANCH_WIDTH), lambda i: (i, 0)),
        compiler_params=_cparams("arbitrary"),
        name="branch_b",
    )(x, mod, w_b, ws, bias_full)


def _mlstm_t_kernel(*refs, seq, has_init, proj_rows):
    if has_init:
        (x_ref, mod_ref, wn_ref, wt_ref, gbr_ref, gbt_ref, gnt_ref, tril_ref, triu_ref, c0_ref, n0_ref, m0_ref,
         y_ref, k_s, qt_s, vt_s, ot_s, g_s, gt_s, hf_s, hb_s, st_s, m_s) = refs
    else:
        (x_ref, mod_ref, wn_ref, wt_ref, gbr_ref, gbt_ref, gnt_ref, tril_ref, triu_ref) = refs[:9]
        (y_ref, cout_ref, nout_ref, mout_ref,
         k_s, qt_s, vt_s, ot_s, g_s, gt_s, hf_s, hb_s, st_s, m_s) = refs[-14:]
    n_chunk = seq // CHUNK
    for r0 in range(0, seq, proj_rows):
        span = pl.ds(r0, proj_rows)
        h = _modulated(x_ref[span, :], mod_ref, 0).astype(BF16)
        p = _dot(h, wn_ref[...])
        k_s[span, :] = (p[:, 0:512] * (C_HEAD_DIM ** -0.5)).astype(BF16)
        g_s[span, :] = p[:, 512:640] + gbr_ref[...]
        pt = _dot_nt(wt_ref[...], h)
        qt_s[:, span] = pt[0:512].astype(BF16)
        vt_s[:, span] = pt[512:1024].astype(BF16)
        ot_s[:, span] = jax.nn.sigmoid(pt[1024:1536]).astype(BF16)
        gt_s[:, span] = pt[1536:1664] + _tile_lanes(gbt_ref[...], proj_rows // 128)

    first_row = lax.broadcasted_iota(jnp.int32, (CHUNK, CHUNK), 0) == 0
    for sidx in range(2 * C_HEADS):
        if has_init:
            st_s[sidx, 0:128, :] = c0_ref[sidx].T
            st_s[sidx, 128:256, :] = jnp.where(first_row, jnp.broadcast_to(n0_ref[sidx:sidx + 1, :], (CHUNK, 128)), 0.0)
        else:
            st_s[sidx] = jnp.zeros((2 * CHUNK, 128), F32)
    m_s[...] = m0_ref[...] if has_init else jnp.zeros_like(m_s)

    tril = tril_ref[...]
    triu = triu_ref[...]
    row_i = lax.broadcasted_iota(jnp.int32, (CHUNK, CHUNK), 0)
    col_i = lax.broadcasted_iota(jnp.int32, (CHUNK, CHUNK), 1)
    visible = (row_i <= col_i, row_i >= col_i)
    ones_blk = jnp.where(first_row, 1.0, 0.0).astype(BF16)

    def chunk_step(c, carry):
        prep = []
        for direction in range(2):
            cc = c if direction == 0 else n_chunk - 1 - c
            span = pl.ds(pl.multiple_of(cc * CHUNK, CHUNK), CHUNK)
            gates = g_s[span, :]
            gates_t = gt_s[:, span]
            logf = jnp.minimum(gates, 0.0) - jnp.log1p(jnp.exp(-jnp.abs(gates)))
            logf_t = jnp.minimum(gates_t, 0.0) - jnp.log1p(jnp.exp(-jnp.abs(gates_t)))
            tri_col = tril if direction == 0 else triu
            tri_row = triu if direction == 0 else tril
            b_col_all = jnp.dot(tri_col, logf, precision=HIGHEST, preferred_element_type=F32)
            b_row_all = jnp.dot(logf_t, tri_row, precision=HIGHEST, preferred_element_type=F32)
            r_col_all = gates - pltpu.roll(b_col_all, 128 - 4, 1)
            base = direction * 8
            prep.append((span, r_col_all, gates_t[base:base + 4, :], b_row_all[base + 4:base + 8, :]))
        for hd in range(C_HEADS):
            for direction in range(2):
                span, r_col_all, i_rows, b_rows = prep[direction]
                base = direction * 8
                last = CHUNK - 1 if direction == 0 else 0
                h_out = hf_s if direction == 0 else hb_s
                sidx = direction * 4 + hd
                blk = slice(hd * 128, (hd + 1) * 128)
                r_col = r_col_all[:, base + hd:base + hd + 1]
                b_row = b_rows[hd:hd + 1, :]
                i_row = i_rows[hd:hd + 1, :]
                m_prev = m_s[sidx:sidx + 1, 0:1]
                r_wide = jnp.broadcast_to(r_col, (CHUNK, CHUNK))
                peak = jnp.max(jnp.where(visible[direction], r_wide, NEG_INF), axis=0, keepdims=True)
                m_row = b_row + jnp.maximum(peak, m_prev)
                kh = k_s[span, blk]
                q_t = qt_s[blk, span]
                arg = jnp.where(visible[direction], r_wide + (b_row - m_row), NEG_INF)
                w_t = (_dot(kh, q_t) * jnp.exp(arg)).astype(BF16)
                inter = jnp.exp(b_row + m_prev - m_row)
                v_one = jnp.concatenate([vt_s[blk, span], ones_blk], axis=0)
                state = st_s[sidx]
                both = _dot(v_one, w_t) + inter * _dot(state.astype(BF16), q_t)
                den = both[128:129, :]
                h_out[blk, span] = both[0:128, :] / jnp.maximum(jnp.abs(den), jnp.exp(-m_row))
                b_last = b_row[:, last:last + 1]
                g_row = b_last - b_row + i_row
                m_new = jnp.maximum(b_last + m_prev, jnp.max(g_row, axis=-1, keepdims=True))
                decay = jnp.exp(b_last + m_prev - m_new)
                scaled = v_one * jnp.exp(g_row - m_new).astype(BF16)
                st_s[sidx] = decay * state + _dot(scaled, kh)
                m_s[sidx:sidx + 1, :] = jnp.broadcast_to(m_new, (1, 128))
        return carry

    lax.fori_loop(0, n_chunk, chunk_step, 0)

    for r0 in range(0, seq, proj_rows):
        span = pl.ds(r0, proj_rows)
        for hd in range(C_HEADS):
            blk = slice(hd * 128, (hd + 1) * 128)
            hh = hf_s[blk, span] + hb_s[blk, span]
            mu = jnp.mean(hh, axis=0, keepdims=True)
            hc = hh - mu
            var = jnp.mean(hc * hc, axis=0, keepdims=True)
            gn = _tile_lanes(gnt_ref[blk, :], proj_rows // 128)
            y_t = hc * lax.rsqrt(var + EPS) * gn * ot_s[blk, span].astype(F32)
            y_ref[span, blk] = y_t.T.astype(BF16)
    if not has_init:
        for sidx in range(2 * C_HEADS):
            cout_ref[sidx] = st_s[sidx, 0:128, :].T
            nout_ref[sidx:sidx + 1, :] = st_s[sidx, 128:129, :]
        mout_ref[...] = m_s[...]


def _mlstm_t(x, mod, w_nat, w_t, gate_bias_row, gate_bias_t, gn_t, tril, triu, *, n_seq, seq, init=None, layer=0,
             carry=None):
    proj_rows = min(seq, 512)
    kern = functools.partial(_mlstm_t_kernel, seq=seq, has_init=init is not None, proj_rows=proj_rows)
    const = lambda s: (0, 0)
    in_specs = [
        pl.BlockSpec((seq, D_MODEL), lambda s: (s, 0)),
        pl.BlockSpec((None, 1, 6 * D_MODEL), lambda s: (s if mod.shape[0] > 1 else 0, 0, 0)),
    ] + [pl.BlockSpec(a.shape, const) for a in (w_nat, w_t, gate_bias_row, gate_bias_t, gn_t, tril, triu)]
    args = [x, mod, w_nat, w_t, gate_bias_row, gate_bias_t, gn_t, tril, triu]
    n_tok = n_seq * seq
    y_shape = jax.ShapeDtypeStruct((n_tok, BRANCH_WIDTH), BF16)
    y_spec = pl.BlockSpec((seq, BRANCH_WIDTH), lambda s: (s, 0))
    if init is not None:
        c0, n0, m0 = init
        in_specs += [pl.BlockSpec((None, None, 8, 128, 128), lambda s: (s, layer, 0, 0, 0)),
                     pl.BlockSpec((None, None, 8, 128), lambda s: (s, layer, 0, 0)),
                     pl.BlockSpec((None, None, 8, 128), lambda s: (s, layer, 0, 0))]
        args += [c0, n0, m0]
        out_shape, out_specs, aliases = y_shape, y_spec, {}
    else:
        out_shape = (y_shape, jax.ShapeDtypeStruct((n_seq, DEPTH, 8, 128, 128), F32),
                     jax.ShapeDtypeStruct((n_seq, DEPTH, 8, 128), F32),
                     jax.ShapeDtypeStruct((n_seq, DEPTH, 8, 128), F32))
        out_specs = (y_spec, pl.BlockSpec((None, None, 8, 128, 128), lambda s: (s, layer, 0, 0, 0)),
                     pl.BlockSpec((None, None, 8, 128), lambda s: (s, layer, 0, 0)),
                     pl.BlockSpec((None, None, 8, 128), lambda s: (s, layer, 0, 0)))
        aliases = _carry_aliases(in_specs, args, carry, first_out=1)
    return pl.pallas_call(
        kern, out_shape=out_shape, grid=(n_seq,), in_specs=in_specs, out_specs=out_specs,
        input_output_aliases=aliases,
        scratch_shapes=[pltpu.VMEM((seq, 512), BF16),
                        pltpu.VMEM((512, seq), BF16), pltpu.VMEM((512, seq), BF16),
                        pltpu.VMEM((512, seq), BF16),
                        pltpu.VMEM((seq, 128), F32), pltpu.VMEM((128, seq), F32),
                        pltpu.VMEM((512, seq), F32), pltpu.VMEM((512, seq), F32),
                        pltpu.VMEM((8, 256, 128), F32), pltpu.VMEM((8, 128), F32)],
        compiler_params=_cparams("arbitrary"),
        name="branch_c_lat" if init is not None else "branch_c_ctx",
    )(*args)


def _merge_kernel(x_ref, mod_ref, ya_ref, yb_ref, yc_ref, yd_ref, wg_ref, wbr_ref, wout_ref, lng_ref, lnb_ref,
                  o_ref):
    x = x_ref[...]
    h = _modulated(x, mod_ref, 0).astype(BF16)
    mix = None
    for n, y_ref in enumerate((ya_ref, yb_ref, yc_ref, yd_ref)):
        gate = jax.nn.sigmoid(_dot(h, wg_ref[:, n * D_MODEL:(n + 1) * D_MODEL]))
        term = gate * _dot(y_ref[...], wbr_ref[n])
        mix = term if mix is None else mix + term
    out = _dot(mix.astype(BF16), wout_ref[...])
    g1 = mod_ref[:, 2 * D_MODEL:3 * D_MODEL]
    o_ref[...] = _layer_norm_rows(ALPHA * x + g1 * out, lng_ref[...], lnb_ref[...])


def _merge(x, mod, ys, w_g, w_br, w_out, ln_g, ln_b, *, seq, rows):
    n_tok = x.shape[0]
    per_seq = seq // rows
    tok = lambda i: (i, 0)
    c2 = lambda i: (0, 0)
    return pl.pallas_call(
        _merge_kernel, out_shape=jax.ShapeDtypeStruct((n_tok, D_MODEL), F32),
        grid=(n_tok // rows,),
        in_specs=[
            pl.BlockSpec((rows, D_MODEL), tok),
            pl.BlockSpec((None, 1, 6 * D_MODEL), lambda i: (i // per_seq if mod.shape[0] > 1 else 0, 0, 0)),
            pl.BlockSpec((rows, BRANCH_WIDTH), tok), pl.BlockSpec((rows, BRANCH_WIDTH), tok),
            pl.BlockSpec((rows, BRANCH_WIDTH), tok), pl.BlockSpec((rows, BRANCH_WIDTH), tok),
            pl.BlockSpec(w_g.shape, c2), pl.BlockSpec(w_br.shape, lambda i: (0, 0, 0)),
            pl.BlockSpec(w_out.shape, c2), pl.BlockSpec(ln_g.shape, c2), pl.BlockSpec(ln_b.shape, c2),
        ],
        out_specs=pl.BlockSpec((rows, D_MODEL), tok),
        compiler_params=_cparams("arbitrary"),
        name="merge",
    )(x, mod, *ys, w_g, w_br, w_out, ln_g, ln_b)


_TAKEN = -(2.0 ** 127)


def _top16(s):
    cur = s
    vals = []
    for r in range(PEER_TOPK):
        mx = jnp.max(cur, axis=0, keepdims=True)
        cur = jnp.where(cur == mx, _TAKEN * (1.0 + r / 32.0), cur)
        vals.append(mx)
    rank = jnp.where(cur <= _TAKEN, cur * (32.0 / _TAKEN) - 31.0, float(PEER_TOPK + 1))
    return jnp.concatenate(vals, axis=0), rank


def _pair_tables():
    pairs = [(k1, k2) for k1 in range(PEER_TOPK) for k2 in range(PEER_TOPK // (k1 + 1))]
    n = 56
    sel_a = np.zeros((n, PEER_TOPK), np.float32)
    sel_b = np.zeros((n, PEER_TOPK), np.float32)
    pad = np.full((n, 1), NEG_INF, np.float32)
    for row, (k1, k2) in enumerate(pairs):
        sel_a[row, k1] = 1.0
        sel_b[row, k2] = 1.0
        pad[row, 0] = 0.0
    return jnp.asarray(sel_a), jnp.asarray(sel_b), jnp.asarray(pad), jnp.asarray(sel_a.T, BF16)


def _route_kernel(x_ref, mod_ref, wq_ref, keys_ref, sela_ref, selb_ref, pad_ref, ind_ref,
                  cnt_ref, e1_ref, r2_ref, e2_ref, h2_s, *, heads):
    @pl.when(pl.program_id(1) == 0)
    def _modulate():
        h2_s[...] = _modulated(x_ref[...], mod_ref, 1).astype(BF16)

    pick = lambda sel_ref, v: jnp.dot(sel_ref[...], v, precision=HIGHEST, preferred_element_type=F32)
    q_all = _dot(h2_s[...], wq_ref[...])
    for hd in range(heads):
        q = q_all[:, hd * PEER_QDIM:(hd + 1) * PEER_QDIM]
        s1 = _dot_nt(keys_ref[hd, 0].astype(BF16), q[:, :128].astype(BF16))
        s2 = _dot_nt(keys_ref[hd, 1].astype(BF16), q[:, 128:].astype(BF16))
        a, rank1 = _top16(s1)
        b, rank2 = _top16(s2)
        ea = jnp.exp(a - a[0:1])
        eb = jnp.exp(b - b[0:1])
        cand = pick(sela_ref, a) + pick(selb_ref, b) + pad_ref[...]
        gate = pick(sela_ref, ea) * pick(selb_ref, eb)
        cur = cand
        thr = None
        for _ in range(PEER_TOPK):
            thr = jnp.max(cur, axis=0, keepdims=True)
            cur = jnp.where(cur == thr, NEG_INF, cur)
        chosen = cand >= thr
        z = jnp.sum(jnp.where(chosen, gate, 0.0), axis=0, keepdims=True)
        cnt_sorted = _dot(ind_ref[...], jnp.where(chosen, 1.0, 0.0).astype(BF16))
        rank1_b = rank1.astype(BF16)
        counts_b = cnt_sorted.astype(BF16)
        cnt = jnp.zeros(s1.shape, BF16)
        for r in range(PEER_TOPK):
            cnt = jnp.where(rank1_b == float(r + 1), counts_b[r:r + 1], cnt)
        cnt_ref[hd] = cnt.astype(F32)
        e1_ref[hd] = jnp.where(rank1 <= float(PEER_TOPK), jnp.exp(s1 - a[0:1]) * (0.5 / z), 0.0)
        packed = (PEER_NKEYS // 16, 16, s2.shape[1])
        r2_ref[hd] = rank2.astype(BF16).reshape(packed)
        e2_ref[hd] = jnp.where(rank2 <= float(PEER_TOPK), jnp.exp(s2 - b[0:1]), 0.0).astype(BF16).reshape(packed)


def _route(x1, mod, wq, keys, *, seq, cols, heads):
    n_tok = x1.shape[0]
    per_seq = seq // cols
    tables = _pair_tables()
    row_shape = jax.ShapeDtypeStruct((PEER_HEADS, PEER_NKEYS, n_tok), F32)
    col_shape = jax.ShapeDtypeStruct((PEER_HEADS, PEER_NKEYS // 16, 16, n_tok), BF16)
    col_spec = pl.BlockSpec((heads, PEER_NKEYS // 16, 16, cols), lambda i, h: (h, 0, 0, i))
    spec = pl.BlockSpec((heads, PEER_NKEYS, cols), lambda i, h: (h, 0, i))
    return pl.pallas_call(
        functools.partial(_route_kernel, heads=heads), out_shape=(row_shape, row_shape, col_shape, col_shape),
        grid=(n_tok // cols, PEER_HEADS // heads),
        in_specs=[
            pl.BlockSpec((cols, D_MODEL), lambda i, h: (i, 0)),
            pl.BlockSpec((None, 1, 6 * D_MODEL), lambda i, h: (i // per_seq if mod.shape[0] > 1 else 0, 0, 0)),
            pl.BlockSpec((D_MODEL, heads * PEER_QDIM), lambda i, h: (0, h)),
            pl.BlockSpec((heads, 2, PEER_NKEYS, PEER_QDIM // 2), lambda i, h: (h, 0, 0, 0)),
        ] + [pl.BlockSpec(t.shape, lambda i, h: (0, 0)) for t in tables],
        out_specs=(spec, spec, col_spec, col_spec),
        scratch_shapes=[pltpu.VMEM((cols, D_MODEL), BF16)],
        compiler_params=_cparams("arbitrary", "arbitrary"),
        name="peer_route",
    )(x1, mod, wq, keys, *tables)


def _peer_kernel(x_ref, mod_ref, u_ref, v_ref, cnt_ref, e1_ref, r2_ref, e2_ref, lng_ref, lnb_ref,
                 o_ref, h2t_s, acc_s, *, key_rows):
    e = pl.program_id(1)

    @pl.when(e == 0)
    def _init():
        h2 = _modulated(x_ref[...], mod_ref, 1)
        h2t_s[...] = h2.T.astype(BF16)
        acc_s[...] = jnp.zeros_like(acc_s)

    n_tok = h2t_s.shape[1]
    first_key = pl.multiple_of(e * key_rows, key_rows)
    zero = jnp.zeros((PEER_NKEYS // 16, 16, n_tok), BF16)
    act = _dot(u_ref[...], h2t_s[...]).astype(BF16)
    act = act * (1.0 + lax.erf(act * (2.0 ** -0.5)))
    pieces = []
    for r in range(key_rows):
        g = None
        for hd in range(PEER_HEADS):
            cnt_blk = cnt_ref[hd, pl.ds(first_key, key_rows), :]
            e1_blk = e1_ref[hd, pl.ds(first_key, key_rows), :]
            cnt_rows = jnp.broadcast_to(cnt_blk[r:r + 1, :], (16, n_tok)).astype(BF16)
            e1_rows = jnp.broadcast_to(e1_blk[r:r + 1, :], (16, n_tok)).astype(BF16)
            term = jnp.where(r2_ref[hd] <= cnt_rows[None], e2_ref[hd], zero) * e1_rows[None]
            g = term if g is None else g + term
        pieces.append(g.reshape(PEER_NKEYS, n_tok) * act[r * PEER_NKEYS:(r + 1) * PEER_NKEYS, :])
    acc_s[...] += lax.dot_general(v_ref[...], jnp.concatenate(pieces, axis=0), (((0,), (0,)), ((), ())),
                                  preferred_element_type=F32)

    @pl.when(e == pl.num_programs(1) - 1)
    def _finish():
        x = x_ref[...]
        g2 = mod_ref[:, 5 * D_MODEL:6 * D_MODEL]
        o_ref[...] = _layer_norm_rows(ALPHA * x + g2 * acc_s[...].T, lng_ref[...], lnb_ref[...])


def _peer(x1, mod, u, v, route, ln_g, ln_b, *, seq, cols, key_rows, layer):
    n_tok = x1.shape[0]
    per_seq = seq // cols
    n_exp = key_rows * PEER_NKEYS
    n_tiles = PEER_EXPERTS // n_exp
    assert key_rows % 8 == 0
    kern = functools.partial(_peer_kernel, key_rows=key_rows)
    cnt, e1, r2, e2 = route
    rspec = pl.BlockSpec((PEER_HEADS, PEER_NKEYS, cols), lambda i, e: (0, 0, i))
    cspec = pl.BlockSpec((PEER_HEADS, PEER_NKEYS // 16, 16, cols), lambda i, e: (0, 0, 0, i))
    return pl.pallas_call(
        kern, out_shape=jax.ShapeDtypeStruct((n_tok, D_MODEL), F32),
        grid=(n_tok // cols, n_tiles),
        in_specs=[
            pl.BlockSpec((cols, D_MODEL), lambda i, e: (i, 0)),
            pl.BlockSpec((None, 1, 6 * D_MODEL), lambda i, e: (i // per_seq if mod.shape[0] > 1 else 0, 0, 0)),
            pl.BlockSpec((None, n_exp, D_MODEL), lambda i, e: (layer, e, 0)),
            pl.BlockSpec((None, n_exp, D_MODEL), lambda i, e: (layer, e, 0)),
            rspec, rspec, cspec, cspec,
            pl.BlockSpec(ln_g.shape, lambda i, e: (0, 0)), pl.BlockSpec(ln_b.shape, lambda i, e: (0, 0)),
        ],
        out_specs=pl.BlockSpec((cols, D_MODEL), lambda i, e: (i, 0)),
        scratch_shapes=[pltpu.VMEM((D_MODEL, cols), BF16), pltpu.VMEM((D_MODEL, cols), F32)],
        compiler_params=_cparams("arbitrary", "arbitrary"),
        name="peer_experts",
    )(x1, mod, u, v, cnt, e1, r2, e2, ln_g, ln_b)


def _rope_tables(seq):
    t = np.arange(seq)
    pos = np.stack([t // GRID_W, t % GRID_W], axis=1).astype(np.float64)
    inv = ROPE_BASE ** (-np.arange(16, dtype=np.float64) / 16)
    lane = np.arange(64)
    ang = pos[:, lane // 32] * inv[lane % 16][None, :]
    sign = np.where((lane % 32) < 16, -1.0, 1.0)[None, :]
    cos = np.tile(np.cos(ang), (1, 2)).astype(np.float32)
    sin = np.tile(np.sin(ang) * sign, (1, 2)).astype(np.float32)
    return jnp.asarray(cos), jnp.asarray(sin)


def _static_tables():
    lane = np.arange(128)
    bd = (lane[:, None] // 64 == lane[None, :] // 64).astype(np.float32) / 64.0
    src = np.arange(128)
    dst = np.arange(512)
    rep = ((src[:, None] // 64 == dst[None, :] // 256) & (src[:, None] % 64 == dst[None, :] % 64))
    idx = np.arange(CHUNK)
    tril = (idx[None, :] <= idx[:, None]).astype(np.float32)
    triu = (idx[None, :] >= idx[:, None]).astype(np.float32)
    return (jnp.asarray(bd, BF16), jnp.asarray(rep.astype(np.float32), BF16), jnp.asarray(tril), jnp.asarray(triu))


def _layer_params(l, w_in, attn_qk_gain, gmlp_ws, gmlp_b, mlstm_gate_bias, mlstm_gn, diff_lambda, diff_gn,
                  w_branch, w_out, ln_g, ln_b, peer_wq, peer_keys, peer_u, peer_v):
    w = w_in[l]
    p = {}
    p["w_a"] = w[:, _OFF_A:_OFF_B].astype(BF16)
    p["w_b"] = w[:, _OFF_B:_OFF_C].astype(BF16)
    cq, ck = w[:, _OFF_C:_OFF_C + 512], w[:, _OFF_C + 512:_OFF_C + 1024]
    cv, co = w[:, _OFF_C + 1024:_OFF_C + 1536], w[:, _OFF_C + 1536:_OFF_C + 2048]
    cg = jnp.concatenate([w[:, _OFF_CG:_OFF_D], jnp.zeros((D_MODEL, 112), F32)], axis=1)
    p["w_c_nat"] = jnp.concatenate([ck, cg], axis=1).astype(BF16)
    p["w_c_t"] = jnp.concatenate([cq, cv, co, cg], axis=1).T.astype(BF16)
    p["w_d"] = w[:, _OFF_D:_OFF_G].astype(BF16)
    p["w_g"] = w[:, _OFF_G:].astype(BF16)
    gain = attn_qk_gain[l]
    p["gain_row"] = jnp.concatenate([jnp.tile(gain[0], A_HEADS), jnp.tile(gain[1], A_KV_HEADS)])[None, :]
    p["ws"] = gmlp_ws[l]
    p["bias_full"] = jnp.repeat(gmlp_b[l].T, 128, axis=1)
    p["gate_bias_row"] = jnp.concatenate([mlstm_gate_bias[l].reshape(16), jnp.zeros((112,), F32)])[None, :]
    p["gate_bias_t"] = jnp.broadcast_to(p["gate_bias_row"].reshape(128, 1), (128, 128))
    p["mlstm_gn_t"] = jnp.broadcast_to(mlstm_gn[l].reshape(BRANCH_WIDTH, 1), (BRANCH_WIDTH, 128))
    p["lam"] = diff_lambda[l]
    p["diff_gn_row"] = diff_gn[l].reshape(1, BRANCH_WIDTH)
    p["w_br"] = w_branch[l].astype(BF16)
    p["w_out"] = w_out[l].astype(BF16)
    p["ln_g0"], p["ln_b0"] = ln_g[l, 0][None, :], ln_b[l, 0][None, :]
    p["ln_g1"], p["ln_b1"] = ln_g[l, 1][None, :], ln_b[l, 1][None, :]
    p["wq"] = peer_wq[l].astype(BF16)
    p["keys"] = peer_keys[l]
    p["u"], p["v"] = peer_u, peer_v
    return p


def _trunk_layer(x, mod, p, tabs, *, l, n_seq, seq, cfg, ctx_cache=None, prev_state=None):
    bd, rep, tril, triu = tabs
    lam_init = 0.8 - 0.6 * math.exp(-0.3 * l)
    state = None
    if ctx_cache is None:
        prev = (None, None, None) if prev_state is None else (prev_state[0:2], prev_state[2:4], prev_state[4:7])
        ya, nk, nv = _attn_a(x, mod, p["w_a"], p["gain_row"], bd, rep, n_seq=n_seq, seq=seq, tq=cfg["tq"],
                             layer=l, carry=prev[0])
        yd, ndk, ndv = _attn_d(x, mod, p["w_d"], p["lam"], p["diff_gn_row"], n_seq=n_seq, seq=seq, tq=cfg["tq"],
                               lam_init=lam_init, layer=l, carry=prev[1])
        yc, c_new, n_new, m_new = _mlstm_t(x, mod, p["w_c_nat"], p["w_c_t"], p["gate_bias_row"], p["gate_bias_t"],
                                           p["mlstm_gn_t"], tril, triu, n_seq=n_seq, seq=seq, layer=l, carry=prev[2])
        state = (nk, nv, ndk, ndv, c_new, n_new, m_new)
    else:
        rope, cak, cav, cdk, cdv, c0, n0, m0 = ctx_cache
        ya = _attn_a(x, mod, p["w_a"], p["gain_row"], bd, rep, n_seq=n_seq, seq=seq, tq=cfg["tq"],
                     rope=rope, cache=(cak, cav), layer=l)
        yd = _attn_d(x, mod, p["w_d"], p["lam"], p["diff_gn_row"], n_seq=n_seq, seq=seq, tq=cfg["tq"],
                     lam_init=lam_init, rope=rope, cache=(cdk, cdv), layer=l)
        yc = _mlstm_t(x, mod, p["w_c_nat"], p["w_c_t"], p["gate_bias_row"], p["gate_bias_t"], p["mlstm_gn_t"],
                      tril, triu, n_seq=n_seq, seq=seq, init=(c0, n0, m0), layer=l)
    yb = _gmlp(x, mod, p["w_b"], p["ws"], p["bias_full"], seq=seq, rows=cfg["rows"])
    x1 = _merge(x, mod, (ya, yb, yc, yd), p["w_g"], p["w_br"], p["w_out"], p["ln_g0"], p["ln_b0"],
                seq=seq, rows=cfg["rows"])
    route = _route(x1, mod, p["wq"], p["keys"], seq=seq, cols=cfg["route_cols"], heads=cfg["route_heads"])
    x2 = _peer(x1, mod, p["u"], p["v"], route, p["ln_g1"], p["ln_b1"], seq=seq, cols=cfg["cols"], layer=l,
               key_rows=cfg["key_rows"])
    return x2, state


def kernel(x_prompt, x_sample, cache_a_k, cache_a_v, cache_d_k, cache_d_v, state_c_C, state_c_n, state_c_m,
           c, c_ctx, w_mod, b_mod, w_in, attn_qk_gain, gmlp_ws, gmlp_b, mlstm_gate_bias, mlstm_gn,
           diff_lambda, diff_gn, w_branch, w_out, ln_g, ln_b, peer_wq, peer_keys, peer_u, peer_v):
    batch, seq, _ = x_prompt.shape
    dec_batch, dec_seq, _ = x_sample.shape
    past = cache_a_k.shape[2]
    c_rows = jnp.concatenate([c_ctx[None, :], c, jnp.zeros((8 - 1 - dec_batch, D_MODEL), F32)], axis=0)
    mods = _modulation(c_rows, w_mod, b_mod)
    tabs = _static_tables()
    rope = _rope_tables(dec_seq)
    cak = cache_a_k.reshape(dec_batch, DEPTH, past, 128)
    cav = cache_a_v.reshape(dec_batch, DEPTH, past, 128)
    cdk = cache_d_k.reshape(dec_batch, DEPTH, past, 512)
    cdv = cache_d_v.reshape(dec_batch, DEPTH, past, 512)
    c0 = state_c_C.reshape(dec_batch, DEPTH, 8, 128, 128)
    n0 = state_c_n.reshape(dec_batch, DEPTH, 8, 128)
    m0 = jnp.broadcast_to(state_c_m.reshape(dec_batch, DEPTH, 8, 1), (dec_batch, DEPTH, 8, 128))
    cfg_ctx = dict(tq=seq, rows=512, cols=512, key_rows=16, route_cols=min(1024, batch * seq), route_heads=4)
    cfg_lat = dict(tq=min(256, dec_seq), rows=min(512, dec_seq), cols=512, key_rows=16,
                   route_cols=min(1024, dec_seq), route_heads=4)
    y_p = x_prompt.reshape(batch * seq, D_MODEL)
    y_s = x_sample.reshape(dec_batch * dec_seq, D_MODEL)
    state = None
    u_all, v_all = peer_u.astype(BF16), peer_v.astype(BF16)
    for l in range(DEPTH):
        p = _layer_params(l, w_in, attn_qk_gain, gmlp_ws, gmlp_b, mlstm_gate_bias, mlstm_gn, diff_lambda, diff_gn,
                          w_branch, w_out, ln_g, ln_b, peer_wq, peer_keys, u_all, v_all)
        mod_ctx = mods[l, 0:1].reshape(1, 1, 6 * D_MODEL)
        mod_lat = mods[l, 1:1 + dec_batch].reshape(dec_batch, 1, 6 * D_MODEL)
        y_p, state = _trunk_layer(y_p, mod_ctx, p, tabs, l=l, n_seq=batch, seq=seq, cfg=cfg_ctx, prev_state=state)
        y_s, _ = _trunk_layer(y_s, mod_lat, p, tabs, l=l, n_seq=dec_batch, seq=dec_seq, cfg=cfg_lat,
                              ctx_cache=(rope, cak, cav, cdk, cdv, c0, n0, m0))
    nk = state[0].reshape(batch, DEPTH, seq, A_KV_HEADS, A_HEAD_DIM)
    nv = state[1].reshape(batch, DEPTH, seq, A_KV_HEADS, A_HEAD_DIM)
    ndk = state[2].reshape(batch, DEPTH, seq, D_HEADS, 2, D_HALF_DIM)
    ndv = state[3].reshape(batch, DEPTH, seq, D_HEADS, D_VDIM)
    nc = state[4].reshape(batch, DEPTH, 2, C_HEADS, C_HEAD_DIM, C_HEAD_DIM)
    nn = state[5].reshape(batch, DEPTH, 2, C_HEADS, C_HEAD_DIM)
    nm = state[6][:, :, :, 0].reshape(batch, DEPTH, 2, C_HEADS)
    return (y_p.reshape(batch, seq, D_MODEL), y_s.reshape(dec_batch, dec_seq, D_MODEL), nk, nv, ndk, ndv, nc, nn, nm)
```

```python
import functools
import math

import numpy as np
import jax
import jax.numpy as jnp
from jax import lax
from jax.experimental import pallas as pl
from jax.experimental.pallas import tpu as pltpu

F32 = jnp.float32
BF16 = jnp.bfloat16
HIGHEST = lax.Precision.HIGHEST

D_MODEL = 1024
DEPTH = 4
GRID_W = 64
ROPE_BASE = 10000.0
EPS = 1e-6
BRANCH_WIDTH = D_MODEL // 2
A_HEAD_DIM = 64
A_HEADS = 8
A_KV_HEADS = 2
B_GROUPS = 4
CHUNK = 128
C_HEADS = 4
C_HEAD_DIM = 128
D_HEADS = 4
D_VDIM = 128
D_HALF_DIM = 64
PEER_HEADS = 8
PEER_NKEYS = 128
PEER_EXPERTS = PEER_NKEYS * PEER_NKEYS
PEER_QDIM = 256
PEER_TOPK = 16
ALPHA = (2 * DEPTH) ** 0.25

_OFF_A = 0
_OFF_B = 768
_OFF_C = 1792
_OFF_CG = 3840
_OFF_D = 3856
_OFF_G = 5392

VMEM_LIMIT_BYTES = 56 * 1024 * 1024
NEG_INF = float("-inf")


def _cparams(*sem):
    return pltpu.CompilerParams(dimension_semantics=sem, vmem_limit_bytes=VMEM_LIMIT_BYTES)


def _dot(a, b):
    return jnp.dot(a, b, preferred_element_type=F32)


def _dot_nt(a, b):
    return lax.dot_general(a, b, (((1,), (1,)), ((), ())), preferred_element_type=F32)


def _modulated(x, mod_ref, which):
    base = 3 * D_MODEL * which
    sh = mod_ref[:, base:base + D_MODEL]
    sc = mod_ref[:, base + D_MODEL:base + 2 * D_MODEL]
    return x * (1.0 + sc) + sh


def _layer_norm_rows(z, g, b):
    mu = jnp.mean(z, axis=-1, keepdims=True)
    zc = z - mu
    var = jnp.mean(zc * zc, axis=-1, keepdims=True)
    return zc * lax.rsqrt(var + EPS) * g + b


def _rope(x, cos, sin_signed, lane):
    w = x.shape[1]
    nxt = pltpu.roll(x, w - 16, 1)
    prv = pltpu.roll(x, 16, 1)
    partner = jnp.where((lane % 32) < 16, nxt, prv)
    return x * cos + partner * sin_signed


def _tile_lanes(t, n):
    return t if n == 1 else jnp.concatenate([t] * n, axis=1)


def _mod_kernel(c_ref, w_ref, b_ref, o_ref):
    c = c_ref[...]
    s = c * jax.nn.sigmoid(c)
    o_ref[...] = jnp.dot(s, w_ref[...], precision=HIGHEST, preferred_element_type=F32) + b_ref[...]


def _modulation(c_rows, w_mod, b_mod):
    n_col = 6 * D_MODEL // 1024
    return pl.pallas_call(
        _mod_kernel,
        out_shape=jax.ShapeDtypeStruct((DEPTH, 8, 6 * D_MODEL), F32),
        grid=(DEPTH, n_col),
        in_specs=[
            pl.BlockSpec((8, D_MODEL), lambda l, j: (0, 0)),
            pl.BlockSpec((None, D_MODEL, 1024), lambda l, j: (l, 0, j)),
            pl.BlockSpec((None, 1, 1024), lambda l, j: (l, 0, j)),
        ],
        out_specs=pl.BlockSpec((None, 8, 1024), lambda l, j: (l, 0, j)),
        compiler_params=_cparams("arbitrary", "arbitrary"),
        name="modulation",
    )(c_rows, w_mod, b_mod.reshape(DEPTH, 1, 6 * D_MODEL))


def _attn_a_kernel(*refs, seq, tq, n_cache, rope, proj_rows):
    if rope:
        (x_ref, mod_ref, w_ref, gain_ref, bd_ref, rep_ref, cos_ref, sin_ref, ck_ref, cv_ref,
         y_ref, q_s, k_s, v_s) = refs
    else:
        (x_ref, mod_ref, w_ref, gain_ref, bd_ref, rep_ref) = refs[:6]
        (y_ref, nk_ref, nv_ref, q_s, k_s, v_s) = refs[-6:]
    qi = pl.program_id(1)

    @pl.when(qi == 0)
    def _project():
        lane = lax.broadcasted_iota(jnp.int32, (1, 640), 1)
        for r0 in range(0, seq, proj_rows):
            rows = pl.ds(r0, proj_rows)
            h = _modulated(x_ref[rows, :], mod_ref, 0).astype(BF16)
            p = _dot(h, w_ref[...])
            qk = p[:, :640]
            sq = qk * qk
            hi = sq.astype(BF16)
            lo = (sq - hi.astype(F32)).astype(BF16)
            ms = jnp.concatenate([_dot(hi[:, c:c + 128], bd_ref[...]) + _dot(lo[:, c:c + 128], bd_ref[...])
                                  for c in range(0, 640, 128)], axis=1)
            qk = qk * lax.rsqrt(ms + EPS) * gain_ref[...]
            v = p[:, 640:768]
            if not rope:
                nk_ref[rows, :] = qk[:, 512:640]
                nv_ref[rows, :] = v
            else:
                cos = _tile_lanes(cos_ref[rows, :], 5)
                sin = _tile_lanes(sin_ref[rows, :], 5)
                qk = _rope(qk, cos, sin, lane)
            q_s[rows, :] = (qk[:, :512] * (A_HEAD_DIM ** -0.5)).astype(BF16)
            k_s[rows, :] = _dot(qk[:, 512:640].astype(BF16), rep_ref[...]).astype(BF16)
            v_s[rows, :] = _dot(v.astype(BF16), rep_ref[...]).astype(BF16)
        if n_cache:
            crow = pl.ds(seq, n_cache)
            k_s[crow, :] = _dot(ck_ref[...].astype(BF16), rep_ref[...]).astype(BF16)
            v_s[crow, :] = _dot(cv_ref[...].astype(BF16), rep_ref[...]).astype(BF16)

    head_of_lane = lax.broadcasted_iota(jnp.int32, (1, 256), 1) // A_HEAD_DIM
    qb = q_s[pl.ds(pl.multiple_of(qi * tq, tq), tq), :]
    for g in range(A_KV_HEADS):
        cols = slice(g * 256, (g + 1) * 256)
        qg = qb[:, cols]
        kg = k_s[:, cols]
        vg = v_s[:, cols]
        acc = jnp.zeros((tq, 256), F32)
        for r in range(A_HEADS // A_KV_HEADS):
            sel = head_of_lane == r
            qm = jnp.where(sel, qg, jnp.zeros_like(qg))
            s = _dot_nt(qm, kg)
            m = jnp.max(s, axis=-1, keepdims=True)
            p = jnp.exp(s - m)
            l = jnp.sum(p, axis=-1, keepdims=True)
            o = _dot(p.astype(BF16), vg)
            acc = acc + jnp.where(sel, o * (1.0 / l), 0.0)
        y_ref[:, cols] = acc.astype(BF16)


def _carry_aliases(in_specs, args, carry, first_out):
    if carry is None:
        return {}
    aliases = {}
    for k, arr in enumerate(carry):
        aliases[len(args)] = first_out + k
        in_specs.append(pl.BlockSpec(memory_space=pl.ANY))
        args.append(arr)
    return aliases


def _attn_a(x, mod, w_a, gain_row, bd, rep, *, n_seq, seq, tq, rope=None, cache=None, layer=0, carry=None):
    n_q = seq // tq
    n_cache = 0 if cache is None else cache[0].shape[2]
    proj_rows = min(seq, 512)
    kern = functools.partial(_attn_a_kernel, seq=seq, tq=tq, n_cache=n_cache, rope=rope is not None,
                             proj_rows=proj_rows)
    const = lambda s, q: (0, 0)
    in_specs = [
        pl.BlockSpec((seq, D_MODEL), lambda s, q: (s, 0)),
        pl.BlockSpec((None, 1, 6 * D_MODEL), lambda s, q: (s if mod.shape[0] > 1 else 0, 0, 0)),
        pl.BlockSpec(w_a.shape, const),
        pl.BlockSpec(gain_row.shape, const),
        pl.BlockSpec(bd.shape, const),
        pl.BlockSpec(rep.shape, const),
    ]
    args = [x, mod, w_a, gain_row, bd, rep]
    n_tok = n_seq * seq
    y_spec = pl.BlockSpec((tq, BRANCH_WIDTH), lambda s, q: (s * n_q + q, 0))
    y_shape = jax.ShapeDtypeStruct((n_tok, BRANCH_WIDTH), BF16)
    if rope is not None:
        cos, sin = rope
        in_specs += [pl.BlockSpec(cos.shape, const), pl.BlockSpec(sin.shape, const),
                     pl.BlockSpec((None, None, n_cache, 128), lambda s, q: (s, layer, 0, 0)),
                     pl.BlockSpec((None, None, n_cache, 128), lambda s, q: (s, layer, 0, 0))]
        args += [cos, sin, cache[0], cache[1]]
        out_shape, out_specs, aliases = y_shape, y_spec, {}
    else:
        kv_shape = jax.ShapeDtypeStruct((n_seq, DEPTH, seq, 128), F32)
        kv_spec = pl.BlockSpec((None, None, seq, 128), lambda s, q: (s, layer, 0, 0))
        out_shape, out_specs = (y_shape, kv_shape, kv_shape), (y_spec, kv_spec, kv_spec)
        aliases = _carry_aliases(in_specs, args, carry, first_out=1)
    return pl.pallas_call(
        kern, out_shape=out_shape, grid=(n_seq, n_q), in_specs=in_specs, out_specs=out_specs,
        input_output_aliases=aliases,
        scratch_shapes=[pltpu.VMEM((seq, 512), BF16), pltpu.VMEM((seq + n_cache, 512), BF16),
                        pltpu.VMEM((seq + n_cache, 512), BF16)],
        compiler_params=_cparams("arbitrary", "arbitrary"),
        name="branch_a_lat" if rope is not None else "branch_a_ctx",
    )(*args)


def _attn_d_kernel(*refs, seq, tq, n_cache, rope, proj_rows, lam_init):
    if rope:
        (x_ref, mod_ref, w_ref, lam_ref, gn_ref, cos_ref, sin_ref, ck_ref, cv_ref,
         y_ref, q_s, k_s, v_s) = refs
    else:
        (x_ref, mod_ref, w_ref, lam_ref, gn_ref) = refs[:5]
        (y_ref, nk_ref, nv_ref, q_s, k_s, v_s) = refs[-6:]
    qi = pl.program_id(1)

    @pl.when(qi == 0)
    def _project():
        lane = lax.broadcasted_iota(jnp.int32, (1, 512), 1)
        for r0 in range(0, seq, proj_rows):
            rows = pl.ds(r0, proj_rows)
            h = _modulated(x_ref[rows, :], mod_ref, 0).astype(BF16)
            p = _dot(h, w_ref[...])
            dq, dk, dv = p[:, :512], p[:, 512:1024], p[:, 1024:1536]
            if not rope:
                nk_ref[rows, :] = dk
                nv_ref[rows, :] = dv
            else:
                cos = _tile_lanes(cos_ref[rows, :], 4)
                sin = _tile_lanes(sin_ref[rows, :], 4)
                dq = _rope(dq, cos, sin, lane)
                dk = _rope(dk, cos, sin, lane)
            q_s[rows, :] = (dq * (D_HALF_DIM ** -0.5)).astype(BF16)
            k_s[rows, :] = dk.astype(BF16)
            v_s[rows, :] = dv.astype(BF16)
        if n_cache:
            crow = pl.ds(seq, n_cache)
            k_s[crow, :] = ck_ref[...].astype(BF16)
            v_s[crow, :] = cv_ref[...].astype(BF16)

    lv = lam_ref[...]
    lam = (jnp.exp(jnp.sum(lv[0:1] * lv[1:2], axis=-1, keepdims=True))
           - jnp.exp(jnp.sum(lv[2:3] * lv[3:4], axis=-1, keepdims=True)) + lam_init)
    half_of_lane = lax.broadcasted_iota(jnp.int32, (1, 128), 1) // D_HALF_DIM
    qb = q_s[pl.ds(pl.multiple_of(qi * tq, tq), tq), :]
    for hd in range(D_HEADS):
        cols = slice(hd * 128, (hd + 1) * 128)
        qh = qb[:, cols]
        kh = k_s[:, cols]
        vh = v_s[:, cols]
        probs = []
        for j in range(2):
            qm = jnp.where(half_of_lane == j, qh, jnp.zeros_like(qh))
            s = _dot_nt(qm, kh)
            m = jnp.max(s, axis=-1, keepdims=True)
            p = jnp.exp(s - m)
            l = jnp.sum(p, axis=-1, keepdims=True)
            probs.append(p * (1.0 / l))
        a = (probs[0] - lam * probs[1]).astype(BF16)
        o = _dot(a, vh)
        ms = jnp.mean(o * o, axis=-1, keepdims=True)
        o = o * lax.rsqrt(ms + EPS) * gn_ref[:, cols] * (1.0 - lam_init)
        y_ref[:, cols] = o.astype(BF16)


def _attn_d(x, mod, w_d, lam_params, gn_row, *, n_seq, seq, tq, lam_init, rope=None, cache=None, layer=0,
            carry=None):
    n_q = seq // tq
    n_cache = 0 if cache is None else cache[0].shape[2]
    proj_rows = min(seq, 512)
    kern = functools.partial(_attn_d_kernel, seq=seq, tq=tq, n_cache=n_cache, rope=rope is not None,
                             proj_rows=proj_rows, lam_init=lam_init)
    const = lambda s, q: (0, 0)
    in_specs = [
        pl.BlockSpec((seq, D_MODEL), lambda s, q: (s, 0)),
        pl.BlockSpec((None, 1, 6 * D_MODEL), lambda s, q: (s if mod.shape[0] > 1 else 0, 0, 0)),
        pl.BlockSpec(w_d.shape, const),
        pl.BlockSpec(lam_params.shape, const),
        pl.BlockSpec(gn_row.shape, const),
    ]
    args = [x, mod, w_d, lam_params, gn_row]
    n_tok = n_seq * seq
    y_spec = pl.BlockSpec((tq, BRANCH_WIDTH), lambda s, q: (s * n_q + q, 0))
    y_shape = jax.ShapeDtypeStruct((n_tok, BRANCH_WIDTH), BF16)
    if rope is not None:
        cos, sin = rope
        in_specs += [pl.BlockSpec(cos.shape, const), pl.BlockSpec(sin.shape, const),
                     pl.BlockSpec((None, None, n_cache, 512), lambda s, q: (s, layer, 0, 0)),
                     pl.BlockSpec((None, None, n_cache, 512), lambda s, q: (s, layer, 0, 0))]
        args += [cos, sin, cache[0], cache[1]]
        out_shape, out_specs, aliases = y_shape, y_spec, {}
    else:
        kv_shape = jax.ShapeDtypeStruct((n_seq, DEPTH, seq, 512), F32)
        kv_spec = pl.BlockSpec((None, None, seq, 512), lambda s, q: (s, layer, 0, 0))
        out_shape, out_specs = (y_shape, kv_shape, kv_shape), (y_spec, kv_spec, kv_spec)
        aliases = _carry_aliases(in_specs, args, carry, first_out=1)
    return pl.pallas_call(
        kern, out_shape=out_shape, grid=(n_seq, n_q), in_specs=in_specs, out_specs=out_specs,
        input_output_aliases=aliases,
        scratch_shapes=[pltpu.VMEM((seq, 512), BF16), pltpu.VMEM((seq + n_cache, 512), BF16),
                        pltpu.VMEM((seq + n_cache, 512), BF16)],
        compiler_params=_cparams("arbitrary", "arbitrary"),
        name="branch_d_lat" if rope is not None else "branch_d_ctx",
    )(*args)


def _gmlp_kernel(x_ref, mod_ref, w_ref, ws_ref, bias_ref, y_ref, *, rows):
    h = _modulated(x_ref[...], mod_ref, 0).astype(BF16)
    p = _dot(h, w_ref[...])
    u, v = p[:, :BRANCH_WIDTH], p[:, BRANCH_WIDTH:]
    mu = jnp.mean(v, axis=-1, keepdims=True)
    vc = v - mu
    var = jnp.mean(vc * vc, axis=-1, keepdims=True)
    vn = (vc * lax.rsqrt(var + EPS)).astype(BF16)
    for c in range(rows // CHUNK):
        rs = slice(c * CHUNK, (c + 1) * CHUNK)
        for g in range(B_GROUPS):
            cs = slice(g * 128, (g + 1) * 128)
            s = _dot(ws_ref[g].astype(BF16), vn[rs, cs]) + bias_ref[:, cs]
            y_ref[rs, cs] = (u[rs, cs] * s).astype(BF16)


def _gmlp(x, mod, w_b, ws, bias_full, *, seq, rows):
    n_tok = x.shape[0]
    per_seq = seq // rows
    kern = functools.partial(_gmlp_kernel, rows=rows)
    return pl.pallas_call(
        kern, out_shape=jax.ShapeDtypeStruct((n_tok, BRANCH_WIDTH), BF16),
        grid=(n_tok // rows,),
        in_specs=[
            pl.BlockSpec((rows, D_MODEL), lambda i: (i, 0)),
            pl.BlockSpec((None, 1, 6 * D_MODEL), lambda i: (i // per_seq if mod.shape[0] > 1 else 0, 0, 0)),
            pl.BlockSpec(w_b.shape, lambda i: (0, 0)),
            pl.BlockSpec(ws.shape, lambda i: (0, 0, 0)),
            pl.BlockSpec(bias_full.shape, lambda i: (0, 0)),
        ],
        out_specs=pl.BlockSpec((rows, BRANCH_WIDTH), lambda i: (i, 0)),
        compiler_params=_cparams("arbitrary"),
        name="branch_b",
    )(x, mod, w_b, ws, bias_full)


def _mlstm_t_kernel(*refs, seq, has_init, proj_rows):
    if has_init:
        (x_ref, mod_ref, wn_ref, wt_ref, gbr_ref, gbt_ref, gnt_ref, tril_ref, triu_ref, c0_ref, n0_ref, m0_ref,
         y_ref, k_s, qt_s, vt_s, ot_s, g_s, gt_s, hf_s, hb_s, st_s, m_s) = refs
    else:
        (x_ref, mod_ref, wn_ref, wt_ref, gbr_ref, gbt_ref, gnt_ref, tril_ref, triu_ref) = refs[:9]
        (y_ref, cout_ref, nout_ref, mout_ref,
         k_s, qt_s, vt_s, ot_s, g_s, gt_s, hf_s, hb_s, st_s, m_s) = refs[-14:]
    n_chunk = seq // CHUNK
    for r0 in range(0, seq, proj_rows):
        span = pl.ds(r0, proj_rows)
        h = _modulated(x_ref[span, :], mod_ref, 0).astype(BF16)
        p = _dot(h, wn_ref[...])
        k_s[span, :] = (p[:, 0:512] * (C_HEAD_DIM ** -0.5)).astype(BF16)
        g_s[span, :] = p[:, 512:640] + gbr_ref[...]
        pt = _dot_nt(wt_ref[...], h)
        qt_s[:, span] = pt[0:512].astype(BF16)
        vt_s[:, span] = pt[512:1024].astype(BF16)
        ot_s[:, span] = jax.nn.sigmoid(pt[1024:1536]).astype(BF16)
        gt_s[:, span] = pt[1536:1664] + _tile_lanes(gbt_ref[...], proj_rows // 128)

    first_row = lax.broadcasted_iota(jnp.int32, (CHUNK, CHUNK), 0) == 0
    for sidx in range(2 * C_HEADS):
        if has_init:
            st_s[sidx, 0:128, :] = c0_ref[sidx].T
            st_s[sidx, 128:256, :] = jnp.where(first_row, jnp.broadcast_to(n0_ref[sidx:sidx + 1, :], (CHUNK, 128)), 0.0)
        else:
            st_s[sidx] = jnp.zeros((2 * CHUNK, 128), F32)
    m_s[...] = m0_ref[...] if has_init else jnp.zeros_like(m_s)

    tril = tril_ref[...]
    triu = triu_ref[...]
    row_i = lax.broadcasted_iota(jnp.int32, (CHUNK, CHUNK), 0)
    col_i = lax.broadcasted_iota(jnp.int32, (CHUNK, CHUNK), 1)
    visible = (row_i <= col_i, row_i >= col_i)
    ones_blk = jnp.where(first_row, 1.0, 0.0).astype(BF16)

    def chunk_step(c, carry):
        prep = []
        for direction in range(2):
            cc = c if direction == 0 else n_chunk - 1 - c
            span = pl.ds(pl.multiple_of(cc * CHUNK, CHUNK), CHUNK)
            gates = g_s[span, :]
            gates_t = gt_s[:, span]
            logf = jnp.minimum(gates, 0.0) - jnp.log1p(jnp.exp(-jnp.abs(gates)))
            logf_t = jnp.minimum(gates_t, 0.0) - jnp.log1p(jnp.exp(-jnp.abs(gates_t)))
            tri_col = tril if direction == 0 else triu
            tri_row = triu if direction == 0 else tril
            b_col_all = jnp.dot(tri_col, logf, precision=HIGHEST, preferred_element_type=F32)
            b_row_all = jnp.dot(logf_t, tri_row, precision=HIGHEST, preferred_element_type=F32)
            r_col_all = gates - pltpu.roll(b_col_all, 128 - 4, 1)
            base = direction * 8
            prep.append((span, r_col_all, gates_t[base:base + 4, :], b_row_all[base + 4:base + 8, :]))
        for hd in range(C_HEADS):
            for direction in range(2):
                span, r_col_all, i_rows, b_rows = prep[direction]
                base = direction * 8
                last = CHUNK - 1 if direction == 0 else 0
                h_out = hf_s if direction == 0 else hb_s
                sidx = direction * 4 + hd
                blk = slice(hd * 128, (hd + 1) * 128)
                r_col = r_col_all[:, base + hd:base + hd + 1]
                b_row = b_rows[hd:hd + 1, :]
                i_row = i_rows[hd:hd + 1, :]
                m_prev = m_s[sidx:sidx + 1, 0:1]
                r_wide = jnp.broadcast_to(r_col, (CHUNK, CHUNK))
                peak = jnp.max(jnp.where(visible[direction], r_wide, NEG_INF), axis=0, keepdims=True)
                m_row = b_row + jnp.maximum(peak, m_prev)
                kh = k_s[span, blk]
                q_t = qt_s[blk, span]
                arg = jnp.where(visible[direction], r_wide + (b_row - m_row), NEG_INF)
                w_t = (_dot(kh, q_t) * jnp.exp(arg)).astype(BF16)
                inter = jnp.exp(b_row + m_prev - m_row)
                v_one = jnp.concatenate([vt_s[blk, span], ones_blk], axis=0)
                state = st_s[sidx]
                both = _dot(v_one, w_t) + inter * _dot(state.astype(BF16), q_t)
                den = both[128:129, :]
                h_out[blk, span] = both[0:128, :] / jnp.maximum(jnp.abs(den), jnp.exp(-m_row))
                b_last = b_row[:, last:last + 1]
                g_row = b_last - b_row + i_row
                m_new = jnp.maximum(b_last + m_prev, jnp.max(g_row, axis=-1, keepdims=True))
                decay = jnp.exp(b_last + m_prev - m_new)
                scaled = v_one * jnp.exp(g_row - m_new).astype(BF16)
                st_s[sidx] = decay * state + _dot(scaled, kh)
                m_s[sidx:sidx + 1, :] = jnp.broadcast_to(m_new, (1, 128))
        return carry

    lax.fori_loop(0, n_chunk, chunk_step, 0)

    for r0 in range(0, seq, proj_rows):
        span = pl.ds(r0, proj_rows)
        for hd in range(C_HEADS):
            blk = slice(hd * 128, (hd + 1) * 128)
            hh = hf_s[blk, span] + hb_s[blk, span]
            mu = jnp.mean(hh, axis=0, keepdims=True)
            hc = hh - mu
            var = jnp.mean(hc * hc, axis=0, keepdims=True)
            gn = _tile_lanes(gnt_ref[blk, :], proj_rows // 128)
            y_t = hc * lax.rsqrt(var + EPS) * gn * ot_s[blk, span].astype(F32)
            y_ref[span, blk] = y_t.T.astype(BF16)
    if not has_init:
        for sidx in range(2 * C_HEADS):
            cout_ref[sidx] = st_s[sidx, 0:128, :].T
            nout_ref[sidx:sidx + 1, :] = st_s[sidx, 128:129, :]
        mout_ref[...] = m_s[...]


def _mlstm_t(x, mod, w_nat, w_t, gate_bias_row, gate_bias_t, gn_t, tril, triu, *, n_seq, seq, init=None, layer=0,
             carry=None):
    proj_rows = min(seq, 512)
    kern = functools.partial(_mlstm_t_kernel, seq=seq, has_init=init is not None, proj_rows=proj_rows)
    const = lambda s: (0, 0)
    in_specs = [
        pl.BlockSpec((seq, D_MODEL), lambda s: (s, 0)),
        pl.BlockSpec((None, 1, 6 * D_MODEL), lambda s: (s if mod.shape[0] > 1 else 0, 0, 0)),
    ] + [pl.BlockSpec(a.shape, const) for a in (w_nat, w_t, gate_bias_row, gate_bias_t, gn_t, tril, triu)]
    args = [x, mod, w_nat, w_t, gate_bias_row, gate_bias_t, gn_t, tril, triu]
    n_tok = n_seq * seq
    y_shape = jax.ShapeDtypeStruct((n_tok, BRANCH_WIDTH), BF16)
    y_spec = pl.BlockSpec((seq, BRANCH_WIDTH), lambda s: (s, 0))
    if init is not None:
        c0, n0, m0 = init
        in_specs += [pl.BlockSpec((None, None, 8, 128, 128), lambda s: (s, layer, 0, 0, 0)),
                     pl.BlockSpec((None, None, 8, 128), lambda s: (s, layer, 0, 0)),
                     pl.BlockSpec((None, None, 8, 128), lambda s: (s, layer, 0, 0))]
        args += [c0, n0, m0]
        out_shape, out_specs, aliases = y_shape, y_spec, {}
    else:
        out_shape = (y_shape, jax.ShapeDtypeStruct((n_seq, DEPTH, 8, 128, 128), F32),
                     jax.ShapeDtypeStruct((n_seq, DEPTH, 8, 128), F32),
                     jax.ShapeDtypeStruct((n_seq, DEPTH, 8, 128), F32))
        out_specs = (y_spec, pl.BlockSpec((None, None, 8, 128, 128), lambda s: (s, layer, 0, 0, 0)),
                     pl.BlockSpec((None, None, 8, 128), lambda s: (s, layer, 0, 0)),
                     pl.BlockSpec((None, None, 8, 128), lambda s: (s, layer, 0, 0)))
        aliases = _carry_aliases(in_specs, args, carry, first_out=1)
    return pl.pallas_call(
        kern, out_shape=out_shape, grid=(n_seq,), in_specs=in_specs, out_specs=out_specs,
        input_output_aliases=aliases,
        scratch_shapes=[pltpu.VMEM((seq, 512), BF16),
                        pltpu.VMEM((512, seq), BF16), pltpu.VMEM((512, seq), BF16),
                        pltpu.VMEM((512, seq), BF16),
                        pltpu.VMEM((seq, 128), F32), pltpu.VMEM((128, seq), F32),
                        pltpu.VMEM((512, seq), F32), pltpu.VMEM((512, seq), F32),
                        pltpu.VMEM((8, 256, 128), F32), pltpu.VMEM((8, 128), F32)],
        compiler_params=_cparams("arbitrary"),
        name="branch_c_lat" if init is not None else "branch_c_ctx",
    )(*args)


def _merge_kernel(x_ref, mod_ref, ya_ref, yb_ref, yc_ref, yd_ref, wg_ref, wbr_ref, wout_ref, lng_ref, lnb_ref,
                  o_ref):
    x = x_ref[...]
    h = _modulated(x, mod_ref, 0).astype(BF16)
    mix = None
    for n, y_ref in enumerate((ya_ref, yb_ref, yc_ref, yd_ref)):
        gate = jax.nn.sigmoid(_dot(h, wg_ref[:, n * D_MODEL:(n + 1) * D_MODEL]))
        term = gate * _dot(y_ref[...], wbr_ref[n])
        mix = term if mix is None else mix + term
    out = _dot(mix.astype(BF16), wout_ref[...])
    g1 = mod_ref[:, 2 * D_MODEL:3 * D_MODEL]
    o_ref[...] = _layer_norm_rows(ALPHA * x + g1 * out, lng_ref[...], lnb_ref[...])


def _merge(x, mod, ys, w_g, w_br, w_out, ln_g, ln_b, *, seq, rows):
    n_tok = x.shape[0]
    per_seq = seq // rows
    tok = lambda i: (i, 0)
    c2 = lambda i: (0, 0)
    return pl.pallas_call(
        _merge_kernel, out_shape=jax.ShapeDtypeStruct((n_tok, D_MODEL), F32),
        grid=(n_tok // rows,),
        in_specs=[
            pl.BlockSpec((rows, D_MODEL), tok),
            pl.BlockSpec((None, 1, 6 * D_MODEL), lambda i: (i // per_seq if mod.shape[0] > 1 else 0, 0, 0)),
            pl.BlockSpec((rows, BRANCH_WIDTH), tok), pl.BlockSpec((rows, BRANCH_WIDTH), tok),
            pl.BlockSpec((rows, BRANCH_WIDTH), tok), pl.BlockSpec((rows, BRANCH_WIDTH), tok),
            pl.BlockSpec(w_g.shape, c2), pl.BlockSpec(w_br.shape, lambda i: (0, 0, 0)),
            pl.BlockSpec(w_out.shape, c2), pl.BlockSpec(ln_g.shape, c2), pl.BlockSpec(ln_b.shape, c2),
        ],
        out_specs=pl.BlockSpec((rows, D_MODEL), tok),
        compiler_params=_cparams("arbitrary"),
        name="merge",
    )(x, mod, *ys, w_g, w_br, w_out, ln_g, ln_b)


_TAKEN = -(2.0 ** 127)


def _top16(s):
    cur = s
    vals = []
    for r in range(PEER_TOPK):
        mx = jnp.max(cur, axis=0, keepdims=True)
        cur = jnp.where(cur == mx, _TAKEN * (1.0 + r / 32.0), cur)
        vals.append(mx)
    rank = jnp.where(cur <= _TAKEN, cur * (32.0 / _TAKEN) - 31.0, float(PEER_TOPK + 1))
    return jnp.concatenate(vals, axis=0), rank


def _pair_tables():
    pairs = [(k1, k2) for k1 in range(PEER_TOPK) for k2 in range(PEER_TOPK // (k1 + 1))]
    n = 56
    sel_a = np.zeros((n, PEER_TOPK), np.float32)
    sel_b = np.zeros((n, PEER_TOPK), np.float32)
    pad = np.full((n, 1), NEG_INF, np.float32)
    for row, (k1, k2) in enumerate(pairs):
        sel_a[row, k1] = 1.0
        sel_b[row, k2] = 1.0
        pad[row, 0] = 0.0
    return jnp.asarray(sel_a), jnp.asarray(sel_b), jnp.asarray(pad), jnp.asarray(sel_a.T, BF16)


def _route_kernel(x_ref, mod_ref, wq_ref, keys_ref, sela_ref, selb_ref, pad_ref, ind_ref,
                  cnt_ref, e1_ref, r2_ref, e2_ref, h2_s, *, heads):
    @pl.when(pl.program_id(1) == 0)
    def _modulate():
        h2_s[...] = _modulated(x_ref[...], mod_ref, 1).astype(BF16)

    pick = lambda sel_ref, v: jnp.dot(sel_ref[...], v, precision=HIGHEST, preferred_element_type=F32)
    q_all = _dot(h2_s[...], wq_ref[...])
    for hd in range(heads):
        q = q_all[:, hd * PEER_QDIM:(hd + 1) * PEER_QDIM]
        s1 = _dot_nt(keys_ref[hd, 0].astype(BF16), q[:, :128].astype(BF16))
        s2 = _dot_nt(keys_ref[hd, 1].astype(BF16), q[:, 128:].astype(BF16))
        a, rank1 = _top16(s1)
        b, rank2 = _top16(s2)
        ea = jnp.exp(a - a[0:1])
        eb = jnp.exp(b - b[0:1])
        cand = pick(sela_ref, a) + pick(selb_ref, b) + pad_ref[...]
        gate = pick(sela_ref, ea) * pick(selb_ref, eb)
        cur = cand
        thr = None
        for _ in range(PEER_TOPK):
            thr = jnp.max(cur, axis=0, keepdims=True)
            cur = jnp.where(cur == thr, NEG_INF, cur)
        chosen = cand >= thr
        z = jnp.sum(jnp.where(chosen, gate, 0.0), axis=0, keepdims=True)
        cnt_sorted = _dot(ind_ref[...], jnp.where(chosen, 1.0, 0.0).astype(BF16))
        rank1_b = rank1.astype(BF16)
        counts_b = cnt_sorted.astype(BF16)
        cnt = jnp.zeros(s1.shape, BF16)
        for r in range(PEER_TOPK):
            cnt = jnp.where(rank1_b == float(r + 1), counts_b[r:r + 1], cnt)
        cnt_ref[hd] = cnt.astype(F32)
        e1_ref[hd] = jnp.where(rank1 <= float(PEER_TOPK), jnp.exp(s1 - a[0:1]) * (0.5 / z), 0.0)
        packed = (PEER_NKEYS // 16, 16, s2.shape[1])
        r2_ref[hd] = rank2.astype(BF16).reshape(packed)
        e2_ref[hd] = jnp.where(rank2 <= float(PEER_TOPK), jnp.exp(s2 - b[0:1]), 0.0).astype(BF16).reshape(packed)


def _route(x1, mod, wq, keys, *, seq, cols, heads):
    n_tok = x1.shape[0]
    per_seq = seq // cols
    tables = _pair_tables()
    row_shape = jax.ShapeDtypeStruct((PEER_HEADS, PEER_NKEYS, n_tok), F32)
    col_shape = jax.ShapeDtypeStruct((PEER_HEADS, PEER_NKEYS // 16, 16, n_tok), BF16)
    col_spec = pl.BlockSpec((heads, PEER_NKEYS // 16, 16, cols), lambda i, h: (h, 0, 0, i))
    spec = pl.BlockSpec((heads, PEER_NKEYS, cols), lambda i, h: (h, 0, i))
    return pl.pallas_call(
        functools.partial(_route_kernel, heads=heads), out_shape=(row_shape, row_shape, col_shape, col_shape),
        grid=(n_tok // cols, PEER_HEADS // heads),
        in_specs=[
            pl.BlockSpec((cols, D_MODEL), lambda i, h: (i, 0)),
            pl.BlockSpec((None, 1, 6 * D_MODEL), lambda i, h: (i // per_seq if mod.shape[0] > 1 else 0, 0, 0)),
            pl.BlockSpec((D_MODEL, heads * PEER_QDIM), lambda i, h: (0, h)),
            pl.BlockSpec((heads, 2, PEER_NKEYS, PEER_QDIM // 2), lambda i, h: (h, 0, 0, 0)),
        ] + [pl.BlockSpec(t.shape, lambda i, h: (0, 0)) for t in tables],
        out_specs=(spec, spec, col_spec, col_spec),
        scratch_shapes=[pltpu.VMEM((cols, D_MODEL), BF16)],
        compiler_params=_cparams("arbitrary", "arbitrary"),
        name="peer_route",
    )(x1, mod, wq, keys, *tables)


def _peer_kernel(x_ref, mod_ref, u_ref, v_ref, cnt_ref, e1_ref, r2_ref, e2_ref, lng_ref, lnb_ref,
                 o_ref, h2t_s, acc_s, *, key_rows):
    e = pl.program_id(1)

    @pl.when(e == 0)
    def _init():
        h2 = _modulated(x_ref[...], mod_ref, 1)
        h2t_s[...] = h2.T.astype(BF16)
        acc_s[...] = jnp.zeros_like(acc_s)

    n_tok = h2t_s.shape[1]
    first_key = pl.multiple_of(e * key_rows, key_rows)
    zero = jnp.zeros((PEER_NKEYS // 16, 16, n_tok), BF16)
    act = _dot(u_ref[...], h2t_s[...]).astype(BF16)
    act = act * (1.0 + lax.erf(act * (2.0 ** -0.5)))
    pieces = []
    for r in range(key_rows):
        g = None
        for hd in range(PEER_HEADS):
            cnt_blk = cnt_ref[hd, pl.ds(first_key, key_rows), :]
            e1_blk = e1_ref[hd, pl.ds(first_key, key_rows), :]
            cnt_rows = jnp.broadcast_to(cnt_blk[r:r + 1, :], (16, n_tok)).astype(BF16)
            e1_rows = jnp.broadcast_to(e1_blk[r:r + 1, :], (16, n_tok)).astype(BF16)
            term = jnp.where(r2_ref[hd] <= cnt_rows[None], e2_ref[hd], zero) * e1_rows[None]
            g = term if g is None else g + term
        pieces.append(g.reshape(PEER_NKEYS, n_tok) * act[r * PEER_NKEYS:(r + 1) * PEER_NKEYS, :])
    acc_s[...] += lax.dot_general(v_ref[...], jnp.concatenate(pieces, axis=0), (((0,), (0,)), ((), ())),
                                  preferred_element_type=F32)

    @pl.when(e == pl.num_programs(1) - 1)
    def _finish():
        x = x_ref[...]
        g2 = mod_ref[:, 5 * D_MODEL:6 * D_MODEL]
        o_ref[...] = _layer_norm_rows(ALPHA * x + g2 * acc_s[...].T, lng_ref[...], lnb_ref[...])


def _peer(x1, mod, u, v, route, ln_g, ln_b, *, seq, cols, key_rows, layer):
    n_tok = x1.shape[0]
    per_seq = seq // cols
    n_exp = key_rows * PEER_NKEYS
    n_tiles = PEER_EXPERTS // n_exp
    assert key_rows % 8 == 0
    kern = functools.partial(_peer_kernel, key_rows=key_rows)
    cnt, e1, r2, e2 = route
    rspec = pl.BlockSpec((PEER_HEADS, PEER_NKEYS, cols), lambda i, e: (0, 0, i))
    cspec = pl.BlockSpec((PEER_HEADS, PEER_NKEYS // 16, 16, cols), lambda i, e: (0, 0, 0, i))
    return pl.pallas_call(
        kern, out_shape=jax.ShapeDtypeStruct((n_tok, D_MODEL), F32),
        grid=(n_tok // cols, n_tiles),
        in_specs=[
            pl.BlockSpec((cols, D_MODEL), lambda i, e: (i, 0)),
            pl.BlockSpec((None, 1, 6 * D_MODEL), lambda i, e: (i // per_seq if mod.shape[0] > 1 else 0, 0, 0)),
            pl.BlockSpec((None, n_exp, D_MODEL), lambda i, e: (layer, e, 0)),
            pl.BlockSpec((None, n_exp, D_MODEL), lambda i, e: (layer, e, 0)),
            rspec, rspec, cspec, cspec,
            pl.BlockSpec(ln_g.shape, lambda i, e: (0, 0)), pl.BlockSpec(ln_b.shape, lambda i, e: (0, 0)),
        ],
        out_specs=pl.BlockSpec((cols, D_MODEL), lambda i, e: (i, 0)),
        scratch_shapes=[pltpu.VMEM((D_MODEL, cols), BF16), pltpu.VMEM((D_MODEL, cols), F32)],
        compiler_params=_cparams("arbitrary", "arbitrary"),
        name="peer_experts",
    )(x1, mod, u, v, cnt, e1, r2, e2, ln_g, ln_b)


def _rope_tables(seq):
    t = np.arange(seq)
    pos = np.stack([t // GRID_W, t % GRID_W], axis=1).astype(np.float64)
    inv = ROPE_BASE ** (-np.arange(16, dtype=np.float64) / 16)
    lane = np.arange(64)
    ang = pos[:, lane // 32] * inv[lane % 16][None, :]
    sign = np.where((lane % 32) < 16, -1.0, 1.0)[None, :]
    cos = np.tile(np.cos(ang), (1, 2)).astype(np.float32)
    sin = np.tile(np.sin(ang) * sign, (1, 2)).astype(np.float32)
    return jnp.asarray(cos), jnp.asarray(sin)


def _static_tables():
    lane = np.arange(128)
    bd = (lane[:, None] // 64 == lane[None, :] // 64).astype(np.float32) / 64.0
    src = np.arange(128)
    dst = np.arange(512)
    rep = ((src[:, None] // 64 == dst[None, :] // 256) & (src[:, None] % 64 == dst[None, :] % 64))
    idx = np.arange(CHUNK)
    tril = (idx[None, :] <= idx[:, None]).astype(np.float32)
    triu = (idx[None, :] >= idx[:, None]).astype(np.float32)
    return (jnp.asarray(bd, BF16), jnp.asarray(rep.astype(np.float32), BF16), jnp.asarray(tril), jnp.asarray(triu))


def _layer_params(l, w_in, attn_qk_gain, gmlp_ws, gmlp_b, mlstm_gate_bias, mlstm_gn, diff_lambda, diff_gn,
                  w_branch, w_out, ln_g, ln_b, peer_wq, peer_keys, peer_u, peer_v):
    w = w_in[l]
    p = {}
    p["w_a"] = w[:, _OFF_A:_OFF_B].astype(BF16)
    p["w_b"] = w[:, _OFF_B:_OFF_C].astype(BF16)
    cq, ck = w[:, _OFF_C:_OFF_C + 512], w[:, _OFF_C + 512:_OFF_C + 1024]
    cv, co = w[:, _OFF_C + 1024:_OFF_C + 1536], w[:, _OFF_C + 1536:_OFF_C + 2048]
    cg = jnp.concatenate([w[:, _OFF_CG:_OFF_D], jnp.zeros((D_MODEL, 112), F32)], axis=1)
    p["w_c_nat"] = jnp.concatenate([ck, cg], axis=1).astype(BF16)
    p["w_c_t"] = jnp.concatenate([cq, cv, co, cg], axis=1).T.astype(BF16)
    p["w_d"] = w[:, _OFF_D:_OFF_G].astype(BF16)
    p["w_g"] = w[:, _OFF_G:].astype(BF16)
    gain = attn_qk_gain[l]
    p["gain_row"] = jnp.concatenate([jnp.tile(gain[0], A_HEADS), jnp.tile(gain[1], A_KV_HEADS)])[None, :]
    p["ws"] = gmlp_ws[l]
    p["bias_full"] = jnp.repeat(gmlp_b[l].T, 128, axis=1)
    p["gate_bias_row"] = jnp.concatenate([mlstm_gate_bias[l].reshape(16), jnp.zeros((112,), F32)])[None, :]
    p["gate_bias_t"] = jnp.broadcast_to(p["gate_bias_row"].reshape(128, 1), (128, 128))
    p["mlstm_gn_t"] = jnp.broadcast_to(mlstm_gn[l].reshape(BRANCH_WIDTH, 1), (BRANCH_WIDTH, 128))
    p["lam"] = diff_lambda[l]
    p["diff_gn_row"] = diff_gn[l].reshape(1, BRANCH_WIDTH)
    p["w_br"] = w_branch[l].astype(BF16)
    p["w_out"] = w_out[l].astype(BF16)
    p["ln_g0"], p["ln_b0"] = ln_g[l, 0][None, :], ln_b[l, 0][None, :]
    p["ln_g1"], p["ln_b1"] = ln_g[l, 1][None, :], ln_b[l, 1][None, :]
    p["wq"] = peer_wq[l].astype(BF16)
    p["keys"] = peer_keys[l]
    p["u"], p["v"] = peer_u, peer_v
    return p


def _trunk_layer(x, mod, p, tabs, *, l, n_seq, seq, cfg, ctx_cache=None, prev_state=None):
    bd, rep, tril, triu = tabs
    lam_init = 0.8 - 0.6 * math.exp(-0.3 * l)
    state = None
    if ctx_cache is None:
        prev = (None, None, None) if prev_state is None else (prev_state[0:2], prev_state[2:4], prev_state[4:7])
        ya, nk, nv = _attn_a(x, mod, p["w_a"], p["gain_row"], bd, rep, n_seq=n_seq, seq=seq, tq=cfg["tq"],
                             layer=l, carry=prev[0])
        yd, ndk, ndv = _attn_d(x, mod, p["w_d"], p["lam"], p["diff_gn_row"], n_seq=n_seq, seq=seq, tq=cfg["tq"],
                               lam_init=lam_init, layer=l, carry=prev[1])
        yc, c_new, n_new, m_new = _mlstm_t(x, mod, p["w_c_nat"], p["w_c_t"], p["gate_bias_row"], p["gate_bias_t"],
                                           p["mlstm_gn_t"], tril, triu, n_seq=n_seq, seq=seq, layer=l, carry=prev[2])
        state = (nk, nv, ndk, ndv, c_new, n_new, m_new)
    else:
        rope, cak, cav, cdk, cdv, c0, n0, m0 = ctx_cache
        ya = _attn_a(x, mod, p["w_a"], p["gain_row"], bd, rep, n_seq=n_seq, seq=seq, tq=cfg["tq"],
                     rope=rope, cache=(cak, cav), layer=l)
        yd = _attn_d(x, mod, p["w_d"], p["lam"], p["diff_gn_row"], n_seq=n_seq, seq=seq, tq=cfg["tq"],
                     lam_init=lam_init, rope=rope, cache=(cdk, cdv), layer=l)
        yc = _mlstm_t(x, mod, p["w_c_nat"], p["w_c_t"], p["gate_bias_row"], p["gate_bias_t"], p["mlstm_gn_t"],
                      tril, triu, n_seq=n_seq, seq=seq, init=(c0, n0, m0), layer=l)
    yb = _gmlp(x, mod, p["w_b"], p["ws"], p["bias_full"], seq=seq, rows=cfg["gmlp_rows"])
    x1 = _merge(x, mod, (ya, yb, yc, yd), p["w_g"], p["w_br"], p["w_out"], p["ln_g0"], p["ln_b0"],
                seq=seq, rows=cfg["rows"])
    route = _route(x1, mod, p["wq"], p["keys"], seq=seq, cols=cfg["route_cols"], heads=cfg["route_heads"])
    x2 = _peer(x1, mod, p["u"], p["v"], route, p["ln_g1"], p["ln_b1"], seq=seq, cols=cfg["cols"], layer=l,
               key_rows=cfg["key_rows"])
    return x2, state


def kernel(x_prompt, x_sample, cache_a_k, cache_a_v, cache_d_k, cache_d_v, state_c_C, state_c_n, state_c_m,
           c, c_ctx, w_mod, b_mod, w_in, attn_qk_gain, gmlp_ws, gmlp_b, mlstm_gate_bias, mlstm_gn,
           diff_lambda, diff_gn, w_branch, w_out, ln_g, ln_b, peer_wq, peer_keys, peer_u, peer_v):
    batch, seq, _ = x_prompt.shape
    dec_batch, dec_seq, _ = x_sample.shape
    past = cache_a_k.shape[2]
    c_rows = jnp.concatenate([c_ctx[None, :], c, jnp.zeros((8 - 1 - dec_batch, D_MODEL), F32)], axis=0)
    mods = _modulation(c_rows, w_mod, b_mod)
    tabs = _static_tables()
    rope = _rope_tables(dec_seq)
    cak = cache_a_k.reshape(dec_batch, DEPTH, past, 128)
    cav = cache_a_v.reshape(dec_batch, DEPTH, past, 128)
    cdk = cache_d_k.reshape(dec_batch, DEPTH, past, 512)
    cdv = cache_d_v.reshape(dec_batch, DEPTH, past, 512)
    c0 = state_c_C.reshape(dec_batch, DEPTH, 8, 128, 128)
    n0 = state_c_n.reshape(dec_batch, DEPTH, 8, 128)
    m0 = jnp.broadcast_to(state_c_m.reshape(dec_batch, DEPTH, 8, 1), (dec_batch, DEPTH, 8, 128))
    cfg_ctx = dict(tq=seq, rows=512, gmlp_rows=min(1024, batch * seq), cols=512, key_rows=16,
                   route_cols=min(1024, batch * seq), route_heads=4)
    cfg_lat = dict(tq=min(256, dec_seq), rows=min(512, dec_seq), gmlp_rows=min(1024, dec_seq), cols=512, key_rows=16,
                   route_cols=min(1024, dec_seq), route_heads=4)
    y_p = x_prompt.reshape(batch * seq, D_MODEL)
    y_s = x_sample.reshape(dec_batch * dec_seq, D_MODEL)
    state = None
    u_all, v_all = peer_u.astype(BF16), peer_v.astype(BF16)
    for l in range(DEPTH):
        p = _layer_params(l, w_in, attn_qk_gain, gmlp_ws, gmlp_b, mlstm_gate_bias, mlstm_gn, diff_lambda, diff_gn,
                          w_branch, w_out, ln_g, ln_b, peer_wq, peer_keys, u_all, v_all)
        mod_ctx = mods[l, 0:1].reshape(1, 1, 6 * D_MODEL)
        mod_lat = mods[l, 1:1 + dec_batch].reshape(dec_batch, 1, 6 * D_MODEL)
        y_p, state = _trunk_layer(y_p, mod_ctx, p, tabs, l=l, n_seq=batch, seq=seq, cfg=cfg_ctx, prev_state=state)
        y_s, _ = _trunk_layer(y_s, mod_lat, p, tabs, l=l, n_seq=dec_batch, seq=dec_seq, cfg=cfg_lat,
                              ctx_cache=(rope, cak, cav, cdk, cdv, c0, n0, m0))
    nk = state[0].reshape(batch, DEPTH, seq, A_KV_HEADS, A_HEAD_DIM)
    nv = state[1].reshape(batch, DEPTH, seq, A_KV_HEADS, A_HEAD_DIM)
    ndk = state[2].reshape(batch, DEPTH, seq, D_HEADS, 2, D_HALF_DIM)
    ndv = state[3].reshape(batch, DEPTH, seq, D_HEADS, D_VDIM)
    nc = state[4].reshape(batch, DEPTH, 2, C_HEADS, C_HEAD_DIM, C_HEAD_DIM)
    nn = state[5].reshape(batch, DEPTH, 2, C_HEADS, C_HEAD_DIM)
    nm = state[6][:, :, :, 0].reshape(batch, DEPTH, 2, C_HEADS)
    return (y_p.reshape(batch, seq, D_MODEL), y_s.reshape(dec_batch, dec_seq, D_MODEL), nk, nv, ndk, ndv, nc, nn, nm)
```

```python
import functools
import math

import numpy as np
import jax
import jax.numpy as jnp
from jax import lax
from jax.experimental import pallas as pl
from jax.experimental.pallas import tpu as pltpu

F32 = jnp.float32
BF16 = jnp.bfloat16
HIGHEST = lax.Precision.HIGHEST

D_MODEL = 1024
DEPTH = 4
GRID_W = 64
ROPE_BASE = 10000.0
EPS = 1e-6
BRANCH_WIDTH = D_MODEL // 2
A_HEAD_DIM = 64
A_HEADS = 8
A_KV_HEADS = 2
B_GROUPS = 4
CHUNK = 128
C_HEADS = 4
C_HEAD_DIM = 128
D_HEADS = 4
D_VDIM = 128
D_HALF_DIM = 64
PEER_HEADS = 8
PEER_NKEYS = 128
PEER_EXPERTS = PEER_NKEYS * PEER_NKEYS
PEER_QDIM = 256
PEER_TOPK = 16
ALPHA = (2 * DEPTH) ** 0.25

_OFF_A = 0
_OFF_B = 768
_OFF_C = 1792
_OFF_CG = 3840
_OFF_D = 3856
_OFF_G = 5392

VMEM_LIMIT_BYTES = 56 * 1024 * 1024
NEG_INF = float("-inf")


def _cparams(*sem):
    return pltpu.CompilerParams(dimension_semantics=sem, vmem_limit_bytes=VMEM_LIMIT_BYTES)


def _dot(a, b):
    return jnp.dot(a, b, preferred_element_type=F32)


def _dot_nt(a, b):
    return lax.dot_general(a, b, (((1,), (1,)), ((), ())), preferred_element_type=F32)


def _modulated(x, mod_ref, which):
    base = 3 * D_MODEL * which
    sh = mod_ref[:, base:base + D_MODEL]
    sc = mod_ref[:, base + D_MODEL:base + 2 * D_MODEL]
    return x * (1.0 + sc) + sh


def _layer_norm_rows(z, g, b):
    mu = jnp.mean(z, axis=-1, keepdims=True)
    zc = z - mu
    var = jnp.mean(zc * zc, axis=-1, keepdims=True)
    return zc * lax.rsqrt(var + EPS) * g + b


def _rope(x, cos, sin_signed, lane):
    w = x.shape[1]
    nxt = pltpu.roll(x, w - 16, 1)
    prv = pltpu.roll(x, 16, 1)
    partner = jnp.where((lane % 32) < 16, nxt, prv)
    return x * cos + partner * sin_signed


def _tile_lanes(t, n):
    return t if n == 1 else jnp.concatenate([t] * n, axis=1)


def _mod_kernel(c_ref, w_ref, b_ref, o_ref):
    c = c_ref[...]
    s = c * jax.nn.sigmoid(c)
    o_ref[...] = jnp.dot(s, w_ref[...], precision=HIGHEST, preferred_element_type=F32) + b_ref[...]


def _modulation(c_rows, w_mod, b_mod):
    n_col = 6 * D_MODEL // 1024
    return pl.pallas_call(
        _mod_kernel,
        out_shape=jax.ShapeDtypeStruct((DEPTH, 8, 6 * D_MODEL), F32),
        grid=(DEPTH, n_col),
        in_specs=[
            pl.BlockSpec((8, D_MODEL), lambda l, j: (0, 0)),
            pl.BlockSpec((None, D_MODEL, 1024), lambda l, j: (l, 0, j)),
            pl.BlockSpec((None, 1, 1024), lambda l, j: (l, 0, j)),
        ],
        out_specs=pl.BlockSpec((None, 8, 1024), lambda l, j: (l, 0, j)),
        compiler_params=_cparams("arbitrary", "arbitrary"),
        name="modulation",
    )(c_rows, w_mod, b_mod.reshape(DEPTH, 1, 6 * D_MODEL))


def _attn_a_kernel(*refs, seq, tq, n_cache, rope, proj_rows):
    if rope:
        (x_ref, mod_ref, w_ref, gain_ref, bd_ref, rep_ref, cos_ref, sin_ref, ck_ref, cv_ref,
         y_ref, q_s, k_s, v_s) = refs
    else:
        (x_ref, mod_ref, w_ref, gain_ref, bd_ref, rep_ref) = refs[:6]
        (y_ref, nk_ref, nv_ref, q_s, k_s, v_s) = refs[-6:]
    qi = pl.program_id(1)

    @pl.when(qi == 0)
    def _project():
        lane = lax.broadcasted_iota(jnp.int32, (1, 640), 1)
        for r0 in range(0, seq, proj_rows):
            rows = pl.ds(r0, proj_rows)
            h = _modulated(x_ref[rows, :], mod_ref, 0).astype(BF16)
            p = _dot(h, w_ref[...])
            qk = p[:, :640]
            sq = qk * qk
            hi = sq.astype(BF16)
            lo = (sq - hi.astype(F32)).astype(BF16)
            ms = jnp.concatenate([_dot(hi[:, c:c + 128], bd_ref[...]) + _dot(lo[:, c:c + 128], bd_ref[...])
                                  for c in range(0, 640, 128)], axis=1)
            qk = qk * lax.rsqrt(ms + EPS) * gain_ref[...]
            v = p[:, 640:768]
            if not rope:
                nk_ref[rows, :] = qk[:, 512:640]
                nv_ref[rows, :] = v
            else:
                cos = _tile_lanes(cos_ref[rows, :], 5)
                sin = _tile_lanes(sin_ref[rows, :], 5)
                qk = _rope(qk, cos, sin, lane)
            q_s[rows, :] = (qk[:, :512] * (A_HEAD_DIM ** -0.5)).astype(BF16)
            k_s[rows, :] = _dot(qk[:, 512:640].astype(BF16), rep_ref[...]).astype(BF16)
            v_s[rows, :] = _dot(v.astype(BF16), rep_ref[...]).astype(BF16)
        if n_cache:
            crow = pl.ds(seq, n_cache)
            k_s[crow, :] = _dot(ck_ref[...].astype(BF16), rep_ref[...]).astype(BF16)
            v_s[crow, :] = _dot(cv_ref[...].astype(BF16), rep_ref[...]).astype(BF16)

    head_of_lane = lax.broadcasted_iota(jnp.int32, (1, 256), 1) // A_HEAD_DIM
    qb = q_s[pl.ds(pl.multiple_of(qi * tq, tq), tq), :]
    for g in range(A_KV_HEADS):
        cols = slice(g * 256, (g + 1) * 256)
        qg = qb[:, cols]
        kg = k_s[:, cols]
        vg = v_s[:, cols]
        acc = jnp.zeros((tq, 256), F32)
        for r in range(A_HEADS // A_KV_HEADS):
            sel = head_of_lane == r
            qm = jnp.where(sel, qg, jnp.zeros_like(qg))
            s = _dot_nt(qm, kg)
            m = jnp.max(s, axis=-1, keepdims=True)
            p = jnp.exp(s - m)
            l = jnp.sum(p, axis=-1, keepdims=True)
            o = _dot(p.astype(BF16), vg)
            acc = acc + jnp.where(sel, o * (1.0 / l), 0.0)
        y_ref[:, cols] = acc.astype(BF16)


def _carry_aliases(in_specs, args, carry, first_out):
    if carry is None:
        return {}
    aliases = {}
    for k, arr in enumerate(carry):
        aliases[len(args)] = first_out + k
        in_specs.append(pl.BlockSpec(memory_space=pl.ANY))
        args.append(arr)
    return aliases


def _attn_a(x, mod, w_a, gain_row, bd, rep, *, n_seq, seq, tq, rope=None, cache=None, layer=0, carry=None):
    n_q = seq // tq
    n_cache = 0 if cache is None else cache[0].shape[2]
    proj_rows = min(seq, 512)
    kern = functools.partial(_attn_a_kernel, seq=seq, tq=tq, n_cache=n_cache, rope=rope is not None,
                             proj_rows=proj_rows)
    const = lambda s, q: (0, 0)
    in_specs = [
        pl.BlockSpec((seq, D_MODEL), lambda s, q: (s, 0)),
        pl.BlockSpec((None, 1, 6 * D_MODEL), lambda s, q: (s if mod.shape[0] > 1 else 0, 0, 0)),
        pl.BlockSpec(w_a.shape, const),
        pl.BlockSpec(gain_row.shape, const),
        pl.BlockSpec(bd.shape, const),
        pl.BlockSpec(rep.shape, const),
    ]
    args = [x, mod, w_a, gain_row, bd, rep]
    n_tok = n_seq * seq
    y_spec = pl.BlockSpec((tq, BRANCH_WIDTH), lambda s, q: (s * n_q + q, 0))
    y_shape = jax.ShapeDtypeStruct((n_tok, BRANCH_WIDTH), BF16)
    if rope is not None:
        cos, sin = rope
        in_specs += [pl.BlockSpec(cos.shape, const), pl.BlockSpec(sin.shape, const),
                     pl.BlockSpec((None, None, n_cache, 128), lambda s, q: (s, layer, 0, 0)),
                     pl.BlockSpec((None, None, n_cache, 128), lambda s, q: (s, layer, 0, 0))]
        args += [cos, sin, cache[0], cache[1]]
        out_shape, out_specs, aliases = y_shape, y_spec, {}
    else:
        kv_shape = jax.ShapeDtypeStruct((n_seq, DEPTH, seq, 128), F32)
        kv_spec = pl.BlockSpec((None, None, seq, 128), lambda s, q: (s, layer, 0, 0))
        out_shape, out_specs = (y_shape, kv_shape, kv_shape), (y_spec, kv_spec, kv_spec)
        aliases = _carry_aliases(in_specs, args, carry, first_out=1)
    return pl.pallas_call(
        kern, out_shape=out_shape, grid=(n_seq, n_q), in_specs=in_specs, out_specs=out_specs,
        input_output_aliases=aliases,
        scratch_shapes=[pltpu.VMEM((seq, 512), BF16), pltpu.VMEM((seq + n_cache, 512), BF16),
                        pltpu.VMEM((seq + n_cache, 512), BF16)],
        compiler_params=_cparams("arbitrary", "arbitrary"),
        name="branch_a_lat" if rope is not None else "branch_a_ctx",
    )(*args)


def _attn_d_kernel(*refs, seq, tq, n_cache, rope, proj_rows, lam_init):
    if rope:
        (x_ref, mod_ref, w_ref, lam_ref, gn_ref, cos_ref, sin_ref, ck_ref, cv_ref,
         y_ref, q_s, k_s, v_s) = refs
    else:
        (x_ref, mod_ref, w_ref, lam_ref, gn_ref) = refs[:5]
        (y_ref, nk_ref, nv_ref, q_s, k_s, v_s) = refs[-6:]
    qi = pl.program_id(1)

    @pl.when(qi == 0)
    def _project():
        lane = lax.broadcasted_iota(jnp.int32, (1, 512), 1)
        for r0 in range(0, seq, proj_rows):
            rows = pl.ds(r0, proj_rows)
            h = _modulated(x_ref[rows, :], mod_ref, 0).astype(BF16)
            p = _dot(h, w_ref[...])
            dq, dk, dv = p[:, :512], p[:, 512:1024], p[:, 1024:1536]
            if not rope:
                nk_ref[rows, :] = dk
                nv_ref[rows, :] = dv
            else:
                cos = _tile_lanes(cos_ref[rows, :], 4)
                sin = _tile_lanes(sin_ref[rows, :], 4)
                dq = _rope(dq, cos, sin, lane)
                dk = _rope(dk, cos, sin, lane)
            q_s[rows, :] = (dq * (D_HALF_DIM ** -0.5)).astype(BF16)
            k_s[rows, :] = dk.astype(BF16)
            v_s[rows, :] = dv.astype(BF16)
        if n_cache:
            crow = pl.ds(seq, n_cache)
            k_s[crow, :] = ck_ref[...].astype(BF16)
            v_s[crow, :] = cv_ref[...].astype(BF16)

    lv = lam_ref[...]
    lam = (jnp.exp(jnp.sum(lv[0:1] * lv[1:2], axis=-1, keepdims=True))
           - jnp.exp(jnp.sum(lv[2:3] * lv[3:4], axis=-1, keepdims=True)) + lam_init)
    half_of_lane = lax.broadcasted_iota(jnp.int32, (1, 128), 1) // D_HALF_DIM
    qb = q_s[pl.ds(pl.multiple_of(qi * tq, tq), tq), :]
    for hd in range(D_HEADS):
        cols = slice(hd * 128, (hd + 1) * 128)
        qh = qb[:, cols]
        kh = k_s[:, cols]
        vh = v_s[:, cols]
        probs = []
        for j in range(2):
            qm = jnp.where(half_of_lane == j, qh, jnp.zeros_like(qh))
            s = _dot_nt(qm, kh)
            m = jnp.max(s, axis=-1, keepdims=True)
            p = jnp.exp(s - m)
            l = jnp.sum(p, axis=-1, keepdims=True)
            probs.append(p * (1.0 / l))
        a = (probs[0] - lam * probs[1]).astype(BF16)
        o = _dot(a, vh)
        ms = jnp.mean(o * o, axis=-1, keepdims=True)
        o = o * lax.rsqrt(ms + EPS) * gn_ref[:, cols] * (1.0 - lam_init)
        y_ref[:, cols] = o.astype(BF16)


def _attn_d(x, mod, w_d, lam_params, gn_row, *, n_seq, seq, tq, lam_init, rope=None, cache=None, layer=0,
            carry=None):
    n_q = seq // tq
    n_cache = 0 if cache is None else cache[0].shape[2]
    proj_rows = min(seq, 512)
    kern = functools.partial(_attn_d_kernel, seq=seq, tq=tq, n_cache=n_cache, rope=rope is not None,
                             proj_rows=proj_rows, lam_init=lam_init)
    const = lambda s, q: (0, 0)
    in_specs = [
        pl.BlockSpec((seq, D_MODEL), lambda s, q: (s, 0)),
        pl.BlockSpec((None, 1, 6 * D_MODEL), lambda s, q: (s if mod.shape[0] > 1 else 0, 0, 0)),
        pl.BlockSpec(w_d.shape, const),
        pl.BlockSpec(lam_params.shape, const),
        pl.BlockSpec(gn_row.shape, const),
    ]
    args = [x, mod, w_d, lam_params, gn_row]
    n_tok = n_seq * seq
    y_spec = pl.BlockSpec((tq, BRANCH_WIDTH), lambda s, q: (s * n_q + q, 0))
    y_shape = jax.ShapeDtypeStruct((n_tok, BRANCH_WIDTH), BF16)
    if rope is not None:
        cos, sin = rope
        in_specs += [pl.BlockSpec(cos.shape, const), pl.BlockSpec(sin.shape, const),
                     pl.BlockSpec((None, None, n_cache, 512), lambda s, q: (s, layer, 0, 0)),
                     pl.BlockSpec((None, None, n_cache, 512), lambda s, q: (s, layer, 0, 0))]
        args += [cos, sin, cache[0], cache[1]]
        out_shape, out_specs, aliases = y_shape, y_spec, {}
    else:
        kv_shape = jax.ShapeDtypeStruct((n_seq, DEPTH, seq, 512), F32)
        kv_spec = pl.BlockSpec((None, None, seq, 512), lambda s, q: (s, layer, 0, 0))
        out_shape, out_specs = (y_shape, kv_shape, kv_shape), (y_spec, kv_spec, kv_spec)
        aliases = _carry_aliases(in_specs, args, carry, first_out=1)
    return pl.pallas_call(
        kern, out_shape=out_shape, grid=(n_seq, n_q), in_specs=in_specs, out_specs=out_specs,
        input_output_aliases=aliases,
        scratch_shapes=[pltpu.VMEM((seq, 512), BF16), pltpu.VMEM((seq + n_cache, 512), BF16),
                        pltpu.VMEM((seq + n_cache, 512), BF16)],
        compiler_params=_cparams("arbitrary", "arbitrary"),
        name="branch_d_lat" if rope is not None else "branch_d_ctx",
    )(*args)


def _gmlp_kernel(x_ref, mod_ref, w_ref, ws_ref, bias_ref, y_ref, *, rows):
    h = _modulated(x_ref[...], mod_ref, 0).astype(BF16)
    p = _dot(h, w_ref[...])
    u, v = p[:, :BRANCH_WIDTH], p[:, BRANCH_WIDTH:]
    mu = jnp.mean(v, axis=-1, keepdims=True)
    vc = v - mu
    var = jnp.mean(vc * vc, axis=-1, keepdims=True)
    vn = (vc * lax.rsqrt(var + EPS)).astype(BF16)
    for c in range(rows // CHUNK):
        rs = slice(c * CHUNK, (c + 1) * CHUNK)
        for g in range(B_GROUPS):
            cs = slice(g * 128, (g + 1) * 128)
            s = _dot(ws_ref[g].astype(BF16), vn[rs, cs]) + bias_ref[:, cs]
            y_ref[rs, cs] = (u[rs, cs] * s).astype(BF16)


def _gmlp(x, mod, w_b, ws, bias_full, *, seq, rows):
    n_tok = x.shape[0]
    per_seq = seq // rows
    kern = functools.partial(_gmlp_kernel, rows=rows)
    return pl.pallas_call(
        kern, out_shape=jax.ShapeDtypeStruct((n_tok, BRANCH_WIDTH), BF16),
        grid=(n_tok // rows,),
        in_specs=[
            pl.BlockSpec((rows, D_MODEL), lambda i: (i, 0)),
            pl.BlockSpec((None, 1, 6 * D_MODEL), lambda i: (i // per_seq if mod.shape[0] > 1 else 0, 0, 0)),
            pl.BlockSpec(w_b.shape, lambda i: (0, 0)),
            pl.BlockSpec(ws.shape, lambda i: (0, 0, 0)),
            pl.BlockSpec(bias_full.shape, lambda i: (0, 0)),
        ],
        out_specs=pl.BlockSpec((rows, BRANCH_WIDTH), lambda i: (i, 0)),
        compiler_params=_cparams("arbitrary"),
        name="branch_b",
    )(x, mod, w_b, ws, bias_full)


def _mlstm_t_kernel(*refs, seq, has_init, proj_rows):
    if has_init:
        (x_ref, mod_ref, wn_ref, wt_ref, gbr_ref, gbt_ref, gnt_ref, tril_ref, triu_ref, c0_ref, n0_ref, m0_ref,
         y_ref, k_s, qt_s, vt_s, ot_s, g_s, gt_s, hf_s, hb_s, st_s, m_s) = refs
    else:
        (x_ref, mod_ref, wn_ref, wt_ref, gbr_ref, gbt_ref, gnt_ref, tril_ref, triu_ref) = refs[:9]
        (y_ref, cout_ref, nout_ref, mout_ref,
         k_s, qt_s, vt_s, ot_s, g_s, gt_s, hf_s, hb_s, st_s, m_s) = refs[-14:]
    n_chunk = seq // CHUNK
    for r0 in range(0, seq, proj_rows):
        span = pl.ds(r0, proj_rows)
        h = _modulated(x_ref[span, :], mod_ref, 0).astype(BF16)
        p = _dot(h, wn_ref[...])
        k_s[span, :] = (p[:, 0:512] * (C_HEAD_DIM ** -0.5)).astype(BF16)
        g_s[span, :] = p[:, 512:640] + gbr_ref[...]
        pt = _dot_nt(wt_ref[...], h)
        qt_s[:, span] = pt[0:512].astype(BF16)
        vt_s[:, span] = pt[512:1024].astype(BF16)
        ot_s[:, span] = jax.nn.sigmoid(pt[1024:1536]).astype(BF16)
        gt_s[:, span] = pt[1536:1664] + _tile_lanes(gbt_ref[...], proj_rows // 128)

    first_row = lax.broadcasted_iota(jnp.int32, (CHUNK, CHUNK), 0) == 0
    for sidx in range(2 * C_HEADS):
        if has_init:
            st_s[sidx, 0:128, :] = c0_ref[sidx].T
            st_s[sidx, 128:256, :] = jnp.where(first_row, jnp.broadcast_to(n0_ref[sidx:sidx + 1, :], (CHUNK, 128)), 0.0)
        else:
            st_s[sidx] = jnp.zeros((2 * CHUNK, 128), F32)
    m_s[...] = m0_ref[...] if has_init else jnp.zeros_like(m_s)

    tril = tril_ref[...]
    triu = triu_ref[...]
    row_i = lax.broadcasted_iota(jnp.int32, (CHUNK, CHUNK), 0)
    col_i = lax.broadcasted_iota(jnp.int32, (CHUNK, CHUNK), 1)
    visible = (row_i <= col_i, row_i >= col_i)
    ones_blk = jnp.where(first_row, 1.0, 0.0).astype(BF16)

    def chunk_step(c, carry):
        prep = []
        for direction in range(2):
            cc = c if direction == 0 else n_chunk - 1 - c
            span = pl.ds(pl.multiple_of(cc * CHUNK, CHUNK), CHUNK)
            gates = g_s[span, :]
            gates_t = gt_s[:, span]
            logf = jnp.minimum(gates, 0.0) - jnp.log1p(jnp.exp(-jnp.abs(gates)))
            logf_t = jnp.minimum(gates_t, 0.0) - jnp.log1p(jnp.exp(-jnp.abs(gates_t)))
            tri_col = tril if direction == 0 else triu
            tri_row = triu if direction == 0 else tril
            b_col_all = jnp.dot(tri_col, logf, precision=HIGHEST, preferred_element_type=F32)
            b_row_all = jnp.dot(logf_t, tri_row, precision=HIGHEST, preferred_element_type=F32)
            r_col_all = gates - pltpu.roll(b_col_all, 128 - 4, 1)
            base = direction * 8
            prep.append((span, r_col_all, gates_t[base:base + 4, :], b_row_all[base + 4:base + 8, :]))
        for hd in range(C_HEADS):
            for direction in range(2):
                span, r_col_all, i_rows, b_rows = prep[direction]
                base = direction * 8
                last = CHUNK - 1 if direction == 0 else 0
                h_out = hf_s if direction == 0 else hb_s
                sidx = direction * 4 + hd
                blk = slice(hd * 128, (hd + 1) * 128)
                r_col = r_col_all[:, base + hd:base + hd + 1]
                b_row = b_rows[hd:hd + 1, :]
                i_row = i_rows[hd:hd + 1, :]
                m_prev = m_s[sidx:sidx + 1, 0:1]
                r_wide = jnp.broadcast_to(r_col, (CHUNK, CHUNK))
                peak = jnp.max(jnp.where(visible[direction], r_wide, NEG_INF), axis=0, keepdims=True)
                m_row = b_row + jnp.maximum(peak, m_prev)
                kh = k_s[span, blk]
                q_t = qt_s[blk, span]
                arg = jnp.where(visible[direction], r_wide + (b_row - m_row), NEG_INF)
                w_t = (_dot(kh, q_t) * jnp.exp(arg)).astype(BF16)
                inter = jnp.exp(b_row + m_prev - m_row)
                v_one = jnp.concatenate([vt_s[blk, span], ones_blk], axis=0)
                state = st_s[sidx]
                both = _dot(v_one, w_t) + inter * _dot(state.astype(BF16), q_t)
                den = both[128:129, :]
                h_out[blk, span] = both[0:128, :] / jnp.maximum(jnp.abs(den), jnp.exp(-m_row))
                b_last = b_row[:, last:last + 1]
                g_row = b_last - b_row + i_row
                m_new = jnp.maximum(b_last + m_prev, jnp.max(g_row, axis=-1, keepdims=True))
                decay = jnp.exp(b_last + m_prev - m_new)
                scaled = v_one * jnp.exp(g_row - m_new).astype(BF16)
                st_s[sidx] = decay * state + _dot(scaled, kh)
                m_s[sidx:sidx + 1, :] = jnp.broadcast_to(m_new, (1, 128))
        return carry

    lax.fori_loop(0, n_chunk, chunk_step, 0)

    for r0 in range(0, seq, proj_rows):
        span = pl.ds(r0, proj_rows)
        for hd in range(C_HEADS):
            blk = slice(hd * 128, (hd + 1) * 128)
            hh = hf_s[blk, span] + hb_s[blk, span]
            mu = jnp.mean(hh, axis=0, keepdims=True)
            hc = hh - mu
            var = jnp.mean(hc * hc, axis=0, keepdims=True)
            gn = _tile_lanes(gnt_ref[blk, :], proj_rows // 128)
            y_t = hc * lax.rsqrt(var + EPS) * gn * ot_s[blk, span].astype(F32)
            y_ref[span, blk] = y_t.T.astype(BF16)
    if not has_init:
        for sidx in range(2 * C_HEADS):
            cout_ref[sidx] = st_s[sidx, 0:128, :].T
            nout_ref[sidx:sidx + 1, :] = st_s[sidx, 128:129, :]
        mout_ref[...] = m_s[...]


def _mlstm_t(x, mod, w_nat, w_t, gate_bias_row, gate_bias_t, gn_t, tril, triu, *, n_seq, seq, init=None, layer=0,
             carry=None):
    proj_rows = min(seq, 512)
    kern = functools.partial(_mlstm_t_kernel, seq=seq, has_init=init is not None, proj_rows=proj_rows)
    const = lambda s: (0, 0)
    in_specs = [
        pl.BlockSpec((seq, D_MODEL), lambda s: (s, 0)),
        pl.BlockSpec((None, 1, 6 * D_MODEL), lambda s: (s if mod.shape[0] > 1 else 0, 0, 0)),
    ] + [pl.BlockSpec(a.shape, const) for a in (w_nat, w_t, gate_bias_row, gate_bias_t, gn_t, tril, triu)]
    args = [x, mod, w_nat, w_t, gate_bias_row, gate_bias_t, gn_t, tril, triu]
    n_tok = n_seq * seq
    y_shape = jax.ShapeDtypeStruct((n_tok, BRANCH_WIDTH), BF16)
    y_spec = pl.BlockSpec((seq, BRANCH_WIDTH), lambda s: (s, 0))
    if init is not None:
        c0, n0, m0 = init
        in_specs += [pl.BlockSpec((None, None, 8, 128, 128), lambda s: (s, layer, 0, 0, 0)),
                     pl.BlockSpec((None, None, 8, 128), lambda s: (s, layer, 0, 0)),
                     pl.BlockSpec((None, None, 8, 128), lambda s: (s, layer, 0, 0))]
        args += [c0, n0, m0]
        out_shape, out_specs, aliases = y_shape, y_spec, {}
    else:
        out_shape = (y_shape, jax.ShapeDtypeStruct((n_seq, DEPTH, 8, 128, 128), F32),
                     jax.ShapeDtypeStruct((n_seq, DEPTH, 8, 128), F32),
                     jax.ShapeDtypeStruct((n_seq, DEPTH, 8, 128), F32))
        out_specs = (y_spec, pl.BlockSpec((None, None, 8, 128, 128), lambda s: (s, layer, 0, 0, 0)),
                     pl.BlockSpec((None, None, 8, 128), lambda s: (s, layer, 0, 0)),
                     pl.BlockSpec((None, None, 8, 128), lambda s: (s, layer, 0, 0)))
        aliases = _carry_aliases(in_specs, args, carry, first_out=1)
    return pl.pallas_call(
        kern, out_shape=out_shape, grid=(n_seq,), in_specs=in_specs, out_specs=out_specs,
        input_output_aliases=aliases,
        scratch_shapes=[pltpu.VMEM((seq, 512), BF16),
                        pltpu.VMEM((512, seq), BF16), pltpu.VMEM((512, seq), BF16),
                        pltpu.VMEM((512, seq), BF16),
                        pltpu.VMEM((seq, 128), F32), pltpu.VMEM((128, seq), F32),
                        pltpu.VMEM((512, seq), F32), pltpu.VMEM((512, seq), F32),
                        pltpu.VMEM((8, 256, 128), F32), pltpu.VMEM((8, 128), F32)],
        compiler_params=_cparams("arbitrary"),
        name="branch_c_lat" if init is not None else "branch_c_ctx",
    )(*args)


def _merge_kernel(x_ref, mod_ref, ya_ref, yb_ref, yc_ref, yd_ref, wg_ref, wbr_ref, wout_ref, lng_ref, lnb_ref,
                  o_ref):
    x = x_ref[...]
    h = _modulated(x, mod_ref, 0).astype(BF16)
    mix = None
    for n, y_ref in enumerate((ya_ref, yb_ref, yc_ref, yd_ref)):
        gate = jax.nn.sigmoid(_dot(h, wg_ref[:, n * D_MODEL:(n + 1) * D_MODEL]))
        term = gate * _dot(y_ref[...], wbr_ref[n])
        mix = term if mix is None else mix + term
    out = _dot(mix.astype(BF16), wout_ref[...])
    g1 = mod_ref[:, 2 * D_MODEL:3 * D_MODEL]
    o_ref[...] = _layer_norm_rows(ALPHA * x + g1 * out, lng_ref[...], lnb_ref[...])


def _merge(x, mod, ys, w_g, w_br, w_out, ln_g, ln_b, *, seq, rows):
    n_tok = x.shape[0]
    per_seq = seq // rows
    tok = lambda i: (i, 0)
    c2 = lambda i: (0, 0)
    return pl.pallas_call(
        _merge_kernel, out_shape=jax.ShapeDtypeStruct((n_tok, D_MODEL), F32),
        grid=(n_tok // rows,),
        in_specs=[
            pl.BlockSpec((rows, D_MODEL), tok),
            pl.BlockSpec((None, 1, 6 * D_MODEL), lambda i: (i // per_seq if mod.shape[0] > 1 else 0, 0, 0)),
            pl.BlockSpec((rows, BRANCH_WIDTH), tok), pl.BlockSpec((rows, BRANCH_WIDTH), tok),
            pl.BlockSpec((rows, BRANCH_WIDTH), tok), pl.BlockSpec((rows, BRANCH_WIDTH), tok),
            pl.BlockSpec(w_g.shape, c2), pl.BlockSpec(w_br.shape, lambda i: (0, 0, 0)),
            pl.BlockSpec(w_out.shape, c2), pl.BlockSpec(ln_g.shape, c2), pl.BlockSpec(ln_b.shape, c2),
        ],
        out_specs=pl.BlockSpec((rows, D_MODEL), tok),
        compiler_params=_cparams("arbitrary"),
        name="merge",
    )(x, mod, *ys, w_g, w_br, w_out, ln_g, ln_b)


_TAKEN = -(2.0 ** 127)


def _top16(s):
    cur = s
    vals = []
    for r in range(PEER_TOPK):
        mx = jnp.max(cur, axis=0, keepdims=True)
        cur = jnp.where(cur == mx, _TAKEN * (1.0 + r / 32.0), cur)
        vals.append(mx)
    rank = jnp.where(cur <= _TAKEN, cur * (32.0 / _TAKEN) - 31.0, float(PEER_TOPK + 1))
    return jnp.concatenate(vals, axis=0), rank


def _pair_tables():
    pairs = [(k1, k2) for k1 in range(PEER_TOPK) for k2 in range(PEER_TOPK // (k1 + 1))]
    n = 56
    sel_a = np.zeros((n, PEER_TOPK), np.float32)
    sel_b = np.zeros((n, PEER_TOPK), np.float32)
    pad = np.full((n, 1), NEG_INF, np.float32)
    for row, (k1, k2) in enumerate(pairs):
        sel_a[row, k1] = 1.0
        sel_b[row, k2] = 1.0
        pad[row, 0] = 0.0
    return jnp.asarray(sel_a), jnp.asarray(sel_b), jnp.asarray(pad), jnp.asarray(sel_a.T, BF16)


def _route_kernel(x_ref, mod_ref, wq_ref, keys_ref, sela_ref, selb_ref, pad_ref, ind_ref,
                  cnt_ref, e1_ref, r2_ref, e2_ref, h2_s, *, heads):
    @pl.when(pl.program_id(1) == 0)
    def _modulate():
        h2_s[...] = _modulated(x_ref[...], mod_ref, 1).astype(BF16)

    pick = lambda sel_ref, v: jnp.dot(sel_ref[...], v, precision=HIGHEST, preferred_element_type=F32)
    q_all = _dot(h2_s[...], wq_ref[...])
    for hd in range(heads):
        q = q_all[:, hd * PEER_QDIM:(hd + 1) * PEER_QDIM]
        s1 = _dot_nt(keys_ref[hd, 0].astype(BF16), q[:, :128].astype(BF16))
        s2 = _dot_nt(keys_ref[hd, 1].astype(BF16), q[:, 128:].astype(BF16))
        a, rank1 = _top16(s1)
        b, rank2 = _top16(s2)
        ea = jnp.exp(a - a[0:1])
        eb = jnp.exp(b - b[0:1])
        cand = pick(sela_ref, a) + pick(selb_ref, b) + pad_ref[...]
        gate = pick(sela_ref, ea) * pick(selb_ref, eb)
        cur = cand
        thr = None
        for _ in range(PEER_TOPK):
            thr = jnp.max(cur, axis=0, keepdims=True)
            cur = jnp.where(cur == thr, NEG_INF, cur)
        chosen = cand >= thr
        z = jnp.sum(jnp.where(chosen, gate, 0.0), axis=0, keepdims=True)
        cnt_sorted = _dot(ind_ref[...], jnp.where(chosen, 1.0, 0.0).astype(BF16))
        rank1_b = rank1.astype(BF16)
        counts_b = cnt_sorted.astype(BF16)
        cnt = jnp.zeros(s1.shape, BF16)
        for r in range(PEER_TOPK):
            cnt = jnp.where(rank1_b == float(r + 1), counts_b[r:r + 1], cnt)
        cnt_ref[hd] = cnt.astype(F32)
        e1_ref[hd] = jnp.where(rank1 <= float(PEER_TOPK), jnp.exp(s1 - a[0:1]) * (0.5 / z), 0.0)
        packed = (PEER_NKEYS // 16, 16, s2.shape[1])
        r2_ref[hd] = rank2.astype(BF16).reshape(packed)
        e2_ref[hd] = jnp.where(rank2 <= float(PEER_TOPK), jnp.exp(s2 - b[0:1]), 0.0).astype(BF16).reshape(packed)


def _route(x1, mod, wq, keys, *, seq, cols, heads):
    n_tok = x1.shape[0]
    per_seq = seq // cols
    tables = _pair_tables()
    row_shape = jax.ShapeDtypeStruct((PEER_HEADS, PEER_NKEYS, n_tok), F32)
    col_shape = jax.ShapeDtypeStruct((PEER_HEADS, PEER_NKEYS // 16, 16, n_tok), BF16)
    col_spec = pl.BlockSpec((heads, PEER_NKEYS // 16, 16, cols), lambda i, h: (h, 0, 0, i))
    spec = pl.BlockSpec((heads, PEER_NKEYS, cols), lambda i, h: (h, 0, i))
    return pl.pallas_call(
        functools.partial(_route_kernel, heads=heads), out_shape=(row_shape, row_shape, col_shape, col_shape),
        grid=(n_tok // cols, PEER_HEADS // heads),
        in_specs=[
            pl.BlockSpec((cols, D_MODEL), lambda i, h: (i, 0)),
            pl.BlockSpec((None, 1, 6 * D_MODEL), lambda i, h: (i // per_seq if mod.shape[0] > 1 else 0, 0, 0)),
            pl.BlockSpec((D_MODEL, heads * PEER_QDIM), lambda i, h: (0, h)),
            pl.BlockSpec((heads, 2, PEER_NKEYS, PEER_QDIM // 2), lambda i, h: (h, 0, 0, 0)),
        ] + [pl.BlockSpec(t.shape, lambda i, h: (0, 0)) for t in tables],
        out_specs=(spec, spec, col_spec, col_spec),
        scratch_shapes=[pltpu.VMEM((cols, D_MODEL), BF16)],
        compiler_params=_cparams("arbitrary", "arbitrary"),
        name="peer_route",
    )(x1, mod, wq, keys, *tables)


def _peer_kernel(x_ref, mod_ref, u_ref, v_ref, cnt_ref, e1_ref, r2_ref, e2_ref, lng_ref, lnb_ref,
                 o_ref, h2t_s, acc_s, *, key_rows):
    e = pl.program_id(1)

    @pl.when(e == 0)
    def _init():
        h2 = _modulated(x_ref[...], mod_ref, 1)
        h2t_s[...] = h2.T.astype(BF16)
        acc_s[...] = jnp.zeros_like(acc_s)

    n_tok = h2t_s.shape[1]
    first_key = pl.multiple_of(e * key_rows, key_rows)
    zero = jnp.zeros((PEER_NKEYS // 16, 16, n_tok), BF16)
    act = _dot(u_ref[...], h2t_s[...]).astype(BF16)
    act = act * (1.0 + lax.erf(act * (2.0 ** -0.5)))
    pieces = []
    for r in range(key_rows):
        g = None
        for hd in range(PEER_HEADS):
            cnt_blk = cnt_ref[hd, pl.ds(first_key, key_rows), :]
            e1_blk = e1_ref[hd, pl.ds(first_key, key_rows), :]
            cnt_rows = jnp.broadcast_to(cnt_blk[r:r + 1, :], (16, n_tok)).astype(BF16)
            e1_rows = jnp.broadcast_to(e1_blk[r:r + 1, :], (16, n_tok)).astype(BF16)
            term = jnp.where(r2_ref[hd] <= cnt_rows[None], e2_ref[hd], zero) * e1_rows[None]
            g = term if g is None else g + term
        pieces.append(g.reshape(PEER_NKEYS, n_tok) * act[r * PEER_NKEYS:(r + 1) * PEER_NKEYS, :])
    acc_s[...] += _dot(v_ref[...], jnp.concatenate(pieces, axis=0))

    @pl.when(e == pl.num_programs(1) - 1)
    def _finish():
        x = x_ref[...]
        g2 = mod_ref[:, 5 * D_MODEL:6 * D_MODEL]
        o_ref[...] = _layer_norm_rows(ALPHA * x + g2 * acc_s[...].T, lng_ref[...], lnb_ref[...])


def _peer(x1, mod, u, v, route, ln_g, ln_b, *, seq, cols, key_rows, layer):
    n_tok = x1.shape[0]
    per_seq = seq // cols
    n_exp = key_rows * PEER_NKEYS
    n_tiles = PEER_EXPERTS // n_exp
    assert key_rows % 8 == 0
    kern = functools.partial(_peer_kernel, key_rows=key_rows)
    cnt, e1, r2, e2 = route
    rspec = pl.BlockSpec((PEER_HEADS, PEER_NKEYS, cols), lambda i, e: (0, 0, i))
    cspec = pl.BlockSpec((PEER_HEADS, PEER_NKEYS // 16, 16, cols), lambda i, e: (0, 0, 0, i))
    return pl.pallas_call(
        kern, out_shape=jax.ShapeDtypeStruct((n_tok, D_MODEL), F32),
        grid=(n_tok // cols, n_tiles),
        in_specs=[
            pl.BlockSpec((cols, D_MODEL), lambda i, e: (i, 0)),
            pl.BlockSpec((None, 1, 6 * D_MODEL), lambda i, e: (i // per_seq if mod.shape[0] > 1 else 0, 0, 0)),
            pl.BlockSpec((None, n_exp, D_MODEL), lambda i, e: (layer, e, 0)),
            pl.BlockSpec((None, D_MODEL, n_exp), lambda i, e: (layer, 0, e)),
            rspec, rspec, cspec, cspec,
            pl.BlockSpec(ln_g.shape, lambda i, e: (0, 0)), pl.BlockSpec(ln_b.shape, lambda i, e: (0, 0)),
        ],
        out_specs=pl.BlockSpec((cols, D_MODEL), lambda i, e: (i, 0)),
        scratch_shapes=[pltpu.VMEM((D_MODEL, cols), BF16), pltpu.VMEM((D_MODEL, cols), F32)],
        compiler_params=_cparams("arbitrary", "arbitrary"),
        name="peer_experts",
    )(x1, mod, u, v, cnt, e1, r2, e2, ln_g, ln_b)


def _rope_tables(seq):
    t = np.arange(seq)
    pos = np.stack([t // GRID_W, t % GRID_W], axis=1).astype(np.float64)
    inv = ROPE_BASE ** (-np.arange(16, dtype=np.float64) / 16)
    lane = np.arange(64)
    ang = pos[:, lane // 32] * inv[lane % 16][None, :]
    sign = np.where((lane % 32) < 16, -1.0, 1.0)[None, :]
    cos = np.tile(np.cos(ang), (1, 2)).astype(np.float32)
    sin = np.tile(np.sin(ang) * sign, (1, 2)).astype(np.float32)
    return jnp.asarray(cos), jnp.asarray(sin)


def _static_tables():
    lane = np.arange(128)
    bd = (lane[:, None] // 64 == lane[None, :] // 64).astype(np.float32) / 64.0
    src = np.arange(128)
    dst = np.arange(512)
    rep = ((src[:, None] // 64 == dst[None, :] // 256) & (src[:, None] % 64 == dst[None, :] % 64))
    idx = np.arange(CHUNK)
    tril = (idx[None, :] <= idx[:, None]).astype(np.float32)
    triu = (idx[None, :] >= idx[:, None]).astype(np.float32)
    return (jnp.asarray(bd, BF16), jnp.asarray(rep.astype(np.float32), BF16), jnp.asarray(tril), jnp.asarray(triu))


def _layer_params(l, w_in, attn_qk_gain, gmlp_ws, gmlp_b, mlstm_gate_bias, mlstm_gn, diff_lambda, diff_gn,
                  w_branch, w_out, ln_g, ln_b, peer_wq, peer_keys, peer_u, peer_v):
    w = w_in[l]
    p = {}
    p["w_a"] = w[:, _OFF_A:_OFF_B].astype(BF16)
    p["w_b"] = w[:, _OFF_B:_OFF_C].astype(BF16)
    cq, ck = w[:, _OFF_C:_OFF_C + 512], w[:, _OFF_C + 512:_OFF_C + 1024]
    cv, co = w[:, _OFF_C + 1024:_OFF_C + 1536], w[:, _OFF_C + 1536:_OFF_C + 2048]
    cg = jnp.concatenate([w[:, _OFF_CG:_OFF_D], jnp.zeros((D_MODEL, 112), F32)], axis=1)
    p["w_c_nat"] = jnp.concatenate([ck, cg], axis=1).astype(BF16)
    p["w_c_t"] = jnp.concatenate([cq, cv, co, cg], axis=1).T.astype(BF16)
    p["w_d"] = w[:, _OFF_D:_OFF_G].astype(BF16)
    p["w_g"] = w[:, _OFF_G:].astype(BF16)
    gain = attn_qk_gain[l]
    p["gain_row"] = jnp.concatenate([jnp.tile(gain[0], A_HEADS), jnp.tile(gain[1], A_KV_HEADS)])[None, :]
    p["ws"] = gmlp_ws[l]
    p["bias_full"] = jnp.repeat(gmlp_b[l].T, 128, axis=1)
    p["gate_bias_row"] = jnp.concatenate([mlstm_gate_bias[l].reshape(16), jnp.zeros((112,), F32)])[None, :]
    p["gate_bias_t"] = jnp.broadcast_to(p["gate_bias_row"].reshape(128, 1), (128, 128))
    p["mlstm_gn_t"] = jnp.broadcast_to(mlstm_gn[l].reshape(BRANCH_WIDTH, 1), (BRANCH_WIDTH, 128))
    p["lam"] = diff_lambda[l]
    p["diff_gn_row"] = diff_gn[l].reshape(1, BRANCH_WIDTH)
    p["w_br"] = w_branch[l].astype(BF16)
    p["w_out"] = w_out[l].astype(BF16)
    p["ln_g0"], p["ln_b0"] = ln_g[l, 0][None, :], ln_b[l, 0][None, :]
    p["ln_g1"], p["ln_b1"] = ln_g[l, 1][None, :], ln_b[l, 1][None, :]
    p["wq"] = peer_wq[l].astype(BF16)
    p["keys"] = peer_keys[l]
    p["u"], p["v"] = peer_u, peer_v
    return p


def _trunk_layer(x, mod, p, tabs, *, l, n_seq, seq, cfg, ctx_cache=None, prev_state=None):
    bd, rep, tril, triu = tabs
    lam_init = 0.8 - 0.6 * math.exp(-0.3 * l)
    state = None
    if ctx_cache is None:
        prev = (None, None, None) if prev_state is None else (prev_state[0:2], prev_state[2:4], prev_state[4:7])
        ya, nk, nv = _attn_a(x, mod, p["w_a"], p["gain_row"], bd, rep, n_seq=n_seq, seq=seq, tq=cfg["tq"],
                             layer=l, carry=prev[0])
        yd, ndk, ndv = _attn_d(x, mod, p["w_d"], p["lam"], p["diff_gn_row"], n_seq=n_seq, seq=seq, tq=cfg["tq"],
                               lam_init=lam_init, layer=l, carry=prev[1])
        yc, c_new, n_new, m_new = _mlstm_t(x, mod, p["w_c_nat"], p["w_c_t"], p["gate_bias_row"], p["gate_bias_t"],
                                           p["mlstm_gn_t"], tril, triu, n_seq=n_seq, seq=seq, layer=l, carry=prev[2])
        state = (nk, nv, ndk, ndv, c_new, n_new, m_new)
    else:
        rope, cak, cav, cdk, cdv, c0, n0, m0 = ctx_cache
        ya = _attn_a(x, mod, p["w_a"], p["gain_row"], bd, rep, n_seq=n_seq, seq=seq, tq=cfg["tq"],
                     rope=rope, cache=(cak, cav), layer=l)
        yd = _attn_d(x, mod, p["w_d"], p["lam"], p["diff_gn_row"], n_seq=n_seq, seq=seq, tq=cfg["tq"],
                     lam_init=lam_init, rope=rope, cache=(cdk, cdv), layer=l)
        yc = _mlstm_t(x, mod, p["w_c_nat"], p["w_c_t"], p["gate_bias_row"], p["gate_bias_t"], p["mlstm_gn_t"],
                      tril, triu, n_seq=n_seq, seq=seq, init=(c0, n0, m0), layer=l)
    yb = _gmlp(x, mod, p["w_b"], p["ws"], p["bias_full"], seq=seq, rows=cfg["gmlp_rows"])
    x1 = _merge(x, mod, (ya, yb, yc, yd), p["w_g"], p["w_br"], p["w_out"], p["ln_g0"], p["ln_b0"],
                seq=seq, rows=cfg["rows"])
    route = _route(x1, mod, p["wq"], p["keys"], seq=seq, cols=cfg["route_cols"], heads=cfg["route_heads"])
    x2 = _peer(x1, mod, p["u"], p["v"], route, p["ln_g1"], p["ln_b1"], seq=seq, cols=cfg["cols"], layer=l,
               key_rows=cfg["key_rows"])
    return x2, state


def kernel(x_prompt, x_sample, cache_a_k, cache_a_v, cache_d_k, cache_d_v, state_c_C, state_c_n, state_c_m,
           c, c_ctx, w_mod, b_mod, w_in, attn_qk_gain, gmlp_ws, gmlp_b, mlstm_gate_bias, mlstm_gn,
           diff_lambda, diff_gn, w_branch, w_out, ln_g, ln_b, peer_wq, peer_keys, peer_u, peer_v):
    batch, seq, _ = x_prompt.shape
    dec_batch, dec_seq, _ = x_sample.shape
    past = cache_a_k.shape[2]
    c_rows = jnp.concatenate([c_ctx[None, :], c, jnp.zeros((8 - 1 - dec_batch, D_MODEL), F32)], axis=0)
    mods = _modulation(c_rows, w_mod, b_mod)
    tabs = _static_tables()
    rope = _rope_tables(dec_seq)
    cak = cache_a_k.reshape(dec_batch, DEPTH, past, 128)
    cav = cache_a_v.reshape(dec_batch, DEPTH, past, 128)
    cdk = cache_d_k.reshape(dec_batch, DEPTH, past, 512)
    cdv = cache_d_v.reshape(dec_batch, DEPTH, past, 512)
    c0 = state_c_C.reshape(dec_batch, DEPTH, 8, 128, 128)
    n0 = state_c_n.reshape(dec_batch, DEPTH, 8, 128)
    m0 = jnp.broadcast_to(state_c_m.reshape(dec_batch, DEPTH, 8, 1), (dec_batch, DEPTH, 8, 128))
    cfg_ctx = dict(tq=seq, rows=512, gmlp_rows=min(1024, batch * seq), cols=512, key_rows=16,
                   route_cols=min(1024, batch * seq), route_heads=4)
    cfg_lat = dict(tq=min(256, dec_seq), rows=min(512, dec_seq), gmlp_rows=min(1024, dec_seq), cols=512, key_rows=16,
                   route_cols=min(1024, dec_seq), route_heads=4)
    y_p = x_prompt.reshape(batch * seq, D_MODEL)
    y_s = x_sample.reshape(dec_batch * dec_seq, D_MODEL)
    state = None
    u_all, v_all = peer_u.astype(BF16), jnp.swapaxes(peer_v, 1, 2).astype(BF16)
    for l in range(DEPTH):
        p = _layer_params(l, w_in, attn_qk_gain, gmlp_ws, gmlp_b, mlstm_gate_bias, mlstm_gn, diff_lambda, diff_gn,
                          w_branch, w_out, ln_g, ln_b, peer_wq, peer_keys, u_all, v_all)
        mod_ctx = mods[l, 0:1].reshape(1, 1, 6 * D_MODEL)
        mod_lat = mods[l, 1:1 + dec_batch].reshape(dec_batch, 1, 6 * D_MODEL)
        y_p, state = _trunk_layer(y_p, mod_ctx, p, tabs, l=l, n_seq=batch, seq=seq, cfg=cfg_ctx, prev_state=state)
        y_s, _ = _trunk_layer(y_s, mod_lat, p, tabs, l=l, n_seq=dec_batch, seq=dec_seq, cfg=cfg_lat,
                              ctx_cache=(rope, cak, cav, cdk, cdv, c0, n0, m0))
    nk = state[0].reshape(batch, DEPTH, seq, A_KV_HEADS, A_HEAD_DIM)
    nv = state[1].reshape(batch, DEPTH, seq, A_KV_HEADS, A_HEAD_DIM)
    ndk = state[2].reshape(batch, DEPTH, seq, D_HEADS, 2, D_HALF_DIM)
    ndv = state[3].reshape(batch, DEPTH, seq, D_HEADS, D_VDIM)
    nc = state[4].reshape(batch, DEPTH, 2, C_HEADS, C_HEAD_DIM, C_HEAD_DIM)
    nn = state[5].reshape(batch, DEPTH, 2, C_HEADS, C_HEAD_DIM)
    nm = state[6][:, :, :, 0].reshape(batch, DEPTH, 2, C_HEADS)
    return (y_p.reshape(batch, seq, D_MODEL), y_s.reshape(dec_batch, dec_seq, D_MODEL), nk, nv, ndk, ndv, nc, nn, nm)
```

```python
import functools
import math

import numpy as np
import jax
import jax.numpy as jnp
from jax import lax
from jax.experimental import pallas as pl
from jax.experimental.pallas import tpu as pltpu

F32 = jnp.float32
BF16 = jnp.bfloat16
HIGHEST = lax.Precision.HIGHEST

D_MODEL = 1024
DEPTH = 4
GRID_W = 64
ROPE_BASE = 10000.0
EPS = 1e-6
BRANCH_WIDTH = D_MODEL // 2
A_HEAD_DIM = 64
A_HEADS = 8
A_KV_HEADS = 2
B_GROUPS = 4
CHUNK = 128
C_HEADS = 4
C_HEAD_DIM = 128
D_HEADS = 4
D_VDIM = 128
D_HALF_DIM = 64
PEER_HEADS = 8
PEER_NKEYS = 128
PEER_EXPERTS = PEER_NKEYS * PEER_NKEYS
PEER_QDIM = 256
PEER_TOPK = 16
ALPHA = (2 * DEPTH) ** 0.25

_OFF_A = 0
_OFF_B = 768
_OFF_C = 1792
_OFF_CG = 3840
_OFF_D = 3856
_OFF_G = 5392

VMEM_LIMIT_BYTES = 56 * 1024 * 1024
NEG_INF = float("-inf")


def _cparams(*sem):
    return pltpu.CompilerParams(dimension_semantics=sem, vmem_limit_bytes=VMEM_LIMIT_BYTES)


def _dot(a, b):
    return jnp.dot(a, b, preferred_element_type=F32)


def _dot_nt(a, b):
    return lax.dot_general(a, b, (((1,), (1,)), ((), ())), preferred_element_type=F32)


def _modulated(x, mod_ref, which):
    base = 3 * D_MODEL * which
    sh = mod_ref[:, base:base + D_MODEL]
    sc = mod_ref[:, base + D_MODEL:base + 2 * D_MODEL]
    return x * (1.0 + sc) + sh


def _layer_norm_rows(z, g, b):
    mu = jnp.mean(z, axis=-1, keepdims=True)
    zc = z - mu
    var = jnp.mean(zc * zc, axis=-1, keepdims=True)
    return zc * lax.rsqrt(var + EPS) * g + b


def _rope(x, cos, sin_signed, lane):
    w = x.shape[1]
    nxt = pltpu.roll(x, w - 16, 1)
    prv = pltpu.roll(x, 16, 1)
    partner = jnp.where((lane % 32) < 16, nxt, prv)
    return x * cos + partner * sin_signed


def _tile_lanes(t, n):
    return t if n == 1 else jnp.concatenate([t] * n, axis=1)


def _mod_kernel(c_ref, w_ref, b_ref, o_ref):
    c = c_ref[...]
    s = c * jax.nn.sigmoid(c)
    o_ref[...] = jnp.dot(s, w_ref[...], precision=HIGHEST, preferred_element_type=F32) + b_ref[...]


def _modulation(c_rows, w_mod, b_mod):
    n_col = 6 * D_MODEL // 1024
    return pl.pallas_call(
        _mod_kernel,
        out_shape=jax.ShapeDtypeStruct((DEPTH, 8, 6 * D_MODEL), F32),
        grid=(DEPTH, n_col),
        in_specs=[
            pl.BlockSpec((8, D_MODEL), lambda l, j: (0, 0)),
            pl.BlockSpec((None, D_MODEL, 1024), lambda l, j: (l, 0, j)),
            pl.BlockSpec((None, 1, 1024), lambda l, j: (l, 0, j)),
        ],
        out_specs=pl.BlockSpec((None, 8, 1024), lambda l, j: (l, 0, j)),
        compiler_params=_cparams("arbitrary", "arbitrary"),
        name="modulation",
    )(c_rows, w_mod, b_mod.reshape(DEPTH, 1, 6 * D_MODEL))


def _attn_a_kernel(*refs, seq, tq, n_cache, rope, proj_rows):
    if rope:
        (x_ref, mod_ref, w_ref, gain_ref, bd_ref, rep_ref, cos_ref, sin_ref, ck_ref, cv_ref,
         y_ref, q_s, k_s, v_s) = refs
    else:
        (x_ref, mod_ref, w_ref, gain_ref, bd_ref, rep_ref) = refs[:6]
        (y_ref, nk_ref, nv_ref, q_s, k_s, v_s) = refs[-6:]
    qi = pl.program_id(1)

    @pl.when(qi == 0)
    def _project():
        lane = lax.broadcasted_iota(jnp.int32, (1, 640), 1)
        for r0 in range(0, seq, proj_rows):
            rows = pl.ds(r0, proj_rows)
            h = _modulated(x_ref[rows, :], mod_ref, 0).astype(BF16)
            p = _dot(h, w_ref[...])
            qk = p[:, :640]
            sq = qk * qk
            hi = sq.astype(BF16)
            lo = (sq - hi.astype(F32)).astype(BF16)
            ms = jnp.concatenate([_dot(hi[:, c:c + 128], bd_ref[...]) + _dot(lo[:, c:c + 128], bd_ref[...])
                                  for c in range(0, 640, 128)], axis=1)
            qk = qk * lax.rsqrt(ms + EPS) * gain_ref[...]
            v = p[:, 640:768]
            if not rope:
                nk_ref[rows, :] = qk[:, 512:640]
                nv_ref[rows, :] = v
            else:
                cos = _tile_lanes(cos_ref[rows, :], 5)
                sin = _tile_lanes(sin_ref[rows, :], 5)
                qk = _rope(qk, cos, sin, lane)
            q_s[rows, :] = (qk[:, :512] * (A_HEAD_DIM ** -0.5)).astype(BF16)
            k_s[rows, :] = _dot(qk[:, 512:640].astype(BF16), rep_ref[...]).astype(BF16)
            v_s[rows, :] = _dot(v.astype(BF16), rep_ref[...]).astype(BF16)
        if n_cache:
            crow = pl.ds(seq, n_cache)
            k_s[crow, :] = _dot(ck_ref[...].astype(BF16), rep_ref[...]).astype(BF16)
            v_s[crow, :] = _dot(cv_ref[...].astype(BF16), rep_ref[...]).astype(BF16)

    head_of_lane = lax.broadcasted_iota(jnp.int32, (1, 256), 1) // A_HEAD_DIM
    qb = q_s[pl.ds(pl.multiple_of(qi * tq, tq), tq), :]
    for g in range(A_KV_HEADS):
        cols = slice(g * 256, (g + 1) * 256)
        qg = qb[:, cols]
        kg = k_s[:, cols]
        vg = v_s[:, cols]
        acc = jnp.zeros((tq, 256), F32)
        for r in range(A_HEADS // A_KV_HEADS):
            sel = head_of_lane == r
            qm = jnp.where(sel, qg, jnp.zeros_like(qg))
            s = _dot_nt(qm, kg)
            m = jnp.max(s, axis=-1, keepdims=True)
            p = jnp.exp(s - m)
            l = jnp.sum(p, axis=-1, keepdims=True)
            o = _dot(p.astype(BF16), vg)
            acc = acc + jnp.where(sel, o * (1.0 / l), 0.0)
        y_ref[:, cols] = acc.astype(BF16)


def _carry_aliases(in_specs, args, carry, first_out):
    if carry is None:
        return {}
    aliases = {}
    for k, arr in enumerate(carry):
        aliases[len(args)] = first_out + k
        in_specs.append(pl.BlockSpec(memory_space=pl.ANY))
        args.append(arr)
    return aliases


def _attn_a(x, mod, w_a, gain_row, bd, rep, *, n_seq, seq, tq, rope=None, cache=None, layer=0, carry=None):
    n_q = seq // tq
    n_cache = 0 if cache is None else cache[0].shape[2]
    proj_rows = min(seq, 512)
    kern = functools.partial(_attn_a_kernel, seq=seq, tq=tq, n_cache=n_cache, rope=rope is not None,
                             proj_rows=proj_rows)
    const = lambda s, q: (0, 0)
    in_specs = [
        pl.BlockSpec((seq, D_MODEL), lambda s, q: (s, 0)),
        pl.BlockSpec((None, 1, 6 * D_MODEL), lambda s, q: (s if mod.shape[0] > 1 else 0, 0, 0)),
        pl.BlockSpec(w_a.shape, const),
        pl.BlockSpec(gain_row.shape, const),
        pl.BlockSpec(bd.shape, const),
        pl.BlockSpec(rep.shape, const),
    ]
    args = [x, mod, w_a, gain_row, bd, rep]
    n_tok = n_seq * seq
    y_spec = pl.BlockSpec((tq, BRANCH_WIDTH), lambda s, q: (s * n_q + q, 0))
    y_shape = jax.ShapeDtypeStruct((n_tok, BRANCH_WIDTH), BF16)
    if rope is not None:
        cos, sin = rope
        in_specs += [pl.BlockSpec(cos.shape, const), pl.BlockSpec(sin.shape, const),
                     pl.BlockSpec((None, None, n_cache, 128), lambda s, q: (s, layer, 0, 0)),
                     pl.BlockSpec((None, None, n_cache, 128), lambda s, q: (s, layer, 0, 0))]
        args += [cos, sin, cache[0], cache[1]]
        out_shape, out_specs, aliases = y_shape, y_spec, {}
    else:
        kv_shape = jax.ShapeDtypeStruct((n_seq, DEPTH, seq, 128), F32)
        kv_spec = pl.BlockSpec((None, None, seq, 128), lambda s, q: (s, layer, 0, 0))
        out_shape, out_specs = (y_shape, kv_shape, kv_shape), (y_spec, kv_spec, kv_spec)
        aliases = _carry_aliases(in_specs, args, carry, first_out=1)
    return pl.pallas_call(
        kern, out_shape=out_shape, grid=(n_seq, n_q), in_specs=in_specs, out_specs=out_specs,
        input_output_aliases=aliases,
        scratch_shapes=[pltpu.VMEM((seq, 512), BF16), pltpu.VMEM((seq + n_cache, 512), BF16),
                        pltpu.VMEM((seq + n_cache, 512), BF16)],
        compiler_params=_cparams("arbitrary", "arbitrary"),
        name="branch_a_lat" if rope is not None else "branch_a_ctx",
    )(*args)


def _attn_d_kernel(*refs, seq, tq, n_cache, rope, proj_rows, lam_init):
    if rope:
        (x_ref, mod_ref, w_ref, lam_ref, gn_ref, cos_ref, sin_ref, ck_ref, cv_ref,
         y_ref, q_s, k_s, v_s) = refs
    else:
        (x_ref, mod_ref, w_ref, lam_ref, gn_ref) = refs[:5]
        (y_ref, nk_ref, nv_ref, q_s, k_s, v_s) = refs[-6:]
    qi = pl.program_id(1)

    @pl.when(qi == 0)
    def _project():
        lane = lax.broadcasted_iota(jnp.int32, (1, 512), 1)
        for r0 in range(0, seq, proj_rows):
            rows = pl.ds(r0, proj_rows)
            h = _modulated(x_ref[rows, :], mod_ref, 0).astype(BF16)
            p = _dot(h, w_ref[...])
            dq, dk, dv = p[:, :512], p[:, 512:1024], p[:, 1024:1536]
            if not rope:
                nk_ref[rows, :] = dk
                nv_ref[rows, :] = dv
            else:
                cos = _tile_lanes(cos_ref[rows, :], 4)
                sin = _tile_lanes(sin_ref[rows, :], 4)
                dq = _rope(dq, cos, sin, lane)
                dk = _rope(dk, cos, sin, lane)
            q_s[rows, :] = (dq * (D_HALF_DIM ** -0.5)).astype(BF16)
            k_s[rows, :] = dk.astype(BF16)
            v_s[rows, :] = dv.astype(BF16)
        if n_cache:
            crow = pl.ds(seq, n_cache)
            k_s[crow, :] = ck_ref[...].astype(BF16)
            v_s[crow, :] = cv_ref[...].astype(BF16)

    lv = lam_ref[...]
    lam = (jnp.exp(jnp.sum(lv[0:1] * lv[1:2], axis=-1, keepdims=True))
           - jnp.exp(jnp.sum(lv[2:3] * lv[3:4], axis=-1, keepdims=True)) + lam_init)
    half_of_lane = lax.broadcasted_iota(jnp.int32, (1, 128), 1) // D_HALF_DIM
    qb = q_s[pl.ds(pl.multiple_of(qi * tq, tq), tq), :]
    for hd in range(D_HEADS):
        cols = slice(hd * 128, (hd + 1) * 128)
        qh = qb[:, cols]
        kh = k_s[:, cols]
        vh = v_s[:, cols]
        probs = []
        for j in range(2):
            qm = jnp.where(half_of_lane == j, qh, jnp.zeros_like(qh))
            s = _dot_nt(qm, kh)
            m = jnp.max(s, axis=-1, keepdims=True)
            p = jnp.exp(s - m)
            l = jnp.sum(p, axis=-1, keepdims=True)
            probs.append(p * (1.0 / l))
        a = (probs[0] - lam * probs[1]).astype(BF16)
        o = _dot(a, vh)
        ms = jnp.mean(o * o, axis=-1, keepdims=True)
        o = o * lax.rsqrt(ms + EPS) * gn_ref[:, cols] * (1.0 - lam_init)
        y_ref[:, cols] = o.astype(BF16)


def _attn_d(x, mod, w_d, lam_params, gn_row, *, n_seq, seq, tq, lam_init, rope=None, cache=None, layer=0,
            carry=None):
    n_q = seq // tq
    n_cache = 0 if cache is None else cache[0].shape[2]
    proj_rows = min(seq, 512)
    kern = functools.partial(_attn_d_kernel, seq=seq, tq=tq, n_cache=n_cache, rope=rope is not None,
                             proj_rows=proj_rows, lam_init=lam_init)
    const = lambda s, q: (0, 0)
    in_specs = [
        pl.BlockSpec((seq, D_MODEL), lambda s, q: (s, 0)),
        pl.BlockSpec((None, 1, 6 * D_MODEL), lambda s, q: (s if mod.shape[0] > 1 else 0, 0, 0)),
        pl.BlockSpec(w_d.shape, const),
        pl.BlockSpec(lam_params.shape, const),
        pl.BlockSpec(gn_row.shape, const),
    ]
    args = [x, mod, w_d, lam_params, gn_row]
    n_tok = n_seq * seq
    y_spec = pl.BlockSpec((tq, BRANCH_WIDTH), lambda s, q: (s * n_q + q, 0))
    y_shape = jax.ShapeDtypeStruct((n_tok, BRANCH_WIDTH), BF16)
    if rope is not None:
        cos, sin = rope
        in_specs += [pl.BlockSpec(cos.shape, const), pl.BlockSpec(sin.shape, const),
                     pl.BlockSpec((None, None, n_cache, 512), lambda s, q: (s, layer, 0, 0)),
                     pl.BlockSpec((None, None, n_cache, 512), lambda s, q: (s, layer, 0, 0))]
        args += [cos, sin, cache[0], cache[1]]
        out_shape, out_specs, aliases = y_shape, y_spec, {}
    else:
        kv_shape = jax.ShapeDtypeStruct((n_seq, DEPTH, seq, 512), F32)
        kv_spec = pl.BlockSpec((None, None, seq, 512), lambda s, q: (s, layer, 0, 0))
        out_shape, out_specs = (y_shape, kv_shape, kv_shape), (y_spec, kv_spec, kv_spec)
        aliases = _carry_aliases(in_specs, args, carry, first_out=1)
    return pl.pallas_call(
        kern, out_shape=out_shape, grid=(n_seq, n_q), in_specs=in_specs, out_specs=out_specs,
        input_output_aliases=aliases,
        scratch_shapes=[pltpu.VMEM((seq, 512), BF16), pltpu.VMEM((seq + n_cache, 512), BF16),
                        pltpu.VMEM((seq + n_cache, 512), BF16)],
        compiler_params=_cparams("arbitrary", "arbitrary"),
        name="branch_d_lat" if rope is not None else "branch_d_ctx",
    )(*args)


def _gmlp_kernel(x_ref, mod_ref, w_ref, ws_ref, bias_ref, y_ref, *, rows):
    h = _modulated(x_ref[...], mod_ref, 0).astype(BF16)
    p = _dot(h, w_ref[...])
    u, v = p[:, :BRANCH_WIDTH], p[:, BRANCH_WIDTH:]
    mu = jnp.mean(v, axis=-1, keepdims=True)
    vc = v - mu
    var = jnp.mean(vc * vc, axis=-1, keepdims=True)
    vn = (vc * lax.rsqrt(var + EPS)).astype(BF16)
    for c in range(rows // CHUNK):
        rs = slice(c * CHUNK, (c + 1) * CHUNK)
        for g in range(B_GROUPS):
            cs = slice(g * 128, (g + 1) * 128)
            s = _dot(ws_ref[g].astype(BF16), vn[rs, cs]) + bias_ref[:, cs]
            y_ref[rs, cs] = (u[rs, cs] * s).astype(BF16)


def _gmlp(x, mod, w_b, ws, bias_full, *, seq, rows):
    n_tok = x.shape[0]
    per_seq = seq // rows
    kern = functools.partial(_gmlp_kernel, rows=rows)
    return pl.pallas_call(
        kern, out_shape=jax.ShapeDtypeStruct((n_tok, BRANCH_WIDTH), BF16),
        grid=(n_tok // rows,),
        in_specs=[
            pl.BlockSpec((rows, D_MODEL), lambda i: (i, 0)),
            pl.BlockSpec((None, 1, 6 * D_MODEL), lambda i: (i // per_seq if mod.shape[0] > 1 else 0, 0, 0)),
            pl.BlockSpec(w_b.shape, lambda i: (0, 0)),
            pl.BlockSpec(ws.shape, lambda i: (0, 0, 0)),
            pl.BlockSpec(bias_full.shape, lambda i: (0, 0)),
        ],
        out_specs=pl.BlockSpec((rows, BRANCH_WIDTH), lambda i: (i, 0)),
        compiler_params=_cparams("arbitrary"),
        name="branch_b",
    )(x, mod, w_b, ws, bias_full)


def _mlstm_t_kernel(*refs, seq, has_init, proj_rows):
    if has_init:
        (x_ref, mod_ref, wn_ref, wt_ref, gbr_ref, gbt_ref, gnt_ref, tril_ref, triu_ref, c0_ref, n0_ref, m0_ref,
         y_ref, k_s, qt_s, vt_s, ot_s, g_s, gt_s, hf_s, hb_s, st_s, m_s) = refs
    else:
        (x_ref, mod_ref, wn_ref, wt_ref, gbr_ref, gbt_ref, gnt_ref, tril_ref, triu_ref) = refs[:9]
        (y_ref, cout_ref, nout_ref, mout_ref,
         k_s, qt_s, vt_s, ot_s, g_s, gt_s, hf_s, hb_s, st_s, m_s) = refs[-14:]
    n_chunk = seq // CHUNK
    for r0 in range(0, seq, proj_rows):
        span = pl.ds(r0, proj_rows)
        h = _modulated(x_ref[span, :], mod_ref, 0).astype(BF16)
        p = _dot(h, wn_ref[...])
        k_s[span, :] = (p[:, 0:512] * (C_HEAD_DIM ** -0.5)).astype(BF16)
        g_s[span, :] = p[:, 512:640] + gbr_ref[...]
        pt = _dot_nt(wt_ref[...], h)
        qt_s[:, span] = pt[0:512].astype(BF16)
        vt_s[:, span] = pt[512:1024].astype(BF16)
        ot_s[:, span] = jax.nn.sigmoid(pt[1024:1536]).astype(BF16)
        gt_s[:, span] = pt[1536:1664] + _tile_lanes(gbt_ref[...], proj_rows // 128)

    first_row = lax.broadcasted_iota(jnp.int32, (CHUNK, CHUNK), 0) == 0
    for sidx in range(2 * C_HEADS):
        if has_init:
            st_s[sidx, 0:128, :] = c0_ref[sidx].T
            st_s[sidx, 128:256, :] = jnp.where(first_row, jnp.broadcast_to(n0_ref[sidx:sidx + 1, :], (CHUNK, 128)), 0.0)
        else:
            st_s[sidx] = jnp.zeros((2 * CHUNK, 128), F32)
    m_s[...] = m0_ref[...] if has_init else jnp.zeros_like(m_s)

    tril = tril_ref[...]
    triu = triu_ref[...]
    row_i = lax.broadcasted_iota(jnp.int32, (CHUNK, CHUNK), 0)
    col_i = lax.broadcasted_iota(jnp.int32, (CHUNK, CHUNK), 1)
    visible = (row_i <= col_i, row_i >= col_i)
    ones_blk = jnp.where(first_row, 1.0, 0.0).astype(BF16)

    def chunk_step(c, carry):
        prep = []
        for direction in range(2):
            cc = c if direction == 0 else n_chunk - 1 - c
            span = pl.ds(pl.multiple_of(cc * CHUNK, CHUNK), CHUNK)
            gates = g_s[span, :]
            gates_t = gt_s[:, span]
            logf = jnp.minimum(gates, 0.0) - jnp.log1p(jnp.exp(-jnp.abs(gates)))
            logf_t = jnp.minimum(gates_t, 0.0) - jnp.log1p(jnp.exp(-jnp.abs(gates_t)))
            tri_col = tril if direction == 0 else triu
            tri_row = triu if direction == 0 else tril
            b_col_all = jnp.dot(tri_col, logf, precision=HIGHEST, preferred_element_type=F32)
            b_row_all = jnp.dot(logf_t, tri_row, precision=HIGHEST, preferred_element_type=F32)
            r_col_all = gates - pltpu.roll(b_col_all, 128 - 4, 1)
            base = direction * 8
            prep.append((span, r_col_all, gates_t[base:base + 4, :], b_row_all[base + 4:base + 8, :]))
        for hd in range(C_HEADS):
            for direction in range(2):
                span, r_col_all, i_rows, b_rows = prep[direction]
                base = direction * 8
                last = CHUNK - 1 if direction == 0 else 0
                h_out = hf_s if direction == 0 else hb_s
                sidx = direction * 4 + hd
                blk = slice(hd * 128, (hd + 1) * 128)
                r_col = r_col_all[:, base + hd:base + hd + 1]
                b_row = b_rows[hd:hd + 1, :]
                i_row = i_rows[hd:hd + 1, :]
                m_prev = m_s[sidx:sidx + 1, 0:1]
                r_wide = jnp.broadcast_to(r_col, (CHUNK, CHUNK))
                peak = jnp.max(jnp.where(visible[direction], r_wide, NEG_INF), axis=0, keepdims=True)
                m_row = b_row + jnp.maximum(peak, m_prev)
                kh = k_s[span, blk]
                q_t = qt_s[blk, span]
                arg = jnp.where(visible[direction], r_wide + (b_row - m_row), NEG_INF)
                w_t = (_dot(kh, q_t) * jnp.exp(arg)).astype(BF16)
                inter = jnp.exp(b_row + m_prev - m_row)
                v_one = jnp.concatenate([vt_s[blk, span], ones_blk], axis=0)
                state = st_s[sidx]
                both = _dot(v_one, w_t) + inter * _dot(state.astype(BF16), q_t)
                den = both[128:129, :]
                h_out[blk, span] = both[0:128, :] / jnp.maximum(jnp.abs(den), jnp.exp(-m_row))
                b_last = b_row[:, last:last + 1]
                g_row = b_last - b_row + i_row
                m_new = jnp.maximum(b_last + m_prev, jnp.max(g_row, axis=-1, keepdims=True))
                decay = jnp.exp(b_last + m_prev - m_new)
                scaled = v_one * jnp.exp(g_row - m_new).astype(BF16)
                st_s[sidx] = decay * state + _dot(scaled, kh)
                m_s[sidx:sidx + 1, :] = jnp.broadcast_to(m_new, (1, 128))
        return carry

    lax.fori_loop(0, n_chunk, chunk_step, 0)

    for r0 in range(0, seq, proj_rows):
        span = pl.ds(r0, proj_rows)
        for hd in range(C_HEADS):
            blk = slice(hd * 128, (hd + 1) * 128)
            hh = hf_s[blk, span] + hb_s[blk, span]
            mu = jnp.mean(hh, axis=0, keepdims=True)
            hc = hh - mu
            var = jnp.mean(hc * hc, axis=0, keepdims=True)
            gn = _tile_lanes(gnt_ref[blk, :], proj_rows // 128)
            y_t = hc * lax.rsqrt(var + EPS) * gn * ot_s[blk, span].astype(F32)
            y_ref[span, blk] = y_t.T.astype(BF16)
    if not has_init:
        for sidx in range(2 * C_HEADS):
            cout_ref[sidx] = st_s[sidx, 0:128, :].T
            nout_ref[sidx:sidx + 1, :] = st_s[sidx, 128:129, :]
        mout_ref[...] = m_s[...]


def _mlstm_t(x, mod, w_nat, w_t, gate_bias_row, gate_bias_t, gn_t, tril, triu, *, n_seq, seq, init=None, layer=0,
             carry=None):
    proj_rows = min(seq, 512)
    kern = functools.partial(_mlstm_t_kernel, seq=seq, has_init=init is not None, proj_rows=proj_rows)
    const = lambda s: (0, 0)
    in_specs = [
        pl.BlockSpec((seq, D_MODEL), lambda s: (s, 0)),
        pl.BlockSpec((None, 1, 6 * D_MODEL), lambda s: (s if mod.shape[0] > 1 else 0, 0, 0)),
    ] + [pl.BlockSpec(a.shape, const) for a in (w_nat, w_t, gate_bias_row, gate_bias_t, gn_t, tril, triu)]
    args = [x, mod, w_nat, w_t, gate_bias_row, gate_bias_t, gn_t, tril, triu]
    n_tok = n_seq * seq
    y_shape = jax.ShapeDtypeStruct((n_tok, BRANCH_WIDTH), BF16)
    y_spec = pl.BlockSpec((seq, BRANCH_WIDTH), lambda s: (s, 0))
    if init is not None:
        c0, n0, m0 = init
        in_specs += [pl.BlockSpec((None, None, 8, 128, 128), lambda s: (s, layer, 0, 0, 0)),
                     pl.BlockSpec((None, None, 8, 128), lambda s: (s, layer, 0, 0)),
                     pl.BlockSpec((None, None, 8, 128), lambda s: (s, layer, 0, 0))]
        args += [c0, n0, m0]
        out_shape, out_specs, aliases = y_shape, y_spec, {}
    else:
        out_shape = (y_shape, jax.ShapeDtypeStruct((n_seq, DEPTH, 8, 128, 128), F32),
                     jax.ShapeDtypeStruct((n_seq, DEPTH, 8, 128), F32),
                     jax.ShapeDtypeStruct((n_seq, DEPTH, 8, 128), F32))
        out_specs = (y_spec, pl.BlockSpec((None, None, 8, 128, 128), lambda s: (s, layer, 0, 0, 0)),
                     pl.BlockSpec((None, None, 8, 128), lambda s: (s, layer, 0, 0)),
                     pl.BlockSpec((None, None, 8, 128), lambda s: (s, layer, 0, 0)))
        aliases = _carry_aliases(in_specs, args, carry, first_out=1)
    return pl.pallas_call(
        kern, out_shape=out_shape, grid=(n_seq,), in_specs=in_specs, out_specs=out_specs,
        input_output_aliases=aliases,
        scratch_shapes=[pltpu.VMEM((seq, 512), BF16),
                        pltpu.VMEM((512, seq), BF16), pltpu.VMEM((512, seq), BF16),
                        pltpu.VMEM((512, seq), BF16),
                        pltpu.VMEM((seq, 128), F32), pltpu.VMEM((128, seq), F32),
                        pltpu.VMEM((512, seq), F32), pltpu.VMEM((512, seq), F32),
                        pltpu.VMEM((8, 256, 128), F32), pltpu.VMEM((8, 128), F32)],
        compiler_params=_cparams("arbitrary"),
        name="branch_c_lat" if init is not None else "branch_c_ctx",
    )(*args)


def _merge_kernel(x_ref, mod_ref, ya_ref, yb_ref, yc_ref, yd_ref, wg_ref, wbr_ref, wout_ref, lng_ref, lnb_ref,
                  o_ref):
    x = x_ref[...]
    h = _modulated(x, mod_ref, 0).astype(BF16)
    mix = None
    for n, y_ref in enumerate((ya_ref, yb_ref, yc_ref, yd_ref)):
        gate = jax.nn.sigmoid(_dot(h, wg_ref[:, n * D_MODEL:(n + 1) * D_MODEL]))
        term = gate * _dot(y_ref[...], wbr_ref[n])
        mix = term if mix is None else mix + term
    out = _dot(mix.astype(BF16), wout_ref[...])
    g1 = mod_ref[:, 2 * D_MODEL:3 * D_MODEL]
    o_ref[...] = _layer_norm_rows(ALPHA * x + g1 * out, lng_ref[...], lnb_ref[...])


def _merge(x, mod, ys, w_g, w_br, w_out, ln_g, ln_b, *, seq, rows):
    n_tok = x.shape[0]
    per_seq = seq // rows
    tok = lambda i: (i, 0)
    c2 = lambda i: (0, 0)
    return pl.pallas_call(
        _merge_kernel, out_shape=jax.ShapeDtypeStruct((n_tok, D_MODEL), F32),
        grid=(n_tok // rows,),
        in_specs=[
            pl.BlockSpec((rows, D_MODEL), tok),
            pl.BlockSpec((None, 1, 6 * D_MODEL), lambda i: (i // per_seq if mod.shape[0] > 1 else 0, 0, 0)),
            pl.BlockSpec((rows, BRANCH_WIDTH), tok), pl.BlockSpec((rows, BRANCH_WIDTH), tok),
            pl.BlockSpec((rows, BRANCH_WIDTH), tok), pl.BlockSpec((rows, BRANCH_WIDTH), tok),
            pl.BlockSpec(w_g.shape, c2), pl.BlockSpec(w_br.shape, lambda i: (0, 0, 0)),
            pl.BlockSpec(w_out.shape, c2), pl.BlockSpec(ln_g.shape, c2), pl.BlockSpec(ln_b.shape, c2),
        ],
        out_specs=pl.BlockSpec((rows, D_MODEL), tok),
        compiler_params=_cparams("arbitrary"),
        name="merge",
    )(x, mod, *ys, w_g, w_br, w_out, ln_g, ln_b)


_TAKEN = -(2.0 ** 127)


def _top16(s):
    cur = s
    vals = []
    for r in range(PEER_TOPK):
        mx = jnp.max(cur, axis=0, keepdims=True)
        cur = jnp.where(cur == mx, _TAKEN * (1.0 + r / 32.0), cur)
        vals.append(mx)
    rank = jnp.where(cur <= _TAKEN, cur * (32.0 / _TAKEN) - 31.0, float(PEER_TOPK + 1))
    return jnp.concatenate(vals, axis=0), rank


def _pair_tables():
    pairs = [(k1, k2) for k1 in range(PEER_TOPK) for k2 in range(PEER_TOPK // (k1 + 1))]
    n = 56
    sel_a = np.zeros((n, PEER_TOPK), np.float32)
    sel_b = np.zeros((n, PEER_TOPK), np.float32)
    pad = np.full((n, 1), NEG_INF, np.float32)
    for row, (k1, k2) in enumerate(pairs):
        sel_a[row, k1] = 1.0
        sel_b[row, k2] = 1.0
        pad[row, 0] = 0.0
    return jnp.asarray(sel_a), jnp.asarray(sel_b), jnp.asarray(pad), jnp.asarray(sel_a.T, BF16)


def _route_kernel(x_ref, mod_ref, wq_ref, keys_ref, sela_ref, selb_ref, pad_ref, ind_ref,
                  cnt_ref, e1_ref, r2_ref, e2_ref, h2_s, *, heads):
    @pl.when(pl.program_id(1) == 0)
    def _modulate():
        h2_s[...] = _modulated(x_ref[...], mod_ref, 1).astype(BF16)

    pick = lambda sel_ref, v: jnp.dot(sel_ref[...], v, precision=HIGHEST, preferred_element_type=F32)
    q_all = _dot(h2_s[...], wq_ref[...])
    for hd in range(heads):
        q = q_all[:, hd * PEER_QDIM:(hd + 1) * PEER_QDIM]
        s1 = _dot_nt(keys_ref[hd, 0].astype(BF16), q[:, :128].astype(BF16))
        s2 = _dot_nt(keys_ref[hd, 1].astype(BF16), q[:, 128:].astype(BF16))
        a, rank1 = _top16(s1)
        b, rank2 = _top16(s2)
        ea = jnp.exp(a - a[0:1])
        eb = jnp.exp(b - b[0:1])
        cand = pick(sela_ref, a) + pick(selb_ref, b) + pad_ref[...]
        gate = pick(sela_ref, ea) * pick(selb_ref, eb)
        cur = cand
        thr = None
        for _ in range(PEER_TOPK):
            thr = jnp.max(cur, axis=0, keepdims=True)
            cur = jnp.where(cur == thr, NEG_INF, cur)
        chosen = cand >= thr
        z = jnp.sum(jnp.where(chosen, gate, 0.0), axis=0, keepdims=True)
        cnt_sorted = _dot(ind_ref[...], jnp.where(chosen, 1.0, 0.0).astype(BF16))
        rank1_b = rank1.astype(BF16)
        counts_b = cnt_sorted.astype(BF16)
        cnt = jnp.zeros(s1.shape, BF16)
        for r in range(PEER_TOPK):
            cnt = jnp.where(rank1_b == float(r + 1), counts_b[r:r + 1], cnt)
        cnt_ref[hd] = cnt.astype(F32)
        e1_ref[hd] = jnp.where(rank1 <= float(PEER_TOPK), jnp.exp(s1 - a[0:1]) * (0.5 / z), 0.0)
        packed = (PEER_NKEYS // 16, 16, s2.shape[1])
        r2_ref[hd] = rank2.astype(BF16).reshape(packed)
        e2_ref[hd] = jnp.where(rank2 <= float(PEER_TOPK), jnp.exp(s2 - b[0:1]), 0.0).astype(BF16).reshape(packed)


def _route(x1, mod, wq, keys, *, seq, cols, heads):
    n_tok = x1.shape[0]
    per_seq = seq // cols
    tables = _pair_tables()
    row_shape = jax.ShapeDtypeStruct((PEER_HEADS, PEER_NKEYS, n_tok), F32)
    col_shape = jax.ShapeDtypeStruct((PEER_HEADS, PEER_NKEYS // 16, 16, n_tok), BF16)
    col_spec = pl.BlockSpec((heads, PEER_NKEYS // 16, 16, cols), lambda i, h: (h, 0, 0, i))
    spec = pl.BlockSpec((heads, PEER_NKEYS, cols), lambda i, h: (h, 0, i))
    return pl.pallas_call(
        functools.partial(_route_kernel, heads=heads), out_shape=(row_shape, row_shape, col_shape, col_shape),
        grid=(n_tok // cols, PEER_HEADS // heads),
        in_specs=[
            pl.BlockSpec((cols, D_MODEL), lambda i, h: (i, 0)),
            pl.BlockSpec((None, 1, 6 * D_MODEL), lambda i, h: (i // per_seq if mod.shape[0] > 1 else 0, 0, 0)),
            pl.BlockSpec((D_MODEL, heads * PEER_QDIM), lambda i, h: (0, h)),
            pl.BlockSpec((heads, 2, PEER_NKEYS, PEER_QDIM // 2), lambda i, h: (h, 0, 0, 0)),
        ] + [pl.BlockSpec(t.shape, lambda i, h: (0, 0)) for t in tables],
        out_specs=(spec, spec, col_spec, col_spec),
        scratch_shapes=[pltpu.VMEM((cols, D_MODEL), BF16)],
        compiler_params=_cparams("arbitrary", "arbitrary"),
        name="peer_route",
    )(x1, mod, wq, keys, *tables)


def _peer_kernel(x_ref, mod_ref, u_ref, v_ref, cnt_ref, e1_ref, r2_ref, e2_ref, lng_ref, lnb_ref,
                 o_ref, h2t_s, acc_s, *, key_rows):
    e = pl.program_id(1)

    @pl.when(e == 0)
    def _init():
        h2 = _modulated(x_ref[...], mod_ref, 1)
        h2t_s[...] = h2.T.astype(BF16)
        acc_s[...] = jnp.zeros_like(acc_s)

    n_tok = h2t_s.shape[1]
    first_key = pl.multiple_of(e * key_rows, key_rows)
    zero = jnp.zeros((PEER_NKEYS // 16, 16, n_tok), BF16)
    act = _dot(u_ref[...], h2t_s[...]).astype(BF16)
    act = act * (1.0 + lax.erf(act * (2.0 ** -0.5)))
    pieces = []
    for r in range(key_rows):
        g = None
        for hd in range(PEER_HEADS):
            cnt_blk = cnt_ref[hd, pl.ds(first_key, key_rows), :]
            e1_blk = e1_ref[hd, pl.ds(first_key, key_rows), :]
            cnt_rows = jnp.broadcast_to(cnt_blk[r:r + 1, :], (16, n_tok)).astype(BF16)
            e1_rows = jnp.broadcast_to(e1_blk[r:r + 1, :], (16, n_tok)).astype(BF16)
            term = jnp.where(r2_ref[hd] <= cnt_rows[None], e2_ref[hd], zero) * e1_rows[None]
            g = term if g is None else g + term
        pieces.append(g.reshape(PEER_NKEYS, n_tok) * act[r * PEER_NKEYS:(r + 1) * PEER_NKEYS, :])
    acc_s[...] += _dot(v_ref[...], jnp.concatenate(pieces, axis=0))

    @pl.when(e == pl.num_programs(1) - 1)
    def _finish():
        x = x_ref[...]
        g2 = mod_ref[:, 5 * D_MODEL:6 * D_MODEL]
        o_ref[...] = _layer_norm_rows(ALPHA * x + g2 * acc_s[...].T, lng_ref[...], lnb_ref[...])


def _peer(x1, mod, u, v, route, ln_g, ln_b, *, seq, cols, key_rows, layer):
    n_tok = x1.shape[0]
    per_seq = seq // cols
    n_exp = key_rows * PEER_NKEYS
    n_tiles = PEER_EXPERTS // n_exp
    assert key_rows % 8 == 0
    kern = functools.partial(_peer_kernel, key_rows=key_rows)
    cnt, e1, r2, e2 = route
    rspec = pl.BlockSpec((PEER_HEADS, PEER_NKEYS, cols), lambda i, e: (0, 0, i))
    cspec = pl.BlockSpec((PEER_HEADS, PEER_NKEYS // 16, 16, cols), lambda i, e: (0, 0, 0, i))
    return pl.pallas_call(
        kern, out_shape=jax.ShapeDtypeStruct((n_tok, D_MODEL), F32),
        grid=(n_tok // cols, n_tiles),
        in_specs=[
            pl.BlockSpec((cols, D_MODEL), lambda i, e: (i, 0)),
            pl.BlockSpec((None, 1, 6 * D_MODEL), lambda i, e: (i // per_seq if mod.shape[0] > 1 else 0, 0, 0)),
            pl.BlockSpec((None, n_exp, D_MODEL), lambda i, e: (layer, e, 0)),
            pl.BlockSpec((None, D_MODEL, n_exp), lambda i, e: (layer, 0, e)),
            rspec, rspec, cspec, cspec,
            pl.BlockSpec(ln_g.shape, lambda i, e: (0, 0)), pl.BlockSpec(ln_b.shape, lambda i, e: (0, 0)),
        ],
        out_specs=pl.BlockSpec((cols, D_MODEL), lambda i, e: (i, 0)),
        scratch_shapes=[pltpu.VMEM((D_MODEL, cols), BF16), pltpu.VMEM((D_MODEL, cols), F32)],
        compiler_params=_cparams("arbitrary", "arbitrary"),
        name="peer_experts",
    )(x1, mod, u, v, cnt, e1, r2, e2, ln_g, ln_b)


def _rope_tables(seq):
    t = np.arange(seq)
    pos = np.stack([t // GRID_W, t % GRID_W], axis=1).astype(np.float64)
    inv = ROPE_BASE ** (-np.arange(16, dtype=np.float64) / 16)
    lane = np.arange(64)
    ang = pos[:, lane // 32] * inv[lane % 16][None, :]
    sign = np.where((lane % 32) < 16, -1.0, 1.0)[None, :]
    cos = np.tile(np.cos(ang), (1, 2)).astype(np.float32)
    sin = np.tile(np.sin(ang) * sign, (1, 2)).astype(np.float32)
    return jnp.asarray(cos), jnp.asarray(sin)


def _static_tables():
    lane = np.arange(128)
    bd = (lane[:, None] // 64 == lane[None, :] // 64).astype(np.float32) / 64.0
    src = np.arange(128)
    dst = np.arange(512)
    rep = ((src[:, None] // 64 == dst[None, :] // 256) & (src[:, None] % 64 == dst[None, :] % 64))
    idx = np.arange(CHUNK)
    tril = (idx[None, :] <= idx[:, None]).astype(np.float32)
    triu = (idx[None, :] >= idx[:, None]).astype(np.float32)
    return (jnp.asarray(bd, BF16), jnp.asarray(rep.astype(np.float32), BF16), jnp.asarray(tril), jnp.asarray(triu))


def _layer_params(l, w_in, attn_qk_gain, gmlp_ws, gmlp_b, mlstm_gate_bias, mlstm_gn, diff_lambda, diff_gn,
                  w_branch, w_out, ln_g, ln_b, peer_wq, peer_keys, peer_u, peer_v):
    w = w_in[l]
    p = {}
    p["w_a"] = w[:, _OFF_A:_OFF_B].astype(BF16)
    p["w_b"] = w[:, _OFF_B:_OFF_C].astype(BF16)
    cq, ck = w[:, _OFF_C:_OFF_C + 512], w[:, _OFF_C + 512:_OFF_C + 1024]
    cv, co = w[:, _OFF_C + 1024:_OFF_C + 1536], w[:, _OFF_C + 1536:_OFF_C + 2048]
    cg = jnp.concatenate([w[:, _OFF_CG:_OFF_D], jnp.zeros((D_MODEL, 112), F32)], axis=1)
    p["w_c_nat"] = jnp.concatenate([ck, cg], axis=1).astype(BF16)
    p["w_c_t"] = jnp.concatenate([cq, cv, co, cg], axis=1).T.astype(BF16)
    p["w_d"] = w[:, _OFF_D:_OFF_G].astype(BF16)
    p["w_g"] = w[:, _OFF_G:].astype(BF16)
    gain = attn_qk_gain[l]
    p["gain_row"] = jnp.concatenate([jnp.tile(gain[0], A_HEADS), jnp.tile(gain[1], A_KV_HEADS)])[None, :]
    p["ws"] = gmlp_ws[l]
    p["bias_full"] = jnp.repeat(gmlp_b[l].T, 128, axis=1)
    p["gate_bias_row"] = jnp.concatenate([mlstm_gate_bias[l].reshape(16), jnp.zeros((112,), F32)])[None, :]
    p["gate_bias_t"] = jnp.broadcast_to(p["gate_bias_row"].reshape(128, 1), (128, 128))
    p["mlstm_gn_t"] = jnp.broadcast_to(mlstm_gn[l].reshape(BRANCH_WIDTH, 1), (BRANCH_WIDTH, 128))
    p["lam"] = diff_lambda[l]
    p["diff_gn_row"] = diff_gn[l].reshape(1, BRANCH_WIDTH)
    p["w_br"] = w_branch[l].astype(BF16)
    p["w_out"] = w_out[l].astype(BF16)
    p["ln_g0"], p["ln_b0"] = ln_g[l, 0][None, :], ln_b[l, 0][None, :]
    p["ln_g1"], p["ln_b1"] = ln_g[l, 1][None, :], ln_b[l, 1][None, :]
    p["wq"] = peer_wq[l].astype(BF16)
    p["keys"] = peer_keys[l]
    p["u"], p["v"] = peer_u, peer_v
    return p


def _trunk_layer(x, mod, p, tabs, *, l, n_seq, seq, cfg, ctx_cache=None, prev_state=None):
    bd, rep, tril, triu = tabs
    lam_init = 0.8 - 0.6 * math.exp(-0.3 * l)
    state = None
    if ctx_cache is None:
        prev = (None, None, None) if prev_state is None else (prev_state[0:2], prev_state[2:4], prev_state[4:7])
        ya, nk, nv = _attn_a(x, mod, p["w_a"], p["gain_row"], bd, rep, n_seq=n_seq, seq=seq, tq=cfg["tq"],
                             layer=l, carry=prev[0])
        yd, ndk, ndv = _attn_d(x, mod, p["w_d"], p["lam"], p["diff_gn_row"], n_seq=n_seq, seq=seq, tq=cfg["tq"],
                               lam_init=lam_init, layer=l, carry=prev[1])
        yc, c_new, n_new, m_new = _mlstm_t(x, mod, p["w_c_nat"], p["w_c_t"], p["gate_bias_row"], p["gate_bias_t"],
                                           p["mlstm_gn_t"], tril, triu, n_seq=n_seq, seq=seq, layer=l, carry=prev[2])
        state = (nk, nv, ndk, ndv, c_new, n_new, m_new)
    else:
        rope, cak, cav, cdk, cdv, c0, n0, m0 = ctx_cache
        ya = _attn_a(x, mod, p["w_a"], p["gain_row"], bd, rep, n_seq=n_seq, seq=seq, tq=cfg["tq"],
                     rope=rope, cache=(cak, cav), layer=l)
        yd = _attn_d(x, mod, p["w_d"], p["lam"], p["diff_gn_row"], n_seq=n_seq, seq=seq, tq=cfg["tq_d"],
                     lam_init=lam_init, rope=rope, cache=(cdk, cdv), layer=l)
        yc = _mlstm_t(x, mod, p["w_c_nat"], p["w_c_t"], p["gate_bias_row"], p["gate_bias_t"], p["mlstm_gn_t"],
                      tril, triu, n_seq=n_seq, seq=seq, init=(c0, n0, m0), layer=l)
    yb = _gmlp(x, mod, p["w_b"], p["ws"], p["bias_full"], seq=seq, rows=cfg["gmlp_rows"])
    x1 = _merge(x, mod, (ya, yb, yc, yd), p["w_g"], p["w_br"], p["w_out"], p["ln_g0"], p["ln_b0"],
                seq=seq, rows=cfg["rows"])
    route = _route(x1, mod, p["wq"], p["keys"], seq=seq, cols=cfg["route_cols"], heads=cfg["route_heads"])
    x2 = _peer(x1, mod, p["u"], p["v"], route, p["ln_g1"], p["ln_b1"], seq=seq, cols=cfg["cols"], layer=l,
               key_rows=cfg["key_rows"])
    return x2, state


def kernel(x_prompt, x_sample, cache_a_k, cache_a_v, cache_d_k, cache_d_v, state_c_C, state_c_n, state_c_m,
           c, c_ctx, w_mod, b_mod, w_in, attn_qk_gain, gmlp_ws, gmlp_b, mlstm_gate_bias, mlstm_gn,
           diff_lambda, diff_gn, w_branch, w_out, ln_g, ln_b, peer_wq, peer_keys, peer_u, peer_v):
    batch, seq, _ = x_prompt.shape
    dec_batch, dec_seq, _ = x_sample.shape
    past = cache_a_k.shape[2]
    c_rows = jnp.concatenate([c_ctx[None, :], c, jnp.zeros((8 - 1 - dec_batch, D_MODEL), F32)], axis=0)
    mods = _modulation(c_rows, w_mod, b_mod)
    tabs = _static_tables()
    rope = _rope_tables(dec_seq)
    cak = cache_a_k.reshape(dec_batch, DEPTH, past, 128)
    cav = cache_a_v.reshape(dec_batch, DEPTH, past, 128)
    cdk = cache_d_k.reshape(dec_batch, DEPTH, past, 512)
    cdv = cache_d_v.reshape(dec_batch, DEPTH, past, 512)
    c0 = state_c_C.reshape(dec_batch, DEPTH, 8, 128, 128)
    n0 = state_c_n.reshape(dec_batch, DEPTH, 8, 128)
    m0 = jnp.broadcast_to(state_c_m.reshape(dec_batch, DEPTH, 8, 1), (dec_batch, DEPTH, 8, 128))
    cfg_ctx = dict(tq=seq, rows=512, gmlp_rows=min(1024, batch * seq), cols=512, key_rows=16,
                   route_cols=min(1024, batch * seq), route_heads=4)
    cfg_lat = dict(tq=min(256, dec_seq), tq_d=min(512, dec_seq), rows=min(512, dec_seq),
                   gmlp_rows=min(1024, dec_seq), cols=512, key_rows=16,
                   route_cols=min(1024, dec_seq), route_heads=4)
    y_p = x_prompt.reshape(batch * seq, D_MODEL)
    y_s = x_sample.reshape(dec_batch * dec_seq, D_MODEL)
    state = None
    u_all, v_all = peer_u.astype(BF16), jnp.swapaxes(peer_v, 1, 2).astype(BF16)
    for l in range(DEPTH):
        p = _layer_params(l, w_in, attn_qk_gain, gmlp_ws, gmlp_b, mlstm_gate_bias, mlstm_gn, diff_lambda, diff_gn,
                          w_branch, w_out, ln_g, ln_b, peer_wq, peer_keys, u_all, v_all)
        mod_ctx = mods[l, 0:1].reshape(1, 1, 6 * D_MODEL)
        mod_lat = mods[l, 1:1 + dec_batch].reshape(dec_batch, 1, 6 * D_MODEL)
        y_p, state = _trunk_layer(y_p, mod_ctx, p, tabs, l=l, n_seq=batch, seq=seq, cfg=cfg_ctx, prev_state=state)
        y_s, _ = _trunk_layer(y_s, mod_lat, p, tabs, l=l, n_seq=dec_batch, seq=dec_seq, cfg=cfg_lat,
                              ctx_cache=(rope, cak, cav, cdk, cdv, c0, n0, m0))
    nk = state[0].reshape(batch, DEPTH, seq, A_KV_HEADS, A_HEAD_DIM)
    nv = state[1].reshape(batch, DEPTH, seq, A_KV_HEADS, A_HEAD_DIM)
    ndk = state[2].reshape(batch, DEPTH, seq, D_HEADS, 2, D_HALF_DIM)
    ndv = state[3].reshape(batch, DEPTH, seq, D_HEADS, D_VDIM)
    nc = state[4].reshape(batch, DEPTH, 2, C_HEADS, C_HEAD_DIM, C_HEAD_DIM)
    nn = state[5].reshape(batch, DEPTH, 2, C_HEADS, C_HEAD_DIM)
    nm = state[6][:, :, :, 0].reshape(batch, DEPTH, 2, C_HEADS)
    return (y_p.reshape(batch, seq, D_MODEL), y_s.reshape(dec_batch, dec_seq, D_MODEL), nk, nv, ndk, ndv, nc, nn, nm)
```

```python
import functools
import math

import numpy as np
import jax
import jax.numpy as jnp
from jax import lax
from jax.experimental import pallas as pl
from jax.experimental.pallas import tpu as pltpu

F32 = jnp.float32
BF16 = jnp.bfloat16
HIGHEST = lax.Precision.HIGHEST

D_MODEL = 1024
DEPTH = 4
GRID_W = 64
ROPE_BASE = 10000.0
EPS = 1e-6
BRANCH_WIDTH = D_MODEL // 2
A_HEAD_DIM = 64
A_HEADS = 8
A_KV_HEADS = 2
B_GROUPS = 4
CHUNK = 128
C_HEADS = 4
C_HEAD_DIM = 128
D_HEADS = 4
D_VDIM = 128
D_HALF_DIM = 64
PEER_HEADS = 8
PEER_NKEYS = 128
PEER_EXPERTS = PEER_NKEYS * PEER_NKEYS
PEER_QDIM = 256
PEER_TOPK = 16
ALPHA = (2 * DEPTH) ** 0.25

_OFF_A = 0
_OFF_B = 768
_OFF_C = 1792
_OFF_CG = 3840
_OFF_D = 3856
_OFF_G = 5392

VMEM_LIMIT_BYTES = 56 * 1024 * 1024
NEG_INF = float("-inf")


def _cparams(*sem):
    return pltpu.CompilerParams(dimension_semantics=sem, vmem_limit_bytes=VMEM_LIMIT_BYTES)


def _dot(a, b):
    return jnp.dot(a, b, preferred_element_type=F32)


def _dot_nt(a, b):
    return lax.dot_general(a, b, (((1,), (1,)), ((), ())), preferred_element_type=F32)


def _modulated(x, mod_ref, which):
    base = 3 * D_MODEL * which
    sh = mod_ref[:, base:base + D_MODEL]
    sc = mod_ref[:, base + D_MODEL:base + 2 * D_MODEL]
    return x * (1.0 + sc) + sh


def _layer_norm_rows(z, g, b):
    mu = jnp.mean(z, axis=-1, keepdims=True)
    zc = z - mu
    var = jnp.mean(zc * zc, axis=-1, keepdims=True)
    return zc * lax.rsqrt(var + EPS) * g + b


def _rope(x, cos, sin_signed, lane):
    w = x.shape[1]
    nxt = pltpu.roll(x, w - 16, 1)
    prv = pltpu.roll(x, 16, 1)
    partner = jnp.where((lane % 32) < 16, nxt, prv)
    return x * cos + partner * sin_signed


def _tile_lanes(t, n):
    return t if n == 1 else jnp.concatenate([t] * n, axis=1)


def _mod_kernel(c_ref, w_ref, b_ref, o_ref):
    c = c_ref[...]
    s = c * jax.nn.sigmoid(c)
    o_ref[...] = jnp.dot(s, w_ref[...], precision=HIGHEST, preferred_element_type=F32) + b_ref[...]


def _modulation(c_rows, w_mod, b_mod):
    n_col = 6 * D_MODEL // 1024
    return pl.pallas_call(
        _mod_kernel,
        out_shape=jax.ShapeDtypeStruct((DEPTH, 8, 6 * D_MODEL), F32),
        grid=(DEPTH, n_col),
        in_specs=[
            pl.BlockSpec((8, D_MODEL), lambda l, j: (0, 0)),
            pl.BlockSpec((None, D_MODEL, 1024), lambda l, j: (l, 0, j)),
            pl.BlockSpec((None, 1, 1024), lambda l, j: (l, 0, j)),
        ],
        out_specs=pl.BlockSpec((None, 8, 1024), lambda l, j: (l, 0, j)),
        compiler_params=_cparams("arbitrary", "arbitrary"),
        name="modulation",
    )(c_rows, w_mod, b_mod.reshape(DEPTH, 1, 6 * D_MODEL))


def _attn_a_kernel(*refs, seq, tq, n_cache, rope, proj_rows):
    if rope:
        (x_ref, mod_ref, w_ref, gain_ref, bd_ref, rep_ref, cos_ref, sin_ref, ck_ref, cv_ref,
         y_ref, q_s, k_s, v_s) = refs
    else:
        (x_ref, mod_ref, w_ref, gain_ref, bd_ref, rep_ref) = refs[:6]
        (y_ref, nk_ref, nv_ref, q_s, k_s, v_s) = refs[-6:]
    qi = pl.program_id(1)

    @pl.when(qi == 0)
    def _project():
        lane = lax.broadcasted_iota(jnp.int32, (1, 640), 1)
        for r0 in range(0, seq, proj_rows):
            rows = pl.ds(r0, proj_rows)
            h = _modulated(x_ref[rows, :], mod_ref, 0).astype(BF16)
            p = _dot(h, w_ref[...])
            qk = p[:, :640]
            sq = qk * qk
            hi = sq.astype(BF16)
            lo = (sq - hi.astype(F32)).astype(BF16)
            ms = jnp.concatenate([_dot(hi[:, c:c + 128], bd_ref[...]) + _dot(lo[:, c:c + 128], bd_ref[...])
                                  for c in range(0, 640, 128)], axis=1)
            qk = qk * lax.rsqrt(ms + EPS) * gain_ref[...]
            v = p[:, 640:768]
            if not rope:
                nk_ref[rows, :] = qk[:, 512:640]
                nv_ref[rows, :] = v
            else:
                cos = _tile_lanes(cos_ref[rows, :], 5)
                sin = _tile_lanes(sin_ref[rows, :], 5)
                qk = _rope(qk, cos, sin, lane)
            q_s[rows, :] = (qk[:, :512] * (A_HEAD_DIM ** -0.5)).astype(BF16)
            k_s[rows, :] = _dot(qk[:, 512:640].astype(BF16), rep_ref[...]).astype(BF16)
            v_s[rows, :] = _dot(v.astype(BF16), rep_ref[...]).astype(BF16)
        if n_cache:
            crow = pl.ds(seq, n_cache)
            k_s[crow, :] = _dot(ck_ref[...].astype(BF16), rep_ref[...]).astype(BF16)
            v_s[crow, :] = _dot(cv_ref[...].astype(BF16), rep_ref[...]).astype(BF16)

    head_of_lane = lax.broadcasted_iota(jnp.int32, (1, 256), 1) // A_HEAD_DIM
    qb = q_s[pl.ds(pl.multiple_of(qi * tq, tq), tq), :]
    for g in range(A_KV_HEADS):
        cols = slice(g * 256, (g + 1) * 256)
        qg = qb[:, cols]
        kg = k_s[:, cols]
        vg = v_s[:, cols]
        acc = jnp.zeros((tq, 256), F32)
        for r in range(A_HEADS // A_KV_HEADS):
            sel = head_of_lane == r
            qm = jnp.where(sel, qg, jnp.zeros_like(qg))
            s = _dot_nt(qm, kg)
            m = jnp.max(s, axis=-1, keepdims=True)
            p = jnp.exp(s - m)
            l = jnp.sum(p, axis=-1, keepdims=True)
            o = _dot(p.astype(BF16), vg)
            acc = acc + jnp.where(sel, o * (1.0 / l), 0.0)
        y_ref[:, cols] = acc.astype(BF16)


def _carry_aliases(in_specs, args, carry, first_out):
    if carry is None:
        return {}
    aliases = {}
    for k, arr in enumerate(carry):
        aliases[len(args)] = first_out + k
        in_specs.append(pl.BlockSpec(memory_space=pl.ANY))
        args.append(arr)
    return aliases


def _attn_a(x, mod, w_a, gain_row, bd, rep, *, n_seq, seq, tq, rope=None, cache=None, layer=0, carry=None):
    n_q = seq // tq
    n_cache = 0 if cache is None else cache[0].shape[2]
    proj_rows = min(seq, 512)
    kern = functools.partial(_attn_a_kernel, seq=seq, tq=tq, n_cache=n_cache, rope=rope is not None,
                             proj_rows=proj_rows)
    const = lambda s, q: (0, 0)
    in_specs = [
        pl.BlockSpec((seq, D_MODEL), lambda s, q: (s, 0)),
        pl.BlockSpec((None, 1, 6 * D_MODEL), lambda s, q: (s if mod.shape[0] > 1 else 0, 0, 0)),
        pl.BlockSpec(w_a.shape, const),
        pl.BlockSpec(gain_row.shape, const),
        pl.BlockSpec(bd.shape, const),
        pl.BlockSpec(rep.shape, const),
    ]
    args = [x, mod, w_a, gain_row, bd, rep]
    n_tok = n_seq * seq
    y_spec = pl.BlockSpec((tq, BRANCH_WIDTH), lambda s, q: (s * n_q + q, 0))
    y_shape = jax.ShapeDtypeStruct((n_tok, BRANCH_WIDTH), BF16)
    if rope is not None:
        cos, sin = rope
        in_specs += [pl.BlockSpec(cos.shape, const), pl.BlockSpec(sin.shape, const),
                     pl.BlockSpec((None, None, n_cache, 128), lambda s, q: (s, layer, 0, 0)),
                     pl.BlockSpec((None, None, n_cache, 128), lambda s, q: (s, layer, 0, 0))]
        args += [cos, sin, cache[0], cache[1]]
        out_shape, out_specs, aliases = y_shape, y_spec, {}
    else:
        kv_shape = jax.ShapeDtypeStruct((n_seq, DEPTH, seq, 128), F32)
        kv_spec = pl.BlockSpec((None, None, seq, 128), lambda s, q: (s, layer, 0, 0))
        out_shape, out_specs = (y_shape, kv_shape, kv_shape), (y_spec, kv_spec, kv_spec)
        aliases = _carry_aliases(in_specs, args, carry, first_out=1)
    return pl.pallas_call(
        kern, out_shape=out_shape, grid=(n_seq, n_q), in_specs=in_specs, out_specs=out_specs,
        input_output_aliases=aliases,
        scratch_shapes=[pltpu.VMEM((seq, 512), BF16), pltpu.VMEM((seq + n_cache, 512), BF16),
                        pltpu.VMEM((seq + n_cache, 512), BF16)],
        compiler_params=_cparams("arbitrary", "arbitrary"),
        name="branch_a_lat" if rope is not None else "branch_a_ctx",
    )(*args)


def _attn_d_kernel(*refs, seq, tq, n_cache, rope, proj_rows, lam_init):
    if rope:
        (x_ref, mod_ref, w_ref, lam_ref, gn_ref, cos_ref, sin_ref, ck_ref, cv_ref,
         y_ref, q_s, k_s, v_s) = refs
    else:
        (x_ref, mod_ref, w_ref, lam_ref, gn_ref) = refs[:5]
        (y_ref, nk_ref, nv_ref, q_s, k_s, v_s) = refs[-6:]
    qi = pl.program_id(1)

    @pl.when(qi == 0)
    def _project():
        lane = lax.broadcasted_iota(jnp.int32, (1, 512), 1)
        for r0 in range(0, seq, proj_rows):
            rows = pl.ds(r0, proj_rows)
            h = _modulated(x_ref[rows, :], mod_ref, 0).astype(BF16)
            p = _dot(h, w_ref[...])
            dq, dk, dv = p[:, :512], p[:, 512:1024], p[:, 1024:1536]
            if not rope:
                nk_ref[rows, :] = dk
                nv_ref[rows, :] = dv
            else:
                cos = _tile_lanes(cos_ref[rows, :], 4)
                sin = _tile_lanes(sin_ref[rows, :], 4)
                dq = _rope(dq, cos, sin, lane)
                dk = _rope(dk, cos, sin, lane)
            q_s[rows, :] = (dq * (D_HALF_DIM ** -0.5)).astype(BF16)
            k_s[rows, :] = dk.astype(BF16)
            v_s[rows, :] = dv.astype(BF16)
        if n_cache:
            crow = pl.ds(seq, n_cache)
            k_s[crow, :] = ck_ref[...].astype(BF16)
            v_s[crow, :] = cv_ref[...].astype(BF16)

    lv = lam_ref[...]
    lam = (jnp.exp(jnp.sum(lv[0:1] * lv[1:2], axis=-1, keepdims=True))
           - jnp.exp(jnp.sum(lv[2:3] * lv[3:4], axis=-1, keepdims=True)) + lam_init)
    half_of_lane = lax.broadcasted_iota(jnp.int32, (1, 128), 1) // D_HALF_DIM
    qb = q_s[pl.ds(pl.multiple_of(qi * tq, tq), tq), :]
    for hd in range(D_HEADS):
        cols = slice(hd * 128, (hd + 1) * 128)
        qh = qb[:, cols]
        kh = k_s[:, cols]
        vh = v_s[:, cols]
        probs = []
        for j in range(2):
            qm = jnp.where(half_of_lane == j, qh, jnp.zeros_like(qh))
            s = _dot_nt(qm, kh)
            m = jnp.max(s, axis=-1, keepdims=True)
            p = jnp.exp(s - m)
            l = jnp.sum(p, axis=-1, keepdims=True)
            probs.append(p * (1.0 / l))
        a = (probs[0] - lam * probs[1]).astype(BF16)
        o = _dot(a, vh)
        ms = jnp.mean(o * o, axis=-1, keepdims=True)
        o = o * lax.rsqrt(ms + EPS) * gn_ref[:, cols] * (1.0 - lam_init)
        y_ref[:, cols] = o.astype(BF16)


def _attn_d(x, mod, w_d, lam_params, gn_row, *, n_seq, seq, tq, lam_init, rope=None, cache=None, layer=0,
            carry=None):
    n_q = seq // tq
    n_cache = 0 if cache is None else cache[0].shape[2]
    proj_rows = min(seq, 512)
    kern = functools.partial(_attn_d_kernel, seq=seq, tq=tq, n_cache=n_cache, rope=rope is not None,
                             proj_rows=proj_rows, lam_init=lam_init)
    const = lambda s, q: (0, 0)
    in_specs = [
        pl.BlockSpec((seq, D_MODEL), lambda s, q: (s, 0)),
        pl.BlockSpec((None, 1, 6 * D_MODEL), lambda s, q: (s if mod.shape[0] > 1 else 0, 0, 0)),
        pl.BlockSpec(w_d.shape, const),
        pl.BlockSpec(lam_params.shape, const),
        pl.BlockSpec(gn_row.shape, const),
    ]
    args = [x, mod, w_d, lam_params, gn_row]
    n_tok = n_seq * seq
    y_spec = pl.BlockSpec((tq, BRANCH_WIDTH), lambda s, q: (s * n_q + q, 0))
    y_shape = jax.ShapeDtypeStruct((n_tok, BRANCH_WIDTH), BF16)
    if rope is not None:
        cos, sin = rope
        in_specs += [pl.BlockSpec(cos.shape, const), pl.BlockSpec(sin.shape, const),
                     pl.BlockSpec((None, None, n_cache, 512), lambda s, q: (s, layer, 0, 0)),
                     pl.BlockSpec((None, None, n_cache, 512), lambda s, q: (s, layer, 0, 0))]
        args += [cos, sin, cache[0], cache[1]]
        out_shape, out_specs, aliases = y_shape, y_spec, {}
    else:
        kv_shape = jax.ShapeDtypeStruct((n_seq, DEPTH, seq, 512), F32)
        kv_spec = pl.BlockSpec((None, None, seq, 512), lambda s, q: (s, layer, 0, 0))
        out_shape, out_specs = (y_shape, kv_shape, kv_shape), (y_spec, kv_spec, kv_spec)
        aliases = _carry_aliases(in_specs, args, carry, first_out=1)
    return pl.pallas_call(
        kern, out_shape=out_shape, grid=(n_seq, n_q), in_specs=in_specs, out_specs=out_specs,
        input_output_aliases=aliases,
        scratch_shapes=[pltpu.VMEM((seq, 512), BF16), pltpu.VMEM((seq + n_cache, 512), BF16),
                        pltpu.VMEM((seq + n_cache, 512), BF16)],
        compiler_params=_cparams("arbitrary", "arbitrary"),
        name="branch_d_lat" if rope is not None else "branch_d_ctx",
    )(*args)


def _gmlp_kernel(x_ref, mod_ref, w_ref, ws_ref, bias_ref, y_ref, *, rows):
    h = _modulated(x_ref[...], mod_ref, 0).astype(BF16)
    p = _dot(h, w_ref[...])
    u, v = p[:, :BRANCH_WIDTH], p[:, BRANCH_WIDTH:]
    mu = jnp.mean(v, axis=-1, keepdims=True)
    vc = v - mu
    var = jnp.mean(vc * vc, axis=-1, keepdims=True)
    vn = (vc * lax.rsqrt(var + EPS)).astype(BF16)
    for c in range(rows // CHUNK):
        rs = slice(c * CHUNK, (c + 1) * CHUNK)
        for g in range(B_GROUPS):
            cs = slice(g * 128, (g + 1) * 128)
            s = _dot(ws_ref[g].astype(BF16), vn[rs, cs]) + bias_ref[:, cs]
            y_ref[rs, cs] = (u[rs, cs] * s).astype(BF16)


def _gmlp(x, mod, w_b, ws, bias_full, *, seq, rows):
    n_tok = x.shape[0]
    per_seq = seq // rows
    kern = functools.partial(_gmlp_kernel, rows=rows)
    return pl.pallas_call(
        kern, out_shape=jax.ShapeDtypeStruct((n_tok, BRANCH_WIDTH), BF16),
        grid=(n_tok // rows,),
        in_specs=[
            pl.BlockSpec((rows, D_MODEL), lambda i: (i, 0)),
            pl.BlockSpec((None, 1, 6 * D_MODEL), lambda i: (i // per_seq if mod.shape[0] > 1 else 0, 0, 0)),
            pl.BlockSpec(w_b.shape, lambda i: (0, 0)),
            pl.BlockSpec(ws.shape, lambda i: (0, 0, 0)),
            pl.BlockSpec(bias_full.shape, lambda i: (0, 0)),
        ],
        out_specs=pl.BlockSpec((rows, BRANCH_WIDTH), lambda i: (i, 0)),
        compiler_params=_cparams("arbitrary"),
        name="branch_b",
    )(x, mod, w_b, ws, bias_full)


def _mlstm_t_kernel(*refs, seq, has_init, proj_rows):
    if has_init:
        (x_ref, mod_ref, wn_ref, wt_ref, gbr_ref, gbt_ref, gnt_ref, tril_ref, triu_ref, c0_ref, n0_ref, m0_ref,
         y_ref, k_s, qt_s, vt_s, ot_s, g_s, gt_s, hf_s, hb_s, st_s, m_s) = refs
    else:
        (x_ref, mod_ref, wn_ref, wt_ref, gbr_ref, gbt_ref, gnt_ref, tril_ref, triu_ref) = refs[:9]
        (y_ref, cout_ref, nout_ref, mout_ref,
         k_s, qt_s, vt_s, ot_s, g_s, gt_s, hf_s, hb_s, st_s, m_s) = refs[-14:]
    n_chunk = seq // CHUNK
    for r0 in range(0, seq, proj_rows):
        span = pl.ds(r0, proj_rows)
        h = _modulated(x_ref[span, :], mod_ref, 0).astype(BF16)
        p = _dot(h, wn_ref[...])
        k_s[span, :] = (p[:, 0:512] * (C_HEAD_DIM ** -0.5)).astype(BF16)
        g_s[span, :] = p[:, 512:640] + gbr_ref[...]
        pt = _dot_nt(wt_ref[...], h)
        qt_s[:, span] = pt[0:512].astype(BF16)
        vt_s[:, span] = pt[512:1024].astype(BF16)
        ot_s[:, span] = jax.nn.sigmoid(pt[1024:1536]).astype(BF16)
        gt_s[:, span] = pt[1536:1664] + _tile_lanes(gbt_ref[...], proj_rows // 128)

    first_row = lax.broadcasted_iota(jnp.int32, (CHUNK, CHUNK), 0) == 0
    for sidx in range(2 * C_HEADS):
        if has_init:
            st_s[sidx, 0:128, :] = c0_ref[sidx].T
            st_s[sidx, 128:256, :] = jnp.where(first_row, jnp.broadcast_to(n0_ref[sidx:sidx + 1, :], (CHUNK, 128)), 0.0)
        else:
            st_s[sidx] = jnp.zeros((2 * CHUNK, 128), F32)
    m_s[...] = m0_ref[...] if has_init else jnp.zeros_like(m_s)

    tril = tril_ref[...]
    triu = triu_ref[...]
    row_i = lax.broadcasted_iota(jnp.int32, (CHUNK, CHUNK), 0)
    col_i = lax.broadcasted_iota(jnp.int32, (CHUNK, CHUNK), 1)
    visible = (row_i <= col_i, row_i >= col_i)
    ones_blk = jnp.where(first_row, 1.0, 0.0).astype(BF16)

    def chunk_step(c, carry):
        prep = []
        for direction in range(2):
            cc = c if direction == 0 else n_chunk - 1 - c
            span = pl.ds(pl.multiple_of(cc * CHUNK, CHUNK), CHUNK)
            gates = g_s[span, :]
            gates_t = gt_s[:, span]
            logf = jnp.minimum(gates, 0.0) - jnp.log1p(jnp.exp(-jnp.abs(gates)))
            logf_t = jnp.minimum(gates_t, 0.0) - jnp.log1p(jnp.exp(-jnp.abs(gates_t)))
            tri_col = tril if direction == 0 else triu
            tri_row = triu if direction == 0 else tril
            b_col_all = jnp.dot(tri_col, logf, precision=HIGHEST, preferred_element_type=F32)
            b_row_all = jnp.dot(logf_t, tri_row, precision=HIGHEST, preferred_element_type=F32)
            r_col_all = gates - pltpu.roll(b_col_all, 128 - 4, 1)
            base = direction * 8
            prep.append((span, r_col_all, gates_t[base:base + 4, :], b_row_all[base + 4:base + 8, :]))
        for hd in range(C_HEADS):
            for direction in range(2):
                span, r_col_all, i_rows, b_rows = prep[direction]
                base = direction * 8
                last = CHUNK - 1 if direction == 0 else 0
                h_out = hf_s if direction == 0 else hb_s
                sidx = direction * 4 + hd
                blk = slice(hd * 128, (hd + 1) * 128)
                r_col = r_col_all[:, base + hd:base + hd + 1]
                b_row = b_rows[hd:hd + 1, :]
                i_row = i_rows[hd:hd + 1, :]
                m_prev = m_s[sidx:sidx + 1, 0:1]
                r_wide = jnp.broadcast_to(r_col, (CHUNK, CHUNK))
                peak = jnp.max(jnp.where(visible[direction], r_wide, NEG_INF), axis=0, keepdims=True)
                m_row = b_row + jnp.maximum(peak, m_prev)
                kh = k_s[span, blk]
                q_t = qt_s[blk, span]
                arg = jnp.where(visible[direction], r_wide + (b_row - m_row), NEG_INF)
                w_t = (_dot(kh, q_t) * jnp.exp(arg)).astype(BF16)
                inter = jnp.exp(b_row + m_prev - m_row)
                v_one = jnp.concatenate([vt_s[blk, span], ones_blk], axis=0)
                state = st_s[sidx]
                both = _dot(v_one, w_t) + inter * _dot(state.astype(BF16), q_t)
                den = both[128:129, :]
                h_out[blk, span] = both[0:128, :] / jnp.maximum(jnp.abs(den), jnp.exp(-m_row))
                b_last = b_row[:, last:last + 1]
                g_row = b_last - b_row + i_row
                m_new = jnp.maximum(b_last + m_prev, jnp.max(g_row, axis=-1, keepdims=True))
                decay = jnp.exp(b_last + m_prev - m_new)
                scaled = v_one * jnp.exp(g_row - m_new).astype(BF16)
                st_s[sidx] = decay * state + _dot(scaled, kh)
                m_s[sidx:sidx + 1, :] = jnp.broadcast_to(m_new, (1, 128))
        return carry

    lax.fori_loop(0, n_chunk, chunk_step, 0)

    for r0 in range(0, seq, proj_rows):
        span = pl.ds(r0, proj_rows)
        for hd in range(C_HEADS):
            blk = slice(hd * 128, (hd + 1) * 128)
            hh = hf_s[blk, span] + hb_s[blk, span]
            mu = jnp.mean(hh, axis=0, keepdims=True)
            hc = hh - mu
            var = jnp.mean(hc * hc, axis=0, keepdims=True)
            gn = _tile_lanes(gnt_ref[blk, :], proj_rows // 128)
            y_t = hc * lax.rsqrt(var + EPS) * gn * ot_s[blk, span].astype(F32)
            y_ref[span, blk] = y_t.T.astype(BF16)
    if not has_init:
        for sidx in range(2 * C_HEADS):
            cout_ref[sidx] = st_s[sidx, 0:128, :].T
            nout_ref[sidx:sidx + 1, :] = st_s[sidx, 128:129, :]
        mout_ref[...] = m_s[...]


def _mlstm_t(x, mod, w_nat, w_t, gate_bias_row, gate_bias_t, gn_t, tril, triu, *, n_seq, seq, init=None, layer=0,
             carry=None):
    proj_rows = min(seq, 512)
    kern = functools.partial(_mlstm_t_kernel, seq=seq, has_init=init is not None, proj_rows=proj_rows)
    const = lambda s: (0, 0)
    in_specs = [
        pl.BlockSpec((seq, D_MODEL), lambda s: (s, 0)),
        pl.BlockSpec((None, 1, 6 * D_MODEL), lambda s: (s if mod.shape[0] > 1 else 0, 0, 0)),
    ] + [pl.BlockSpec(a.shape, const) for a in (w_nat, w_t, gate_bias_row, gate_bias_t, gn_t, tril, triu)]
    args = [x, mod, w_nat, w_t, gate_bias_row, gate_bias_t, gn_t, tril, triu]
    n_tok = n_seq * seq
    y_shape = jax.ShapeDtypeStruct((n_tok, BRANCH_WIDTH), BF16)
    y_spec = pl.BlockSpec((seq, BRANCH_WIDTH), lambda s: (s, 0))
    if init is not None:
        c0, n0, m0 = init
        in_specs += [pl.BlockSpec((None, None, 8, 128, 128), lambda s: (s, layer, 0, 0, 0)),
                     pl.BlockSpec((None, None, 8, 128), lambda s: (s, layer, 0, 0)),
                     pl.BlockSpec((None, None, 8, 128), lambda s: (s, layer, 0, 0))]
        args += [c0, n0, m0]
        out_shape, out_specs, aliases = y_shape, y_spec, {}
    else:
        out_shape = (y_shape, jax.ShapeDtypeStruct((n_seq, DEPTH, 8, 128, 128), F32),
                     jax.ShapeDtypeStruct((n_seq, DEPTH, 8, 128), F32),
                     jax.ShapeDtypeStruct((n_seq, DEPTH, 8, 128), F32))
        out_specs = (y_spec, pl.BlockSpec((None, None, 8, 128, 128), lambda s: (s, layer, 0, 0, 0)),
                     pl.BlockSpec((None, None, 8, 128), lambda s: (s, layer, 0, 0)),
                     pl.BlockSpec((None, None, 8, 128), lambda s: (s, layer, 0, 0)))
        aliases = _carry_aliases(in_specs, args, carry, first_out=1)
    return pl.pallas_call(
        kern, out_shape=out_shape, grid=(n_seq,), in_specs=in_specs, out_specs=out_specs,
        input_output_aliases=aliases,
        scratch_shapes=[pltpu.VMEM((seq, 512), BF16),
                        pltpu.VMEM((512, seq), BF16), pltpu.VMEM((512, seq), BF16),
                        pltpu.VMEM((512, seq), BF16),
                        pltpu.VMEM((seq, 128), F32), pltpu.VMEM((128, seq), F32),
                        pltpu.VMEM((512, seq), F32), pltpu.VMEM((512, seq), F32),
                        pltpu.VMEM((8, 256, 128), F32), pltpu.VMEM((8, 128), F32)],
        compiler_params=_cparams("arbitrary"),
        name="branch_c_lat" if init is not None else "branch_c_ctx",
    )(*args)


def _merge_kernel(x_ref, mod_ref, ya_ref, yb_ref, yc_ref, yd_ref, wg_ref, wbr_ref, wout_ref, lng_ref, lnb_ref,
                  o_ref):
    x = x_ref[...]
    h = _modulated(x, mod_ref, 0).astype(BF16)
    mix = None
    for n, y_ref in enumerate((ya_ref, yb_ref, yc_ref, yd_ref)):
        gate = jax.nn.sigmoid(_dot(h, wg_ref[:, n * D_MODEL:(n + 1) * D_MODEL]))
        term = gate * _dot(y_ref[...], wbr_ref[n])
        mix = term if mix is None else mix + term
    out = _dot(mix.astype(BF16), wout_ref[...])
    g1 = mod_ref[:, 2 * D_MODEL:3 * D_MODEL]
    o_ref[...] = _layer_norm_rows(ALPHA * x + g1 * out, lng_ref[...], lnb_ref[...])


def _merge(x, mod, ys, w_g, w_br, w_out, ln_g, ln_b, *, seq, rows):
    n_tok = x.shape[0]
    per_seq = seq // rows
    tok = lambda i: (i, 0)
    c2 = lambda i: (0, 0)
    return pl.pallas_call(
        _merge_kernel, out_shape=jax.ShapeDtypeStruct((n_tok, D_MODEL), F32),
        grid=(n_tok // rows,),
        in_specs=[
            pl.BlockSpec((rows, D_MODEL), tok),
            pl.BlockSpec((None, 1, 6 * D_MODEL), lambda i: (i // per_seq if mod.shape[0] > 1 else 0, 0, 0)),
            pl.BlockSpec((rows, BRANCH_WIDTH), tok), pl.BlockSpec((rows, BRANCH_WIDTH), tok),
            pl.BlockSpec((rows, BRANCH_WIDTH), tok), pl.BlockSpec((rows, BRANCH_WIDTH), tok),
            pl.BlockSpec(w_g.shape, c2), pl.BlockSpec(w_br.shape, lambda i: (0, 0, 0)),
            pl.BlockSpec(w_out.shape, c2), pl.BlockSpec(ln_g.shape, c2), pl.BlockSpec(ln_b.shape, c2),
        ],
        out_specs=pl.BlockSpec((rows, D_MODEL), tok),
        compiler_params=_cparams("arbitrary"),
        name="merge",
    )(x, mod, *ys, w_g, w_br, w_out, ln_g, ln_b)


_TAKEN = -(2.0 ** 127)


def _top16(s):
    cur = s
    vals = []
    for r in range(PEER_TOPK):
        mx = jnp.max(cur, axis=0, keepdims=True)
        cur = jnp.where(cur == mx, _TAKEN * (1.0 + r / 32.0), cur)
        vals.append(mx)
    rank = jnp.where(cur <= _TAKEN, cur * (32.0 / _TAKEN) - 31.0, float(PEER_TOPK + 1))
    return jnp.concatenate(vals, axis=0), rank


def _pair_tables():
    pairs = [(k1, k2) for k1 in range(PEER_TOPK) for k2 in range(PEER_TOPK // (k1 + 1))]
    n = 56
    sel_a = np.zeros((n, PEER_TOPK), np.float32)
    sel_b = np.zeros((n, PEER_TOPK), np.float32)
    pad = np.full((n, 1), NEG_INF, np.float32)
    for row, (k1, k2) in enumerate(pairs):
        sel_a[row, k1] = 1.0
        sel_b[row, k2] = 1.0
        pad[row, 0] = 0.0
    return jnp.asarray(sel_a), jnp.asarray(sel_b), jnp.asarray(pad), jnp.asarray(sel_a.T, BF16)


def _route_kernel(x_ref, mod_ref, wq_ref, keys_ref, sela_ref, selb_ref, pad_ref, ind_ref,
                  cnt_ref, e1_ref, r2_ref, e2_ref, h2_s, *, heads):
    @pl.when(pl.program_id(1) == 0)
    def _modulate():
        h2_s[...] = _modulated(x_ref[...], mod_ref, 1).astype(BF16)

    pick = lambda sel_ref, v: jnp.dot(sel_ref[...], v, precision=HIGHEST, preferred_element_type=F32)
    q_all = _dot(h2_s[...], wq_ref[...])
    for hd in range(heads):
        q = q_all[:, hd * PEER_QDIM:(hd + 1) * PEER_QDIM]
        s1 = _dot_nt(keys_ref[hd, 0].astype(BF16), q[:, :128].astype(BF16))
        s2 = _dot_nt(keys_ref[hd, 1].astype(BF16), q[:, 128:].astype(BF16))
        a, rank1 = _top16(s1)
        b, rank2 = _top16(s2)
        ea = jnp.exp(a - a[0:1])
        eb = jnp.exp(b - b[0:1])
        cand = pick(sela_ref, a) + pick(selb_ref, b) + pad_ref[...]
        gate = pick(sela_ref, ea) * pick(selb_ref, eb)
        cur = cand
        thr = None
        for _ in range(PEER_TOPK):
            thr = jnp.max(cur, axis=0, keepdims=True)
            cur = jnp.where(cur == thr, NEG_INF, cur)
        chosen = cand >= thr
        z = jnp.sum(jnp.where(chosen, gate, 0.0), axis=0, keepdims=True)
        cnt_sorted = _dot(ind_ref[...], jnp.where(chosen, 1.0, 0.0).astype(BF16))
        rank1_b = rank1.astype(BF16)
        counts_b = cnt_sorted.astype(BF16)
        cnt = jnp.zeros(s1.shape, BF16)
        for r in range(PEER_TOPK):
            cnt = jnp.where(rank1_b == float(r + 1), counts_b[r:r + 1], cnt)
        cnt_ref[hd] = cnt.astype(F32)
        e1_ref[hd] = jnp.where(rank1 <= float(PEER_TOPK), jnp.exp(s1 - a[0:1]) * (0.5 / z), 0.0)
        packed = (PEER_NKEYS // 16, 16, s2.shape[1])
        r2_ref[hd] = rank2.astype(BF16).reshape(packed)
        e2_ref[hd] = jnp.where(rank2 <= float(PEER_TOPK), jnp.exp(s2 - b[0:1]), 0.0).astype(BF16).reshape(packed)


def _route(x1, mod, wq, keys, *, seq, cols, heads):
    n_tok = x1.shape[0]
    per_seq = seq // cols
    tables = _pair_tables()
    row_shape = jax.ShapeDtypeStruct((PEER_HEADS, PEER_NKEYS, n_tok), F32)
    col_shape = jax.ShapeDtypeStruct((PEER_HEADS, PEER_NKEYS // 16, 16, n_tok), BF16)
    col_spec = pl.BlockSpec((heads, PEER_NKEYS // 16, 16, cols), lambda i, h: (h, 0, 0, i))
    spec = pl.BlockSpec((heads, PEER_NKEYS, cols), lambda i, h: (h, 0, i))
    return pl.pallas_call(
        functools.partial(_route_kernel, heads=heads), out_shape=(row_shape, row_shape, col_shape, col_shape),
        grid=(n_tok // cols, PEER_HEADS // heads),
        in_specs=[
            pl.BlockSpec((cols, D_MODEL), lambda i, h: (i, 0)),
            pl.BlockSpec((None, 1, 6 * D_MODEL), lambda i, h: (i // per_seq if mod.shape[0] > 1 else 0, 0, 0)),
            pl.BlockSpec((D_MODEL, heads * PEER_QDIM), lambda i, h: (0, h)),
            pl.BlockSpec((heads, 2, PEER_NKEYS, PEER_QDIM // 2), lambda i, h: (h, 0, 0, 0)),
        ] + [pl.BlockSpec(t.shape, lambda i, h: (0, 0)) for t in tables],
        out_specs=(spec, spec, col_spec, col_spec),
        scratch_shapes=[pltpu.VMEM((cols, D_MODEL), BF16)],
        compiler_params=_cparams("arbitrary", "arbitrary"),
        name="peer_route",
    )(x1, mod, wq, keys, *tables)


def _peer_kernel(x_ref, mod_ref, u_ref, v_ref, cnt_ref, e1_ref, r2_ref, e2_ref, lng_ref, lnb_ref,
                 o_ref, h2t_s, acc_s, *, key_rows):
    e = pl.program_id(1)

    @pl.when(e == 0)
    def _init():
        h2 = _modulated(x_ref[...], mod_ref, 1)
        h2t_s[...] = h2.T.astype(BF16)
        acc_s[...] = jnp.zeros_like(acc_s)

    n_tok = h2t_s.shape[1]
    first_key = pl.multiple_of(e * key_rows, key_rows)
    zero = jnp.zeros((PEER_NKEYS // 16, 16, n_tok), BF16)
    act = _dot(u_ref[...], h2t_s[...]).astype(BF16)
    act = act * (1.0 + lax.erf(act * (2.0 ** -0.5)))
    pieces = []
    for r in range(key_rows):
        g = None
        for hd in range(PEER_HEADS):
            cnt_blk = cnt_ref[hd, pl.ds(first_key, key_rows), :]
            e1_blk = e1_ref[hd, pl.ds(first_key, key_rows), :]
            cnt_rows = jnp.broadcast_to(cnt_blk[r:r + 1, :], (16, n_tok)).astype(BF16)
            e1_rows = jnp.broadcast_to(e1_blk[r:r + 1, :], (16, n_tok)).astype(BF16)
            term = jnp.where(r2_ref[hd] <= cnt_rows[None], e2_ref[hd], zero) * e1_rows[None]
            g = term if g is None else g + term
        pieces.append(g.reshape(PEER_NKEYS, n_tok) * act[r * PEER_NKEYS:(r + 1) * PEER_NKEYS, :])
    acc_s[...] += _dot(v_ref[...], jnp.concatenate(pieces, axis=0))

    @pl.when(e == pl.num_programs(1) - 1)
    def _finish():
        x = x_ref[...]
        g2 = mod_ref[:, 5 * D_MODEL:6 * D_MODEL]
        o_ref[...] = _layer_norm_rows(ALPHA * x + g2 * acc_s[...].T, lng_ref[...], lnb_ref[...])


def _peer(x1, mod, u, v, route, ln_g, ln_b, *, seq, cols, key_rows, layer):
    n_tok = x1.shape[0]
    per_seq = seq // cols
    n_exp = key_rows * PEER_NKEYS
    n_tiles = PEER_EXPERTS // n_exp
    assert key_rows % 8 == 0
    kern = functools.partial(_peer_kernel, key_rows=key_rows)
    cnt, e1, r2, e2 = route
    rspec = pl.BlockSpec((PEER_HEADS, PEER_NKEYS, cols), lambda i, e: (0, 0, i))
    cspec = pl.BlockSpec((PEER_HEADS, PEER_NKEYS // 16, 16, cols), lambda i, e: (0, 0, 0, i))
    return pl.pallas_call(
        kern, out_shape=jax.ShapeDtypeStruct((n_tok, D_MODEL), F32),
        grid=(n_tok // cols, n_tiles),
        in_specs=[
            pl.BlockSpec((cols, D_MODEL), lambda i, e: (i, 0)),
            pl.BlockSpec((None, 1, 6 * D_MODEL), lambda i, e: (i // per_seq if mod.shape[0] > 1 else 0, 0, 0)),
            pl.BlockSpec((None, n_exp, D_MODEL), lambda i, e: (layer, e, 0)),
            pl.BlockSpec((None, D_MODEL, n_exp), lambda i, e: (layer, 0, e)),
            rspec, rspec, cspec, cspec,
            pl.BlockSpec(ln_g.shape, lambda i, e: (0, 0)), pl.BlockSpec(ln_b.shape, lambda i, e: (0, 0)),
        ],
        out_specs=pl.BlockSpec((cols, D_MODEL), lambda i, e: (i, 0)),
        scratch_shapes=[pltpu.VMEM((D_MODEL, cols), BF16), pltpu.VMEM((D_MODEL, cols), F32)],
        compiler_params=_cparams("arbitrary", "arbitrary"),
        name="peer_experts",
    )(x1, mod, u, v, cnt, e1, r2, e2, ln_g, ln_b)


def _rope_tables(seq):
    t = np.arange(seq)
    pos = np.stack([t // GRID_W, t % GRID_W], axis=1).astype(np.float64)
    inv = ROPE_BASE ** (-np.arange(16, dtype=np.float64) / 16)
    lane = np.arange(64)
    ang = pos[:, lane // 32] * inv[lane % 16][None, :]
    sign = np.where((lane % 32) < 16, -1.0, 1.0)[None, :]
    cos = np.tile(np.cos(ang), (1, 2)).astype(np.float32)
    sin = np.tile(np.sin(ang) * sign, (1, 2)).astype(np.float32)
    return jnp.asarray(cos), jnp.asarray(sin)


def _static_tables():
    lane = np.arange(128)
    bd = (lane[:, None] // 64 == lane[None, :] // 64).astype(np.float32) / 64.0
    src = np.arange(128)
    dst = np.arange(512)
    rep = ((src[:, None] // 64 == dst[None, :] // 256) & (src[:, None] % 64 == dst[None, :] % 64))
    idx = np.arange(CHUNK)
    tril = (idx[None, :] <= idx[:, None]).astype(np.float32)
    triu = (idx[None, :] >= idx[:, None]).astype(np.float32)
    return (jnp.asarray(bd, BF16), jnp.asarray(rep.astype(np.float32), BF16), jnp.asarray(tril), jnp.asarray(triu))


def _layer_params(l, w_in, attn_qk_gain, gmlp_ws, gmlp_b, mlstm_gate_bias, mlstm_gn, diff_lambda, diff_gn,
                  w_branch, w_out, ln_g, ln_b, peer_wq, peer_keys, peer_u, peer_v):
    w = w_in[l]
    p = {}
    p["w_a"] = w[:, _OFF_A:_OFF_B].astype(BF16)
    p["w_b"] = w[:, _OFF_B:_OFF_C].astype(BF16)
    cq, ck = w[:, _OFF_C:_OFF_C + 512], w[:, _OFF_C + 512:_OFF_C + 1024]
    cv, co = w[:, _OFF_C + 1024:_OFF_C + 1536], w[:, _OFF_C + 1536:_OFF_C + 2048]
    cg = jnp.concatenate([w[:, _OFF_CG:_OFF_D], jnp.zeros((D_MODEL, 112), F32)], axis=1)
    p["w_c_nat"] = jnp.concatenate([ck, cg], axis=1).astype(BF16)
    p["w_c_t"] = jnp.concatenate([cq, cv, co, cg], axis=1).T.astype(BF16)
    p["w_d"] = w[:, _OFF_D:_OFF_G].astype(BF16)
    p["w_g"] = w[:, _OFF_G:].astype(BF16)
    gain = attn_qk_gain[l]
    p["gain_row"] = jnp.concatenate([jnp.tile(gain[0], A_HEADS), jnp.tile(gain[1], A_KV_HEADS)])[None, :]
    p["ws"] = gmlp_ws[l]
    p["bias_full"] = jnp.repeat(gmlp_b[l].T, 128, axis=1)
    p["gate_bias_row"] = jnp.concatenate([mlstm_gate_bias[l].reshape(16), jnp.zeros((112,), F32)])[None, :]
    p["gate_bias_t"] = jnp.broadcast_to(p["gate_bias_row"].reshape(128, 1), (128, 128))
    p["mlstm_gn_t"] = jnp.broadcast_to(mlstm_gn[l].reshape(BRANCH_WIDTH, 1), (BRANCH_WIDTH, 128))
    p["lam"] = diff_lambda[l]
    p["diff_gn_row"] = diff_gn[l].reshape(1, BRANCH_WIDTH)
    p["w_br"] = w_branch[l].astype(BF16)
    p["w_out"] = w_out[l].astype(BF16)
    p["ln_g0"], p["ln_b0"] = ln_g[l, 0][None, :], ln_b[l, 0][None, :]
    p["ln_g1"], p["ln_b1"] = ln_g[l, 1][None, :], ln_b[l, 1][None, :]
    p["wq"] = peer_wq[l].astype(BF16)
    p["keys"] = peer_keys[l]
    p["u"], p["v"] = peer_u, peer_v
    return p


def _trunk_layer(x, mod, p, tabs, *, l, n_seq, seq, cfg, ctx_cache=None, prev_state=None):
    bd, rep, tril, triu = tabs
    lam_init = 0.8 - 0.6 * math.exp(-0.3 * l)
    state = None
    if ctx_cache is None:
        prev = (None, None, None) if prev_state is None else (prev_state[0:2], prev_state[2:4], prev_state[4:7])
        ya, nk, nv = _attn_a(x, mod, p["w_a"], p["gain_row"], bd, rep, n_seq=n_seq, seq=seq, tq=cfg["tq"],
                             layer=l, carry=prev[0])
        yd, ndk, ndv = _attn_d(x, mod, p["w_d"], p["lam"], p["diff_gn_row"], n_seq=n_seq, seq=seq, tq=cfg["tq"],
                               lam_init=lam_init, layer=l, carry=prev[1])
        yc, c_new, n_new, m_new = _mlstm_t(x, mod, p["w_c_nat"], p["w_c_t"], p["gate_bias_row"], p["gate_bias_t"],
                                           p["mlstm_gn_t"], tril, triu, n_seq=n_seq, seq=seq, layer=l, carry=prev[2])
        state = (nk, nv, ndk, ndv, c_new, n_new, m_new)
    else:
        rope, cak, cav, cdk, cdv, c0, n0, m0 = ctx_cache
        ya = _attn_a(x, mod, p["w_a"], p["gain_row"], bd, rep, n_seq=n_seq, seq=seq, tq=cfg["tq"],
                     rope=rope, cache=(cak, cav), layer=l)
        yd = _attn_d(x, mod, p["w_d"], p["lam"], p["diff_gn_row"], n_seq=n_seq, seq=seq, tq=cfg["tq_d"],
                     lam_init=lam_init, rope=rope, cache=(cdk, cdv), layer=l)
        yc = _mlstm_t(x, mod, p["w_c_nat"], p["w_c_t"], p["gate_bias_row"], p["gate_bias_t"], p["mlstm_gn_t"],
                      tril, triu, n_seq=n_seq, seq=seq, init=(c0, n0, m0), layer=l)
    yb = _gmlp(x, mod, p["w_b"], p["ws"], p["bias_full"], seq=seq, rows=cfg["gmlp_rows"])
    x1 = _merge(x, mod, (ya, yb, yc, yd), p["w_g"], p["w_br"], p["w_out"], p["ln_g0"], p["ln_b0"],
                seq=seq, rows=cfg["rows"])
    route = _route(x1, mod, p["wq"], p["keys"], seq=seq, cols=cfg["route_cols"], heads=cfg["route_heads"])
    x2 = _peer(x1, mod, p["u"], p["v"], route, p["ln_g1"], p["ln_b1"], seq=seq, cols=cfg["cols"], layer=l,
               key_rows=cfg["key_rows"])
    return x2, state


def kernel(x_prompt, x_sample, cache_a_k, cache_a_v, cache_d_k, cache_d_v, state_c_C, state_c_n, state_c_m,
           c, c_ctx, w_mod, b_mod, w_in, attn_qk_gain, gmlp_ws, gmlp_b, mlstm_gate_bias, mlstm_gn,
           diff_lambda, diff_gn, w_branch, w_out, ln_g, ln_b, peer_wq, peer_keys, peer_u, peer_v):
    batch, seq, _ = x_prompt.shape
    dec_batch, dec_seq, _ = x_sample.shape
    past = cache_a_k.shape[2]
    c_rows = jnp.concatenate([c_ctx[None, :], c, jnp.zeros((8 - 1 - dec_batch, D_MODEL), F32)], axis=0)
    mods = _modulation(c_rows, w_mod, b_mod)
    tabs = _static_tables()
    rope = _rope_tables(dec_seq)
    cak = cache_a_k.reshape(dec_batch, DEPTH, past, 128)
    cav = cache_a_v.reshape(dec_batch, DEPTH, past, 128)
    cdk = cache_d_k.reshape(dec_batch, DEPTH, past, 512)
    cdv = cache_d_v.reshape(dec_batch, DEPTH, past, 512)
    c0 = state_c_C.reshape(dec_batch, DEPTH, 8, 128, 128)
    n0 = state_c_n.reshape(dec_batch, DEPTH, 8, 128)
    m0 = jnp.broadcast_to(state_c_m.reshape(dec_batch, DEPTH, 8, 1), (dec_batch, DEPTH, 8, 128))
    cfg_ctx = dict(tq=seq, rows=min(1024, batch * seq), gmlp_rows=min(1024, batch * seq), cols=512, key_rows=16,
                   route_cols=min(1024, batch * seq), route_heads=4)
    cfg_lat = dict(tq=min(256, dec_seq), tq_d=min(512, dec_seq), rows=min(1024, dec_seq),
                   gmlp_rows=min(1024, dec_seq), cols=512, key_rows=16,
                   route_cols=min(1024, dec_seq), route_heads=4)
    y_p = x_prompt.reshape(batch * seq, D_MODEL)
    y_s = x_sample.reshape(dec_batch * dec_seq, D_MODEL)
    state = None
    u_all, v_all = peer_u.astype(BF16), jnp.swapaxes(peer_v, 1, 2).astype(BF16)
    for l in range(DEPTH):
        p = _layer_params(l, w_in, attn_qk_gain, gmlp_ws, gmlp_b, mlstm_gate_bias, mlstm_gn, diff_lambda, diff_gn,
                          w_branch, w_out, ln_g, ln_b, peer_wq, peer_keys, u_all, v_all)
        mod_ctx = mods[l, 0:1].reshape(1, 1, 6 * D_MODEL)
        mod_lat = mods[l, 1:1 + dec_batch].reshape(dec_batch, 1, 6 * D_MODEL)
        y_p, state = _trunk_layer(y_p, mod_ctx, p, tabs, l=l, n_seq=batch, seq=seq, cfg=cfg_ctx, prev_state=state)
        y_s, _ = _trunk_layer(y_s, mod_lat, p, tabs, l=l, n_seq=dec_batch, seq=dec_seq, cfg=cfg_lat,
                              ctx_cache=(rope, cak, cav, cdk, cdv, c0, n0, m0))
    nk = state[0].reshape(batch, DEPTH, seq, A_KV_HEADS, A_HEAD_DIM)
    nv = state[1].reshape(batch, DEPTH, seq, A_KV_HEADS, A_HEAD_DIM)
    ndk = state[2].reshape(batch, DEPTH, seq, D_HEADS, 2, D_HALF_DIM)
    ndv = state[3].reshape(batch, DEPTH, seq, D_HEADS, D_VDIM)
    nc = state[4].reshape(batch, DEPTH, 2, C_HEADS, C_HEAD_DIM, C_HEAD_DIM)
    nn = state[5].reshape(batch, DEPTH, 2, C_HEADS, C_HEAD_DIM)
    nm = state[6][:, :, :, 0].reshape(batch, DEPTH, 2, C_HEADS)
    return (y_p.reshape(batch, seq, D_MODEL), y_s.reshape(dec_batch, dec_seq, D_MODEL), nk, nv, ndk, ndv, nc, nn, nm)
```
